```python
import jax, jax.numpy as jnp
from jax import lax
import numpy as np

D_MODEL = 2048
BATCH = 32
SEQ = 256
DEPTH = 1
DEC_BATCH = 8
DEC_SEQ = 4096
PAST_LEN = 256

GRID_W = 64
D_LRU = D_MODEL // 2
D_POOL = D_MODEL - D_LRU
D_MIX = D_LRU + D_POOL
LRU_HEADS = 4
LRU_BW = D_LRU // LRU_HEADS
CONV_W = 4
RG_C = 8.0
POOL_WINDOWS = (2, 4, 8, 16)
N_POOL_GROUPS = len(POOL_WINDOWS)
POOL_GW = D_POOL // N_POOL_GROUPS
N_GROUPS = 4
EXPERTS_PER_GROUP = 4
N_EXPERTS = N_GROUPS * EXPERTS_PER_GROUP
TOP_K_FINE = 2
D_EXPERT = D_MODEL // 4
N_MOD = 6
EPS = 1e-6

kernel_name = "hybrid_rglru_pool_hmoe_diffusion_step"


def rms_norm(x, g):
    xf = x.astype(jnp.float32)
    y = xf * lax.rsqrt(jnp.mean(xf * xf, axis=-1, keepdims=True) + EPS)
    return (y * g.astype(jnp.float32)).astype(x.dtype)


def depthwise_conv_centred(x, w, b):
    n = x.shape[1]
    left = CONV_W // 2
    right = CONV_W - 1 - left
    xp = jnp.pad(x, ((0, 0), (left, right), (0, 0)))
    y = xp[:, 0:n] * w[0]
    for k in range(1, CONV_W):
        y = y + xp[:, k:k + n] * w[k]
    return y + b


def block_diag_linear(x, w, b):
    bsz, n, _ = x.shape
    y = jnp.einsum('bnhi,hij->bnhj', x.reshape(bsz, n, LRU_HEADS, LRU_BW), w)
    return y.reshape(bsz, n, D_LRU) + b


def _lin_combine(left, right):
    a1, b1 = left
    a2, b2 = right
    return (a1 * a2, a2 * b1 + b2)


def rglru_scan(x, wa, ba, wx, bx, lam, h0, reverse):
    xf = x.astype(jnp.float32)
    r = jax.nn.sigmoid(block_diag_linear(xf, wa.astype(jnp.float32), ba.astype(jnp.float32)))
    ig = jax.nn.sigmoid(block_diag_linear(xf, wx.astype(jnp.float32), bx.astype(jnp.float32)))
    log_a = RG_C * r * jax.nn.log_sigmoid(lam.astype(jnp.float32))
    a = jnp.exp(log_a)
    u = jnp.sqrt(-jnp.expm1(2.0 * log_a)) * (ig * xf)
    if reverse:
        a = a[:, ::-1]
        u = u[:, ::-1]
    a_cum, h_part = lax.associative_scan(_lin_combine, (a, u), axis=1)
    h = a_cum * h0.astype(jnp.float32)[:, None, :] + h_part
    h_final = h[:, -1]
    if reverse:
        h = h[:, ::-1]
    return h, h_final


def box_sum(x, w, axis):
    n = x.shape[axis]
    lo = w // 2
    hi = w - 1 - lo
    pad = [(0, 0)] * x.ndim
    pad[axis] = (1, 0)
    cs = jnp.pad(jnp.cumsum(x, axis=axis), pad)
    t = jnp.arange(n)
    end = jnp.minimum(t + hi + 1, n)
    start = jnp.maximum(t - lo, 0)
    s = jnp.take(cs, end, axis=axis) - jnp.take(cs, start, axis=axis)
    return s, (end - start).astype(jnp.float32)


def pool_mixer(xp, pw, pscale, rows):
    bsz, n, _ = xp.shape
    xf = xp.astype(jnp.float32)
    outs = []
    for g, w in enumerate(POOL_WINDOWS):
        xg = xf[..., g * POOL_GW:(g + 1) * POOL_GW]
        if rows is None:
            s, cnt = box_sum(xg, w, 1)
            mean = s / cnt[None, :, None]
        else:
            xg2 = xg.reshape(bsz, rows, GRID_W, POOL_GW)
            s1, cw = box_sum(xg2, w, 2)
            s2, ch = box_sum(s1, w, 1)
            mean = (s2 / (ch[:, None] * cw[None, :])[None, :, :, None]).reshape(bsz, n, POOL_GW)
        outs.append(mean - xg)
    d = jnp.stack(outs, axis=2)
    y = jnp.einsum('bngi,gij->bngj', d, pw.astype(jnp.float32)).reshape(bsz, n, D_POOL)
    return (y * pscale.astype(jnp.float32)).astype(xp.dtype)


def hier_moe(h, rcw, rcb, rfw, rfb, w1, w3, w2):
    bsz, n, d = h.shape
    ht = h.reshape(bsz * n, d)
    t = ht.shape[0]
    coarse = (ht @ rcw + rcb).astype(jnp.float32)
    p_coarse = jax.nn.softmax(coarse, axis=-1)
    pg, gi = lax.top_k(p_coarse, 1)
    fine = (jnp.einsum('td,dge->tge', ht, rfw) + rfb).astype(jnp.float32)
    fine_sel = jnp.take_along_axis(fine, gi[:, :, None], axis=1)[:, 0]
    fv, fi = lax.top_k(fine_sel, TOP_K_FINE)
    fw = jax.nn.softmax(fv, axis=-1) * pg
    within = jnp.sum(jax.nn.one_hot(fi, EXPERTS_PER_GROUP, dtype=jnp.float32) * fw[..., None], axis=1)
    comb = (jax.nn.one_hot(gi[:, 0], N_GROUPS, dtype=jnp.float32)[:, :, None]
            * within[:, None, :]).reshape(t, N_EXPERTS).astype(h.dtype)
    out = jnp.zeros_like(ht)
    for e in range(N_EXPERTS):
        y = (jax.nn.silu(ht @ w1[e]) * (ht @ w3[e])) @ w2[e]
        out = out + comb[:, e:e + 1] * y
    return out.reshape(bsz, n, d)


def trunk_layer(x, cvec, h0, lp, grid):
    (w_mod, b_mod, norm1_g, w_in, conv_w, conv_b, lru_wa, lru_ba, lru_wx, lru_bx, lru_lambda,
     pool_w, pool_scale, w_out, norm2_g, rcw, rcb, rfw, rfb, w1, w3, w2) = lp
    mod = (jax.nn.silu(cvec) @ w_mod + b_mod)[..., None, :]
    sh1, sc1, g1, sh2, sc2, g2 = jnp.split(mod, N_MOD, axis=-1)
    h = rms_norm(x, norm1_g) * (1.0 + sc1) + sh1
    proj = h @ w_in
    xa, ga, xb = jnp.split(proj, [D_LRU, 2 * D_LRU], axis=-1)
    xa = depthwise_conv_centred(xa, conv_w, conv_b)
    hf, hf_last = rglru_scan(xa, lru_wa[0], lru_ba[0], lru_wx[0], lru_bx[0], lru_lambda[0], h0[:, 0], False)
    hb, hb_last = rglru_scan(xa, lru_wa[1], lru_ba[1], lru_wx[1], lru_bx[1], lru_lambda[1], h0[:, 1], True)
    ya = (hf + hb).astype(x.dtype) * jax.nn.gelu(ga)
    rows = x.shape[1] // GRID_W if grid else None
    yb = pool_mixer(xb, pool_w, pool_scale, rows)
    x = x + g1 * (jnp.concatenate([ya, yb], axis=-1) @ w_out)
    h2 = rms_norm(x, norm2_g) * (1.0 + sc2) + sh2
    x = x + g2 * hier_moe(h2, rcw, rcb, rfw, rfb, w1, w3, w2)
    h_last = jnp.stack([hf_last, hb_last], axis=1)
    return x, h_last


def setup_inputs(seed: int = 0) -> dict:
    key = jax.random.key(seed)
    ks = jax.random.split(key, 32)
    f32 = jnp.float32

    def nrm(k, shape, scale):
        return jax.random.normal(k, shape, f32) * scale

    u = jax.random.uniform(ks[14], (DEPTH, 2, D_LRU), f32, minval=0.9, maxval=0.999)
    s = u ** (1.0 / RG_C)
    lam = jnp.log(s) - jnp.log1p(-s)
    return {
        "x_prompt": nrm(ks[0], (BATCH, SEQ, D_MODEL), 1.0),
        "x_sample": nrm(ks[1], (DEC_BATCH, DEC_SEQ, D_MODEL), 1.0),
        "state_lru": nrm(ks[2], (DEC_BATCH, DEPTH, 2, D_LRU), 0.5),
        "c": nrm(ks[3], (DEC_BATCH, D_MODEL), 1.0),
        "c_ctx": nrm(ks[4], (D_MODEL,), 1.0),
        "w_mod": nrm(ks[5], (DEPTH, D_MODEL, N_MOD * D_MODEL), 0.5 * D_MODEL ** -0.5),
        "b_mod": nrm(ks[6], (DEPTH, N_MOD * D_MODEL), 0.02),
        "norm1_g": 1.0 + nrm(ks[7], (DEPTH, D_MODEL), 0.05),
        "w_in": nrm(ks[8], (DEPTH, D_MODEL, 2 * D_LRU + D_POOL), D_MODEL ** -0.5),
        "conv_w": nrm(ks[9], (DEPTH, CONV_W, D_LRU), CONV_W ** -0.5),
        "conv_b": nrm(ks[10], (DEPTH, D_LRU), 0.02),
        "lru_wa": nrm(ks[11], (DEPTH, 2, LRU_HEADS, LRU_BW, LRU_BW), LRU_BW ** -0.5),
        "lru_ba": nrm(ks[12], (DEPTH, 2, D_LRU), 0.02),
        "lru_wx": nrm(ks[13], (DEPTH, 2, LRU_HEADS, LRU_BW, LRU_BW), LRU_BW ** -0.5),
        "lru_bx": nrm(ks[15], (DEPTH, 2, D_LRU), 0.02),
        "lru_lambda": lam,
        "pool_w": nrm(ks[16], (DEPTH, N_POOL_GROUPS, POOL_GW, POOL_GW), POOL_GW ** -0.5),
        "pool_scale": 1.0 + nrm(ks[17], (DEPTH, D_POOL), 0.05),
        "w_out": nrm(ks[18], (DEPTH, D_MIX, D_MODEL), D_MIX ** -0.5),
        "norm2_g": 1.0 + nrm(ks[19], (DEPTH, D_MODEL), 0.05),
        "router_coarse_w": nrm(ks[20], (DEPTH, D_MODEL, N_GROUPS), D_MODEL ** -0.5),
        "router_coarse_b": nrm(ks[21], (DEPTH, N_GROUPS), 0.01),
        "router_fine_w": nrm(ks[22], (DEPTH, D_MODEL, N_GROUPS, EXPERTS_PER_GROUP), D_MODEL ** -0.5),
        "router_fine_b": nrm(ks[23], (DEPTH, N_GROUPS, EXPERTS_PER_GROUP), 0.01),
        "exp_w1": nrm(ks[24], (DEPTH, N_EXPERTS, D_MODEL, D_EXPERT), D_MODEL ** -0.5),
        "exp_w3": nrm(ks[25], (DEPTH, N_EXPERTS, D_MODEL, D_EXPERT), D_MODEL ** -0.5),
        "exp_w2": nrm(ks[26], (DEPTH, N_EXPERTS, D_EXPERT, D_MODEL), D_EXPERT ** -0.5),
        "final_norm_g": 1.0 + nrm(ks[27], (D_MODEL,), 0.05),
    }


def reference(x_prompt, x_sample, state_lru, c, c_ctx, w_mod, b_mod, norm1_g, w_in, conv_w, conv_b,
              lru_wa, lru_ba, lru_wx, lru_bx, lru_lambda, pool_w, pool_scale, w_out, norm2_g,
              router_coarse_w, router_coarse_b, router_fine_w, router_fine_b,
              exp_w1, exp_w3, exp_w2, final_norm_g):
    xp = x_prompt
    xs = x_sample
    ctx_states = []
    h0_ctx = jnp.zeros((x_prompt.shape[0], 2, D_LRU), jnp.float32)
    for l in range(DEPTH):
        lp = (w_mod[l], b_mod[l], norm1_g[l], w_in[l], conv_w[l], conv_b[l], lru_wa[l], lru_ba[l],
              lru_wx[l], lru_bx[l], lru_lambda[l], pool_w[l], pool_scale[l], w_out[l], norm2_g[l],
              router_coarse_w[l], router_coarse_b[l], router_fine_w[l], router_fine_b[l],
              exp_w1[l], exp_w3[l], exp_w2[l])
        xp, st = trunk_layer(xp, c_ctx, h0_ctx, lp, False)
        ctx_states.append(st)
        xs, _ = trunk_layer(xs, c, state_lru[:, l], lp, True)
    y_prompt = rms_norm(xp, final_norm_g)
    y_sample = rms_norm(xs, final_norm_g)
    state_lru_new = jnp.stack(ctx_states, axis=1).astype(x_prompt.dtype)
    return (y_prompt, y_sample, state_lru_new)
```

```python
import functools

import numpy as np
import jax
import jax.numpy as jnp
from jax import lax
from jax.experimental import pallas as pl
from jax.experimental.pallas import tpu as pltpu

GRID_W = 64
CONV_W = 4
RG_C = 8.0
POOL_WINDOWS = (2, 4, 8, 16)
TOP_K_FINE = 2
EPS = 1e-6
EXPM1_SERIES_BELOW = 0.25

SUBLANES = 8
LANES = 128
BF16_ROWS = 16
TOKEN_TILE = 512
EXPERT_TILE = 512
ROUTE_TILE = 2048
CUMSUM_BLOCK = 512
MOD_COL_TILE = 1024
VMEM_LIMIT = 60 * 1024 * 1024

F32 = jnp.float32
BF16 = jnp.bfloat16


def _cparams(sem):
    return pltpu.CompilerParams(dimension_semantics=sem, vmem_limit_bytes=VMEM_LIMIT)


def _per_sequence(y, m):
    rows, d = y.shape
    return y.reshape(rows // SUBLANES, SUBLANES, d), m[None]


def _modulate(y, scale, shift):
    y3, sc = _per_sequence(y, scale)
    _, sh = _per_sequence(y, shift)
    return (y3 * (1.0 + sc) + sh).reshape(y.shape)


def _gate(y, g):
    y3, g3 = _per_sequence(y, g)
    return (y3 * g3).reshape(y.shape)


def _rms(x, g):
    ms = jnp.mean(x * x, axis=-1, keepdims=True)
    return x * lax.rsqrt(ms + EPS) * g


def _mod_kernel(c_ref, w_ref, b_ref, o_ref):
    c = c_ref[...]
    s = c * jax.nn.sigmoid(c)
    o_ref[...] = jnp.dot(s.astype(BF16), w_ref[...].astype(BF16),
                         preferred_element_type=F32) + b_ref[...]


def _modulation(cond, w_mod, b_mod):
    rows, d = cond.shape
    n = w_mod.shape[1]
    tn = min(MOD_COL_TILE, n)
    return pl.pallas_call(
        _mod_kernel,
        grid=(n // tn,),
        in_specs=[pl.BlockSpec((rows, d), lambda j: (0, 0)),
                  pl.BlockSpec((d, tn), lambda j: (0, j)),
                  pl.BlockSpec((1, tn), lambda j: (0, j))],
        out_specs=pl.BlockSpec((rows, tn), lambda j: (0, j)),
        out_shape=jax.ShapeDtypeStruct((rows, n), F32),
        compiler_params=_cparams(("arbitrary",)),
        name="modulation",
    )(cond, w_mod, b_mod.reshape(1, n))


def _proj_kernel(x_ref, sh_ref, sc_ref, g_ref, win_ref, pw_ref, xa_ref, ga_ref, z_ref, *, d_lru, gw):
    h = _modulate(_rms(x_ref[...], g_ref[...]), sc_ref[0], sh_ref[0])
    proj = jnp.dot(h.astype(BF16), win_ref[...], preferred_element_type=F32)
    xa_ref[...] = proj[:, :d_lru].astype(BF16)
    ga_ref[...] = proj[:, d_lru:2 * d_lru].astype(BF16)
    for g in range(pw_ref.shape[0]):
        lo = 2 * d_lru + g * gw
        z_ref[:, g * gw:(g + 1) * gw] = jnp.dot(
            proj[:, lo:lo + gw].astype(BF16), pw_ref[g], preferred_element_type=F32).astype(BF16)


def _input_projection(x_all, modtab, norm_g, w_in, pool_w, n_lat_tiles, tm):
    t, d = x_all.shape
    d_lru = (w_in.shape[1] - pool_w.shape[0] * pool_w.shape[1]) // 2
    d_pool = pool_w.shape[0] * pool_w.shape[1]
    mset = lambda i: jnp.where(i < n_lat_tiles, 0, 1)
    const = dict(pipeline_mode=pl.Buffered(1))
    return pl.pallas_call(
        functools.partial(_proj_kernel, d_lru=d_lru, gw=pool_w.shape[1]),
        grid=(t // tm,),
        in_specs=[pl.BlockSpec((tm, d), lambda i: (i, 0)),
                  pl.BlockSpec((1, SUBLANES, d), lambda i: (mset(i), 0, 0)),
                  pl.BlockSpec((1, SUBLANES, d), lambda i: (mset(i), 0, 1)),
                  pl.BlockSpec((1, d), lambda i: (0, 0)),
                  pl.BlockSpec(w_in.shape, lambda i: (0, 0), **const),
                  pl.BlockSpec(pool_w.shape, lambda i: (0, 0, 0), **const)],
        out_specs=[pl.BlockSpec((tm, d_lru), lambda i: (i, 0)),
                   pl.BlockSpec((tm, d_lru), lambda i: (i, 0)),
                   pl.BlockSpec((tm, d_pool), lambda i: (i, 0))],
        out_shape=[jax.ShapeDtypeStruct((t, d_lru), BF16),
                   jax.ShapeDtypeStruct((t, d_lru), BF16),
                   jax.ShapeDtypeStruct((t, d_pool), BF16)],
        compiler_params=_cparams(("arbitrary",)),
        name="input_projection",
    )(x_all, modtab, modtab, norm_g.reshape(1, d), w_in, pool_w)


def _fill_ext(ext_ref, prev_ref, main_ref, next_ref, first, last):
    tm = main_ref.shape[0]
    prev = prev_ref[...].astype(F32)
    nxt = next_ref[...].astype(F32)
    ext_ref[0:BF16_ROWS, :] = jnp.where(first, 0.0, prev)
    ext_ref[BF16_ROWS:BF16_ROWS + tm, :] = main_ref[...].astype(F32)
    ext_ref[BF16_ROWS + tm:, :] = jnp.where(last, 0.0, nxt)


def _one_minus_exp(y, exp_y):
    p = 1.0 / 5040.0
    for c in (1.0 / 720.0, 1.0 / 120.0, 1.0 / 24.0, 1.0 / 6.0, 0.5, 1.0):
        p = p * y + c
    return jnp.where(y > -EXPM1_SERIES_BELOW, -y * p, 1.0 - exp_y)


def _decay_and_input(ext_ref, cw_ref, cb_ref, wg_ref, bg_ref, lam_ref, a_ref, u_ref, tm):
    heads, bw = wg_ref.shape[0], wg_ref.shape[1]
    for hd in range(heads):
        sl = slice(hd * bw, (hd + 1) * bw)
        xc = cb_ref[:, sl]
        for k in range(CONV_W):
            xc = xc + cw_ref[k:k + 1, sl] * ext_ref[SUBLANES * k:SUBLANES * k + tm, sl]
        g = jnp.dot(xc.astype(BF16), wg_ref[hd], preferred_element_type=F32) + bg_ref[hd]
        r = jax.nn.sigmoid(g[:, :bw])
        ig = jax.nn.sigmoid(g[:, bw:])
        log_a = (RG_C * r) * lam_ref[:, sl]
        a = jnp.exp(log_a)
        a_ref[:, sl] = a
        u_ref[:, sl] = jnp.sqrt(_one_minus_exp(2.0 * log_a, a * a)) * (ig * xc)


def _scan(a_ref, u_ref, h, tm, reverse):
    nblk = tm // SUBLANES

    def body(s, h):
        j = (nblk - 1 - s) if reverse else s
        rows = pl.ds(pl.multiple_of(j * SUBLANES, SUBLANES), SUBLANES)
        h = a_ref[rows, :] * h + u_ref[rows, :]
        u_ref[rows, :] = h
        return h

    return lax.fori_loop(0, nblk, body, h, unroll=8)


def _halo_specs(tm, d_lru, n_rows):
    per = tm // BF16_ROWS
    last_blk = n_rows // BF16_ROWS - 1
    return [pl.BlockSpec((BF16_ROWS, d_lru), lambda i, tr, fl: (jnp.maximum(tr[i] * per - 1, 0), 0)),
            pl.BlockSpec((tm, d_lru), lambda i, tr, fl: (tr[i], 0)),
            pl.BlockSpec((BF16_ROWS, d_lru), lambda i, tr, fl: (jnp.minimum((tr[i] + 1) * per, last_blk), 0))]


def _fwd_kernel(tr_ref, fl_ref, prev_ref, main_ref, next_ref, cw_ref, cb_ref, wg_ref, bg_ref, lam_ref,
                h0_ref, hf_ref, hlast_ref, ext_ref, a_ref, u_ref, h_ref):
    i = pl.program_id(0)
    tm = main_ref.shape[0]
    first = fl_ref[0, i] == 1
    last = fl_ref[1, i] == 1
    _fill_ext(ext_ref, prev_ref, main_ref, next_ref, first, last)
    _decay_and_input(ext_ref, cw_ref, cb_ref, wg_ref, bg_ref, lam_ref, a_ref, u_ref, tm)

    @pl.when(first)
    def _():
        h_ref[...] = h0_ref[0]

    h = _scan(a_ref, u_ref, h_ref[...], tm, reverse=False)
    h_ref[...] = h
    hlast_ref[0] = h
    hf_ref[...] = u_ref[...].astype(BF16)


def _forward_scan(xa, meta, conv_w, conv_b, wg, bg, lam, h0, tm):
    t, d_lru = xa.shape
    tile_row, flags, tile_grp = meta
    ngrp = h0.shape[0]
    const = dict(pipeline_mode=pl.Buffered(1))
    grid_spec = pltpu.PrefetchScalarGridSpec(
        num_scalar_prefetch=2,
        grid=(t // tm,),
        in_specs=_halo_specs(tm, d_lru, t) + [
            pl.BlockSpec(conv_w.shape, lambda i, tr, fl: (0, 0)),
            pl.BlockSpec((1, d_lru), lambda i, tr, fl: (0, 0)),
            pl.BlockSpec(wg.shape, lambda i, tr, fl: (0, 0, 0), **const),
            pl.BlockSpec(bg.shape, lambda i, tr, fl: (0, 0, 0)),
            pl.BlockSpec((1, d_lru), lambda i, tr, fl: (0, 0)),
            pl.BlockSpec((1, SUBLANES, d_lru), lambda i, tr, fl: (fl[2, i], 0, 0))],
        out_specs=[pl.BlockSpec((tm, d_lru), lambda i, tr, fl: (tr[i], 0)),
                   pl.BlockSpec((1, SUBLANES, d_lru), lambda i, tr, fl: (fl[2, i], 0, 0))],
        scratch_shapes=[pltpu.VMEM((tm + 2 * BF16_ROWS, d_lru), F32),
                        pltpu.VMEM((tm, d_lru), F32),
                        pltpu.VMEM((tm, d_lru), F32),
                        pltpu.VMEM((SUBLANES, d_lru), F32)])
    del tile_grp
    return pl.pallas_call(
        _fwd_kernel,
        grid_spec=grid_spec,
        out_shape=[jax.ShapeDtypeStruct((t, d_lru), BF16),
                   jax.ShapeDtypeStruct((ngrp, SUBLANES, d_lru), F32)],
        compiler_params=_cparams(("arbitrary",)),
        name="forward_scan",
    )(tile_row, flags, xa, xa, xa, conv_w, conv_b.reshape(1, d_lru), wg, bg, lam.reshape(1, d_lru), h0)


def _bwd_kernel(tr_ref, fl_ref, prev_ref, main_ref, next_ref, cw_ref, cb_ref, wg_ref, bg_ref, lam_ref,
                h0_ref, hf_ref, ga_ref, yb_ref, x_ref, g1_ref, sh2_ref, sc2_ref, wout_ref, n2_ref,
                rwt_ref, rb_ref,
                x1_ref, h2_ref, lg_ref, hlast_ref, ext_ref, a_ref, u_ref, h_ref, cat_ref):
    i = pl.program_id(0)
    tm, d_lru = main_ref.shape
    first = fl_ref[0, i] == 1
    last = fl_ref[1, i] == 1
    _fill_ext(ext_ref, prev_ref, main_ref, next_ref, first, last)
    _decay_and_input(ext_ref, cw_ref, cb_ref, wg_ref, bg_ref, lam_ref, a_ref, u_ref, tm)

    @pl.when(last)
    def _():
        h_ref[...] = h0_ref[0]

    h = _scan(a_ref, u_ref, h_ref[...], tm, reverse=True)
    h_ref[...] = h
    hlast_ref[0] = h

    ga = ga_ref[...].astype(F32)
    ya = (hf_ref[...].astype(F32) + u_ref[...]) * jax.nn.gelu(ga)
    cat_ref[:, :d_lru] = ya.astype(BF16)
    cat_ref[:, d_lru:] = yb_ref[...]
    mix = jnp.dot(cat_ref[...], wout_ref[...], preferred_element_type=F32)
    x1 = x_ref[...] + _gate(mix, g1_ref[0])
    x1_ref[...] = x1
    h2 = _modulate(_rms(x1, n2_ref[...]), sc2_ref[0], sh2_ref[0]).astype(BF16)
    h2_ref[...] = h2
    lg_ref[...] = lax.dot_general(rwt_ref[...], h2, (((1,), (1,)), ((), ())),
                                  preferred_element_type=F32) + rb_ref[...]


def _backward_scan_mix(xa, ga, hf, yb, x_all, modtab, meta, conv_w, conv_b, wg, bg, lam, h0, w_out,
                       norm2_g, rwt, rb, n_lat_tiles, tm):
    t, d_lru = xa.shape
    d = x_all.shape[1]
    d_pool = yb.shape[1]
    tile_row, flags, _ = meta
    ngrp = h0.shape[0]
    nr = rwt.shape[0]
    const = dict(pipeline_mode=pl.Buffered(1))
    mset = lambda i, tr, fl: jnp.where(tr[i] < n_lat_tiles, 0, 1)
    row = lambda i, tr, fl: (tr[i], 0)
    mod = lambda col: pl.BlockSpec((1, SUBLANES, d), lambda i, tr, fl: (mset(i, tr, fl), 0, col))
    grid_spec = pltpu.PrefetchScalarGridSpec(
        num_scalar_prefetch=2,
        grid=(t // tm,),
        in_specs=_halo_specs(tm, d_lru, t) + [
            pl.BlockSpec(conv_w.shape, lambda i, tr, fl: (0, 0)),
            pl.BlockSpec((1, d_lru), lambda i, tr, fl: (0, 0)),
            pl.BlockSpec(wg.shape, lambda i, tr, fl: (0, 0, 0), **const),
            pl.BlockSpec(bg.shape, lambda i, tr, fl: (0, 0, 0)),
            pl.BlockSpec((1, d_lru), lambda i, tr, fl: (0, 0)),
            pl.BlockSpec((1, SUBLANES, d_lru), lambda i, tr, fl: (fl[2, i], 0, 0)),
            pl.BlockSpec((tm, d_lru), row),
            pl.BlockSpec((tm, d_lru), row),
            pl.BlockSpec((tm, d_pool), row),
            pl.BlockSpec((tm, d), row),
            mod(2), mod(3), mod(4),
            pl.BlockSpec(w_out.shape, lambda i, tr, fl: (0, 0), **const),
            pl.BlockSpec((1, d), lambda i, tr, fl: (0, 0)),
            pl.BlockSpec(rwt.shape, lambda i, tr, fl: (0, 0)),
            pl.BlockSpec((nr, 1), lambda i, tr, fl: (0, 0))],
        out_specs=[pl.BlockSpec((tm, d), row),
                   pl.BlockSpec((tm, d), row),
                   pl.BlockSpec((nr, tm), lambda i, tr, fl: (0, tr[i])),
                   pl.BlockSpec((1, SUBLANES, d_lru), lambda i, tr, fl: (fl[2, i], 0, 0))],
        scratch_shapes=[pltpu.VMEM((tm + 2 * BF16_ROWS, d_lru), F32),
                        pltpu.VMEM((tm, d_lru), F32),
                        pltpu.VMEM((tm, d_lru), F32),
                        pltpu.VMEM((SUBLANES, d_lru), F32),
                        pltpu.VMEM((tm, d_lru + d_pool), BF16)])
    return pl.pallas_call(
        _bwd_kernel,
        grid_spec=grid_spec,
        out_shape=[jax.ShapeDtypeStruct((t, d), F32),
                   jax.ShapeDtypeStruct((t, d), BF16),
                   jax.ShapeDtypeStruct((nr, t), F32),
                   jax.ShapeDtypeStruct((ngrp, SUBLANES, d_lru), F32)],
        compiler_params=_cparams(("arbitrary",)),
        name="backward_scan_mix",
    )(tile_row, flags, xa, xa, xa, conv_w, conv_b.reshape(1, d_lru), wg, bg, lam.reshape(1, d_lru), h0,
      hf, ga, yb, x_all, modtab, modtab, modtab, w_out, norm2_g.reshape(1, d), rwt, rb)


def _shift_rows(v, k):
    if k == 0:
        return v
    z = jnp.zeros((abs(k) * SUBLANES, v.shape[1]), v.dtype)
    if k > 0:
        return jnp.concatenate([z, v[:-k * SUBLANES]], axis=0)
    return jnp.concatenate([v[-k * SUBLANES:], z], axis=0)


def _run_sum(v, m, direction):
    if m & (m - 1) == 0:
        k = 1
        while k < m:
            v = v + _shift_rows(v, -direction * k)
            k *= 2
        return v
    out = v
    for j in range(1, m):
        out = out + _shift_rows(v, -direction * j)
    return out


def _box_sum(v, w):
    lo = w // 2
    hi = w - 1 - lo
    s = _run_sum(v, hi + 1, +1)
    if lo:
        s = s + _shift_rows(_run_sum(v, lo, -1), 1)
    return s


def _window_count(n_rows, lanes, n_pos, w):
    lo = w // 2
    hi = w - 1 - lo
    p = lax.shift_right_logical(lax.broadcasted_iota(jnp.int32, (n_rows, lanes), 0), 3)
    return (jnp.minimum(p + hi + 1, n_pos) - jnp.maximum(p - lo, 0)).astype(F32)


def _pool_kernel(z_ref, ps_ref, o_ref, v_ref, *, tiles_per_group, grid_rows, grid_cols):
    group = pl.program_id(1) // tiles_per_group
    lanes = z_ref.shape[1]
    ps = ps_ref[...]

    def pool_1d(w):
        z = z_ref[...].astype(F32)
        n = z.shape[0] // SUBLANES
        mean = _box_sum(z, w) / _window_count(z.shape[0], lanes, n, w)
        o_ref[...] = ((mean - z) * ps).astype(o_ref.dtype)

    def pool_2d(w):
        lo = w // 2
        hi = w - 1 - lo
        blk = grid_cols * SUBLANES
        cw = _window_count(blk, lanes, grid_cols, w)

        def zrow(r):
            return z_ref[pl.ds(pl.multiple_of(r * blk, blk), blk), :].astype(F32)

        v = jnp.zeros((blk, lanes), F32)
        for r in range(hi):
            v = v + zrow(r)
        v_ref[...] = v

        def body(r, carry):
            add = r + hi
            sub = r - lo - 1
            v = v_ref[...]
            v = v + jnp.where(add < grid_rows, zrow(jnp.minimum(add, grid_rows - 1)), 0.0)
            v = v - jnp.where(sub >= 0, zrow(jnp.maximum(sub, 0)), 0.0)
            v_ref[...] = v
            ch = (jnp.minimum(r + hi + 1, grid_rows) - jnp.maximum(r - lo, 0)).astype(F32)
            mean = _box_sum(v, w) / (ch * cw)
            o_ref[pl.ds(pl.multiple_of(r * blk, blk), blk), :] = ((mean - zrow(r)) * ps).astype(o_ref.dtype)
            return carry

        lax.fori_loop(0, grid_rows, body, 0)

    for g, w in enumerate(POOL_WINDOWS):
        @pl.when(group == g)
        def _(w=w):
            if grid_rows is None:
                pool_1d(w)
            else:
                pool_2d(w)


def _pool(z, prev_out, pool_scale, rows_per_group, n_groups, first_block, grid_rows):
    t, d_pool = z.shape
    gw = d_pool // len(POOL_WINDOWS)
    lanes = LANES
    blk = GRID_W * SUBLANES
    kern = functools.partial(_pool_kernel, tiles_per_group=gw // lanes, grid_rows=grid_rows, grid_cols=GRID_W)
    in_specs = [pl.BlockSpec((rows_per_group, lanes), lambda g, j: (first_block + g, j)),
                pl.BlockSpec((1, lanes), lambda g, j: (0, j))]
    args = [z, pool_scale.reshape(1, d_pool)]
    aliases = {}
    if prev_out is not None:
        in_specs.append(pl.BlockSpec(memory_space=pl.ANY))
        args.append(prev_out)
        aliases = {2: 0}
        kern_fn = lambda z_ref, ps_ref, prev_ref, o_ref, v_ref: kern(z_ref, ps_ref, o_ref, v_ref)
    else:
        kern_fn = kern
    return pl.pallas_call(
        kern_fn,
        grid=(n_groups, d_pool // lanes),
        in_specs=in_specs,
        out_specs=pl.BlockSpec((rows_per_group, lanes), lambda g, j: (first_block + g, j)),
        out_shape=jax.ShapeDtypeStruct((t, d_pool), BF16),
        scratch_shapes=[pltpu.VMEM((blk, lanes), F32)],
        input_output_aliases=aliases,
        compiler_params=_cparams(("arbitrary", "arbitrary")),
        name="pool_grid" if grid_rows is not None else "pool_seq",
    )(*args)


def _route_kernel(lg_ref, oi_ref, of_ref, cnt_ref, tri_ref, carry_ref, *, n_groups, per_group):
    step = pl.program_id(0)
    tt = lg_ref.shape[1]
    n_exp = n_groups * per_group

    @pl.when(step == 0)
    def _():
        r = lax.broadcasted_iota(jnp.int32, tri_ref.shape, 0)
        c = lax.broadcasted_iota(jnp.int32, tri_ref.shape, 1)
        tri_ref[...] = (r <= c).astype(BF16)
        carry_ref[...] = jnp.zeros_like(carry_ref)

    row = lambda k: lg_ref[k:k + 1, :]
    cmax = row(0)
    gi = jnp.zeros((1, tt), jnp.int32)
    for g in range(1, n_groups):
        better = row(g) > cmax
        gi = jnp.where(better, g, gi)
        cmax = jnp.where(better, row(g), cmax)
    denom = jnp.zeros((1, tt), F32)
    for g in range(n_groups):
        denom = denom + jnp.exp(row(g) - cmax)
    pg = 1.0 / denom
    fine = []
    for j in range(per_group):
        f = row(n_groups + j)
        for g in range(1, n_groups):
            f = jnp.where(gi == g, row(n_groups + g * per_group + j), f)
        fine.append(f)
    v1 = fine[0]
    i1 = jnp.zeros((1, tt), jnp.int32)
    for j in range(1, per_group):
        better = fine[j] > v1
        i1 = jnp.where(better, j, i1)
        v1 = jnp.where(better, fine[j], v1)
    v2 = jnp.full((1, tt), -jnp.inf, F32)
    i2 = jnp.zeros((1, tt), jnp.int32)
    for j in range(per_group):
        better = jnp.logical_and(i1 != j, fine[j] > v2)
        i2 = jnp.where(better, j, i2)
        v2 = jnp.where(better, fine[j], v2)
    ex = jnp.exp(v2 - v1)
    w1 = (1.0 / (1.0 + ex)) * pg
    w2 = (ex / (1.0 + ex)) * pg
    e1 = gi * per_group + i1
    e2 = gi * per_group + i2
    eid = lax.broadcasted_iota(jnp.int32, (n_exp, tt), 0)
    hit1 = eid == e1
    hit2 = eid == e2
    member = jnp.logical_or(hit1, hit2).astype(BF16)
    cb = tri_ref.shape[0]
    carry = carry_ref[...]
    cums = []
    for s in range(tt // cb):
        c = jnp.dot(member[:, s * cb:(s + 1) * cb], tri_ref[...], preferred_element_type=F32) + carry
        carry = c[:, cb - 1:cb]
        cums.append(c)
    carry_ref[...] = carry
    cum = jnp.concatenate(cums, axis=1)
    rank1 = jnp.sum(jnp.where(hit1, cum, 0.0), axis=0, keepdims=True) - 1.0
    rank2 = jnp.sum(jnp.where(hit2, cum, 0.0), axis=0, keepdims=True) - 1.0
    zi = jnp.zeros((SUBLANES - 4, tt), jnp.int32)
    oi_ref[...] = jnp.concatenate([e1, e2, rank1.astype(jnp.int32), rank2.astype(jnp.int32), zi], axis=0)
    of_ref[...] = jnp.concatenate([w1, w2, jnp.zeros((SUBLANES - 2, tt), F32)], axis=0)
    cnt_ref[...] = jnp.broadcast_to(carry, cnt_ref.shape)


def _route(logits_t, n_groups, per_group):
    nr, t = logits_t.shape
    cb = min(CUMSUM_BLOCK, t)
    tt = max(k for k in range(cb, min(ROUTE_TILE, t) + 1, cb) if t % k == 0)
    n_exp = n_groups * per_group
    return pl.pallas_call(
        functools.partial(_route_kernel, n_groups=n_groups, per_group=per_group),
        grid=(t // tt,),
        in_specs=[pl.BlockSpec((nr, tt), lambda i: (0, i))],
        out_specs=[pl.BlockSpec((SUBLANES, tt), lambda i: (0, i)),
                   pl.BlockSpec((SUBLANES, tt), lambda i: (0, i)),
                   pl.BlockSpec((n_exp, LANES), lambda i: (0, 0))],
        out_shape=[jax.ShapeDtypeStruct((SUBLANES, t), jnp.int32),
                   jax.ShapeDtypeStruct((SUBLANES, t), F32),
                   jax.ShapeDtypeStruct((n_exp, LANES), F32)],
        scratch_shapes=[pltpu.VMEM((cb, cb), BF16), pltpu.VMEM((n_exp, 1), F32)],
        compiler_params=_cparams(("arbitrary",)),
        name="route",
    )(logits_t)


def _expert_kernel(te_ref, nv_ref, x_ref, w1_ref, w3_ref, w2_ref, y_ref):
    @pl.when(pl.program_id(0) < nv_ref[0])
    def _():
        x = x_ref[...]
        h1 = jnp.dot(x, w1_ref[0], preferred_element_type=F32)
        h3 = jnp.dot(x, w3_ref[0], preferred_element_type=F32)
        hh = (h1 * jax.nn.sigmoid(h1) * h3).astype(BF16)
        y_ref[...] = jnp.dot(hh, w2_ref[0], preferred_element_type=F32).astype(y_ref.dtype)


def _expert_mlp(xs, tile_expert, n_valid, w1, w3, w2, tmx):
    p, d = xs.shape
    de = w1.shape[2]
    tile = lambda i, te, nv: (jnp.minimum(i, nv[0] - 1), 0)
    grid_spec = pltpu.PrefetchScalarGridSpec(
        num_scalar_prefetch=2,
        grid=(p // tmx,),
        in_specs=[pl.BlockSpec((tmx, d), tile),
                  pl.BlockSpec((1, d, de), lambda i, te, nv: (te[i], 0, 0)),
                  pl.BlockSpec((1, d, de), lambda i, te, nv: (te[i], 0, 0)),
                  pl.BlockSpec((1, de, d), lambda i, te, nv: (te[i], 0, 0))],
        out_specs=pl.BlockSpec((tmx, d), tile))
    return pl.pallas_call(
        _expert_kernel,
        grid_spec=grid_spec,
        out_shape=jax.ShapeDtypeStruct((p, d), BF16),
        compiler_params=_cparams(("arbitrary",)),
        name="expert_mlp",
    )(tile_expert, n_valid, xs, w1, w3, w2)


def _combine_kernel(x1_ref, y1_ref, y2_ref, wt_ref, g2_ref, fg_ref, o_ref):
    wt = wt_ref[...]
    moe = wt[:, 0:1] * y1_ref[...].astype(F32) + wt[:, 1:2] * y2_ref[...].astype(F32)
    x = x1_ref[...] + _gate(moe, g2_ref[0])
    o_ref[...] = _rms(x, fg_ref[...])


def _combine(x1, y1, y2, wt, modtab, final_g, n_lat_tiles, tm):
    t, d = x1.shape
    mset = lambda i: jnp.where(i < n_lat_tiles, 0, 1)
    row = lambda i: (i, 0)
    return pl.pallas_call(
        _combine_kernel,
        grid=(t // tm,),
        in_specs=[pl.BlockSpec((tm, d), row),
                  pl.BlockSpec((tm, d), row),
                  pl.BlockSpec((tm, d), row),
                  pl.BlockSpec((tm, LANES), row),
                  pl.BlockSpec((1, SUBLANES, d), lambda i: (mset(i), 0, 5)),
                  pl.BlockSpec((1, d), lambda i: (0, 0))],
        out_specs=pl.BlockSpec((tm, d), row),
        out_shape=jax.ShapeDtypeStruct((t, d), F32),
        compiler_params=_cparams(("arbitrary",)),
        name="combine",
    )(x1, y1, y2, wt, modtab, final_g.reshape(1, d))


def _to_rows(x):
    b, n, d = x.shape
    return x.reshape(b // SUBLANES, SUBLANES, n, d).transpose(0, 2, 1, 3).reshape(b * n, d)


def _from_rows(y, b, n):
    d = y.shape[1]
    return y.reshape(b // SUBLANES, n, SUBLANES, d).transpose(0, 2, 1, 3).reshape(b, n, d)


def _tile_meta(groups, tm):
    rows, first, last, grp = [], [], [], []
    blk = 0
    for g, r in enumerate(groups):
        nc = r // tm
        for c in range(nc):
            rows.append(blk + c)
            first.append(int(c == 0))
            last.append(int(c == nc - 1))
            grp.append(g)
        blk += nc
    fwd = (np.array(rows, np.int32), np.array([first, last, grp], np.int32))
    order = []
    blk = 0
    for r in groups:
        nc = r // tm
        order.extend(range(blk + nc - 1, blk - 1, -1))
        blk += nc
    order = np.array(order)
    bwd = (fwd[0][order], fwd[1][:, order])
    return fwd, bwd


def _layer(x_all, cond_mod, h0, lp, final_g, dims):
    (norm1_g, w_in, conv_w, conv_b, lru_wa, lru_ba, lru_wx, lru_bx, lru_lambda, pool_w, pool_scale,
     w_out, norm2_g, rcw, rcb, rfw, rfb, w1, w3, w2) = lp
    n_lat_groups, lat_rows, n_ctx_groups, ctx_rows = dims
    t, d = x_all.shape
    d_lru = lru_lambda.shape[-1]
    heads, bw = lru_wa.shape[1], lru_wa.shape[2]
    n_groups, per_group = rfw.shape[1], rfw.shape[2]
    n_exp = n_groups * per_group
    tm = min(TOKEN_TILE, ctx_rows, lat_rows)
    n_lat_tiles = n_lat_groups * lat_rows // tm
    groups = [lat_rows] * n_lat_groups + [ctx_rows] * n_ctx_groups
    (f_rows, f_flags), (b_rows, b_flags) = _tile_meta(groups, tm)

    xa, ga, z = _input_projection(x_all, cond_mod, norm1_g, w_in.astype(BF16), pool_w.astype(BF16),
                                  n_lat_tiles, tm)

    def gate_weights(direction):
        wg = jnp.concatenate([lru_wa[direction], lru_wx[direction]], axis=-1).astype(BF16)
        bg = jnp.concatenate([lru_ba[direction].reshape(heads, 1, bw),
                              lru_bx[direction].reshape(heads, 1, bw)], axis=-1)
        return wg, bg

    log_decay = jax.nn.log_sigmoid(lru_lambda)
    wg_f, bg_f = gate_weights(0)
    wg_b, bg_b = gate_weights(1)
    hf, hf_last = _forward_scan(xa, (jnp.asarray(f_rows), jnp.asarray(f_flags), None), conv_w, conv_b,
                                wg_f, bg_f, log_decay[0], h0[:, 0], tm)

    yb = _pool(z, None, pool_scale, lat_rows, n_lat_groups, 0, lat_rows // (GRID_W * SUBLANES))
    yb = _pool(z, yb, pool_scale, ctx_rows, n_ctx_groups, n_lat_groups * lat_rows // ctx_rows, None)

    nr = LANES
    rw = jnp.concatenate([rcw, rfw.reshape(d, n_exp)], axis=1)
    rwt = jnp.zeros((nr, d), BF16).at[:n_groups + n_exp].set(rw.T.astype(BF16))
    rb = jnp.zeros((nr, 1), F32).at[:n_groups + n_exp, 0].set(jnp.concatenate([rcb, rfb.reshape(n_exp)]))
    x1, h2, logits_t, hb_last = _backward_scan_mix(
        xa, ga, hf, yb, x_all, cond_mod, (jnp.asarray(b_rows), jnp.asarray(b_flags), None), conv_w, conv_b,
        wg_b, bg_b, log_decay[1], h0[:, 1], w_out.astype(BF16), norm2_g, rwt, rb, n_lat_tiles, tm)

    ri, rf, counts = _route(logits_t, n_groups, per_group)
    e1, e2, rank1, rank2 = ri[0], ri[1], ri[2], ri[3]
    tmx = EXPERT_TILE
    cnt = counts[:, 0].astype(jnp.int32)
    padded = ((cnt + tmx - 1) // tmx) * tmx
    ends = jnp.cumsum(padded)
    offsets = ends - padded
    pos1 = offsets[e1] + rank1
    pos2 = offsets[e2] + rank2
    p_max = TOP_K_FINE * t + n_exp * tmx
    n_tiles = p_max // tmx
    tile_expert = jnp.minimum(jnp.searchsorted(ends // tmx, jnp.arange(n_tiles, dtype=jnp.int32), side="right"),
                              n_exp - 1).astype(jnp.int32)
    n_valid = (ends[-1] // tmx).astype(jnp.int32).reshape(1)
    tok = jnp.arange(t, dtype=jnp.int32)
    tok_of = jnp.zeros((p_max,), jnp.int32).at[pos1].set(tok).at[pos2].set(tok)
    xs = jnp.take(h2, tok_of, axis=0)
    ys = _expert_mlp(xs, tile_expert, n_valid, w1.astype(BF16), w3.astype(BF16), w2.astype(BF16), tmx)
    y1 = jnp.take(ys, pos1, axis=0)
    y2 = jnp.take(ys, pos2, axis=0)
    wt = jnp.zeros((t, LANES), F32).at[:, 0].set(rf[0]).at[:, 1].set(rf[1])
    out = _combine(x1, y1, y2, wt, cond_mod, final_g, n_lat_tiles, tm)
    return out, hf_last, hb_last


def kernel(x_prompt, x_sample, state_lru, c, c_ctx, w_mod, b_mod, norm1_g, w_in, conv_w, conv_b, lru_wa, lru_ba, lru_wx, lru_bx, lru_lambda, pool_w, pool_scale, w_out, norm2_g, router_coarse_w, router_coarse_b, router_fine_w, router_fine_b, exp_w1, exp_w3, exp_w2, final_norm_g):
    bp, sp, d = x_prompt.shape
    bs, ss, _ = x_sample.shape
    d_lru = lru_lambda.shape[-1]
    assert w_mod.shape[0] == 1 and bs == SUBLANES and bp % SUBLANES == 0 and ss % GRID_W == 0
    n_lat_groups, n_ctx_groups = bs // SUBLANES, bp // SUBLANES
    lat_rows, ctx_rows = ss * SUBLANES, sp * SUBLANES
    assert (n_lat_groups * lat_rows) % ctx_rows == 0
    dims = (n_lat_groups, lat_rows, n_ctx_groups, ctx_rows)

    x_all = jnp.concatenate([_to_rows(x_sample), _to_rows(x_prompt)], axis=0)
    n_lat = bs * ss
    cond = jnp.zeros((2 * SUBLANES, d), F32).at[:bs].set(c).at[bs].set(c_ctx)
    mod = _modulation(cond, w_mod[0], b_mod[0])
    cond_mod = jnp.stack([mod[:SUBLANES], jnp.broadcast_to(mod[SUBLANES], (SUBLANES, mod.shape[1]))])
    h0_lat = state_lru[:, 0].reshape(n_lat_groups, SUBLANES, 2, d_lru)
    h0 = jnp.concatenate([h0_lat, jnp.zeros((n_ctx_groups, SUBLANES, 2, d_lru), F32)], axis=0)
    h0 = h0.transpose(0, 2, 1, 3)
    lp = (norm1_g[0], w_in[0], conv_w[0], conv_b[0], lru_wa[0], lru_ba[0], lru_wx[0], lru_bx[0],
          lru_lambda[0], pool_w[0], pool_scale[0], w_out[0], norm2_g[0], router_coarse_w[0],
          router_coarse_b[0], router_fine_w[0], router_fine_b[0], exp_w1[0], exp_w3[0], exp_w2[0])
    out, hf_last, hb_last = _layer(x_all, cond_mod, h0, lp, final_norm_g, dims)
    st = jnp.stack([hf_last[n_lat_groups:], hb_last[n_lat_groups:]], axis=2)
    y_sample = _from_rows(out[:n_lat], bs, ss)
    y_prompt = _from_rows(out[n_lat:], bp, sp)
    state_new = st.reshape(bp, 1, 2, d_lru).astype(x_prompt.dtype)
    return (y_prompt, y_sample, state_new)
```

```python
import functools
from typing import NamedTuple

import numpy as np
import jax
import jax.numpy as jnp
from jax import lax
from jax.experimental import pallas as pl
from jax.experimental.pallas import tpu as pltpu

GRID_W = 64
CONV_W = 4
RG_C = 8.0
POOL_WINDOWS = (2, 4, 8, 16)
TOP_K_FINE = 2
EPS = 1e-6
EXPM1_SERIES_BELOW = 0.25

SUBLANES = 8
LANES = 128
BF16_ROWS = 16
TOKEN_TILE = 512
EXPERT_TILE = 512
DISPATCH_TILE = 1024
ROUTE_TILE = 2048
CUMSUM_BLOCK = 512
MOD_COL_TILE = 1024
WAIT_UNROLL = 32
VMEM_LIMIT = 60 * 1024 * 1024

F32 = jnp.float32
BF16 = jnp.bfloat16
U32 = jnp.uint32
I32 = jnp.int32
HIGH_HALF = 0xFFFF0000


class _Geom(NamedTuple):
    tm: int
    n_lat_tiles: int
    lat_chunks: int
    ctx_chunks: int


def _cparams(sem):
    return pltpu.CompilerParams(dimension_semantics=sem, vmem_limit_bytes=VMEM_LIMIT)


def _per_sequence(y, m):
    rows, d = y.shape
    return y.reshape(rows // SUBLANES, SUBLANES, d), m[None]


def _modulate(y, scale, shift):
    y3, sc = _per_sequence(y, scale)
    _, sh = _per_sequence(y, shift)
    return (y3 * (1.0 + sc) + sh).reshape(y.shape)


def _gate(y, g):
    y3, g3 = _per_sequence(y, g)
    return (y3 * g3).reshape(y.shape)


def _rms(x, g):
    ms = jnp.mean(x * x, axis=-1, keepdims=True)
    return x * lax.rsqrt(ms + EPS) * g


def _pack_bf16_pair(lo, hi):
    lo_bits = lax.shift_right_logical(lax.bitcast_convert_type(lo.astype(BF16).astype(F32), U32), U32(16))
    hi_bits = lax.bitcast_convert_type(hi.astype(BF16).astype(F32), U32) & U32(HIGH_HALF)
    return lo_bits | hi_bits


def _unpack_bf16_pair(w):
    lo = lax.bitcast_convert_type(lax.shift_left(w, U32(16)), F32)
    hi = lax.bitcast_convert_type(w & U32(HIGH_HALF), F32)
    return lo, hi


def _wait_many(copy, n):
    def body(_, c):
        for _ in range(WAIT_UNROLL):
            copy.wait()
        return c
    lax.fori_loop(0, n // WAIT_UNROLL, body, 0)


def _tile_copies(lat_hbm, ctx_hbm, buf, sem, tile, geom, to_rows, fn):
    tt = geom.tm // SUBLANES

    def run(hbm, k, chunks):
        g = lax.div(k, jnp.int32(chunks))
        c = lax.rem(k, jnp.int32(chunks))
        for b in range(SUBLANES):
            h = hbm.at[g * SUBLANES + b, pl.ds(c * tt, tt), :]
            v = buf.at[:, b, :]
            fn(pltpu.make_async_copy(h, v, sem) if to_rows else pltpu.make_async_copy(v, h, sem))

    @pl.when(tile < geom.n_lat_tiles)
    def _():
        run(lat_hbm, tile, geom.lat_chunks)

    @pl.when(tile >= geom.n_lat_tiles)
    def _():
        run(ctx_hbm, tile - geom.n_lat_tiles, geom.ctx_chunks)


def _fetch_rows(lat_hbm, ctx_hbm, xbuf, sem, step, n_steps, tile_of, geom):
    slot = lax.rem(step, 2)
    start = lambda cp: cp.start()
    wait = lambda cp: cp.wait()

    @pl.when(step == 0)
    def _():
        _tile_copies(lat_hbm, ctx_hbm, xbuf.at[0], sem.at[0], tile_of(jnp.int32(0)), geom, True, start)

    @pl.when(step + 1 < n_steps)
    def _():
        nxt = 1 - slot
        _tile_copies(lat_hbm, ctx_hbm, xbuf.at[nxt], sem.at[nxt], tile_of(step + 1), geom, True, start)

    _tile_copies(lat_hbm, ctx_hbm, xbuf.at[slot], sem.at[slot], tile_of(step), geom, True, wait)
    tt, _, d = xbuf.shape[1:]
    return xbuf[slot].reshape(tt * SUBLANES, d)


def _mod_kernel(c_ref, w_ref, b_ref, o_ref):
    c = c_ref[...]
    s = c * jax.nn.sigmoid(c)
    o_ref[...] = jnp.dot(s.astype(BF16), w_ref[...].astype(BF16),
                         preferred_element_type=F32) + b_ref[...]


def _modulation(cond, w_mod, b_mod):
    rows, d = cond.shape
    n = w_mod.shape[1]
    tn = min(MOD_COL_TILE, n)
    return pl.pallas_call(
        _mod_kernel,
        grid=(n // tn,),
        in_specs=[pl.BlockSpec((rows, d), lambda j: (0, 0)),
                  pl.BlockSpec((d, tn), lambda j: (0, j)),
                  pl.BlockSpec((1, tn), lambda j: (0, j))],
        out_specs=pl.BlockSpec((rows, tn), lambda j: (0, j)),
        out_shape=jax.ShapeDtypeStruct((rows, n), F32),
        compiler_params=_cparams(("arbitrary",)),
        name="modulation",
    )(cond, w_mod, b_mod.reshape(1, n))


def _proj_kernel(lat_hbm, ctx_hbm, sh_ref, sc_ref, g_ref, win_ref, pw_ref, xa_ref, ga_ref, z_ref,
                 xbuf, sem, *, d_lru, gw, geom):
    step = pl.program_id(0)
    x = _fetch_rows(lat_hbm, ctx_hbm, xbuf, sem, step, pl.num_programs(0), lambda s: s, geom)
    h = _modulate(_rms(x, g_ref[...]), sc_ref[0], sh_ref[0])
    proj = jnp.dot(h.astype(BF16), win_ref[...], preferred_element_type=F32)
    xa_ref[...] = proj[:, :d_lru].astype(BF16)
    ga_ref[...] = proj[:, d_lru:2 * d_lru].astype(BF16)
    for g in range(pw_ref.shape[0]):
        lo = 2 * d_lru + g * gw
        z_ref[:, g * gw:(g + 1) * gw] = jnp.dot(
            proj[:, lo:lo + gw].astype(BF16), pw_ref[g], preferred_element_type=F32).astype(BF16)


def _input_projection(x_lat, x_ctx, modtab, norm_g, w_in, pool_w, n_rows, geom):
    d = x_lat.shape[-1]
    tm = geom.tm
    d_pool = pool_w.shape[0] * pool_w.shape[1]
    d_lru = (w_in.shape[1] - d_pool) // 2
    mset = lambda i: jnp.where(i < geom.n_lat_tiles, 0, 1)
    const = dict(pipeline_mode=pl.Buffered(1))
    return pl.pallas_call(
        functools.partial(_proj_kernel, d_lru=d_lru, gw=pool_w.shape[1], geom=geom),
        grid=(n_rows // tm,),
        in_specs=[pl.BlockSpec(memory_space=pl.ANY),
                  pl.BlockSpec(memory_space=pl.ANY),
                  pl.BlockSpec((1, SUBLANES, d), lambda i: (mset(i), 0, 0)),
                  pl.BlockSpec((1, SUBLANES, d), lambda i: (mset(i), 0, 1)),
                  pl.BlockSpec((1, d), lambda i: (0, 0)),
                  pl.BlockSpec(w_in.shape, lambda i: (0, 0), **const),
                  pl.BlockSpec(pool_w.shape, lambda i: (0, 0, 0), **const)],
        out_specs=[pl.BlockSpec((tm, d_lru), lambda i: (i, 0)),
                   pl.BlockSpec((tm, d_lru), lambda i: (i, 0)),
                   pl.BlockSpec((tm, d_pool), lambda i: (i, 0))],
        out_shape=[jax.ShapeDtypeStruct((n_rows, d_lru), BF16),
                   jax.ShapeDtypeStruct((n_rows, d_lru), BF16),
                   jax.ShapeDtypeStruct((n_rows, d_pool), BF16)],
        scratch_shapes=[pltpu.VMEM((2, tm // SUBLANES, SUBLANES, d), F32),
                        pltpu.SemaphoreType.DMA((2,))],
        compiler_params=_cparams(("arbitrary",)),
        name="input_projection",
    )(x_lat, x_ctx, modtab, modtab, norm_g.reshape(1, d), w_in, pool_w)


def _fill_ext(ext_ref, prev_ref, main_ref, next_ref, first, last):
    tm = main_ref.shape[0]
    prev = prev_ref[...].astype(F32)
    nxt = next_ref[...].astype(F32)
    ext_ref[0:BF16_ROWS, :] = jnp.where(first, 0.0, prev)
    ext_ref[BF16_ROWS:BF16_ROWS + tm, :] = main_ref[...].astype(F32)
    ext_ref[BF16_ROWS + tm:, :] = jnp.where(last, 0.0, nxt)


def _one_minus_exp(y, exp_y):
    p = 1.0 / 5040.0
    for c in (1.0 / 720.0, 1.0 / 120.0, 1.0 / 24.0, 1.0 / 6.0, 0.5, 1.0):
        p = p * y + c
    return jnp.where(y > -EXPM1_SERIES_BELOW, -y * p, 1.0 - exp_y)


def _decay_and_input(ext_ref, cw_ref, cb_ref, wg_ref, bg_ref, lam_ref, a_ref, u_ref, tm):
    heads, bw = wg_ref.shape[0], wg_ref.shape[1]
    for hd in range(heads):
        sl = slice(hd * bw, (hd + 1) * bw)
        xc = cb_ref[:, sl]
        for k in range(CONV_W):
            xc = xc + cw_ref[k:k + 1, sl] * ext_ref[SUBLANES * k:SUBLANES * k + tm, sl]
        g = jnp.dot(xc.astype(BF16), wg_ref[hd], preferred_element_type=F32) + bg_ref[hd]
        r = jax.nn.sigmoid(g[:, :bw])
        ig = jax.nn.sigmoid(g[:, bw:])
        log_a = (RG_C * r) * lam_ref[:, sl]
        a = jnp.exp(log_a)
        a_ref[:, sl] = a
        u_ref[:, sl] = jnp.sqrt(_one_minus_exp(2.0 * log_a, a * a)) * (ig * xc)


def _scan(a_ref, u_ref, h, tm, reverse):
    nblk = tm // SUBLANES

    def body(s, h):
        j = (nblk - 1 - s) if reverse else s
        rows = pl.ds(pl.multiple_of(j * SUBLANES, SUBLANES), SUBLANES)
        h = a_ref[rows, :] * h + u_ref[rows, :]
        u_ref[rows, :] = h
        return h

    return lax.fori_loop(0, nblk, body, h, unroll=8)


def _halo_specs(tm, d_lru, n_rows):
    per = tm // BF16_ROWS
    last_blk = n_rows // BF16_ROWS - 1
    return [pl.BlockSpec((BF16_ROWS, d_lru), lambda i, tr, fl: (jnp.maximum(tr[i] * per - 1, 0), 0)),
            pl.BlockSpec((tm, d_lru), lambda i, tr, fl: (tr[i], 0)),
            pl.BlockSpec((BF16_ROWS, d_lru), lambda i, tr, fl: (jnp.minimum((tr[i] + 1) * per, last_blk), 0))]


def _fwd_kernel(tr_ref, fl_ref, prev_ref, main_ref, next_ref, cw_ref, cb_ref, wg_ref, bg_ref, lam_ref,
                h0_ref, hf_ref, hlast_ref, ext_ref, a_ref, u_ref, h_ref):
    i = pl.program_id(0)
    tm = main_ref.shape[0]
    first = fl_ref[0, i] == 1
    last = fl_ref[1, i] == 1
    _fill_ext(ext_ref, prev_ref, main_ref, next_ref, first, last)
    _decay_and_input(ext_ref, cw_ref, cb_ref, wg_ref, bg_ref, lam_ref, a_ref, u_ref, tm)

    @pl.when(first)
    def _():
        h_ref[...] = h0_ref[0]

    h = _scan(a_ref, u_ref, h_ref[...], tm, reverse=False)
    h_ref[...] = h
    hlast_ref[0] = h
    hf_ref[...] = u_ref[...].astype(BF16)


def _forward_scan(xa, meta, conv_w, conv_b, wg, bg, lam, h0, tm):
    t, d_lru = xa.shape
    tile_row, flags = meta
    ngrp = h0.shape[0]
    const = dict(pipeline_mode=pl.Buffered(1))
    grid_spec = pltpu.PrefetchScalarGridSpec(
        num_scalar_prefetch=2,
        grid=(t // tm,),
        in_specs=_halo_specs(tm, d_lru, t) + [
            pl.BlockSpec(conv_w.shape, lambda i, tr, fl: (0, 0)),
            pl.BlockSpec((1, d_lru), lambda i, tr, fl: (0, 0)),
            pl.BlockSpec(wg.shape, lambda i, tr, fl: (0, 0, 0), **const),
            pl.BlockSpec(bg.shape, lambda i, tr, fl: (0, 0, 0)),
            pl.BlockSpec((1, d_lru), lambda i, tr, fl: (0, 0)),
            pl.BlockSpec((1, SUBLANES, d_lru), lambda i, tr, fl: (fl[2, i], 0, 0))],
        out_specs=[pl.BlockSpec((tm, d_lru), lambda i, tr, fl: (tr[i], 0)),
                   pl.BlockSpec((1, SUBLANES, d_lru), lambda i, tr, fl: (fl[2, i], 0, 0))],
        scratch_shapes=[pltpu.VMEM((tm + 2 * BF16_ROWS, d_lru), F32),
                        pltpu.VMEM((tm, d_lru), F32),
                        pltpu.VMEM((tm, d_lru), F32),
                        pltpu.VMEM((SUBLANES, d_lru), F32)])
    return pl.pallas_call(
        _fwd_kernel,
        grid_spec=grid_spec,
        out_shape=[jax.ShapeDtypeStruct((t, d_lru), BF16),
                   jax.ShapeDtypeStruct((ngrp, SUBLANES, d_lru), F32)],
        compiler_params=_cparams(("arbitrary",)),
        name="forward_scan",
    )(tile_row, flags, xa, xa, xa, conv_w, conv_b.reshape(1, d_lru), wg, bg, lam.reshape(1, d_lru), h0)


def _bwd_kernel(tr_ref, fl_ref, prev_ref, main_ref, next_ref, cw_ref, cb_ref, wg_ref, bg_ref, lam_ref,
                h0_ref, hf_ref, ga_ref, yb_ref, lat_hbm, ctx_hbm, g1_ref, sh2_ref, sc2_ref, wout_ref,
                n2_ref, rwt_ref, rb_ref,
                x1_ref, h2_ref, lg_ref, hlast_ref,
                ext_ref, a_ref, u_ref, h_ref, cat_ref, xbuf, xsem, *, geom):
    i = pl.program_id(0)
    n_steps = pl.num_programs(0)
    tm, d_lru = main_ref.shape
    x = _fetch_rows(lat_hbm, ctx_hbm, xbuf, xsem, i, n_steps,
                    lambda s: tr_ref[jnp.minimum(s, n_steps - 1)], geom)
    first = fl_ref[0, i] == 1
    last = fl_ref[1, i] == 1
    _fill_ext(ext_ref, prev_ref, main_ref, next_ref, first, last)
    _decay_and_input(ext_ref, cw_ref, cb_ref, wg_ref, bg_ref, lam_ref, a_ref, u_ref, tm)

    @pl.when(last)
    def _():
        h_ref[...] = h0_ref[0]

    h = _scan(a_ref, u_ref, h_ref[...], tm, reverse=True)
    h_ref[...] = h
    hlast_ref[0] = h

    ga = ga_ref[...].astype(F32)
    ya = (hf_ref[...].astype(F32) + u_ref[...]) * jax.nn.gelu(ga)
    cat_ref[:, :d_lru] = ya.astype(BF16)
    cat_ref[:, d_lru:] = yb_ref[...]
    mix = jnp.dot(cat_ref[...], wout_ref[...], preferred_element_type=F32)
    x1 = x + _gate(mix, g1_ref[0])
    x1_ref[...] = x1
    h2 = _modulate(_rms(x1, n2_ref[...]), sc2_ref[0], sh2_ref[0])
    half = h2.shape[1] // 2
    h2_ref[...] = _pack_bf16_pair(h2[:, :half], h2[:, half:])
    lg_ref[...] = lax.dot_general(rwt_ref[...], h2.astype(BF16), (((1,), (1,)), ((), ())),
                                  preferred_element_type=F32) + rb_ref[...]


def _backward_scan_mix(xa, ga, hf, yb, x_lat, x_ctx, modtab, meta, conv_w, conv_b, wg, bg, lam, h0, w_out,
                       norm2_g, rwt, rb, geom):
    t, d_lru = xa.shape
    d = x_lat.shape[-1]
    d_pool = yb.shape[1]
    tm = geom.tm
    tile_row, flags = meta
    ngrp = h0.shape[0]
    nr = rwt.shape[0]
    const = dict(pipeline_mode=pl.Buffered(1))
    mset = lambda i, tr, fl: jnp.where(tr[i] < geom.n_lat_tiles, 0, 1)
    row = lambda i, tr, fl: (tr[i], 0)
    mod = lambda col: pl.BlockSpec((1, SUBLANES, d), lambda i, tr, fl: (mset(i, tr, fl), 0, col))
    grid_spec = pltpu.PrefetchScalarGridSpec(
        num_scalar_prefetch=2,
        grid=(t // tm,),
        in_specs=_halo_specs(tm, d_lru, t) + [
            pl.BlockSpec(conv_w.shape, lambda i, tr, fl: (0, 0)),
            pl.BlockSpec((1, d_lru), lambda i, tr, fl: (0, 0)),
            pl.BlockSpec(wg.shape, lambda i, tr, fl: (0, 0, 0), **const),
            pl.BlockSpec(bg.shape, lambda i, tr, fl: (0, 0, 0)),
            pl.BlockSpec((1, d_lru), lambda i, tr, fl: (0, 0)),
            pl.BlockSpec((1, SUBLANES, d_lru), lambda i, tr, fl: (fl[2, i], 0, 0)),
            pl.BlockSpec((tm, d_lru), row),
            pl.BlockSpec((tm, d_lru), row),
            pl.BlockSpec((tm, d_pool), row),
            pl.BlockSpec(memory_space=pl.ANY),
            pl.BlockSpec(memory_space=pl.ANY),
            mod(2), mod(3), mod(4),
            pl.BlockSpec(w_out.shape, lambda i, tr, fl: (0, 0), **const),
            pl.BlockSpec((1, d), lambda i, tr, fl: (0, 0)),
            pl.BlockSpec(rwt.shape, lambda i, tr, fl: (0, 0)),
            pl.BlockSpec((nr, 1), lambda i, tr, fl: (0, 0))],
        out_specs=[pl.BlockSpec((tm, d), row),
                   pl.BlockSpec((tm, d // 2), row),
                   pl.BlockSpec((nr, tm), lambda i, tr, fl: (0, tr[i])),
                   pl.BlockSpec((1, SUBLANES, d_lru), lambda i, tr, fl: (fl[2, i], 0, 0))],
        scratch_shapes=[pltpu.VMEM((tm + 2 * BF16_ROWS, d_lru), F32),
                        pltpu.VMEM((tm, d_lru), F32),
                        pltpu.VMEM((tm, d_lru), F32),
                        pltpu.VMEM((SUBLANES, d_lru), F32),
                        pltpu.VMEM((tm, d_lru + d_pool), BF16),
                        pltpu.VMEM((2, tm // SUBLANES, SUBLANES, d), F32),
                        pltpu.SemaphoreType.DMA((2,))])
    return pl.pallas_call(
        functools.partial(_bwd_kernel, geom=geom),
        grid_spec=grid_spec,
        out_shape=[jax.ShapeDtypeStruct((t, d), F32),
                   jax.ShapeDtypeStruct((t, d // 2), U32),
                   jax.ShapeDtypeStruct((nr, t), F32),
                   jax.ShapeDtypeStruct((ngrp, SUBLANES, d_lru), F32)],
        compiler_params=_cparams(("arbitrary",)),
        name="backward_scan_mix",
    )(tile_row, flags, xa, xa, xa, conv_w, conv_b.reshape(1, d_lru), wg, bg, lam.reshape(1, d_lru), h0,
      hf, ga, yb, x_lat, x_ctx, modtab, modtab, modtab, w_out, norm2_g.reshape(1, d), rwt, rb)


def _shift_rows(v, k):
    if k == 0:
        return v
    z = jnp.zeros((abs(k) * SUBLANES, v.shape[1]), v.dtype)
    if k > 0:
        return jnp.concatenate([z, v[:-k * SUBLANES]], axis=0)
    return jnp.concatenate([v[-k * SUBLANES:], z], axis=0)


def _run_sum(v, m, direction):
    if m & (m - 1) == 0:
        k = 1
        while k < m:
            v = v + _shift_rows(v, -direction * k)
            k *= 2
        return v
    out = v
    for j in range(1, m):
        out = out + _shift_rows(v, -direction * j)
    return out


def _box_sum(v, w):
    lo = w // 2
    hi = w - 1 - lo
    s = _run_sum(v, hi + 1, +1)
    if lo:
        s = s + _shift_rows(_run_sum(v, lo, -1), 1)
    return s


def _window_count(n_rows, lanes, n_pos, w):
    lo = w // 2
    hi = w - 1 - lo
    p = lax.shift_right_logical(lax.broadcasted_iota(I32, (n_rows, lanes), 0), 3)
    return (jnp.minimum(p + hi + 1, n_pos) - jnp.maximum(p - lo, 0)).astype(F32)


def _pool_kernel(z_ref, ps_ref, o_ref, v_ref, *, tiles_per_group, grid_rows, grid_cols):
    group = pl.program_id(1) // tiles_per_group
    lanes = z_ref.shape[1]
    ps = ps_ref[...]

    def pool_1d(w):
        z = z_ref[...].astype(F32)
        n = z.shape[0] // SUBLANES
        mean = _box_sum(z, w) / _window_count(z.shape[0], lanes, n, w)
        o_ref[...] = ((mean - z) * ps).astype(o_ref.dtype)

    def pool_2d(w):
        lo = w // 2
        hi = w - 1 - lo
        blk = grid_cols * SUBLANES
        cw = _window_count(blk, lanes, grid_cols, w)

        def zrow(r):
            return z_ref[pl.ds(pl.multiple_of(r * blk, blk), blk), :].astype(F32)

        v = jnp.zeros((blk, lanes), F32)
        for r in range(hi):
            v = v + zrow(r)
        v_ref[...] = v

        def body(r, carry):
            add = r + hi
            sub = r - lo - 1
            v = v_ref[...]
            v = v + jnp.where(add < grid_rows, zrow(jnp.minimum(add, grid_rows - 1)), 0.0)
            v = v - jnp.where(sub >= 0, zrow(jnp.maximum(sub, 0)), 0.0)
            v_ref[...] = v
            ch = (jnp.minimum(r + hi + 1, grid_rows) - jnp.maximum(r - lo, 0)).astype(F32)
            mean = _box_sum(v, w) / (ch * cw)
            o_ref[pl.ds(pl.multiple_of(r * blk, blk), blk), :] = ((mean - zrow(r)) * ps).astype(o_ref.dtype)
            return carry

        lax.fori_loop(0, grid_rows, body, 0)

    for g, w in enumerate(POOL_WINDOWS):
        @pl.when(group == g)
        def _(w=w):
            if grid_rows is None:
                pool_1d(w)
            else:
                pool_2d(w)


def _pool(z, prev_out, pool_scale, rows_per_group, n_groups, first_block, grid_rows):
    t, d_pool = z.shape
    gw = d_pool // len(POOL_WINDOWS)
    lanes = LANES
    blk = GRID_W * SUBLANES
    kern = functools.partial(_pool_kernel, tiles_per_group=gw // lanes, grid_rows=grid_rows, grid_cols=GRID_W)
    in_specs = [pl.BlockSpec((rows_per_group, lanes), lambda g, j: (first_block + g, j)),
                pl.BlockSpec((1, lanes), lambda g, j: (0, j))]
    args = [z, pool_scale.reshape(1, d_pool)]
    aliases = {}
    if prev_out is not None:
        in_specs.append(pl.BlockSpec(memory_space=pl.ANY))
        args.append(prev_out)
        aliases = {2: 0}
        kern_fn = lambda z_ref, ps_ref, prev_ref, o_ref, v_ref: kern(z_ref, ps_ref, o_ref, v_ref)
    else:
        kern_fn = kern
    return pl.pallas_call(
        kern_fn,
        grid=(n_groups, d_pool // lanes),
        in_specs=in_specs,
        out_specs=pl.BlockSpec((rows_per_group, lanes), lambda g, j: (first_block + g, j)),
        out_shape=jax.ShapeDtypeStruct((t, d_pool), BF16),
        scratch_shapes=[pltpu.VMEM((blk, lanes), F32)],
        input_output_aliases=aliases,
        compiler_params=_cparams(("arbitrary", "arbitrary")),
        name="pool_grid" if grid_rows is not None else "pool_seq",
    )(*args)


def _route_kernel(lg_ref, oi_ref, of_ref, cnt_ref, tri_ref, carry_ref, *, n_groups, per_group):
    step = pl.program_id(0)
    tt = lg_ref.shape[1]
    n_exp = n_groups * per_group

    @pl.when(step == 0)
    def _():
        r = lax.broadcasted_iota(I32, tri_ref.shape, 0)
        c = lax.broadcasted_iota(I32, tri_ref.shape, 1)
        tri_ref[...] = (r <= c).astype(BF16)
        carry_ref[...] = jnp.zeros_like(carry_ref)

    row = lambda k: lg_ref[k:k + 1, :]
    cmax = row(0)
    gi = jnp.zeros((1, tt), I32)
    for g in range(1, n_groups):
        better = row(g) > cmax
        gi = jnp.where(better, g, gi)
        cmax = jnp.where(better, row(g), cmax)
    denom = jnp.zeros((1, tt), F32)
    for g in range(n_groups):
        denom = denom + jnp.exp(row(g) - cmax)
    pg = 1.0 / denom
    fine = []
    for j in range(per_group):
        f = row(n_groups + j)
        for g in range(1, n_groups):
            f = jnp.where(gi == g, row(n_groups + g * per_group + j), f)
        fine.append(f)
    v1 = fine[0]
    i1 = jnp.zeros((1, tt), I32)
    for j in range(1, per_group):
        better = fine[j] > v1
        i1 = jnp.where(better, j, i1)
        v1 = jnp.where(better, fine[j], v1)
    v2 = jnp.full((1, tt), -jnp.inf, F32)
    i2 = jnp.zeros((1, tt), I32)
    for j in range(per_group):
        better = jnp.logical_and(i1 != j, fine[j] > v2)
        i2 = jnp.where(better, j, i2)
        v2 = jnp.where(better, fine[j], v2)
    ex = jnp.exp(v2 - v1)
    w1 = (1.0 / (1.0 + ex)) * pg
    w2 = (ex / (1.0 + ex)) * pg
    e1 = gi * per_group + i1
    e2 = gi * per_group + i2
    eid = lax.broadcasted_iota(I32, (n_exp, tt), 0)
    hit1 = eid == e1
    hit2 = eid == e2
    member = jnp.logical_or(hit1, hit2).astype(BF16)
    cb = tri_ref.shape[0]
    carry = carry_ref[...]
    cums = []
    for s in range(tt // cb):
        c = jnp.dot(member[:, s * cb:(s + 1) * cb], tri_ref[...], preferred_element_type=F32) + carry
        carry = c[:, cb - 1:cb]
        cums.append(c)
    carry_ref[...] = carry
    cum = jnp.concatenate(cums, axis=1)
    rank1 = jnp.sum(jnp.where(hit1, cum, 0.0), axis=0, keepdims=True) - 1.0
    rank2 = jnp.sum(jnp.where(hit2, cum, 0.0), axis=0, keepdims=True) - 1.0
    zi = jnp.zeros((SUBLANES - 4, tt), I32)
    oi_ref[...] = jnp.concatenate([e1, e2, rank1.astype(I32), rank2.astype(I32), zi], axis=0)
    of_ref[...] = jnp.concatenate([w1, w2, jnp.zeros((of_ref.shape[0] - 2, tt), F32)], axis=0)
    cnt_ref[...] = jnp.broadcast_to(carry, cnt_ref.shape).astype(I32)


def _route(logits_t, n_groups, per_group):
    nr, t = logits_t.shape
    cb = min(CUMSUM_BLOCK, t)
    tt = max(k for k in range(cb, min(ROUTE_TILE, t) + 1, cb) if t % k == 0)
    n_exp = n_groups * per_group
    return pl.pallas_call(
        functools.partial(_route_kernel, n_groups=n_groups, per_group=per_group),
        grid=(t // tt,),
        in_specs=[pl.BlockSpec((nr, tt), lambda i: (0, i))],
        out_specs=[pl.BlockSpec((SUBLANES, tt), lambda i: (0, i)),
                   pl.BlockSpec((LANES, tt), lambda i: (0, i)),
                   pl.BlockSpec((n_exp, LANES), lambda i: (0, 0))],
        out_shape=[jax.ShapeDtypeStruct((SUBLANES, t), I32),
                   jax.ShapeDtypeStruct((LANES, t), F32),
                   jax.ShapeDtypeStruct((n_exp, LANES), I32)],
        scratch_shapes=[pltpu.VMEM((cb, cb), BF16), pltpu.VMEM((n_exp, 1), F32)],
        compiler_params=_cparams(("arbitrary",)),
        name="route",
    )(logits_t)


def _dispatch_kernel(cnt_ref, ri_hbm, h2_hbm, zero_hbm, xs_hbm, off_ref, te_ref, nv_ref,
                     idx_ref, isem, sem, zsem, *, tmx, n_exp):
    step = pl.program_id(0)
    ts = idx_ref.shape[1]
    shift = tmx.bit_length() - 1
    n_tiles_of = lambda n: jnp.right_shift(n + (tmx - 1), shift)
    round_up = lambda n: jnp.left_shift(n_tiles_of(n), shift)

    @pl.when(step == 0)
    def _():
        def per_expert(e, run):
            off_ref[e] = run
            first_tile = jnp.right_shift(run, shift)
            n_tiles = n_tiles_of(cnt_ref[e, 0])

            def mark(k, c):
                te_ref[first_tile + k] = e
                return c
            lax.fori_loop(0, n_tiles, mark, 0)
            return run + jnp.left_shift(n_tiles, shift)

        total = lax.fori_loop(0, n_exp, per_expert, jnp.int32(0))
        used = jnp.right_shift(total, shift)
        nv_ref[0] = used

        def rest(k, c):
            te_ref[k] = n_exp - 1
            return c
        lax.fori_loop(used, te_ref.shape[0], rest, 0)

    base = pl.multiple_of(step * ts, ts)
    idx_copy = pltpu.make_async_copy(ri_hbm.at[:, pl.ds(base, ts)], idx_ref, isem)
    idx_copy.start()
    idx_copy.wait()

    def row_copy(t, p):
        return pltpu.make_async_copy(h2_hbm.at[pl.ds(t, 1), :], xs_hbm.at[pl.ds(p, 1), :], sem)

    def send(j, c):
        p1 = off_ref[idx_ref[0, j]] + idx_ref[2, j]
        p2 = off_ref[idx_ref[1, j]] + idx_ref[3, j]
        row_copy(base + j, p1).start()
        row_copy(base + j, p2).start()
        return c
    lax.fori_loop(0, ts, send, 0, unroll=8)
    _wait_many(row_copy(0, 0), TOP_K_FINE * ts)

    @pl.when(step == pl.num_programs(0) - 1)
    def _():
        def pad_expert(e, c):
            cnt = cnt_ref[e, 0]
            n_pad = round_up(cnt) - cnt
            first = off_ref[e] + cnt
            zero_copy = lambda r: pltpu.make_async_copy(zero_hbm, xs_hbm.at[pl.ds(first + r, 1), :], zsem)

            def fill(r, c2):
                zero_copy(r).start()
                return c2
            lax.fori_loop(0, n_pad, fill, 0)

            def done(r, c2):
                zero_copy(r).wait()
                return c2
            lax.fori_loop(0, n_pad, done, 0)
            return c
        lax.fori_loop(0, n_exp, pad_expert, 0)


def _dispatch(counts, ri, h2p, tmx, n_exp):
    t, half = h2p.shape
    ts = max(k for k in range(LANES, min(DISPATCH_TILE, t) + 1, LANES) if t % k == 0)
    p_max = TOP_K_FINE * t + n_exp * tmx
    smem = lambda: pl.BlockSpec(memory_space=pltpu.SMEM)
    hbm = lambda: pl.BlockSpec(memory_space=pl.ANY)
    return pl.pallas_call(
        functools.partial(_dispatch_kernel, tmx=tmx, n_exp=n_exp),
        grid=(t // ts,),
        in_specs=[smem(), hbm(), hbm(), hbm()],
        out_specs=[hbm(), smem(), smem(), smem()],
        out_shape=[jax.ShapeDtypeStruct((p_max, half), U32),
                   jax.ShapeDtypeStruct((n_exp,), I32),
                   jax.ShapeDtypeStruct((p_max // tmx,), I32),
                   jax.ShapeDtypeStruct((1,), I32)],
        scratch_shapes=[pltpu.SMEM((SUBLANES, ts), I32),
                        pltpu.SemaphoreType.DMA(()),
                        pltpu.SemaphoreType.DMA(()),
                        pltpu.SemaphoreType.DMA(())],
        compiler_params=_cparams(("arbitrary",)),
        name="dispatch",
    )(counts, ri, h2p, jnp.zeros((1, half), U32))


def _expert_kernel(te_ref, nv_ref, x_ref, w1_ref, w3_ref, w2_ref, y_ref):
    @pl.when(pl.program_id(0) < nv_ref[0])
    def _():
        lo, hi = _unpack_bf16_pair(x_ref[...])
        lo = lo.astype(BF16)
        hi = hi.astype(BF16)
        half = lo.shape[1]
        mm = lambda w_ref: (jnp.dot(lo, w_ref[0, :half, :], preferred_element_type=F32)
                            + jnp.dot(hi, w_ref[0, half:, :], preferred_element_type=F32))
        h1 = mm(w1_ref)
        h3 = mm(w3_ref)
        hh = (h1 * jax.nn.sigmoid(h1) * h3).astype(BF16)
        y = jnp.dot(hh, w2_ref[0], preferred_element_type=F32)
        y_ref[...] = _pack_bf16_pair(y[:, :half], y[:, half:])


def _expert_mlp(xs, tile_expert, n_valid, w1, w3, w2, tmx):
    p, half = xs.shape
    d, de = w1.shape[1], w1.shape[2]
    tile = lambda i, te, nv: (jnp.minimum(i, nv[0] - 1), 0)
    grid_spec = pltpu.PrefetchScalarGridSpec(
        num_scalar_prefetch=2,
        grid=(p // tmx,),
        in_specs=[pl.BlockSpec((tmx, half), tile),
                  pl.BlockSpec((1, d, de), lambda i, te, nv: (te[i], 0, 0)),
                  pl.BlockSpec((1, d, de), lambda i, te, nv: (te[i], 0, 0)),
                  pl.BlockSpec((1, de, d), lambda i, te, nv: (te[i], 0, 0))],
        out_specs=pl.BlockSpec((tmx, half), tile))
    return pl.pallas_call(
        _expert_kernel,
        grid_spec=grid_spec,
        out_shape=jax.ShapeDtypeStruct((p, half), U32),
        compiler_params=_cparams(("arbitrary",)),
        name="expert_mlp",
    )(tile_expert, n_valid, xs, w1, w3, w2)


def _combine_kernel(off_ref, ri_hbm, ys_hbm, x1_ref, rf_ref, g2_ref, fg_ref, ylat_hbm, yctx_hbm,
                    idx_ref, gbuf, obuf, isem, gsem, osem, *, geom):
    step = pl.program_id(0)
    n_steps = pl.num_programs(0)
    tm = geom.tm
    slot = lax.rem(step, 2)
    base = pl.multiple_of(step * tm, tm)
    idx_copy = pltpu.make_async_copy(ri_hbm.at[:, pl.ds(base, tm)], idx_ref, isem)
    idx_copy.start()
    idx_copy.wait()

    def row_copy(k, p, j):
        return pltpu.make_async_copy(ys_hbm.at[pl.ds(p, 1), :], gbuf.at[k, pl.ds(j, 1), :], gsem)

    def fetch(j, c):
        row_copy(0, off_ref[idx_ref[0, j]] + idx_ref[2, j], j).start()
        row_copy(1, off_ref[idx_ref[1, j]] + idx_ref[3, j], j).start()
        return c
    lax.fori_loop(0, tm, fetch, 0, unroll=8)
    _wait_many(row_copy(0, 0, 0), TOP_K_FINE * tm)

    wt = rf_ref[...].T
    a_lo, a_hi = _unpack_bf16_pair(gbuf[0])
    b_lo, b_hi = _unpack_bf16_pair(gbuf[1])
    w1 = wt[:, 0:1]
    w2 = wt[:, 1:2]
    moe = jnp.concatenate([w1 * a_lo + w2 * b_lo, w1 * a_hi + w2 * b_hi], axis=1)
    x = x1_ref[...] + _gate(moe, g2_ref[0])
    out = _rms(x, fg_ref[...])

    start = lambda cp: cp.start()
    wait = lambda cp: cp.wait()
    put = functools.partial(_tile_copies, ylat_hbm, yctx_hbm, geom=geom, to_rows=False)

    @pl.when(step >= 2)
    def _():
        put(obuf.at[slot], osem.at[slot], step - 2, fn=wait)

    obuf[slot] = out.reshape(obuf.shape[1:])
    put(obuf.at[slot], osem.at[slot], step, fn=start)

    @pl.when(step == n_steps - 1)
    def _():
        @pl.when(n_steps > 1)
        def _():
            put(obuf.at[1 - slot], osem.at[1 - slot], step - 1, fn=wait)
        put(obuf.at[slot], osem.at[slot], step, fn=wait)


def _combine(offsets, ri, ys, x1, rf, modtab, final_g, lat_shape, ctx_shape, geom):
    t, d = x1.shape
    tm = geom.tm
    mset = lambda i, off: jnp.where(i < geom.n_lat_tiles, 0, 1)
    hbm = lambda: pl.BlockSpec(memory_space=pl.ANY)
    grid_spec = pltpu.PrefetchScalarGridSpec(
        num_scalar_prefetch=1,
        grid=(t // tm,),
        in_specs=[hbm(), hbm(),
                  pl.BlockSpec((tm, d), lambda i, off: (i, 0)),
                  pl.BlockSpec((LANES, tm), lambda i, off: (0, i)),
                  pl.BlockSpec((1, SUBLANES, d), lambda i, off: (mset(i, off), 0, 5)),
                  pl.BlockSpec((1, d), lambda i, off: (0, 0))],
        out_specs=[hbm(), hbm()],
        scratch_shapes=[pltpu.SMEM((SUBLANES, tm), I32),
                        pltpu.VMEM((TOP_K_FINE, tm, d // 2), U32),
                        pltpu.VMEM((2, tm // SUBLANES, SUBLANES, d), F32),
                        pltpu.SemaphoreType.DMA(()),
                        pltpu.SemaphoreType.DMA(()),
                        pltpu.SemaphoreType.DMA((2,))])
    return pl.pallas_call(
        functools.partial(_combine_kernel, geom=geom),
        grid_spec=grid_spec,
        out_shape=[jax.ShapeDtypeStruct(lat_shape, F32), jax.ShapeDtypeStruct(ctx_shape, F32)],
        compiler_params=_cparams(("arbitrary",)),
        name="combine",
    )(offsets, ri, ys, x1, rf, modtab, final_g.reshape(1, d))


def _tile_meta(groups, tm):
    rows, first, last, grp = [], [], [], []
    blk = 0
    for g, r in enumerate(groups):
        nc = r // tm
        for c in range(nc):
            rows.append(blk + c)
            first.append(int(c == 0))
            last.append(int(c == nc - 1))
            grp.append(g)
        blk += nc
    fwd = (np.array(rows, np.int32), np.array([first, last, grp], np.int32))
    order = []
    blk = 0
    for r in groups:
        nc = r // tm
        order.extend(range(blk + nc - 1, blk - 1, -1))
        blk += nc
    order = np.array(order)
    bwd = (fwd[0][order], fwd[1][:, order])
    return fwd, bwd


def kernel(x_prompt, x_sample, state_lru, c, c_ctx, w_mod, b_mod, norm1_g, w_in, conv_w, conv_b, lru_wa, lru_ba, lru_wx, lru_bx, lru_lambda, pool_w, pool_scale, w_out, norm2_g, router_coarse_w, router_coarse_b, router_fine_w, router_fine_b, exp_w1, exp_w3, exp_w2, final_norm_g):
    bp, sp, d = x_prompt.shape
    bs, ss, _ = x_sample.shape
    d_lru = lru_lambda.shape[-1]
    heads, bw = lru_wa.shape[2], lru_wa.shape[3]
    n_groups, per_group = router_fine_w.shape[2], router_fine_w.shape[3]
    n_exp = n_groups * per_group
    assert w_mod.shape[0] == 1 and bs == SUBLANES and bp % SUBLANES == 0 and ss % GRID_W == 0
    assert EXPERT_TILE & (EXPERT_TILE - 1) == 0
    n_lat_groups, n_ctx_groups = bs // SUBLANES, bp // SUBLANES
    lat_rows, ctx_rows = ss * SUBLANES, sp * SUBLANES
    assert (n_lat_groups * lat_rows) % ctx_rows == 0
    tm = min(TOKEN_TILE, ctx_rows, lat_rows)
    geom = _Geom(tm=tm, n_lat_tiles=n_lat_groups * lat_rows // tm, lat_chunks=lat_rows // tm,
                 ctx_chunks=ctx_rows // tm)
    n_rows = n_lat_groups * lat_rows + n_ctx_groups * ctx_rows
    groups = [lat_rows] * n_lat_groups + [ctx_rows] * n_ctx_groups
    (f_rows, f_flags), (b_rows, b_flags) = _tile_meta(groups, tm)

    cond = jnp.zeros((2 * SUBLANES, d), F32).at[:bs].set(c).at[bs].set(c_ctx)
    mod = _modulation(cond, w_mod[0], b_mod[0])
    modtab = jnp.stack([mod[:SUBLANES], jnp.broadcast_to(mod[SUBLANES], (SUBLANES, mod.shape[1]))])
    h0_lat = state_lru[:, 0].reshape(n_lat_groups, SUBLANES, 2, d_lru)
    h0 = jnp.concatenate([h0_lat, jnp.zeros((n_ctx_groups, SUBLANES, 2, d_lru), F32)], axis=0)
    h0 = h0.transpose(0, 2, 1, 3)

    xa, ga, z = _input_projection(x_sample, x_prompt, modtab, norm1_g[0], w_in[0].astype(BF16),
                                  pool_w[0].astype(BF16), n_rows, geom)

    def gate_weights(direction):
        wg = jnp.concatenate([lru_wa[0, direction], lru_wx[0, direction]], axis=-1).astype(BF16)
        bg = jnp.concatenate([lru_ba[0, direction].reshape(heads, 1, bw),
                              lru_bx[0, direction].reshape(heads, 1, bw)], axis=-1)
        return wg, bg

    log_decay = jax.nn.log_sigmoid(lru_lambda[0])
    wg_f, bg_f = gate_weights(0)
    wg_b, bg_b = gate_weights(1)
    hf, hf_last = _forward_scan(xa, (jnp.asarray(f_rows), jnp.asarray(f_flags)), conv_w[0], conv_b[0],
                                wg_f, bg_f, log_decay[0], h0[:, 0], tm)

    yb = _pool(z, None, pool_scale[0], lat_rows, n_lat_groups, 0, lat_rows // (GRID_W * SUBLANES))
    yb = _pool(z, yb, pool_scale[0], ctx_rows, n_ctx_groups, n_lat_groups * lat_rows // ctx_rows, None)

    n_logits = n_groups + n_exp
    rw = jnp.concatenate([router_coarse_w[0], router_fine_w[0].reshape(d, n_exp)], axis=1)
    rwt = jnp.zeros((LANES, d), BF16).at[:n_logits].set(rw.T.astype(BF16))
    rb = jnp.zeros((LANES, 1), F32).at[:n_logits, 0].set(
        jnp.concatenate([router_coarse_b[0], router_fine_b[0].reshape(n_exp)]))
    x1, h2p, logits_t, hb_last = _backward_scan_mix(
        xa, ga, hf, yb, x_sample, x_prompt, modtab, (jnp.asarray(b_rows), jnp.asarray(b_flags)),
        conv_w[0], conv_b[0], wg_b, bg_b, log_decay[1], h0[:, 1], w_out[0].astype(BF16), norm2_g[0],
        rwt, rb, geom)

    ri, rf, counts = _route(logits_t, n_groups, per_group)
    xs, offsets, tile_expert, n_valid = _dispatch(counts, ri, h2p, EXPERT_TILE, n_exp)
    ys = _expert_mlp(xs, tile_expert, n_valid, exp_w1[0].astype(BF16), exp_w3[0].astype(BF16),
                     exp_w2[0].astype(BF16), EXPERT_TILE)
    y_sample, y_prompt = _combine(offsets, ri, ys, x1, rf, modtab, final_norm_g, x_sample.shape,
                                  x_prompt.shape, geom)

    st = jnp.stack([hf_last[n_lat_groups:], hb_last[n_lat_groups:]], axis=2)
    state_new = st.reshape(bp, 1, 2, d_lru).astype(x_prompt.dtype)
    return (y_prompt, y_sample, state_new)
```

```python
import functools
from typing import NamedTuple

import numpy as np
import jax
import jax.numpy as jnp
from jax import lax
from jax.experimental import pallas as pl
from jax.experimental.pallas import tpu as pltpu

GRID_W = 64
CONV_W = 4
RG_C = 8.0
POOL_WINDOWS = (2, 4, 8, 16)
TOP_K_FINE = 2
EPS = 1e-6
EXPM1_SERIES_BELOW = 0.25

SUBLANES = 8
LANES = 128
BF16_ROWS = 16
TOKEN_TILE = 512
EXPERT_TILE = 512
ROUTE_TILE = 2048
CUMSUM_BLOCK = 512
MOD_COL_TILE = 1024
WAIT_UNROLL = 32
VMEM_LIMIT = 60 * 1024 * 1024

F32 = jnp.float32
BF16 = jnp.bfloat16
U32 = jnp.uint32
I32 = jnp.int32
HIGH_HALF = 0xFFFF0000


class _Geom(NamedTuple):
    tm: int
    n_lat_tiles: int
    lat_chunks: int
    ctx_chunks: int


def _cparams(sem):
    return pltpu.CompilerParams(dimension_semantics=sem, vmem_limit_bytes=VMEM_LIMIT)


def _per_sequence(y, m):
    rows, d = y.shape
    return y.reshape(rows // SUBLANES, SUBLANES, d), m[None]


def _modulate(y, scale, shift):
    y3, sc = _per_sequence(y, scale)
    _, sh = _per_sequence(y, shift)
    return (y3 * (1.0 + sc) + sh).reshape(y.shape)


def _gate(y, g):
    y3, g3 = _per_sequence(y, g)
    return (y3 * g3).reshape(y.shape)


def _rms(x, g):
    ms = jnp.mean(x * x, axis=-1, keepdims=True)
    return x * lax.rsqrt(ms + EPS) * g


def _pack_bf16_pair(lo, hi):
    lo_bits = lax.shift_right_logical(lax.bitcast_convert_type(lo.astype(BF16).astype(F32), U32), U32(16))
    hi_bits = lax.bitcast_convert_type(hi.astype(BF16).astype(F32), U32) & U32(HIGH_HALF)
    return lo_bits | hi_bits


def _unpack_bf16_pair(w):
    lo = lax.bitcast_convert_type(lax.shift_left(w, U32(16)), F32)
    hi = lax.bitcast_convert_type(w & U32(HIGH_HALF), F32)
    return lo, hi


def _wait_many(copy, n):
    def body(_, c):
        for _ in range(WAIT_UNROLL):
            copy.wait()
        return c
    lax.fori_loop(0, n // WAIT_UNROLL, body, 0)


def _tile_copies(lat_hbm, ctx_hbm, buf, sem, tile, geom, to_rows, fn):
    tt = geom.tm // SUBLANES

    def run(hbm, k, chunks):
        g = lax.div(k, jnp.int32(chunks))
        c = lax.rem(k, jnp.int32(chunks))
        for b in range(SUBLANES):
            h = hbm.at[g * SUBLANES + b, pl.ds(c * tt, tt), :]
            v = buf.at[:, b, :]
            fn(pltpu.make_async_copy(h, v, sem) if to_rows else pltpu.make_async_copy(v, h, sem))

    @pl.when(tile < geom.n_lat_tiles)
    def _():
        run(lat_hbm, tile, geom.lat_chunks)

    @pl.when(tile >= geom.n_lat_tiles)
    def _():
        run(ctx_hbm, tile - geom.n_lat_tiles, geom.ctx_chunks)


def _fetch_rows(lat_hbm, ctx_hbm, xbuf, sem, step, n_steps, tile_of, geom):
    slot = lax.rem(step, 2)
    start = lambda cp: cp.start()
    wait = lambda cp: cp.wait()

    @pl.when(step == 0)
    def _():
        _tile_copies(lat_hbm, ctx_hbm, xbuf.at[0], sem.at[0], tile_of(jnp.int32(0)), geom, True, start)

    @pl.when(step + 1 < n_steps)
    def _():
        nxt = 1 - slot
        _tile_copies(lat_hbm, ctx_hbm, xbuf.at[nxt], sem.at[nxt], tile_of(step + 1), geom, True, start)

    _tile_copies(lat_hbm, ctx_hbm, xbuf.at[slot], sem.at[slot], tile_of(step), geom, True, wait)
    tt, _, d = xbuf.shape[1:]
    return xbuf[slot].reshape(tt * SUBLANES, d)


def _mod_kernel(c_ref, w_ref, b_ref, o_ref):
    c = c_ref[...]
    s = c * jax.nn.sigmoid(c)
    o_ref[...] = jnp.dot(s.astype(BF16), w_ref[...].astype(BF16),
                         preferred_element_type=F32) + b_ref[...]


def _modulation(cond, w_mod, b_mod):
    rows, d = cond.shape
    n = w_mod.shape[1]
    tn = min(MOD_COL_TILE, n)
    return pl.pallas_call(
        _mod_kernel,
        grid=(n // tn,),
        in_specs=[pl.BlockSpec((rows, d), lambda j: (0, 0)),
                  pl.BlockSpec((d, tn), lambda j: (0, j)),
                  pl.BlockSpec((1, tn), lambda j: (0, j))],
        out_specs=pl.BlockSpec((rows, tn), lambda j: (0, j)),
        out_shape=jax.ShapeDtypeStruct((rows, n), F32),
        compiler_params=_cparams(("arbitrary",)),
        name="modulation",
    )(cond, w_mod, b_mod.reshape(1, n))


def _proj_kernel(lat_hbm, ctx_hbm, sh_ref, sc_ref, g_ref, win_ref, pw_ref, xa_ref, ga_ref, z_ref,
                 xbuf, sem, *, d_lru, gw, geom):
    step = pl.program_id(0)
    x = _fetch_rows(lat_hbm, ctx_hbm, xbuf, sem, step, pl.num_programs(0), lambda s: s, geom)
    h = _modulate(_rms(x, g_ref[...]), sc_ref[0], sh_ref[0])
    proj = jnp.dot(h.astype(BF16), win_ref[...], preferred_element_type=F32)
    xa_ref[...] = proj[:, :d_lru].astype(BF16)
    ga_ref[...] = proj[:, d_lru:2 * d_lru].astype(BF16)
    for g in range(pw_ref.shape[0]):
        lo = 2 * d_lru + g * gw
        z_ref[:, g * gw:(g + 1) * gw] = jnp.dot(
            proj[:, lo:lo + gw].astype(BF16), pw_ref[g], preferred_element_type=F32).astype(BF16)


def _input_projection(x_lat, x_ctx, modtab, norm_g, w_in, pool_w, n_rows, geom):
    d = x_lat.shape[-1]
    tm = geom.tm
    d_pool = pool_w.shape[0] * pool_w.shape[1]
    d_lru = (w_in.shape[1] - d_pool) // 2
    mset = lambda i: jnp.where(i < geom.n_lat_tiles, 0, 1)
    const = dict(pipeline_mode=pl.Buffered(1))
    return pl.pallas_call(
        functools.partial(_proj_kernel, d_lru=d_lru, gw=pool_w.shape[1], geom=geom),
        grid=(n_rows // tm,),
        in_specs=[pl.BlockSpec(memory_space=pl.ANY),
                  pl.BlockSpec(memory_space=pl.ANY),
                  pl.BlockSpec((1, SUBLANES, d), lambda i: (mset(i), 0, 0)),
                  pl.BlockSpec((1, SUBLANES, d), lambda i: (mset(i), 0, 1)),
                  pl.BlockSpec((1, d), lambda i: (0, 0)),
                  pl.BlockSpec(w_in.shape, lambda i: (0, 0), **const),
                  pl.BlockSpec(pool_w.shape, lambda i: (0, 0, 0), **const)],
        out_specs=[pl.BlockSpec((tm, d_lru), lambda i: (i, 0)),
                   pl.BlockSpec((tm, d_lru), lambda i: (i, 0)),
                   pl.BlockSpec((tm, d_pool), lambda i: (i, 0))],
        out_shape=[jax.ShapeDtypeStruct((n_rows, d_lru), BF16),
                   jax.ShapeDtypeStruct((n_rows, d_lru), BF16),
                   jax.ShapeDtypeStruct((n_rows, d_pool), BF16)],
        scratch_shapes=[pltpu.VMEM((2, tm // SUBLANES, SUBLANES, d), F32),
                        pltpu.SemaphoreType.DMA((2,))],
        compiler_params=_cparams(("arbitrary",)),
        name="input_projection",
    )(x_lat, x_ctx, modtab, modtab, norm_g.reshape(1, d), w_in, pool_w)


def _fill_ext(ext_ref, prev_ref, main_ref, next_ref, first, last):
    tm = main_ref.shape[0]
    prev = prev_ref[...].astype(F32)
    nxt = next_ref[...].astype(F32)
    ext_ref[0:BF16_ROWS, :] = jnp.where(first, 0.0, prev)
    ext_ref[BF16_ROWS:BF16_ROWS + tm, :] = main_ref[...].astype(F32)
    ext_ref[BF16_ROWS + tm:, :] = jnp.where(last, 0.0, nxt)


def _one_minus_exp(y, exp_y):
    p = 1.0 / 5040.0
    for c in (1.0 / 720.0, 1.0 / 120.0, 1.0 / 24.0, 1.0 / 6.0, 0.5, 1.0):
        p = p * y + c
    return jnp.where(y > -EXPM1_SERIES_BELOW, -y * p, 1.0 - exp_y)


def _decay_and_input(ext_ref, cw_ref, cb_ref, wg_ref, bg_ref, lam_ref, a_ref, u_ref, tm):
    heads, bw = wg_ref.shape[0], wg_ref.shape[1]
    for hd in range(heads):
        sl = slice(hd * bw, (hd + 1) * bw)
        xc = cb_ref[:, sl]
        for k in range(CONV_W):
            xc = xc + cw_ref[k:k + 1, sl] * ext_ref[SUBLANES * k:SUBLANES * k + tm, sl]
        g = jnp.dot(xc.astype(BF16), wg_ref[hd], preferred_element_type=F32) + bg_ref[hd]
        r = jax.nn.sigmoid(g[:, :bw])
        ig = jax.nn.sigmoid(g[:, bw:])
        log_a = (RG_C * r) * lam_ref[:, sl]
        a = jnp.exp(log_a)
        a_ref[:, sl] = a
        u_ref[:, sl] = jnp.sqrt(_one_minus_exp(2.0 * log_a, a * a)) * (ig * xc)


def _scan(a_ref, u_ref, h, tm, reverse):
    nblk = tm // SUBLANES

    def body(s, h):
        j = (nblk - 1 - s) if reverse else s
        rows = pl.ds(pl.multiple_of(j * SUBLANES, SUBLANES), SUBLANES)
        h = a_ref[rows, :] * h + u_ref[rows, :]
        u_ref[rows, :] = h
        return h

    return lax.fori_loop(0, nblk, body, h, unroll=8)


def _halo_specs(tm, d_lru, n_rows):
    per = tm // BF16_ROWS
    last_blk = n_rows // BF16_ROWS - 1
    return [pl.BlockSpec((BF16_ROWS, d_lru), lambda i, tr, fl: (jnp.maximum(tr[i] * per - 1, 0), 0)),
            pl.BlockSpec((tm, d_lru), lambda i, tr, fl: (tr[i], 0)),
            pl.BlockSpec((BF16_ROWS, d_lru), lambda i, tr, fl: (jnp.minimum((tr[i] + 1) * per, last_blk), 0))]


def _fwd_kernel(tr_ref, fl_ref, prev_ref, main_ref, next_ref, cw_ref, cb_ref, wg_ref, bg_ref, lam_ref,
                h0_ref, hf_ref, hlast_ref, ext_ref, a_ref, u_ref, h_ref):
    i = pl.program_id(0)
    tm = main_ref.shape[0]
    first = fl_ref[0, i] == 1
    last = fl_ref[1, i] == 1
    _fill_ext(ext_ref, prev_ref, main_ref, next_ref, first, last)
    _decay_and_input(ext_ref, cw_ref, cb_ref, wg_ref, bg_ref, lam_ref, a_ref, u_ref, tm)

    @pl.when(first)
    def _():
        h_ref[...] = h0_ref[0]

    h = _scan(a_ref, u_ref, h_ref[...], tm, reverse=False)
    h_ref[...] = h
    hlast_ref[0] = h
    hf_ref[...] = u_ref[...].astype(BF16)


def _forward_scan(xa, meta, conv_w, conv_b, wg, bg, lam, h0, tm):
    t, d_lru = xa.shape
    tile_row, flags = meta
    ngrp = h0.shape[0]
    const = dict(pipeline_mode=pl.Buffered(1))
    grid_spec = pltpu.PrefetchScalarGridSpec(
        num_scalar_prefetch=2,
        grid=(t // tm,),
        in_specs=_halo_specs(tm, d_lru, t) + [
            pl.BlockSpec(conv_w.shape, lambda i, tr, fl: (0, 0)),
            pl.BlockSpec((1, d_lru), lambda i, tr, fl: (0, 0)),
            pl.BlockSpec(wg.shape, lambda i, tr, fl: (0, 0, 0), **const),
            pl.BlockSpec(bg.shape, lambda i, tr, fl: (0, 0, 0)),
            pl.BlockSpec((1, d_lru), lambda i, tr, fl: (0, 0)),
            pl.BlockSpec((1, SUBLANES, d_lru), lambda i, tr, fl: (fl[2, i], 0, 0))],
        out_specs=[pl.BlockSpec((tm, d_lru), lambda i, tr, fl: (tr[i], 0)),
                   pl.BlockSpec((1, SUBLANES, d_lru), lambda i, tr, fl: (fl[2, i], 0, 0))],
        scratch_shapes=[pltpu.VMEM((tm + 2 * BF16_ROWS, d_lru), F32),
                        pltpu.VMEM((tm, d_lru), F32),
                        pltpu.VMEM((tm, d_lru), F32),
                        pltpu.VMEM((SUBLANES, d_lru), F32)])
    return pl.pallas_call(
        _fwd_kernel,
        grid_spec=grid_spec,
        out_shape=[jax.ShapeDtypeStruct((t, d_lru), BF16),
                   jax.ShapeDtypeStruct((ngrp, SUBLANES, d_lru), F32)],
        compiler_params=_cparams(("arbitrary",)),
        name="forward_scan",
    )(tile_row, flags, xa, xa, xa, conv_w, conv_b.reshape(1, d_lru), wg, bg, lam.reshape(1, d_lru), h0)


def _bwd_kernel(tr_ref, fl_ref, prev_ref, main_ref, next_ref, cw_ref, cb_ref, wg_ref, bg_ref, lam_ref,
                h0_ref, hf_ref, ga_ref, yb_ref, lat_hbm, ctx_hbm, g1_ref, sh2_ref, sc2_ref, wout_ref,
                n2_ref, rwt_ref, rb_ref,
                x1_ref, h2_ref, lg_ref, hlast_ref,
                ext_ref, a_ref, u_ref, h_ref, cat_ref, xbuf, xsem, *, geom):
    i = pl.program_id(0)
    n_steps = pl.num_programs(0)
    tm, d_lru = main_ref.shape
    x = _fetch_rows(lat_hbm, ctx_hbm, xbuf, xsem, i, n_steps,
                    lambda s: tr_ref[jnp.minimum(s, n_steps - 1)], geom)
    first = fl_ref[0, i] == 1
    last = fl_ref[1, i] == 1
    _fill_ext(ext_ref, prev_ref, main_ref, next_ref, first, last)
    _decay_and_input(ext_ref, cw_ref, cb_ref, wg_ref, bg_ref, lam_ref, a_ref, u_ref, tm)

    @pl.when(last)
    def _():
        h_ref[...] = h0_ref[0]

    h = _scan(a_ref, u_ref, h_ref[...], tm, reverse=True)
    h_ref[...] = h
    hlast_ref[0] = h

    ga = ga_ref[...].astype(F32)
    ya = (hf_ref[...].astype(F32) + u_ref[...]) * jax.nn.gelu(ga)
    cat_ref[:, :d_lru] = ya.astype(BF16)
    cat_ref[:, d_lru:] = yb_ref[...]
    mix = jnp.dot(cat_ref[...], wout_ref[...], preferred_element_type=F32)
    x1 = x + _gate(mix, g1_ref[0])
    x1_ref[...] = x1
    h2 = _modulate(_rms(x1, n2_ref[...]), sc2_ref[0], sh2_ref[0])
    half = h2.shape[1] // 2
    h2_ref[...] = _pack_bf16_pair(h2[:, :half], h2[:, half:])
    lg_ref[...] = lax.dot_general(rwt_ref[...], h2.astype(BF16), (((1,), (1,)), ((), ())),
                                  preferred_element_type=F32) + rb_ref[...]


def _backward_scan_mix(xa, ga, hf, yb, x_lat, x_ctx, modtab, meta, conv_w, conv_b, wg, bg, lam, h0, w_out,
                       norm2_g, rwt, rb, geom):
    t, d_lru = xa.shape
    d = x_lat.shape[-1]
    d_pool = yb.shape[1]
    tm = geom.tm
    tile_row, flags = meta
    ngrp = h0.shape[0]
    nr = rwt.shape[0]
    const = dict(pipeline_mode=pl.Buffered(1))
    mset = lambda i, tr, fl: jnp.where(tr[i] < geom.n_lat_tiles, 0, 1)
    row = lambda i, tr, fl: (tr[i], 0)
    mod = lambda col: pl.BlockSpec((1, SUBLANES, d), lambda i, tr, fl: (mset(i, tr, fl), 0, col))
    grid_spec = pltpu.PrefetchScalarGridSpec(
        num_scalar_prefetch=2,
        grid=(t // tm,),
        in_specs=_halo_specs(tm, d_lru, t) + [
            pl.BlockSpec(conv_w.shape, lambda i, tr, fl: (0, 0)),
            pl.BlockSpec((1, d_lru), lambda i, tr, fl: (0, 0)),
            pl.BlockSpec(wg.shape, lambda i, tr, fl: (0, 0, 0), **const),
            pl.BlockSpec(bg.shape, lambda i, tr, fl: (0, 0, 0)),
            pl.BlockSpec((1, d_lru), lambda i, tr, fl: (0, 0)),
            pl.BlockSpec((1, SUBLANES, d_lru), lambda i, tr, fl: (fl[2, i], 0, 0)),
            pl.BlockSpec((tm, d_lru), row),
            pl.BlockSpec((tm, d_lru), row),
            pl.BlockSpec((tm, d_pool), row),
            pl.BlockSpec(memory_space=pl.ANY),
            pl.BlockSpec(memory_space=pl.ANY),
            mod(2), mod(3), mod(4),
            pl.BlockSpec(w_out.shape, lambda i, tr, fl: (0, 0), **const),
            pl.BlockSpec((1, d), lambda i, tr, fl: (0, 0)),
            pl.BlockSpec(rwt.shape, lambda i, tr, fl: (0, 0)),
            pl.BlockSpec((nr, 1), lambda i, tr, fl: (0, 0))],
        out_specs=[pl.BlockSpec((tm, d), row),
                   pl.BlockSpec((tm, d // 2), row),
                   pl.BlockSpec((nr, tm), lambda i, tr, fl: (0, tr[i])),
                   pl.BlockSpec((1, SUBLANES, d_lru), lambda i, tr, fl: (fl[2, i], 0, 0))],
        scratch_shapes=[pltpu.VMEM((tm + 2 * BF16_ROWS, d_lru), F32),
                        pltpu.VMEM((tm, d_lru), F32),
                        pltpu.VMEM((tm, d_lru), F32),
                        pltpu.VMEM((SUBLANES, d_lru), F32),
                        pltpu.VMEM((tm, d_lru + d_pool), BF16),
                        pltpu.VMEM((2, tm // SUBLANES, SUBLANES, d), F32),
                        pltpu.SemaphoreType.DMA((2,))])
    return pl.pallas_call(
        functools.partial(_bwd_kernel, geom=geom),
        grid_spec=grid_spec,
        out_shape=[jax.ShapeDtypeStruct((t, d), F32),
                   jax.ShapeDtypeStruct((t, d // 2), U32),
                   jax.ShapeDtypeStruct((nr, t), F32),
                   jax.ShapeDtypeStruct((ngrp, SUBLANES, d_lru), F32)],
        compiler_params=_cparams(("arbitrary",)),
        name="backward_scan_mix",
    )(tile_row, flags, xa, xa, xa, conv_w, conv_b.reshape(1, d_lru), wg, bg, lam.reshape(1, d_lru), h0,
      hf, ga, yb, x_lat, x_ctx, modtab, modtab, modtab, w_out, norm2_g.reshape(1, d), rwt, rb)


def _shift_rows(v, k):
    if k == 0:
        return v
    z = jnp.zeros((abs(k) * SUBLANES, v.shape[1]), v.dtype)
    if k > 0:
        return jnp.concatenate([z, v[:-k * SUBLANES]], axis=0)
    return jnp.concatenate([v[-k * SUBLANES:], z], axis=0)


def _run_sum(v, m, direction):
    if m & (m - 1) == 0:
        k = 1
        while k < m:
            v = v + _shift_rows(v, -direction * k)
            k *= 2
        return v
    out = v
    for j in range(1, m):
        out = out + _shift_rows(v, -direction * j)
    return out


def _box_sum(v, w):
    lo = w // 2
    hi = w - 1 - lo
    s = _run_sum(v, hi + 1, +1)
    if lo:
        s = s + _shift_rows(_run_sum(v, lo, -1), 1)
    return s


def _window_count(n_rows, lanes, n_pos, w):
    lo = w // 2
    hi = w - 1 - lo
    p = lax.shift_right_logical(lax.broadcasted_iota(I32, (n_rows, lanes), 0), 3)
    return (jnp.minimum(p + hi + 1, n_pos) - jnp.maximum(p - lo, 0)).astype(F32)


def _pool_kernel(z_ref, ps_ref, o_ref, v_ref, *, tiles_per_group, grid_rows, grid_cols):
    group = pl.program_id(1) // tiles_per_group
    lanes = z_ref.shape[1]
    ps = ps_ref[...]

    def pool_1d(w):
        z = z_ref[...].astype(F32)
        n = z.shape[0] // SUBLANES
        mean = _box_sum(z, w) / _window_count(z.shape[0], lanes, n, w)
        o_ref[...] = ((mean - z) * ps).astype(o_ref.dtype)

    def pool_2d(w):
        lo = w // 2
        hi = w - 1 - lo
        blk = grid_cols * SUBLANES
        cw = _window_count(blk, lanes, grid_cols, w)

        def zrow(r):
            return z_ref[pl.ds(pl.multiple_of(r * blk, blk), blk), :].astype(F32)

        v = jnp.zeros((blk, lanes), F32)
        for r in range(hi):
            v = v + zrow(r)
        v_ref[...] = v

        def body(r, carry):
            add = r + hi
            sub = r - lo - 1
            v = v_ref[...]
            v = v + jnp.where(add < grid_rows, zrow(jnp.minimum(add, grid_rows - 1)), 0.0)
            v = v - jnp.where(sub >= 0, zrow(jnp.maximum(sub, 0)), 0.0)
            v_ref[...] = v
            ch = (jnp.minimum(r + hi + 1, grid_rows) - jnp.maximum(r - lo, 0)).astype(F32)
            mean = _box_sum(v, w) / (ch * cw)
            o_ref[pl.ds(pl.multiple_of(r * blk, blk), blk), :] = ((mean - zrow(r)) * ps).astype(o_ref.dtype)
            return carry

        lax.fori_loop(0, grid_rows, body, 0)

    for g, w in enumerate(POOL_WINDOWS):
        @pl.when(group == g)
        def _(w=w):
            if grid_rows is None:
                pool_1d(w)
            else:
                pool_2d(w)


def _pool(z, prev_out, pool_scale, rows_per_group, n_groups, first_block, grid_rows):
    t, d_pool = z.shape
    gw = d_pool // len(POOL_WINDOWS)
    lanes = LANES
    blk = GRID_W * SUBLANES
    kern = functools.partial(_pool_kernel, tiles_per_group=gw // lanes, grid_rows=grid_rows, grid_cols=GRID_W)
    in_specs = [pl.BlockSpec((rows_per_group, lanes), lambda g, j: (first_block + g, j)),
                pl.BlockSpec((1, lanes), lambda g, j: (0, j))]
    args = [z, pool_scale.reshape(1, d_pool)]
    aliases = {}
    if prev_out is not None:
        in_specs.append(pl.BlockSpec(memory_space=pl.ANY))
        args.append(prev_out)
        aliases = {2: 0}
        kern_fn = lambda z_ref, ps_ref, prev_ref, o_ref, v_ref: kern(z_ref, ps_ref, o_ref, v_ref)
    else:
        kern_fn = kern
    return pl.pallas_call(
        kern_fn,
        grid=(n_groups, d_pool // lanes),
        in_specs=in_specs,
        out_specs=pl.BlockSpec((rows_per_group, lanes), lambda g, j: (first_block + g, j)),
        out_shape=jax.ShapeDtypeStruct((t, d_pool), BF16),
        scratch_shapes=[pltpu.VMEM((blk, lanes), F32)],
        input_output_aliases=aliases,
        compiler_params=_cparams(("arbitrary", "arbitrary")),
        name="pool_grid" if grid_rows is not None else "pool_seq",
    )(*args)


def _route_kernel(lg_ref, oi_ref, of_ref, cnt_ref, tri_ref, carry_ref, *, n_groups, per_group):
    step = pl.program_id(0)
    tt = lg_ref.shape[1]
    n_exp = n_groups * per_group

    @pl.when(step == 0)
    def _():
        r = lax.broadcasted_iota(I32, tri_ref.shape, 0)
        c = lax.broadcasted_iota(I32, tri_ref.shape, 1)
        tri_ref[...] = (r <= c).astype(BF16)
        carry_ref[...] = jnp.zeros_like(carry_ref)

    row = lambda k: lg_ref[k:k + 1, :]
    cmax = row(0)
    gi = jnp.zeros((1, tt), I32)
    for g in range(1, n_groups):
        better = row(g) > cmax
        gi = jnp.where(better, g, gi)
        cmax = jnp.where(better, row(g), cmax)
    denom = jnp.zeros((1, tt), F32)
    for g in range(n_groups):
        denom = denom + jnp.exp(row(g) - cmax)
    pg = 1.0 / denom
    fine = []
    for j in range(per_group):
        f = row(n_groups + j)
        for g in range(1, n_groups):
            f = jnp.where(gi == g, row(n_groups + g * per_group + j), f)
        fine.append(f)
    v1 = fine[0]
    i1 = jnp.zeros((1, tt), I32)
    for j in range(1, per_group):
        better = fine[j] > v1
        i1 = jnp.where(better, j, i1)
        v1 = jnp.where(better, fine[j], v1)
    v2 = jnp.full((1, tt), -jnp.inf, F32)
    i2 = jnp.zeros((1, tt), I32)
    for j in range(per_group):
        better = jnp.logical_and(i1 != j, fine[j] > v2)
        i2 = jnp.where(better, j, i2)
        v2 = jnp.where(better, fine[j], v2)
    ex = jnp.exp(v2 - v1)
    w1 = (1.0 / (1.0 + ex)) * pg
    w2 = (ex / (1.0 + ex)) * pg
    e1 = gi * per_group + i1
    e2 = gi * per_group + i2
    eid = lax.broadcasted_iota(I32, (n_exp, tt), 0)
    hit1 = eid == e1
    hit2 = eid == e2
    member = jnp.logical_or(hit1, hit2).astype(BF16)
    cb = tri_ref.shape[0]
    carry = carry_ref[...]
    cums = []
    for s in range(tt // cb):
        c = jnp.dot(member[:, s * cb:(s + 1) * cb], tri_ref[...], preferred_element_type=F32) + carry
        carry = c[:, cb - 1:cb]
        cums.append(c)
    carry_ref[...] = carry
    cum = jnp.concatenate(cums, axis=1)
    rank1 = jnp.sum(jnp.where(hit1, cum, 0.0), axis=0, keepdims=True) - 1.0
    rank2 = jnp.sum(jnp.where(hit2, cum, 0.0), axis=0, keepdims=True) - 1.0
    zi = jnp.zeros((SUBLANES - 4, tt), I32)
    oi_ref[...] = jnp.concatenate([e1, e2, rank1.astype(I32), rank2.astype(I32), zi], axis=0)
    of_ref[...] = jnp.concatenate([w1, w2, jnp.zeros((of_ref.shape[0] - 2, tt), F32)], axis=0)
    cnt_ref[...] = jnp.broadcast_to(carry, cnt_ref.shape).astype(I32)


def _route(logits_t, n_groups, per_group):
    nr, t = logits_t.shape
    cb = min(CUMSUM_BLOCK, t)
    tt = max(k for k in range(cb, min(ROUTE_TILE, t) + 1, cb) if t % k == 0)
    n_exp = n_groups * per_group
    return pl.pallas_call(
        functools.partial(_route_kernel, n_groups=n_groups, per_group=per_group),
        grid=(t // tt,),
        in_specs=[pl.BlockSpec((nr, tt), lambda i: (0, i))],
        out_specs=[pl.BlockSpec((SUBLANES, tt), lambda i: (0, i)),
                   pl.BlockSpec((LANES, tt), lambda i: (0, i)),
                   pl.BlockSpec((n_exp, LANES), lambda i: (0, 0))],
        out_shape=[jax.ShapeDtypeStruct((SUBLANES, t), I32),
                   jax.ShapeDtypeStruct((LANES, t), F32),
                   jax.ShapeDtypeStruct((n_exp, LANES), I32)],
        scratch_shapes=[pltpu.VMEM((cb, cb), BF16), pltpu.VMEM((n_exp, 1), F32)],
        compiler_params=_cparams(("arbitrary",)),
        name="route",
    )(logits_t)


def _plan_kernel(ri_ref, cnt_ref, pos_ref, te_ref, nv_ref, off_ref, *, tmx):
    n_exp = cnt_ref.shape[0]
    tt = ri_ref.shape[1]
    shift = tmx.bit_length() - 1
    cnt = cnt_ref[:, 0:1]
    n_tile = jnp.right_shift(cnt + (tmx - 1), shift)
    run = jnp.zeros((1, 1), I32)
    offs = []
    for e in range(n_exp):
        offs.append(run)
        run = run + jnp.left_shift(n_tile[e:e + 1, :], shift)
    off = jnp.concatenate(offs, axis=0)
    eid = lax.broadcasted_iota(I32, (n_exp, tt), 0)
    place = lambda e_row, r_row: jnp.sum(jnp.where(eid == e_row, off, 0), axis=0, keepdims=True) + r_row
    p1 = place(ri_ref[0:1, :], ri_ref[2:3, :])
    p2 = place(ri_ref[1:2, :], ri_ref[3:4, :])
    pos_ref[...] = jnp.concatenate([p1, p2, jnp.zeros((SUBLANES - TOP_K_FINE, tt), I32)], axis=0)
    end_tile = jnp.right_shift(off, shift) + n_tile
    k = lax.broadcasted_iota(I32, (n_exp, te_ref.shape[1]), 1)
    te = jnp.sum((k >= end_tile).astype(I32), axis=0, keepdims=True)
    te_ref[...] = jnp.minimum(te, n_exp - 1)
    nv_ref[...] = jnp.broadcast_to(jnp.right_shift(run, shift), nv_ref.shape)
    off_ref[...] = jnp.broadcast_to(off, off_ref.shape)


def _plan(ri, counts, tmx, n_tiles):
    _, t = ri.shape
    n_exp = counts.shape[0]
    tt = max(k for k in range(LANES, min(ROUTE_TILE, t) + 1, LANES) if t % k == 0)
    ntp = -(-n_tiles // LANES) * LANES
    pos, te, nv, off = pl.pallas_call(
        functools.partial(_plan_kernel, tmx=tmx),
        grid=(t // tt,),
        in_specs=[pl.BlockSpec((SUBLANES, tt), lambda i: (0, i)),
                  pl.BlockSpec((n_exp, LANES), lambda i: (0, 0))],
        out_specs=[pl.BlockSpec((SUBLANES, tt), lambda i: (0, i)),
                   pl.BlockSpec((1, ntp), lambda i: (0, 0)),
                   pl.BlockSpec((1, LANES), lambda i: (0, 0)),
                   pl.BlockSpec((n_exp, LANES), lambda i: (0, 0))],
        out_shape=[jax.ShapeDtypeStruct((SUBLANES, t), I32),
                   jax.ShapeDtypeStruct((1, ntp), I32),
                   jax.ShapeDtypeStruct((1, LANES), I32),
                   jax.ShapeDtypeStruct((n_exp, LANES), I32)],
        compiler_params=_cparams(("arbitrary",)),
        name="plan",
    )(ri, counts)
    return pos[0], pos[1], te[0, :n_tiles], nv[0, :1], off[:, 0]


def _load_positions(pos1_hbm, pos2_hbm, p1_ref, p2_ref, isem, step):
    c1 = pltpu.make_async_copy(pos1_hbm.at[step], p1_ref, isem.at[0])
    c2 = pltpu.make_async_copy(pos2_hbm.at[step], p2_ref, isem.at[1])
    c1.start()
    c2.start()
    c1.wait()
    c2.wait()


def _dispatch_kernel(off_ref, cnt_ref, pos1_hbm, pos2_hbm, h2_ref, xs_hbm, p1_ref, p2_ref, zero_ref,
                     isem, sem, zsem, *, tmx):
    step = pl.program_id(0)
    n_blk = h2_ref.shape[0]
    _load_positions(pos1_hbm, pos2_hbm, p1_ref, p2_ref, isem, step)

    def row_copy(blk, s, p):
        return pltpu.make_async_copy(h2_ref.at[blk, pl.ds(s, 1), :], xs_hbm.at[pl.ds(p, 1), :], sem)

    def send(blk, c):
        j0 = blk * SUBLANES
        for s in range(SUBLANES):
            row_copy(blk, s, p1_ref[j0 + s]).start(priority=0)
            row_copy(blk, s, p2_ref[j0 + s]).start(priority=1)
        return c
    lax.fori_loop(0, n_blk, send, 0, unroll=2)
    _wait_many(row_copy(0, 0, 0), TOP_K_FINE * n_blk * SUBLANES)

    @pl.when(step == pl.num_programs(0) - 1)
    def _():
        zero_ref[...] = jnp.zeros_like(zero_ref)
        shift = tmx.bit_length() - 1

        def pad_expert(e, c):
            cnt = cnt_ref[e]
            n_pad = jnp.left_shift(jnp.right_shift(cnt + (tmx - 1), shift), shift) - cnt
            first = off_ref[e] + cnt
            zero_copy = lambda r: pltpu.make_async_copy(
                zero_ref.at[pl.ds(0, 1), :], xs_hbm.at[pl.ds(first + r, 1), :], zsem)

            def fill(r, c2):
                zero_copy(r).start()
                return c2
            lax.fori_loop(0, n_pad, fill, 0)

            def done(r, c2):
                zero_copy(r).wait()
                return c2
            lax.fori_loop(0, n_pad, done, 0)
            return c
        lax.fori_loop(0, off_ref.shape[0], pad_expert, 0)


def _dispatch(offsets, counts, pos1, pos2, h2p, tmx, ts):
    t, half = h2p.shape
    n_exp = offsets.shape[0]
    p_max = TOP_K_FINE * t + n_exp * tmx
    hbm = lambda: pl.BlockSpec(memory_space=pl.ANY)
    grid_spec = pltpu.PrefetchScalarGridSpec(
        num_scalar_prefetch=2,
        grid=(t // ts,),
        in_specs=[hbm(), hbm(),
                  pl.BlockSpec((ts // SUBLANES, SUBLANES, half), lambda i, off, cnt: (i, 0, 0))],
        out_specs=hbm(),
        scratch_shapes=[pltpu.SMEM((ts,), I32),
                        pltpu.SMEM((ts,), I32),
                        pltpu.VMEM((SUBLANES, half), U32),
                        pltpu.SemaphoreType.DMA((2,)),
                        pltpu.SemaphoreType.DMA(()),
                        pltpu.SemaphoreType.DMA(())])
    return pl.pallas_call(
        functools.partial(_dispatch_kernel, tmx=tmx),
        grid_spec=grid_spec,
        out_shape=jax.ShapeDtypeStruct((p_max, half), U32),
        compiler_params=_cparams(("arbitrary",)),
        name="dispatch",
    )(offsets, counts, pos1.reshape(t // ts, ts), pos2.reshape(t // ts, ts),
      h2p.reshape(t // SUBLANES, SUBLANES, half))


def _expert_kernel(te_ref, nv_ref, x_ref, w1_ref, w3_ref, w2_ref, y_ref):
    @pl.when(pl.program_id(0) < nv_ref[0])
    def _():
        lo, hi = _unpack_bf16_pair(x_ref[...])
        lo = lo.astype(BF16)
        hi = hi.astype(BF16)
        half = lo.shape[1]
        mm = lambda w_ref: (jnp.dot(lo, w_ref[0, :half, :], preferred_element_type=F32)
                            + jnp.dot(hi, w_ref[0, half:, :], preferred_element_type=F32))
        h1 = mm(w1_ref)
        h3 = mm(w3_ref)
        hh = (h1 * jax.nn.sigmoid(h1) * h3).astype(BF16)
        y = jnp.dot(hh, w2_ref[0], preferred_element_type=F32)
        y_ref[...] = _pack_bf16_pair(y[:, :half], y[:, half:])


def _expert_mlp(xs, tile_expert, n_valid, w1, w3, w2, tmx):
    p, half = xs.shape
    d, de = w1.shape[1], w1.shape[2]
    tile = lambda i, te, nv: (jnp.minimum(i, nv[0] - 1), 0)
    grid_spec = pltpu.PrefetchScalarGridSpec(
        num_scalar_prefetch=2,
        grid=(p // tmx,),
        in_specs=[pl.BlockSpec((tmx, half), tile),
                  pl.BlockSpec((1, d, de), lambda i, te, nv: (te[i], 0, 0)),
                  pl.BlockSpec((1, d, de), lambda i, te, nv: (te[i], 0, 0)),
                  pl.BlockSpec((1, de, d), lambda i, te, nv: (te[i], 0, 0))],
        out_specs=pl.BlockSpec((tmx, half), tile))
    return pl.pallas_call(
        _expert_kernel,
        grid_spec=grid_spec,
        out_shape=jax.ShapeDtypeStruct((p, half), U32),
        compiler_params=_cparams(("arbitrary",)),
        name="expert_mlp",
    )(tile_expert, n_valid, xs, w1, w3, w2)


def _combine_kernel(pos1_hbm, pos2_hbm, ys_hbm, x1_ref, rf_ref, g2_ref, fg_ref, ylat_hbm, yctx_hbm,
                    p1_ref, p2_ref, gbuf, obuf, isem, gsem, osem, *, geom):
    step = pl.program_id(0)
    n_steps = pl.num_programs(0)
    tm = geom.tm
    n_blk = tm // SUBLANES
    slot = lax.rem(step, 2)
    _load_positions(pos1_hbm, pos2_hbm, p1_ref, p2_ref, isem, step)

    def row_copy(k, p, blk, s):
        return pltpu.make_async_copy(ys_hbm.at[pl.ds(p, 1), :], gbuf.at[k, blk, pl.ds(s, 1), :], gsem)

    def fetch(blk, c):
        j0 = blk * SUBLANES
        for s in range(SUBLANES):
            row_copy(0, p1_ref[j0 + s], blk, s).start(priority=0)
            row_copy(1, p2_ref[j0 + s], blk, s).start(priority=1)
        return c
    lax.fori_loop(0, n_blk, fetch, 0, unroll=2)
    _wait_many(row_copy(0, 0, 0, 0), TOP_K_FINE * tm)

    wt = rf_ref[...].T
    half = gbuf.shape[-1]
    a_lo, a_hi = _unpack_bf16_pair(gbuf[0].reshape(tm, half))
    b_lo, b_hi = _unpack_bf16_pair(gbuf[1].reshape(tm, half))
    w1 = wt[:, 0:1]
    w2 = wt[:, 1:2]
    moe = jnp.concatenate([w1 * a_lo + w2 * b_lo, w1 * a_hi + w2 * b_hi], axis=1)
    x = x1_ref[...] + _gate(moe, g2_ref[0])
    out = _rms(x, fg_ref[...])

    start = lambda cp: cp.start()
    wait = lambda cp: cp.wait()
    put = functools.partial(_tile_copies, ylat_hbm, yctx_hbm, geom=geom, to_rows=False)

    @pl.when(step >= 2)
    def _():
        put(obuf.at[slot], osem.at[slot], step - 2, fn=wait)

    obuf[slot] = out.reshape(obuf.shape[1:])
    put(obuf.at[slot], osem.at[slot], step, fn=start)

    @pl.when(step == n_steps - 1)
    def _():
        @pl.when(n_steps > 1)
        def _():
            put(obuf.at[1 - slot], osem.at[1 - slot], step - 1, fn=wait)
        put(obuf.at[slot], osem.at[slot], step, fn=wait)


def _combine(pos1, pos2, ys, x1, rf, modtab, final_g, lat_shape, ctx_shape, geom):
    t, d = x1.shape
    tm = geom.tm
    mset = lambda i: jnp.where(i < geom.n_lat_tiles, 0, 1)
    hbm = lambda: pl.BlockSpec(memory_space=pl.ANY)
    return pl.pallas_call(
        functools.partial(_combine_kernel, geom=geom),
        grid=(t // tm,),
        in_specs=[hbm(), hbm(), hbm(),
                  pl.BlockSpec((tm, d), lambda i: (i, 0)),
                  pl.BlockSpec((LANES, tm), lambda i: (0, i)),
                  pl.BlockSpec((1, SUBLANES, d), lambda i: (mset(i), 0, 5)),
                  pl.BlockSpec((1, d), lambda i: (0, 0))],
        out_specs=[hbm(), hbm()],
        out_shape=[jax.ShapeDtypeStruct(lat_shape, F32), jax.ShapeDtypeStruct(ctx_shape, F32)],
        scratch_shapes=[pltpu.SMEM((tm,), I32),
                        pltpu.SMEM((tm,), I32),
                        pltpu.VMEM((TOP_K_FINE, tm // SUBLANES, SUBLANES, d // 2), U32),
                        pltpu.VMEM((2, tm // SUBLANES, SUBLANES, d), F32),
                        pltpu.SemaphoreType.DMA((2,)),
                        pltpu.SemaphoreType.DMA(()),
                        pltpu.SemaphoreType.DMA((2,))],
        compiler_params=_cparams(("arbitrary",)),
        name="combine",
    )(pos1.reshape(t // tm, tm), pos2.reshape(t // tm, tm), ys, x1, rf, modtab, final_g.reshape(1, d))


def _tile_meta(groups, tm):
    rows, first, last, grp = [], [], [], []
    blk = 0
    for g, r in enumerate(groups):
        nc = r // tm
        for c in range(nc):
            rows.append(blk + c)
            first.append(int(c == 0))
            last.append(int(c == nc - 1))
            grp.append(g)
        blk += nc
    fwd = (np.array(rows, np.int32), np.array([first, last, grp], np.int32))
    order = []
    blk = 0
    for r in groups:
        nc = r // tm
        order.extend(range(blk + nc - 1, blk - 1, -1))
        blk += nc
    order = np.array(order)
    bwd = (fwd[0][order], fwd[1][:, order])
    return fwd, bwd


def kernel(x_prompt, x_sample, state_lru, c, c_ctx, w_mod, b_mod, norm1_g, w_in, conv_w, conv_b, lru_wa, lru_ba, lru_wx, lru_bx, lru_lambda, pool_w, pool_scale, w_out, norm2_g, router_coarse_w, router_coarse_b, router_fine_w, router_fine_b, exp_w1, exp_w3, exp_w2, final_norm_g):
    bp, sp, d = x_prompt.shape
    bs, ss, _ = x_sample.shape
    d_lru = lru_lambda.shape[-1]
    heads, bw = lru_wa.shape[2], lru_wa.shape[3]
    n_groups, per_group = router_fine_w.shape[2], router_fine_w.shape[3]
    n_exp = n_groups * per_group
    assert w_mod.shape[0] == 1 and bs == SUBLANES and bp % SUBLANES == 0 and ss % GRID_W == 0
    assert EXPERT_TILE & (EXPERT_TILE - 1) == 0
    n_lat_groups, n_ctx_groups = bs // SUBLANES, bp // SUBLANES
    lat_rows, ctx_rows = ss * SUBLANES, sp * SUBLANES
    assert (n_lat_groups * lat_rows) % ctx_rows == 0
    tm = min(TOKEN_TILE, ctx_rows, lat_rows)
    geom = _Geom(tm=tm, n_lat_tiles=n_lat_groups * lat_rows // tm, lat_chunks=lat_rows // tm,
                 ctx_chunks=ctx_rows // tm)
    n_rows = n_lat_groups * lat_rows + n_ctx_groups * ctx_rows
    groups = [lat_rows] * n_lat_groups + [ctx_rows] * n_ctx_groups
    (f_rows, f_flags), (b_rows, b_flags) = _tile_meta(groups, tm)

    cond = jnp.zeros((2 * SUBLANES, d), F32).at[:bs].set(c).at[bs].set(c_ctx)
    mod = _modulation(cond, w_mod[0], b_mod[0])
    modtab = jnp.stack([mod[:SUBLANES], jnp.broadcast_to(mod[SUBLANES], (SUBLANES, mod.shape[1]))])
    h0_lat = state_lru[:, 0].reshape(n_lat_groups, SUBLANES, 2, d_lru)
    h0 = jnp.concatenate([h0_lat, jnp.zeros((n_ctx_groups, SUBLANES, 2, d_lru), F32)], axis=0)
    h0 = h0.transpose(0, 2, 1, 3)

    xa, ga, z = _input_projection(x_sample, x_prompt, modtab, norm1_g[0], w_in[0].astype(BF16),
                                  pool_w[0].astype(BF16), n_rows, geom)

    def gate_weights(direction):
        wg = jnp.concatenate([lru_wa[0, direction], lru_wx[0, direction]], axis=-1).astype(BF16)
        bg = jnp.concatenate([lru_ba[0, direction].reshape(heads, 1, bw),
                              lru_bx[0, direction].reshape(heads, 1, bw)], axis=-1)
        return wg, bg

    log_decay = jax.nn.log_sigmoid(lru_lambda[0])
    wg_f, bg_f = gate_weights(0)
    wg_b, bg_b = gate_weights(1)
    hf, hf_last = _forward_scan(xa, (jnp.asarray(f_rows), jnp.asarray(f_flags)), conv_w[0], conv_b[0],
                                wg_f, bg_f, log_decay[0], h0[:, 0], tm)

    yb = _pool(z, None, pool_scale[0], lat_rows, n_lat_groups, 0, lat_rows // (GRID_W * SUBLANES))
    yb = _pool(z, yb, pool_scale[0], ctx_rows, n_ctx_groups, n_lat_groups * lat_rows // ctx_rows, None)

    n_logits = n_groups + n_exp
    rw = jnp.concatenate([router_coarse_w[0], router_fine_w[0].reshape(d, n_exp)], axis=1)
    rwt = jnp.zeros((LANES, d), BF16).at[:n_logits].set(rw.T.astype(BF16))
    rb = jnp.zeros((LANES, 1), F32).at[:n_logits, 0].set(
        jnp.concatenate([router_coarse_b[0], router_fine_b[0].reshape(n_exp)]))
    x1, h2p, logits_t, hb_last = _backward_scan_mix(
        xa, ga, hf, yb, x_sample, x_prompt, modtab, (jnp.asarray(b_rows), jnp.asarray(b_flags)),
        conv_w[0], conv_b[0], wg_b, bg_b, log_decay[1], h0[:, 1], w_out[0].astype(BF16), norm2_g[0],
        rwt, rb, geom)

    ri, rf, counts = _route(logits_t, n_groups, per_group)
    n_tiles = (TOP_K_FINE * n_rows + n_exp * EXPERT_TILE) // EXPERT_TILE
    pos1, pos2, tile_expert, n_valid, offsets = _plan(ri, counts, EXPERT_TILE, n_tiles)
    xs = _dispatch(offsets, counts[:, 0], pos1, pos2, h2p, EXPERT_TILE, tm)
    ys = _expert_mlp(xs, tile_expert, n_valid, exp_w1[0].astype(BF16), exp_w3[0].astype(BF16),
                     exp_w2[0].astype(BF16), EXPERT_TILE)
    y_sample, y_prompt = _combine(pos1, pos2, ys, x1, rf, modtab, final_norm_g, x_sample.shape,
                                  x_prompt.shape, geom)

    st = jnp.stack([hf_last[n_lat_groups:], hb_last[n_lat_groups:]], axis=2)
    state_new = st.reshape(bp, 1, 2, d_lru).astype(x_prompt.dtype)
    return (y_prompt, y_sample, state_new)
```

```python
import functools
from typing import NamedTuple

import numpy as np
import jax
import jax.numpy as jnp
from jax import lax
from jax.experimental import pallas as pl
from jax.experimental.pallas import tpu as pltpu

GRID_W = 64
CONV_W = 4
RG_C = 8.0
POOL_WINDOWS = (2, 4, 8, 16)
TOP_K_FINE = 2
EPS = 1e-6
EXPM1_SERIES_BELOW = 0.125

SUBLANES = 8
LANES = 128
BF16_ROWS = 16
TOKEN_TILE = 512
EXPERT_TILE = 512
ROUTE_TILE = 2048
CUMSUM_BLOCK = 512
MOD_COL_TILE = 1024
WEIGHT_STAGE_ROWS = 256
WAIT_UNROLL = 32
VMEM_LIMIT = 60 * 1024 * 1024

F32 = jnp.float32
BF16 = jnp.bfloat16
U32 = jnp.uint32
I32 = jnp.int32
HIGH_HALF = 0xFFFF0000


class _Geom(NamedTuple):
    tm: int
    n_lat_tiles: int
    lat_chunks: int
    ctx_chunks: int


def _cparams(sem):
    return pltpu.CompilerParams(dimension_semantics=sem, vmem_limit_bytes=VMEM_LIMIT)


def _per_sequence(y, m):
    rows, d = y.shape
    return y.reshape(rows // SUBLANES, SUBLANES, d), m[None]


def _modulate(y, scale, shift):
    y3, sc = _per_sequence(y, scale)
    _, sh = _per_sequence(y, shift)
    return (y3 * (1.0 + sc) + sh).reshape(y.shape)


def _gate(y, g):
    y3, g3 = _per_sequence(y, g)
    return (y3 * g3).reshape(y.shape)


def _rms(x, g):
    ms = jnp.mean(x * x, axis=-1, keepdims=True)
    return x * lax.rsqrt(ms + EPS) * g


def _pack_bf16_pair(lo, hi):
    lo_bits = lax.shift_right_logical(lax.bitcast_convert_type(lo.astype(BF16).astype(F32), U32), U32(16))
    hi_bits = lax.bitcast_convert_type(hi.astype(BF16).astype(F32), U32) & U32(HIGH_HALF)
    return lo_bits | hi_bits


def _unpack_bf16_pair(w):
    lo = lax.bitcast_convert_type(lax.shift_left(w, U32(16)), F32)
    hi = lax.bitcast_convert_type(w & U32(HIGH_HALF), F32)
    return lo, hi


def _wait_many(copy, n):
    def body(_, c):
        for _ in range(WAIT_UNROLL):
            copy.wait()
        return c
    lax.fori_loop(0, n // WAIT_UNROLL, body, 0)


def _tile_copies(lat_hbm, ctx_hbm, buf, sem, tile, geom, to_rows, fn):
    tt = geom.tm // SUBLANES

    def run(hbm, k, chunks):
        g = lax.div(k, jnp.int32(chunks))
        c = lax.rem(k, jnp.int32(chunks))
        for b in range(SUBLANES):
            h = hbm.at[g * SUBLANES + b, pl.ds(c * tt, tt), :]
            v = buf.at[:, b, :]
            fn(pltpu.make_async_copy(h, v, sem) if to_rows else pltpu.make_async_copy(v, h, sem))

    @pl.when(tile < geom.n_lat_tiles)
    def _():
        run(lat_hbm, tile, geom.lat_chunks)

    @pl.when(tile >= geom.n_lat_tiles)
    def _():
        run(ctx_hbm, tile - geom.n_lat_tiles, geom.ctx_chunks)


def _fetch_rows(lat_hbm, ctx_hbm, xbuf, sem, step, n_steps, tile_of, geom):
    slot = lax.rem(step, 2)
    start = lambda cp: cp.start()
    wait = lambda cp: cp.wait()

    @pl.when(step == 0)
    def _():
        _tile_copies(lat_hbm, ctx_hbm, xbuf.at[0], sem.at[0], tile_of(jnp.int32(0)), geom, True, start)

    @pl.when(step + 1 < n_steps)
    def _():
        nxt = 1 - slot
        _tile_copies(lat_hbm, ctx_hbm, xbuf.at[nxt], sem.at[nxt], tile_of(step + 1), geom, True, start)

    _tile_copies(lat_hbm, ctx_hbm, xbuf.at[slot], sem.at[slot], tile_of(step), geom, True, wait)
    tt, _, d = xbuf.shape[1:]
    return xbuf[slot].reshape(tt * SUBLANES, d)


def _mod_kernel(c_ref, w_ref, b_ref, o_ref):
    c = c_ref[...]
    s = c * jax.nn.sigmoid(c)
    o_ref[...] = jnp.dot(s.astype(BF16), w_ref[...].astype(BF16),
                         preferred_element_type=F32) + b_ref[...]


def _modulation(cond, w_mod, b_mod):
    rows, d = cond.shape
    n = w_mod.shape[1]
    tn = min(MOD_COL_TILE, n)
    return pl.pallas_call(
        _mod_kernel,
        grid=(n // tn,),
        in_specs=[pl.BlockSpec((rows, d), lambda j: (0, 0)),
                  pl.BlockSpec((d, tn), lambda j: (0, j)),
                  pl.BlockSpec((1, tn), lambda j: (0, j))],
        out_specs=pl.BlockSpec((rows, tn), lambda j: (0, j)),
        out_shape=jax.ShapeDtypeStruct((rows, n), F32),
        compiler_params=_cparams(("arbitrary",)),
        name="modulation",
    )(cond, w_mod, b_mod.reshape(1, n))


def _round_weights(w_hbm, wbf_ref, stage, sem):
    rows = stage.shape[1]
    n = w_hbm.shape[0] // rows
    copy = lambda r: pltpu.make_async_copy(w_hbm.at[pl.ds(r * rows, rows), :], stage.at[r % 2], sem.at[r % 2])
    copy(0).start()
    for r in range(n):
        if r + 1 < n:
            copy(r + 1).start()
        copy(r).wait()
        wbf_ref[r * rows:(r + 1) * rows, :] = stage[r % 2].astype(wbf_ref.dtype)


def _proj_kernel(lat_hbm, ctx_hbm, sh_ref, sc_ref, g_ref, win_hbm, pw_ref, xa_ref, ga_ref, z_ref,
                 xbuf, sem, wbf_ref, wstage, wsem, *, d_lru, gw, geom):
    step = pl.program_id(0)

    @pl.when(step == 0)
    def _():
        _round_weights(win_hbm, wbf_ref, wstage, wsem)

    x = _fetch_rows(lat_hbm, ctx_hbm, xbuf, sem, step, pl.num_programs(0), lambda s: s, geom)
    h = _modulate(_rms(x, g_ref[...]), sc_ref[0], sh_ref[0])
    proj = jnp.dot(h.astype(BF16), wbf_ref[...], preferred_element_type=F32)
    xa_ref[...] = proj[:, :d_lru].astype(BF16)
    ga_ref[...] = proj[:, d_lru:2 * d_lru].astype(BF16)
    for g in range(pw_ref.shape[0]):
        lo = 2 * d_lru + g * gw
        z_ref[:, g * gw:(g + 1) * gw] = jnp.dot(
            proj[:, lo:lo + gw].astype(BF16), pw_ref[g], preferred_element_type=F32).astype(BF16)


def _input_projection(x_lat, x_ctx, modtab, norm_g, w_in, pool_w, n_rows, geom):
    d = x_lat.shape[-1]
    tm = geom.tm
    d_pool = pool_w.shape[0] * pool_w.shape[1]
    d_lru = (w_in.shape[1] - d_pool) // 2
    mset = lambda i: jnp.where(i < geom.n_lat_tiles, 0, 1)
    const = dict(pipeline_mode=pl.Buffered(1))
    return pl.pallas_call(
        functools.partial(_proj_kernel, d_lru=d_lru, gw=pool_w.shape[1], geom=geom),
        grid=(n_rows // tm,),
        in_specs=[pl.BlockSpec(memory_space=pl.ANY),
                  pl.BlockSpec(memory_space=pl.ANY),
                  pl.BlockSpec((1, SUBLANES, d), lambda i: (mset(i), 0, 0)),
                  pl.BlockSpec((1, SUBLANES, d), lambda i: (mset(i), 0, 1)),
                  pl.BlockSpec((1, d), lambda i: (0, 0)),
                  pl.BlockSpec(memory_space=pl.ANY),
                  pl.BlockSpec(pool_w.shape, lambda i: (0, 0, 0), **const)],
        out_specs=[pl.BlockSpec((tm, d_lru), lambda i: (i, 0)),
                   pl.BlockSpec((tm, d_lru), lambda i: (i, 0)),
                   pl.BlockSpec((tm, d_pool), lambda i: (i, 0))],
        out_shape=[jax.ShapeDtypeStruct((n_rows, d_lru), BF16),
                   jax.ShapeDtypeStruct((n_rows, d_lru), BF16),
                   jax.ShapeDtypeStruct((n_rows, d_pool), BF16)],
        scratch_shapes=[pltpu.VMEM((2, tm // SUBLANES, SUBLANES, d), F32),
                        pltpu.SemaphoreType.DMA((2,)),
                        pltpu.VMEM(w_in.shape, BF16),
                        pltpu.VMEM((2, min(WEIGHT_STAGE_ROWS, d), w_in.shape[1]), F32),
                        pltpu.SemaphoreType.DMA((2,))],
        compiler_params=_cparams(("arbitrary",)),
        name="input_projection",
    )(x_lat, x_ctx, modtab, modtab, norm_g.reshape(1, d), w_in, pool_w)


def _fill_ext(ext_ref, prev_ref, main_ref, next_ref, first, last):
    tm = main_ref.shape[0]
    prev = prev_ref[...].astype(F32)
    nxt = next_ref[...].astype(F32)
    ext_ref[0:BF16_ROWS, :] = jnp.where(first, 0.0, prev)
    ext_ref[BF16_ROWS:BF16_ROWS + tm, :] = main_ref[...].astype(F32)
    ext_ref[BF16_ROWS + tm:, :] = jnp.where(last, 0.0, nxt)


def _one_minus_exp(y, exp_y):
    p = 1.0 / 120.0
    for c in (1.0 / 24.0, 1.0 / 6.0, 0.5, 1.0):
        p = p * y + c
    return jnp.where(y > -EXPM1_SERIES_BELOW, -y * p, 1.0 - exp_y)


def _sqrt_nonneg(q):
    return jnp.where(q > 0.0, q * lax.rsqrt(q), 0.0)


def _decay_and_input(ext_ref, cw_ref, cb_ref, wg_ref, bg_ref, lam_ref, a_ref, u_ref, tm):
    heads, bw = wg_ref.shape[0], wg_ref.shape[1]
    for hd in range(heads):
        sl = slice(hd * bw, (hd + 1) * bw)
        xc = cb_ref[:, sl]
        for k in range(CONV_W):
            xc = xc + cw_ref[k:k + 1, sl] * ext_ref[SUBLANES * k:SUBLANES * k + tm, sl]
        g = jnp.dot(xc.astype(BF16), wg_ref[hd], preferred_element_type=F32) + bg_ref[hd]
        r = jax.nn.sigmoid(g[:, :bw])
        ig = jax.nn.sigmoid(g[:, bw:])
        log_a = (RG_C * r) * lam_ref[:, sl]
        a = jnp.exp(log_a)
        a_ref[:, sl] = a
        u_ref[:, sl] = _sqrt_nonneg(_one_minus_exp(2.0 * log_a, a * a)) * (ig * xc)


def _scan(a_ref, u_ref, h, tm, reverse):
    nblk = tm // SUBLANES

    def body(s, h):
        j = (nblk - 1 - s) if reverse else s
        rows = pl.ds(pl.multiple_of(j * SUBLANES, SUBLANES), SUBLANES)
        h = a_ref[rows, :] * h + u_ref[rows, :]
        u_ref[rows, :] = h
        return h

    return lax.fori_loop(0, nblk, body, h, unroll=8)


def _halo_specs(tm, d_lru, n_rows):
    per = tm // BF16_ROWS
    last_blk = n_rows // BF16_ROWS - 1
    return [pl.BlockSpec((BF16_ROWS, d_lru), lambda i, tr, fl: (jnp.maximum(tr[i] * per - 1, 0), 0)),
            pl.BlockSpec((tm, d_lru), lambda i, tr, fl: (tr[i], 0)),
            pl.BlockSpec((BF16_ROWS, d_lru), lambda i, tr, fl: (jnp.minimum((tr[i] + 1) * per, last_blk), 0))]


def _fwd_kernel(tr_ref, fl_ref, prev_ref, main_ref, next_ref, cw_ref, cb_ref, wg_ref, bg_ref, lam_ref,
                h0_ref, hf_ref, hlast_ref, ext_ref, a_ref, u_ref, h_ref):
    i = pl.program_id(0)
    tm = main_ref.shape[0]
    first = fl_ref[0, i] == 1
    last = fl_ref[1, i] == 1
    _fill_ext(ext_ref, prev_ref, main_ref, next_ref, first, last)
    _decay_and_input(ext_ref, cw_ref, cb_ref, wg_ref, bg_ref, lam_ref, a_ref, u_ref, tm)

    @pl.when(first)
    def _():
        h_ref[...] = h0_ref[0]

    h = _scan(a_ref, u_ref, h_ref[...], tm, reverse=False)
    h_ref[...] = h
    hlast_ref[0] = h
    hf_ref[...] = u_ref[...].astype(BF16)


def _forward_scan(xa, meta, conv_w, conv_b, wg, bg, lam, h0, tm):
    t, d_lru = xa.shape
    tile_row, flags = meta
    ngrp = h0.shape[0]
    const = dict(pipeline_mode=pl.Buffered(1))
    grid_spec = pltpu.PrefetchScalarGridSpec(
        num_scalar_prefetch=2,
        grid=(t // tm,),
        in_specs=_halo_specs(tm, d_lru, t) + [
            pl.BlockSpec(conv_w.shape, lambda i, tr, fl: (0, 0)),
            pl.BlockSpec((1, d_lru), lambda i, tr, fl: (0, 0)),
            pl.BlockSpec(wg.shape, lambda i, tr, fl: (0, 0, 0), **const),
            pl.BlockSpec(bg.shape, lambda i, tr, fl: (0, 0, 0)),
            pl.BlockSpec((1, d_lru), lambda i, tr, fl: (0, 0)),
            pl.BlockSpec((1, SUBLANES, d_lru), lambda i, tr, fl: (fl[2, i], 0, 0))],
        out_specs=[pl.BlockSpec((tm, d_lru), lambda i, tr, fl: (tr[i], 0)),
                   pl.BlockSpec((1, SUBLANES, d_lru), lambda i, tr, fl: (fl[2, i], 0, 0))],
        scratch_shapes=[pltpu.VMEM((tm + 2 * BF16_ROWS, d_lru), F32),
                        pltpu.VMEM((tm, d_lru), F32),
                        pltpu.VMEM((tm, d_lru), F32),
                        pltpu.VMEM((SUBLANES, d_lru), F32)])
    return pl.pallas_call(
        _fwd_kernel,
        grid_spec=grid_spec,
        out_shape=[jax.ShapeDtypeStruct((t, d_lru), BF16),
                   jax.ShapeDtypeStruct((ngrp, SUBLANES, d_lru), F32)],
        compiler_params=_cparams(("arbitrary",)),
        name="forward_scan",
    )(tile_row, flags, xa, xa, xa, conv_w, conv_b.reshape(1, d_lru), wg, bg, lam.reshape(1, d_lru), h0)


def _bwd_kernel(tr_ref, fl_ref, prev_ref, main_ref, next_ref, cw_ref, cb_ref, wg_ref, bg_ref, lam_ref,
                h0_ref, hf_ref, ga_ref, yb_ref, lat_hbm, ctx_hbm, g1_ref, sh2_ref, sc2_ref, wout_ref,
                n2_ref, rwt_ref, rb_ref,
                x1_ref, h2_ref, lg_ref, hlast_ref,
                ext_ref, a_ref, u_ref, h_ref, cat_ref, xbuf, xsem, *, geom):
    i = pl.program_id(0)
    n_steps = pl.num_programs(0)
    tm, d_lru = main_ref.shape
    x = _fetch_rows(lat_hbm, ctx_hbm, xbuf, xsem, i, n_steps,
                    lambda s: tr_ref[jnp.minimum(s, n_steps - 1)], geom)
    first = fl_ref[0, i] == 1
    last = fl_ref[1, i] == 1
    _fill_ext(ext_ref, prev_ref, main_ref, next_ref, first, last)
    _decay_and_input(ext_ref, cw_ref, cb_ref, wg_ref, bg_ref, lam_ref, a_ref, u_ref, tm)

    @pl.when(last)
    def _():
        h_ref[...] = h0_ref[0]

    h = _scan(a_ref, u_ref, h_ref[...], tm, reverse=True)
    h_ref[...] = h
    hlast_ref[0] = h

    ga = ga_ref[...].astype(F32)
    ya = (hf_ref[...].astype(F32) + u_ref[...]) * jax.nn.gelu(ga)
    cat_ref[:, :d_lru] = ya.astype(BF16)
    cat_ref[:, d_lru:] = yb_ref[...]
    mix = jnp.dot(cat_ref[...], wout_ref[...], preferred_element_type=F32)
    x1 = x + _gate(mix, g1_ref[0])
    x1_ref[...] = x1
    h2 = _modulate(_rms(x1, n2_ref[...]), sc2_ref[0], sh2_ref[0])
    half = h2.shape[1] // 2
    h2_ref[...] = _pack_bf16_pair(h2[:, :half], h2[:, half:])
    lg_ref[...] = lax.dot_general(rwt_ref[...], h2.astype(BF16), (((1,), (1,)), ((), ())),
                                  preferred_element_type=F32) + rb_ref[...]


def _backward_scan_mix(xa, ga, hf, yb, x_lat, x_ctx, modtab, meta, conv_w, conv_b, wg, bg, lam, h0, w_out,
                       norm2_g, rwt, rb, geom):
    t, d_lru = xa.shape
    d = x_lat.shape[-1]
    d_pool = yb.shape[1]
    tm = geom.tm
    tile_row, flags = meta
    ngrp = h0.shape[0]
    nr = rwt.shape[0]
    const = dict(pipeline_mode=pl.Buffered(1))
    mset = lambda i, tr, fl: jnp.where(tr[i] < geom.n_lat_tiles, 0, 1)
    row = lambda i, tr, fl: (tr[i], 0)
    mod = lambda col: pl.BlockSpec((1, SUBLANES, d), lambda i, tr, fl: (mset(i, tr, fl), 0, col))
    grid_spec = pltpu.PrefetchScalarGridSpec(
        num_scalar_prefetch=2,
        grid=(t // tm,),
        in_specs=_halo_specs(tm, d_lru, t) + [
            pl.BlockSpec(conv_w.shape, lambda i, tr, fl: (0, 0)),
            pl.BlockSpec((1, d_lru), lambda i, tr, fl: (0, 0)),
            pl.BlockSpec(wg.shape, lambda i, tr, fl: (0, 0, 0), **const),
            pl.BlockSpec(bg.shape, lambda i, tr, fl: (0, 0, 0)),
            pl.BlockSpec((1, d_lru), lambda i, tr, fl: (0, 0)),
            pl.BlockSpec((1, SUBLANES, d_lru), lambda i, tr, fl: (fl[2, i], 0, 0)),
            pl.BlockSpec((tm, d_lru), row),
            pl.BlockSpec((tm, d_lru), row),
            pl.BlockSpec((tm, d_pool), row),
            pl.BlockSpec(memory_space=pl.ANY),
            pl.BlockSpec(memory_space=pl.ANY),
            mod(2), mod(3), mod(4),
            pl.BlockSpec(w_out.shape, lambda i, tr, fl: (0, 0), **const),
            pl.BlockSpec((1, d), lambda i, tr, fl: (0, 0)),
            pl.BlockSpec(rwt.shape, lambda i, tr, fl: (0, 0)),
            pl.BlockSpec((nr, 1), lambda i, tr, fl: (0, 0))],
        out_specs=[pl.BlockSpec((tm, d), row),
                   pl.BlockSpec((tm, d // 2), row),
                   pl.BlockSpec((nr, tm), lambda i, tr, fl: (0, tr[i])),
                   pl.BlockSpec((1, SUBLANES, d_lru), lambda i, tr, fl: (fl[2, i], 0, 0))],
        scratch_shapes=[pltpu.VMEM((tm + 2 * BF16_ROWS, d_lru), F32),
                        pltpu.VMEM((tm, d_lru), F32),
                        pltpu.VMEM((tm, d_lru), F32),
                        pltpu.VMEM((SUBLANES, d_lru), F32),
                        pltpu.VMEM((tm, d_lru + d_pool), BF16),
                        pltpu.VMEM((2, tm // SUBLANES, SUBLANES, d), F32),
                        pltpu.SemaphoreType.DMA((2,))])
    return pl.pallas_call(
        functools.partial(_bwd_kernel, geom=geom),
        grid_spec=grid_spec,
        out_shape=[jax.ShapeDtypeStruct((t, d), F32),
                   jax.ShapeDtypeStruct((t, d // 2), U32),
                   jax.ShapeDtypeStruct((nr, t), F32),
                   jax.ShapeDtypeStruct((ngrp, SUBLANES, d_lru), F32)],
        compiler_params=_cparams(("arbitrary",)),
        name="backward_scan_mix",
    )(tile_row, flags, xa, xa, xa, conv_w, conv_b.reshape(1, d_lru), wg, bg, lam.reshape(1, d_lru), h0,
      hf, ga, yb, x_lat, x_ctx, modtab, modtab, modtab, w_out, norm2_g.reshape(1, d), rwt, rb)


def _shift_rows(v, k):
    if k == 0:
        return v
    z = jnp.zeros((abs(k) * SUBLANES, v.shape[1]), v.dtype)
    if k > 0:
        return jnp.concatenate([z, v[:-k * SUBLANES]], axis=0)
    return jnp.concatenate([v[-k * SUBLANES:], z], axis=0)


def _run_sum(v, m, direction):
    if m & (m - 1) == 0:
        k = 1
        while k < m:
            v = v + _shift_rows(v, -direction * k)
            k *= 2
        return v
    out = v
    for j in range(1, m):
        out = out + _shift_rows(v, -direction * j)
    return out


def _box_sum(v, w):
    lo = w // 2
    hi = w - 1 - lo
    s = _run_sum(v, hi + 1, +1)
    if lo:
        s = s + _shift_rows(_run_sum(v, lo, -1), 1)
    return s


def _window_count(n_rows, lanes, n_pos, w):
    lo = w // 2
    hi = w - 1 - lo
    p = lax.shift_right_logical(lax.broadcasted_iota(I32, (n_rows, lanes), 0), 3)
    return (jnp.minimum(p + hi + 1, n_pos) - jnp.maximum(p - lo, 0)).astype(F32)


def _pool_kernel(z_ref, ps_ref, o_ref, v_ref, *, tiles_per_group, grid_rows, grid_cols):
    group = pl.program_id(1) // tiles_per_group
    lanes = z_ref.shape[1]
    ps = ps_ref[...]

    def pool_1d(w):
        z = z_ref[...].astype(F32)
        n = z.shape[0] // SUBLANES
        mean = _box_sum(z, w) / _window_count(z.shape[0], lanes, n, w)
        o_ref[...] = ((mean - z) * ps).astype(o_ref.dtype)

    def pool_2d(w):
        lo = w // 2
        hi = w - 1 - lo
        blk = grid_cols * SUBLANES
        cw = _window_count(blk, lanes, grid_cols, w)

        def zrow(r):
            return z_ref[pl.ds(pl.multiple_of(r * blk, blk), blk), :].astype(F32)

        v = jnp.zeros((blk, lanes), F32)
        for r in range(hi):
            v = v + zrow(r)
        v_ref[...] = v

        def body(r, carry):
            add = r + hi
            sub = r - lo - 1
            v = v_ref[...]
            v = v + jnp.where(add < grid_rows, zrow(jnp.minimum(add, grid_rows - 1)), 0.0)
            v = v - jnp.where(sub >= 0, zrow(jnp.maximum(sub, 0)), 0.0)
            v_ref[...] = v
            ch = (jnp.minimum(r + hi + 1, grid_rows) - jnp.maximum(r - lo, 0)).astype(F32)
            mean = _box_sum(v, w) / (ch * cw)
            o_ref[pl.ds(pl.multiple_of(r * blk, blk), blk), :] = ((mean - zrow(r)) * ps).astype(o_ref.dtype)
            return carry

        lax.fori_loop(0, grid_rows, body, 0)

    for g, w in enumerate(POOL_WINDOWS):
        @pl.when(group == g)
        def _(w=w):
            if grid_rows is None:
                pool_1d(w)
            else:
                pool_2d(w)


def _pool(z, prev_out, pool_scale, rows_per_group, n_groups, first_block, grid_rows):
    t, d_pool = z.shape
    gw = d_pool // len(POOL_WINDOWS)
    lanes = LANES
    blk = GRID_W * SUBLANES
    kern = functools.partial(_pool_kernel, tiles_per_group=gw // lanes, grid_rows=grid_rows, grid_cols=GRID_W)
    in_specs = [pl.BlockSpec((rows_per_group, lanes), lambda g, j: (first_block + g, j)),
                pl.BlockSpec((1, lanes), lambda g, j: (0, j))]
    args = [z, pool_scale.reshape(1, d_pool)]
    aliases = {}
    if prev_out is not None:
        in_specs.append(pl.BlockSpec(memory_space=pl.ANY))
        args.append(prev_out)
        aliases = {2: 0}
        kern_fn = lambda z_ref, ps_ref, prev_ref, o_ref, v_ref: kern(z_ref, ps_ref, o_ref, v_ref)
    else:
        kern_fn = kern
    return pl.pallas_call(
        kern_fn,
        grid=(n_groups, d_pool // lanes),
        in_specs=in_specs,
        out_specs=pl.BlockSpec((rows_per_group, lanes), lambda g, j: (first_block + g, j)),
        out_shape=jax.ShapeDtypeStruct((t, d_pool), BF16),
        scratch_shapes=[pltpu.VMEM((blk, lanes), F32)],
        input_output_aliases=aliases,
        compiler_params=_cparams(("arbitrary", "arbitrary")),
        name="pool_grid" if grid_rows is not None else "pool_seq",
    )(*args)


def _route_kernel(lg_ref, oi_ref, of_ref, cnt_ref, tri_ref, carry_ref, *, n_groups, per_group):
    step = pl.program_id(0)
    tt = lg_ref.shape[1]
    n_exp = n_groups * per_group

    @pl.when(step == 0)
    def _():
        r = lax.broadcasted_iota(I32, tri_ref.shape, 0)
        c = lax.broadcasted_iota(I32, tri_ref.shape, 1)
        tri_ref[...] = (r <= c).astype(BF16)
        carry_ref[...] = jnp.zeros_like(carry_ref)

    row = lambda k: lg_ref[k:k + 1, :]
    cmax = row(0)
    gi = jnp.zeros((1, tt), I32)
    for g in range(1, n_groups):
        better = row(g) > cmax
        gi = jnp.where(better, g, gi)
        cmax = jnp.where(better, row(g), cmax)
    denom = jnp.zeros((1, tt), F32)
    for g in range(n_groups):
        denom = denom + jnp.exp(row(g) - cmax)
    pg = 1.0 / denom
    fine = []
    for j in range(per_group):
        f = row(n_groups + j)
        for g in range(1, n_groups):
            f = jnp.where(gi == g, row(n_groups + g * per_group + j), f)
        fine.append(f)
    v1 = fine[0]
    i1 = jnp.zeros((1, tt), I32)
    for j in range(1, per_group):
        better = fine[j] > v1
        i1 = jnp.where(better, j, i1)
        v1 = jnp.where(better, fine[j], v1)
    v2 = jnp.full((1, tt), -jnp.inf, F32)
    i2 = jnp.zeros((1, tt), I32)
    for j in range(per_group):
        better = jnp.logical_and(i1 != j, fine[j] > v2)
        i2 = jnp.where(better, j, i2)
        v2 = jnp.where(better, fine[j], v2)
    ex = jnp.exp(v2 - v1)
    w1 = (1.0 / (1.0 + ex)) * pg
    w2 = (ex / (1.0 + ex)) * pg
    e1 = gi * per_group + i1
    e2 = gi * per_group + i2
    eid = lax.broadcasted_iota(I32, (n_exp, tt), 0)
    hit1 = eid == e1
    hit2 = eid == e2
    member = jnp.logical_or(hit1, hit2).astype(BF16)
    cb = tri_ref.shape[0]
    carry = carry_ref[...]
    cums = []
    for s in range(tt // cb):
        c = jnp.dot(member[:, s * cb:(s + 1) * cb], tri_ref[...], preferred_element_type=F32) + carry
        carry = c[:, cb - 1:cb]
        cums.append(c)
    carry_ref[...] = carry
    cum = jnp.concatenate(cums, axis=1)
    rank1 = jnp.sum(jnp.where(hit1, cum, 0.0), axis=0, keepdims=True) - 1.0
    rank2 = jnp.sum(jnp.where(hit2, cum, 0.0), axis=0, keepdims=True) - 1.0
    zi = jnp.zeros((SUBLANES - 4, tt), I32)
    oi_ref[...] = jnp.concatenate([e1, e2, rank1.astype(I32), rank2.astype(I32), zi], axis=0)
    of_ref[...] = jnp.concatenate([w1, w2, jnp.zeros((of_ref.shape[0] - 2, tt), F32)], axis=0)
    cnt_ref[...] = jnp.broadcast_to(carry, cnt_ref.shape).astype(I32)


def _route(logits_t, n_groups, per_group):
    nr, t = logits_t.shape
    cb = min(CUMSUM_BLOCK, t)
    tt = max(k for k in range(cb, min(ROUTE_TILE, t) + 1, cb) if t % k == 0)
    n_exp = n_groups * per_group
    return pl.pallas_call(
        functools.partial(_route_kernel, n_groups=n_groups, per_group=per_group),
        grid=(t // tt,),
        in_specs=[pl.BlockSpec((nr, tt), lambda i: (0, i))],
        out_specs=[pl.BlockSpec((SUBLANES, tt), lambda i: (0, i)),
                   pl.BlockSpec((LANES, tt), lambda i: (0, i)),
                   pl.BlockSpec((n_exp, LANES), lambda i: (0, 0))],
        out_shape=[jax.ShapeDtypeStruct((SUBLANES, t), I32),
                   jax.ShapeDtypeStruct((LANES, t), F32),
                   jax.ShapeDtypeStruct((n_exp, LANES), I32)],
        scratch_shapes=[pltpu.VMEM((cb, cb), BF16), pltpu.VMEM((n_exp, 1), F32)],
        compiler_params=_cparams(("arbitrary",)),
        name="route",
    )(logits_t)


def _plan_kernel(ri_ref, cnt_ref, pos_ref, te_ref, nv_ref, off_ref, *, tmx):
    n_exp = cnt_ref.shape[0]
    tt = ri_ref.shape[1]
    shift = tmx.bit_length() - 1
    cnt = cnt_ref[:, 0:1]
    n_tile = jnp.right_shift(cnt + (tmx - 1), shift)
    run = jnp.zeros((1, 1), I32)
    offs = []
    for e in range(n_exp):
        offs.append(run)
        run = run + jnp.left_shift(n_tile[e:e + 1, :], shift)
    off = jnp.concatenate(offs, axis=0)
    eid = lax.broadcasted_iota(I32, (n_exp, tt), 0)
    place = lambda e_row, r_row: jnp.sum(jnp.where(eid == e_row, off, 0), axis=0, keepdims=True) + r_row
    p1 = place(ri_ref[0:1, :], ri_ref[2:3, :])
    p2 = place(ri_ref[1:2, :], ri_ref[3:4, :])
    pos_ref[...] = jnp.concatenate([p1, p2, jnp.zeros((SUBLANES - TOP_K_FINE, tt), I32)], axis=0)
    end_tile = jnp.right_shift(off, shift) + n_tile
    k = lax.broadcasted_iota(I32, (n_exp, te_ref.shape[1]), 1)
    te = jnp.sum((k >= end_tile).astype(I32), axis=0, keepdims=True)
    te_ref[...] = jnp.minimum(te, n_exp - 1)
    nv_ref[...] = jnp.broadcast_to(jnp.right_shift(run, shift), nv_ref.shape)
    off_ref[...] = jnp.broadcast_to(off, off_ref.shape)


def _plan(ri, counts, tmx, n_tiles):
    _, t = ri.shape
    n_exp = counts.shape[0]
    tt = max(k for k in range(LANES, min(ROUTE_TILE, t) + 1, LANES) if t % k == 0)
    ntp = -(-n_tiles // LANES) * LANES
    pos, te, nv, off = pl.pallas_call(
        functools.partial(_plan_kernel, tmx=tmx),
        grid=(t // tt,),
        in_specs=[pl.BlockSpec((SUBLANES, tt), lambda i: (0, i)),
                  pl.BlockSpec((n_exp, LANES), lambda i: (0, 0))],
        out_specs=[pl.BlockSpec((SUBLANES, tt), lambda i: (0, i)),
                   pl.BlockSpec((1, ntp), lambda i: (0, 0)),
                   pl.BlockSpec((1, LANES), lambda i: (0, 0)),
                   pl.BlockSpec((n_exp, LANES), lambda i: (0, 0))],
        out_shape=[jax.ShapeDtypeStruct((SUBLANES, t), I32),
                   jax.ShapeDtypeStruct((1, ntp), I32),
                   jax.ShapeDtypeStruct((1, LANES), I32),
                   jax.ShapeDtypeStruct((n_exp, LANES), I32)],
        compiler_params=_cparams(("arbitrary",)),
        name="plan",
    )(ri, counts)
    return pos[0], pos[1], te[0, :n_tiles], nv[0, :1], off[:, 0]


def _load_positions(pos1_hbm, pos2_hbm, p1_ref, p2_ref, isem, step):
    c1 = pltpu.make_async_copy(pos1_hbm.at[step], p1_ref, isem.at[0])
    c2 = pltpu.make_async_copy(pos2_hbm.at[step], p2_ref, isem.at[1])
    c1.start()
    c2.start()
    c1.wait()
    c2.wait()


def _dispatch_kernel(off_ref, cnt_ref, pos1_hbm, pos2_hbm, h2_ref, xs_hbm, p1_ref, p2_ref, zero_ref,
                     isem, sem, zsem, *, tmx):
    step = pl.program_id(0)
    n_blk = h2_ref.shape[0]
    _load_positions(pos1_hbm, pos2_hbm, p1_ref, p2_ref, isem, step)

    def row_copy(blk, s, p):
        return pltpu.make_async_copy(h2_ref.at[blk, pl.ds(s, 1), :], xs_hbm.at[pl.ds(p, 1), :], sem)

    def send(blk, c):
        j0 = blk * SUBLANES
        for s in range(SUBLANES):
            row_copy(blk, s, p1_ref[j0 + s]).start(priority=0)
            row_copy(blk, s, p2_ref[j0 + s]).start(priority=1)
        return c
    lax.fori_loop(0, n_blk, send, 0, unroll=2)
    _wait_many(row_copy(0, 0, 0), TOP_K_FINE * n_blk * SUBLANES)

    @pl.when(step == pl.num_programs(0) - 1)
    def _():
        zero_ref[...] = jnp.zeros_like(zero_ref)
        shift = tmx.bit_length() - 1

        def pad_expert(e, c):
            cnt = cnt_ref[e]
            n_pad = jnp.left_shift(jnp.right_shift(cnt + (tmx - 1), shift), shift) - cnt
            first = off_ref[e] + cnt
            zero_copy = lambda r: pltpu.make_async_copy(
                zero_ref.at[pl.ds(0, 1), :], xs_hbm.at[pl.ds(first + r, 1), :], zsem)

            def fill(r, c2):
                zero_copy(r).start()
                return c2
            lax.fori_loop(0, n_pad, fill, 0)

            def done(r, c2):
                zero_copy(r).wait()
                return c2
            lax.fori_loop(0, n_pad, done, 0)
            return c
        lax.fori_loop(0, off_ref.shape[0], pad_expert, 0)


def _dispatch(offsets, counts, pos1, pos2, h2p, tmx, ts):
    t, half = h2p.shape
    n_exp = offsets.shape[0]
    p_max = TOP_K_FINE * t + n_exp * tmx
    hbm = lambda: pl.BlockSpec(memory_space=pl.ANY)
    grid_spec = pltpu.PrefetchScalarGridSpec(
        num_scalar_prefetch=2,
        grid=(t // ts,),
        in_specs=[hbm(), hbm(),
                  pl.BlockSpec((ts // SUBLANES, SUBLANES, half), lambda i, off, cnt: (i, 0, 0))],
        out_specs=hbm(),
        scratch_shapes=[pltpu.SMEM((ts,), I32),
                        pltpu.SMEM((ts,), I32),
                        pltpu.VMEM((SUBLANES, half), U32),
                        pltpu.SemaphoreType.DMA((2,)),
                        pltpu.SemaphoreType.DMA(()),
                        pltpu.SemaphoreType.DMA(())])
    return pl.pallas_call(
        functools.partial(_dispatch_kernel, tmx=tmx),
        grid_spec=grid_spec,
        out_shape=jax.ShapeDtypeStruct((p_max, half), U32),
        compiler_params=_cparams(("arbitrary",)),
        name="dispatch",
    )(offsets, counts, pos1.reshape(t // ts, ts), pos2.reshape(t // ts, ts),
      h2p.reshape(t // SUBLANES, SUBLANES, half))


def _expert_kernel(te_ref, nv_ref, x_ref, w1_ref, w3_ref, w2_ref, y_ref):
    @pl.when(pl.program_id(0) < nv_ref[0])
    def _():
        lo, hi = _unpack_bf16_pair(x_ref[...])
        lo = lo.astype(BF16)
        hi = hi.astype(BF16)
        half = lo.shape[1]
        mm = lambda w_ref: (jnp.dot(lo, w_ref[0, :half, :].astype(BF16), preferred_element_type=F32)
                            + jnp.dot(hi, w_ref[0, half:, :].astype(BF16), preferred_element_type=F32))
        h1 = mm(w1_ref)
        h3 = mm(w3_ref)
        hh = (h1 * jax.nn.sigmoid(h1) * h3).astype(BF16)
        y = jnp.dot(hh, w2_ref[0].astype(BF16), preferred_element_type=F32)
        y_ref[...] = _pack_bf16_pair(y[:, :half], y[:, half:])


def _expert_mlp(xs, tile_expert, n_valid, w1, w3, w2, tmx):
    p, half = xs.shape
    d, de = w1.shape[1], w1.shape[2]
    tile = lambda i, te, nv: (jnp.minimum(i, nv[0] - 1), 0)
    grid_spec = pltpu.PrefetchScalarGridSpec(
        num_scalar_prefetch=2,
        grid=(p // tmx,),
        in_specs=[pl.BlockSpec((tmx, half), tile),
                  pl.BlockSpec((1, d, de), lambda i, te, nv: (te[i], 0, 0)),
                  pl.BlockSpec((1, d, de), lambda i, te, nv: (te[i], 0, 0)),
                  pl.BlockSpec((1, de, d), lambda i, te, nv: (te[i], 0, 0))],
        out_specs=pl.BlockSpec((tmx, half), tile))
    return pl.pallas_call(
        _expert_kernel,
        grid_spec=grid_spec,
        out_shape=jax.ShapeDtypeStruct((p, half), U32),
        compiler_params=_cparams(("arbitrary",)),
        name="expert_mlp",
    )(tile_expert, n_valid, xs, w1, w3, w2)


def _combine_kernel(pos1_hbm, pos2_hbm, ys_hbm, x1_ref, rf_ref, g2_ref, fg_ref, ylat_hbm, yctx_hbm,
                    p1_ref, p2_ref, gbuf, obuf, isem, gsem, osem, *, geom):
    step = pl.program_id(0)
    n_steps = pl.num_programs(0)
    tm = geom.tm
    n_blk = tm // SUBLANES
    slot = lax.rem(step, 2)

    def row_copy(gs, k, p, blk, s):
        return pltpu.make_async_copy(ys_hbm.at[pl.ds(p, 1), :], gbuf.at[gs, k, blk, pl.ds(s, 1), :],
                                     gsem.at[gs])

    def gather(tile, gs):
        _load_positions(pos1_hbm, pos2_hbm, p1_ref, p2_ref, isem, tile)

        def fetch(blk, c):
            j0 = blk * SUBLANES
            for s in range(SUBLANES):
                row_copy(gs, 0, p1_ref[j0 + s], blk, s).start(priority=0)
                row_copy(gs, 1, p2_ref[j0 + s], blk, s).start(priority=1)
            return c
        lax.fori_loop(0, n_blk, fetch, 0, unroll=2)

    @pl.when(step == 0)
    def _():
        gather(step, slot)

    @pl.when(step + 1 < n_steps)
    def _():
        gather(step + 1, 1 - slot)

    _wait_many(row_copy(slot, 0, 0, 0, 0), TOP_K_FINE * tm)

    wt = rf_ref[...].T
    half = gbuf.shape[-1]
    a_lo, a_hi = _unpack_bf16_pair(gbuf[slot, 0].reshape(tm, half))
    b_lo, b_hi = _unpack_bf16_pair(gbuf[slot, 1].reshape(tm, half))
    w1 = wt[:, 0:1]
    w2 = wt[:, 1:2]
    moe = jnp.concatenate([w1 * a_lo + w2 * b_lo, w1 * a_hi + w2 * b_hi], axis=1)
    x = x1_ref[...] + _gate(moe, g2_ref[0])
    out = _rms(x, fg_ref[...])

    start = lambda cp: cp.start()
    wait = lambda cp: cp.wait()
    put = functools.partial(_tile_copies, ylat_hbm, yctx_hbm, geom=geom, to_rows=False)

    @pl.when(step >= 2)
    def _():
        put(obuf.at[slot], osem.at[slot], step - 2, fn=wait)

    obuf[slot] = out.reshape(obuf.shape[1:])
    put(obuf.at[slot], osem.at[slot], step, fn=start)

    @pl.when(step == n_steps - 1)
    def _():
        @pl.when(n_steps > 1)
        def _():
            put(obuf.at[1 - slot], osem.at[1 - slot], step - 1, fn=wait)
        put(obuf.at[slot], osem.at[slot], step, fn=wait)


def _combine(pos1, pos2, ys, x1, rf, modtab, final_g, lat_shape, ctx_shape, geom):
    t, d = x1.shape
    tm = geom.tm
    mset = lambda i: jnp.where(i < geom.n_lat_tiles, 0, 1)
    hbm = lambda: pl.BlockSpec(memory_space=pl.ANY)
    return pl.pallas_call(
        functools.partial(_combine_kernel, geom=geom),
        grid=(t // tm,),
        in_specs=[hbm(), hbm(), hbm(),
                  pl.BlockSpec((tm, d), lambda i: (i, 0)),
                  pl.BlockSpec((LANES, tm), lambda i: (0, i)),
                  pl.BlockSpec((1, SUBLANES, d), lambda i: (mset(i), 0, 5)),
                  pl.BlockSpec((1, d), lambda i: (0, 0))],
        out_specs=[hbm(), hbm()],
        out_shape=[jax.ShapeDtypeStruct(lat_shape, F32), jax.ShapeDtypeStruct(ctx_shape, F32)],
        scratch_shapes=[pltpu.SMEM((tm,), I32),
                        pltpu.SMEM((tm,), I32),
                        pltpu.VMEM((2, TOP_K_FINE, tm // SUBLANES, SUBLANES, d // 2), U32),
                        pltpu.VMEM((2, tm // SUBLANES, SUBLANES, d), F32),
                        pltpu.SemaphoreType.DMA((2,)),
                        pltpu.SemaphoreType.DMA((2,)),
                        pltpu.SemaphoreType.DMA((2,))],
        compiler_params=_cparams(("arbitrary",)),
        name="combine",
    )(pos1.reshape(t // tm, tm), pos2.reshape(t // tm, tm), ys, x1, rf, modtab, final_g.reshape(1, d))


def _tile_meta(groups, tm):
    rows, first, last, grp = [], [], [], []
    blk = 0
    for g, r in enumerate(groups):
        nc = r // tm
        for c in range(nc):
            rows.append(blk + c)
            first.append(int(c == 0))
            last.append(int(c == nc - 1))
            grp.append(g)
        blk += nc
    fwd = (np.array(rows, np.int32), np.array([first, last, grp], np.int32))
    order = []
    blk = 0
    for r in groups:
        nc = r // tm
        order.extend(range(blk + nc - 1, blk - 1, -1))
        blk += nc
    order = np.array(order)
    bwd = (fwd[0][order], fwd[1][:, order])
    return fwd, bwd


def kernel(x_prompt, x_sample, state_lru, c, c_ctx, w_mod, b_mod, norm1_g, w_in, conv_w, conv_b, lru_wa, lru_ba, lru_wx, lru_bx, lru_lambda, pool_w, pool_scale, w_out, norm2_g, router_coarse_w, router_coarse_b, router_fine_w, router_fine_b, exp_w1, exp_w3, exp_w2, final_norm_g):
    bp, sp, d = x_prompt.shape
    bs, ss, _ = x_sample.shape
    d_lru = lru_lambda.shape[-1]
    heads, bw = lru_wa.shape[2], lru_wa.shape[3]
    n_groups, per_group = router_fine_w.shape[2], router_fine_w.shape[3]
    n_exp = n_groups * per_group
    assert w_mod.shape[0] == 1 and bs == SUBLANES and bp % SUBLANES == 0 and ss % GRID_W == 0
    assert EXPERT_TILE & (EXPERT_TILE - 1) == 0
    n_lat_groups, n_ctx_groups = bs // SUBLANES, bp // SUBLANES
    lat_rows, ctx_rows = ss * SUBLANES, sp * SUBLANES
    assert (n_lat_groups * lat_rows) % ctx_rows == 0
    tm = min(TOKEN_TILE, ctx_rows, lat_rows)
    geom = _Geom(tm=tm, n_lat_tiles=n_lat_groups * lat_rows // tm, lat_chunks=lat_rows // tm,
                 ctx_chunks=ctx_rows // tm)
    n_rows = n_lat_groups * lat_rows + n_ctx_groups * ctx_rows
    groups = [lat_rows] * n_lat_groups + [ctx_rows] * n_ctx_groups
    (f_rows, f_flags), (b_rows, b_flags) = _tile_meta(groups, tm)

    cond = jnp.zeros((2 * SUBLANES, d), F32).at[:bs].set(c).at[bs].set(c_ctx)
    mod = _modulation(cond, w_mod[0], b_mod[0])
    modtab = jnp.stack([mod[:SUBLANES], jnp.broadcast_to(mod[SUBLANES], (SUBLANES, mod.shape[1]))])
    h0_lat = state_lru[:, 0].reshape(n_lat_groups, SUBLANES, 2, d_lru)
    h0 = jnp.concatenate([h0_lat, jnp.zeros((n_ctx_groups, SUBLANES, 2, d_lru), F32)], axis=0)
    h0 = h0.transpose(0, 2, 1, 3)

    xa, ga, z = _input_projection(x_sample, x_prompt, modtab, norm1_g[0], w_in[0],
                                  pool_w[0].astype(BF16), n_rows, geom)

    def gate_weights(direction):
        wg = jnp.concatenate([lru_wa[0, direction], lru_wx[0, direction]], axis=-1).astype(BF16)
        bg = jnp.concatenate([lru_ba[0, direction].reshape(heads, 1, bw),
                              lru_bx[0, direction].reshape(heads, 1, bw)], axis=-1)
        return wg, bg

    log_decay = jax.nn.log_sigmoid(lru_lambda[0])
    wg_f, bg_f = gate_weights(0)
    wg_b, bg_b = gate_weights(1)
    hf, hf_last = _forward_scan(xa, (jnp.asarray(f_rows), jnp.asarray(f_flags)), conv_w[0], conv_b[0],
                                wg_f, bg_f, log_decay[0], h0[:, 0], tm)

    yb = _pool(z, None, pool_scale[0], lat_rows, n_lat_groups, 0, lat_rows // (GRID_W * SUBLANES))
    yb = _pool(z, yb, pool_scale[0], ctx_rows, n_ctx_groups, n_lat_groups * lat_rows // ctx_rows, None)

    n_logits = n_groups + n_exp
    rw = jnp.concatenate([router_coarse_w[0], router_fine_w[0].reshape(d, n_exp)], axis=1)
    rwt = jnp.zeros((LANES, d), BF16).at[:n_logits].set(rw.T.astype(BF16))
    rb = jnp.zeros((LANES, 1), F32).at[:n_logits, 0].set(
        jnp.concatenate([router_coarse_b[0], router_fine_b[0].reshape(n_exp)]))
    x1, h2p, logits_t, hb_last = _backward_scan_mix(
        xa, ga, hf, yb, x_sample, x_prompt, modtab, (jnp.asarray(b_rows), jnp.asarray(b_flags)),
        conv_w[0], conv_b[0], wg_b, bg_b, log_decay[1], h0[:, 1], w_out[0].astype(BF16), norm2_g[0],
        rwt, rb, geom)

    ri, rf, counts = _route(logits_t, n_groups, per_group)
    n_tiles = (TOP_K_FINE * n_rows + n_exp * EXPERT_TILE) // EXPERT_TILE
    pos1, pos2, tile_expert, n_valid, offsets = _plan(ri, counts, EXPERT_TILE, n_tiles)
    xs = _dispatch(offsets, counts[:, 0], pos1, pos2, h2p, EXPERT_TILE, tm)
    ys = _expert_mlp(xs, tile_expert, n_valid, exp_w1[0], exp_w3[0], exp_w2[0], EXPERT_TILE)
    y_sample, y_prompt = _combine(pos1, pos2, ys, x1, rf, modtab, final_norm_g, x_sample.shape,
                                  x_prompt.shape, geom)

    st = jnp.stack([hf_last[n_lat_groups:], hb_last[n_lat_groups:]], axis=2)
    state_new = st.reshape(bp, 1, 2, d_lru).astype(x_prompt.dtype)
    return (y_prompt, y_sample, state_new)
```

```python
import functools
from typing import NamedTuple

import numpy as np
import jax
import jax.numpy as jnp
from jax import lax
from jax.experimental import pallas as pl
from jax.experimental.pallas import tpu as pltpu

GRID_W = 64
CONV_W = 4
RG_C = 8.0
POOL_WINDOWS = (2, 4, 8, 16)
TOP_K_FINE = 2
EPS = 1e-6
EXPM1_SERIES_BELOW = 0.125

SUBLANES = 8
LANES = 128
BF16_ROWS = 16
TOKEN_TILE = 512
EXPERT_TILE = 512
ROUTE_TILE = 2048
CUMSUM_BLOCK = 512
MOD_COL_TILE = 1024
WEIGHT_STAGE_ROWS = 256
WAIT_UNROLL = 32
VMEM_LIMIT = 60 * 1024 * 1024

F32 = jnp.float32
BF16 = jnp.bfloat16
U32 = jnp.uint32
I32 = jnp.int32
HIGH_HALF = 0xFFFF0000


class _Geom(NamedTuple):
    tm: int
    n_lat_tiles: int
    lat_chunks: int
    ctx_chunks: int


def _cparams(sem):
    return pltpu.CompilerParams(dimension_semantics=sem, vmem_limit_bytes=VMEM_LIMIT)


def _per_sequence(y, m):
    rows, d = y.shape
    return y.reshape(rows // SUBLANES, SUBLANES, d), m[None]


def _modulate(y, scale, shift):
    y3, sc = _per_sequence(y, scale)
    _, sh = _per_sequence(y, shift)
    return (y3 * (1.0 + sc) + sh).reshape(y.shape)


def _gate(y, g):
    y3, g3 = _per_sequence(y, g)
    return (y3 * g3).reshape(y.shape)


def _rms(x, g):
    ms = jnp.mean(x * x, axis=-1, keepdims=True)
    return x * lax.rsqrt(ms + EPS) * g


def _pack_bf16_pair(lo, hi):
    lo_bits = lax.shift_right_logical(lax.bitcast_convert_type(lo.astype(BF16).astype(F32), U32), U32(16))
    hi_bits = lax.bitcast_convert_type(hi.astype(BF16).astype(F32), U32) & U32(HIGH_HALF)
    return lo_bits | hi_bits


def _unpack_bf16_pair(w):
    lo = lax.bitcast_convert_type(lax.shift_left(w, U32(16)), F32)
    hi = lax.bitcast_convert_type(w & U32(HIGH_HALF), F32)
    return lo, hi


def _store_token_tiles(ref, w):
    n, words = w.shape
    nc = words // LANES
    for c in range(nc):
        ref[pl.ds(c, n, stride=nc), :] = w[:, c * LANES:(c + 1) * LANES]


def _load_token_tiles(ref, n):
    nc = ref.shape[0] // n
    return jnp.concatenate([ref[pl.ds(c, n, stride=nc), :] for c in range(nc)], axis=1)


def _wait_many(copy, n):
    def body(_, c):
        for _ in range(WAIT_UNROLL):
            copy.wait()
        return c
    lax.fori_loop(0, n // WAIT_UNROLL, body, 0)


def _tile_copies(lat_hbm, ctx_hbm, buf, sem, tile, geom, to_rows, fn):
    tt = geom.tm // SUBLANES

    def run(hbm, k, chunks):
        g = lax.div(k, jnp.int32(chunks))
        c = lax.rem(k, jnp.int32(chunks))
        for b in range(SUBLANES):
            h = hbm.at[g * SUBLANES + b, pl.ds(c * tt, tt), :]
            v = buf.at[:, b, :]
            fn(pltpu.make_async_copy(h, v, sem) if to_rows else pltpu.make_async_copy(v, h, sem))

    @pl.when(tile < geom.n_lat_tiles)
    def _():
        run(lat_hbm, tile, geom.lat_chunks)

    @pl.when(tile >= geom.n_lat_tiles)
    def _():
        run(ctx_hbm, tile - geom.n_lat_tiles, geom.ctx_chunks)


def _fetch_rows(lat_hbm, ctx_hbm, xbuf, sem, step, n_steps, tile_of, geom):
    slot = lax.rem(step, 2)
    start = lambda cp: cp.start()
    wait = lambda cp: cp.wait()

    @pl.when(step == 0)
    def _():
        _tile_copies(lat_hbm, ctx_hbm, xbuf.at[0], sem.at[0], tile_of(jnp.int32(0)), geom, True, start)

    @pl.when(step + 1 < n_steps)
    def _():
        nxt = 1 - slot
        _tile_copies(lat_hbm, ctx_hbm, xbuf.at[nxt], sem.at[nxt], tile_of(step + 1), geom, True, start)

    _tile_copies(lat_hbm, ctx_hbm, xbuf.at[slot], sem.at[slot], tile_of(step), geom, True, wait)
    tt, _, d = xbuf.shape[1:]
    return xbuf[slot].reshape(tt * SUBLANES, d)


def _mod_kernel(c_ref, w_ref, b_ref, o_ref):
    c = c_ref[...]
    s = c * jax.nn.sigmoid(c)
    o_ref[...] = jnp.dot(s.astype(BF16), w_ref[...].astype(BF16),
                         preferred_element_type=F32) + b_ref[...]


def _modulation(cond, w_mod, b_mod):
    rows, d = cond.shape
    n = w_mod.shape[1]
    tn = min(MOD_COL_TILE, n)
    return pl.pallas_call(
        _mod_kernel,
        grid=(n // tn,),
        in_specs=[pl.BlockSpec((rows, d), lambda j: (0, 0)),
                  pl.BlockSpec((d, tn), lambda j: (0, j)),
                  pl.BlockSpec((1, tn), lambda j: (0, j))],
        out_specs=pl.BlockSpec((rows, tn), lambda j: (0, j)),
        out_shape=jax.ShapeDtypeStruct((rows, n), F32),
        compiler_params=_cparams(("arbitrary",)),
        name="modulation",
    )(cond, w_mod, b_mod.reshape(1, n))


def _round_weights(w_hbm, wbf_ref, stage, sem):
    rows = stage.shape[1]
    n = w_hbm.shape[0] // rows
    copy = lambda r: pltpu.make_async_copy(w_hbm.at[pl.ds(r * rows, rows), :], stage.at[r % 2], sem.at[r % 2])
    copy(0).start()
    for r in range(n):
        if r + 1 < n:
            copy(r + 1).start()
        copy(r).wait()
        wbf_ref[r * rows:(r + 1) * rows, :] = stage[r % 2].astype(wbf_ref.dtype)


def _proj_kernel(lat_hbm, ctx_hbm, sh_ref, sc_ref, g_ref, win_hbm, pw_ref, xa_ref, ga_ref, z_ref,
                 xbuf, sem, wbf_ref, wstage, wsem, *, d_lru, gw, geom):
    step = pl.program_id(0)

    @pl.when(step == 0)
    def _():
        _round_weights(win_hbm, wbf_ref, wstage, wsem)

    x = _fetch_rows(lat_hbm, ctx_hbm, xbuf, sem, step, pl.num_programs(0), lambda s: s, geom)
    h = _modulate(_rms(x, g_ref[...]), sc_ref[0], sh_ref[0])
    proj = jnp.dot(h.astype(BF16), wbf_ref[...], preferred_element_type=F32)
    xa_ref[...] = proj[:, :d_lru].astype(BF16)
    ga_ref[...] = proj[:, d_lru:2 * d_lru].astype(BF16)
    for g in range(pw_ref.shape[0]):
        lo = 2 * d_lru + g * gw
        z_ref[:, g * gw:(g + 1) * gw] = jnp.dot(
            proj[:, lo:lo + gw].astype(BF16), pw_ref[g], preferred_element_type=F32).astype(BF16)


def _input_projection(x_lat, x_ctx, modtab, norm_g, w_in, pool_w, n_rows, geom):
    d = x_lat.shape[-1]
    tm = geom.tm
    d_pool = pool_w.shape[0] * pool_w.shape[1]
    d_lru = (w_in.shape[1] - d_pool) // 2
    mset = lambda i: jnp.where(i < geom.n_lat_tiles, 0, 1)
    const = dict(pipeline_mode=pl.Buffered(1))
    return pl.pallas_call(
        functools.partial(_proj_kernel, d_lru=d_lru, gw=pool_w.shape[1], geom=geom),
        grid=(n_rows // tm,),
        in_specs=[pl.BlockSpec(memory_space=pl.ANY),
                  pl.BlockSpec(memory_space=pl.ANY),
                  pl.BlockSpec((1, SUBLANES, d), lambda i: (mset(i), 0, 0)),
                  pl.BlockSpec((1, SUBLANES, d), lambda i: (mset(i), 0, 1)),
                  pl.BlockSpec((1, d), lambda i: (0, 0)),
                  pl.BlockSpec(memory_space=pl.ANY),
                  pl.BlockSpec(pool_w.shape, lambda i: (0, 0, 0), **const)],
        out_specs=[pl.BlockSpec((tm, d_lru), lambda i: (i, 0)),
                   pl.BlockSpec((tm, d_lru), lambda i: (i, 0)),
                   pl.BlockSpec((tm, d_pool), lambda i: (i, 0))],
        out_shape=[jax.ShapeDtypeStruct((n_rows, d_lru), BF16),
                   jax.ShapeDtypeStruct((n_rows, d_lru), BF16),
                   jax.ShapeDtypeStruct((n_rows, d_pool), BF16)],
        scratch_shapes=[pltpu.VMEM((2, tm // SUBLANES, SUBLANES, d), F32),
                        pltpu.SemaphoreType.DMA((2,)),
                        pltpu.VMEM(w_in.shape, BF16),
                        pltpu.VMEM((2, min(WEIGHT_STAGE_ROWS, d), w_in.shape[1]), F32),
                        pltpu.SemaphoreType.DMA((2,))],
        compiler_params=_cparams(("arbitrary",)),
        name="input_projection",
    )(x_lat, x_ctx, modtab, modtab, norm_g.reshape(1, d), w_in, pool_w)


def _fill_ext(ext_ref, prev_ref, main_ref, next_ref, first, last):
    tm = main_ref.shape[0]
    prev = prev_ref[...].astype(F32)
    nxt = next_ref[...].astype(F32)
    ext_ref[0:BF16_ROWS, :] = jnp.where(first, 0.0, prev)
    ext_ref[BF16_ROWS:BF16_ROWS + tm, :] = main_ref[...].astype(F32)
    ext_ref[BF16_ROWS + tm:, :] = jnp.where(last, 0.0, nxt)


def _one_minus_exp(y, exp_y):
    p = 1.0 / 120.0
    for c in (1.0 / 24.0, 1.0 / 6.0, 0.5, 1.0):
        p = p * y + c
    return jnp.where(y > -EXPM1_SERIES_BELOW, -y * p, 1.0 - exp_y)


def _sqrt_nonneg(q):
    return jnp.where(q > 0.0, q * lax.rsqrt(q), 0.0)


def _decay_and_input(ext_ref, cw_ref, cb_ref, wg_ref, bg_ref, lam_ref, a_ref, u_ref, tm):
    heads, bw = wg_ref.shape[0], wg_ref.shape[1]
    for hd in range(heads):
        sl = slice(hd * bw, (hd + 1) * bw)
        xc = cb_ref[:, sl]
        for k in range(CONV_W):
            xc = xc + cw_ref[k:k + 1, sl] * ext_ref[SUBLANES * k:SUBLANES * k + tm, sl]
        g = jnp.dot(xc.astype(BF16), wg_ref[hd], preferred_element_type=F32) + bg_ref[hd]
        r = jax.nn.sigmoid(g[:, :bw])
        ig = jax.nn.sigmoid(g[:, bw:])
        log_a = (RG_C * r) * lam_ref[:, sl]
        a = jnp.exp(log_a)
        a_ref[:, sl] = a
        u_ref[:, sl] = _sqrt_nonneg(_one_minus_exp(2.0 * log_a, a * a)) * (ig * xc)


def _scan(a_ref, u_ref, h, tm, reverse):
    nblk = tm // SUBLANES

    def body(s, h):
        j = (nblk - 1 - s) if reverse else s
        rows = pl.ds(pl.multiple_of(j * SUBLANES, SUBLANES), SUBLANES)
        h = a_ref[rows, :] * h + u_ref[rows, :]
        u_ref[rows, :] = h
        return h

    return lax.fori_loop(0, nblk, body, h, unroll=8)


def _halo_specs(tm, d_lru, n_rows):
    per = tm // BF16_ROWS
    last_blk = n_rows // BF16_ROWS - 1
    return [pl.BlockSpec((BF16_ROWS, d_lru), lambda i, tr, fl: (jnp.maximum(tr[i] * per - 1, 0), 0)),
            pl.BlockSpec((tm, d_lru), lambda i, tr, fl: (tr[i], 0)),
            pl.BlockSpec((BF16_ROWS, d_lru), lambda i, tr, fl: (jnp.minimum((tr[i] + 1) * per, last_blk), 0))]


def _fwd_kernel(tr_ref, fl_ref, prev_ref, main_ref, next_ref, cw_ref, cb_ref, wg_ref, bg_ref, lam_ref,
                h0_ref, hf_ref, hlast_ref, ext_ref, a_ref, u_ref, h_ref):
    i = pl.program_id(0)
    tm = main_ref.shape[0]
    first = fl_ref[0, i] == 1
    last = fl_ref[1, i] == 1
    _fill_ext(ext_ref, prev_ref, main_ref, next_ref, first, last)
    _decay_and_input(ext_ref, cw_ref, cb_ref, wg_ref, bg_ref, lam_ref, a_ref, u_ref, tm)

    @pl.when(first)
    def _():
        h_ref[...] = h0_ref[0]

    h = _scan(a_ref, u_ref, h_ref[...], tm, reverse=False)
    h_ref[...] = h
    hlast_ref[0] = h
    hf_ref[...] = u_ref[...].astype(BF16)


def _forward_scan(xa, meta, conv_w, conv_b, wg, bg, lam, h0, tm):
    t, d_lru = xa.shape
    tile_row, flags = meta
    ngrp = h0.shape[0]
    const = dict(pipeline_mode=pl.Buffered(1))
    grid_spec = pltpu.PrefetchScalarGridSpec(
        num_scalar_prefetch=2,
        grid=(t // tm,),
        in_specs=_halo_specs(tm, d_lru, t) + [
            pl.BlockSpec(conv_w.shape, lambda i, tr, fl: (0, 0)),
            pl.BlockSpec((1, d_lru), lambda i, tr, fl: (0, 0)),
            pl.BlockSpec(wg.shape, lambda i, tr, fl: (0, 0, 0), **const),
            pl.BlockSpec(bg.shape, lambda i, tr, fl: (0, 0, 0)),
            pl.BlockSpec((1, d_lru), lambda i, tr, fl: (0, 0)),
            pl.BlockSpec((1, SUBLANES, d_lru), lambda i, tr, fl: (fl[2, i], 0, 0))],
        out_specs=[pl.BlockSpec((tm, d_lru), lambda i, tr, fl: (tr[i], 0)),
                   pl.BlockSpec((1, SUBLANES, d_lru), lambda i, tr, fl: (fl[2, i], 0, 0))],
        scratch_shapes=[pltpu.VMEM((tm + 2 * BF16_ROWS, d_lru), F32),
                        pltpu.VMEM((tm, d_lru), F32),
                        pltpu.VMEM((tm, d_lru), F32),
                        pltpu.VMEM((SUBLANES, d_lru), F32)])
    return pl.pallas_call(
        _fwd_kernel,
        grid_spec=grid_spec,
        out_shape=[jax.ShapeDtypeStruct((t, d_lru), BF16),
                   jax.ShapeDtypeStruct((ngrp, SUBLANES, d_lru), F32)],
        compiler_params=_cparams(("arbitrary",)),
        name="forward_scan",
    )(tile_row, flags, xa, xa, xa, conv_w, conv_b.reshape(1, d_lru), wg, bg, lam.reshape(1, d_lru), h0)


def _bwd_kernel(tr_ref, fl_ref, prev_ref, main_ref, next_ref, cw_ref, cb_ref, wg_ref, bg_ref, lam_ref,
                h0_ref, hf_ref, ga_ref, yb_ref, lat_hbm, ctx_hbm, g1_ref, sh2_ref, sc2_ref, wout_ref,
                n2_ref, rwt_ref, rb_ref,
                x1_ref, h2_ref, lg_ref, hlast_ref,
                ext_ref, a_ref, u_ref, h_ref, cat_ref, xbuf, xsem, *, geom):
    i = pl.program_id(0)
    n_steps = pl.num_programs(0)
    tm, d_lru = main_ref.shape
    x = _fetch_rows(lat_hbm, ctx_hbm, xbuf, xsem, i, n_steps,
                    lambda s: tr_ref[jnp.minimum(s, n_steps - 1)], geom)
    first = fl_ref[0, i] == 1
    last = fl_ref[1, i] == 1
    _fill_ext(ext_ref, prev_ref, main_ref, next_ref, first, last)
    _decay_and_input(ext_ref, cw_ref, cb_ref, wg_ref, bg_ref, lam_ref, a_ref, u_ref, tm)

    @pl.when(last)
    def _():
        h_ref[...] = h0_ref[0]

    h = _scan(a_ref, u_ref, h_ref[...], tm, reverse=True)
    h_ref[...] = h
    hlast_ref[0] = h

    ga = ga_ref[...].astype(F32)
    ya = (hf_ref[...].astype(F32) + u_ref[...]) * jax.nn.gelu(ga)
    cat_ref[:, :d_lru] = ya.astype(BF16)
    cat_ref[:, d_lru:] = yb_ref[...]
    mix = jnp.dot(cat_ref[...], wout_ref[...], preferred_element_type=F32)
    x1 = x + _gate(mix, g1_ref[0])
    x1_ref[...] = x1
    h2 = _modulate(_rms(x1, n2_ref[...]), sc2_ref[0], sh2_ref[0])
    half = h2.shape[1] // 2
    _store_token_tiles(h2_ref, _pack_bf16_pair(h2[:, :half], h2[:, half:]))
    lg_ref[...] = lax.dot_general(rwt_ref[...], h2.astype(BF16), (((1,), (1,)), ((), ())),
                                  preferred_element_type=F32) + rb_ref[...]


def _backward_scan_mix(xa, ga, hf, yb, x_lat, x_ctx, modtab, meta, conv_w, conv_b, wg, bg, lam, h0, w_out,
                       norm2_g, rwt, rb, geom):
    t, d_lru = xa.shape
    d = x_lat.shape[-1]
    d_pool = yb.shape[1]
    tm = geom.tm
    tile_row, flags = meta
    ngrp = h0.shape[0]
    nr = rwt.shape[0]
    const = dict(pipeline_mode=pl.Buffered(1))
    mset = lambda i, tr, fl: jnp.where(tr[i] < geom.n_lat_tiles, 0, 1)
    row = lambda i, tr, fl: (tr[i], 0)
    mod = lambda col: pl.BlockSpec((1, SUBLANES, d), lambda i, tr, fl: (mset(i, tr, fl), 0, col))
    grid_spec = pltpu.PrefetchScalarGridSpec(
        num_scalar_prefetch=2,
        grid=(t // tm,),
        in_specs=_halo_specs(tm, d_lru, t) + [
            pl.BlockSpec(conv_w.shape, lambda i, tr, fl: (0, 0)),
            pl.BlockSpec((1, d_lru), lambda i, tr, fl: (0, 0)),
            pl.BlockSpec(wg.shape, lambda i, tr, fl: (0, 0, 0), **const),
            pl.BlockSpec(bg.shape, lambda i, tr, fl: (0, 0, 0)),
            pl.BlockSpec((1, d_lru), lambda i, tr, fl: (0, 0)),
            pl.BlockSpec((1, SUBLANES, d_lru), lambda i, tr, fl: (fl[2, i], 0, 0)),
            pl.BlockSpec((tm, d_lru), row),
            pl.BlockSpec((tm, d_lru), row),
            pl.BlockSpec((tm, d_pool), row),
            pl.BlockSpec(memory_space=pl.ANY),
            pl.BlockSpec(memory_space=pl.ANY),
            mod(2), mod(3), mod(4),
            pl.BlockSpec(w_out.shape, lambda i, tr, fl: (0, 0), **const),
            pl.BlockSpec((1, d), lambda i, tr, fl: (0, 0)),
            pl.BlockSpec(rwt.shape, lambda i, tr, fl: (0, 0)),
            pl.BlockSpec((nr, 1), lambda i, tr, fl: (0, 0))],
        out_specs=[pl.BlockSpec((tm, d), row),
                   pl.BlockSpec((tm * (d // 2) // LANES, LANES), row),
                   pl.BlockSpec((nr, tm), lambda i, tr, fl: (0, tr[i])),
                   pl.BlockSpec((1, SUBLANES, d_lru), lambda i, tr, fl: (fl[2, i], 0, 0))],
        scratch_shapes=[pltpu.VMEM((tm + 2 * BF16_ROWS, d_lru), F32),
                        pltpu.VMEM((tm, d_lru), F32),
                        pltpu.VMEM((tm, d_lru), F32),
                        pltpu.VMEM((SUBLANES, d_lru), F32),
                        pltpu.VMEM((tm, d_lru + d_pool), BF16),
                        pltpu.VMEM((2, tm // SUBLANES, SUBLANES, d), F32),
                        pltpu.SemaphoreType.DMA((2,))])
    return pl.pallas_call(
        functools.partial(_bwd_kernel, geom=geom),
        grid_spec=grid_spec,
        out_shape=[jax.ShapeDtypeStruct((t, d), F32),
                   jax.ShapeDtypeStruct((t * (d // 2) // LANES, LANES), U32),
                   jax.ShapeDtypeStruct((nr, t), F32),
                   jax.ShapeDtypeStruct((ngrp, SUBLANES, d_lru), F32)],
        compiler_params=_cparams(("arbitrary",)),
        name="backward_scan_mix",
    )(tile_row, flags, xa, xa, xa, conv_w, conv_b.reshape(1, d_lru), wg, bg, lam.reshape(1, d_lru), h0,
      hf, ga, yb, x_lat, x_ctx, modtab, modtab, modtab, w_out, norm2_g.reshape(1, d), rwt, rb)


def _shift_rows(v, k):
    if k == 0:
        return v
    z = jnp.zeros((abs(k) * SUBLANES, v.shape[1]), v.dtype)
    if k > 0:
        return jnp.concatenate([z, v[:-k * SUBLANES]], axis=0)
    return jnp.concatenate([v[-k * SUBLANES:], z], axis=0)


def _run_sum(v, m, direction):
    if m & (m - 1) == 0:
        k = 1
        while k < m:
            v = v + _shift_rows(v, -direction * k)
            k *= 2
        return v
    out = v
    for j in range(1, m):
        out = out + _shift_rows(v, -direction * j)
    return out


def _box_sum(v, w):
    lo = w // 2
    hi = w - 1 - lo
    s = _run_sum(v, hi + 1, +1)
    if lo:
        s = s + _shift_rows(_run_sum(v, lo, -1), 1)
    return s


def _window_count(n_rows, lanes, n_pos, w):
    lo = w // 2
    hi = w - 1 - lo
    p = lax.shift_right_logical(lax.broadcasted_iota(I32, (n_rows, lanes), 0), 3)
    return (jnp.minimum(p + hi + 1, n_pos) - jnp.maximum(p - lo, 0)).astype(F32)


def _pool_kernel(z_ref, ps_ref, o_ref, v_ref, *, tiles_per_group, grid_rows, grid_cols):
    group = pl.program_id(1) // tiles_per_group
    lanes = z_ref.shape[1]
    ps = ps_ref[...]

    def pool_1d(w):
        z = z_ref[...].astype(F32)
        n = z.shape[0] // SUBLANES
        mean = _box_sum(z, w) / _window_count(z.shape[0], lanes, n, w)
        o_ref[...] = ((mean - z) * ps).astype(o_ref.dtype)

    def pool_2d(w):
        lo = w // 2
        hi = w - 1 - lo
        blk = grid_cols * SUBLANES
        cw = _window_count(blk, lanes, grid_cols, w)

        def zrow(r):
            return z_ref[pl.ds(pl.multiple_of(r * blk, blk), blk), :].astype(F32)

        v = jnp.zeros((blk, lanes), F32)
        for r in range(hi):
            v = v + zrow(r)
        v_ref[...] = v

        def body(r, carry):
            add = r + hi
            sub = r - lo - 1
            v = v_ref[...]
            v = v + jnp.where(add < grid_rows, zrow(jnp.minimum(add, grid_rows - 1)), 0.0)
            v = v - jnp.where(sub >= 0, zrow(jnp.maximum(sub, 0)), 0.0)
            v_ref[...] = v
            ch = (jnp.minimum(r + hi + 1, grid_rows) - jnp.maximum(r - lo, 0)).astype(F32)
            mean = _box_sum(v, w) / (ch * cw)
            o_ref[pl.ds(pl.multiple_of(r * blk, blk), blk), :] = ((mean - zrow(r)) * ps).astype(o_ref.dtype)
            return carry

        lax.fori_loop(0, grid_rows, body, 0)

    for g, w in enumerate(POOL_WINDOWS):
        @pl.when(group == g)
        def _(w=w):
            if grid_rows is None:
                pool_1d(w)
            else:
                pool_2d(w)


def _pool(z, prev_out, pool_scale, rows_per_group, n_groups, first_block, grid_rows):
    t, d_pool = z.shape
    gw = d_pool // len(POOL_WINDOWS)
    lanes = LANES
    blk = GRID_W * SUBLANES
    kern = functools.partial(_pool_kernel, tiles_per_group=gw // lanes, grid_rows=grid_rows, grid_cols=GRID_W)
    in_specs = [pl.BlockSpec((rows_per_group, lanes), lambda g, j: (first_block + g, j)),
                pl.BlockSpec((1, lanes), lambda g, j: (0, j))]
    args = [z, pool_scale.reshape(1, d_pool)]
    aliases = {}
    if prev_out is not None:
        in_specs.append(pl.BlockSpec(memory_space=pl.ANY))
        args.append(prev_out)
        aliases = {2: 0}
        kern_fn = lambda z_ref, ps_ref, prev_ref, o_ref, v_ref: kern(z_ref, ps_ref, o_ref, v_ref)
    else:
        kern_fn = kern
    return pl.pallas_call(
        kern_fn,
        grid=(n_groups, d_pool // lanes),
        in_specs=in_specs,
        out_specs=pl.BlockSpec((rows_per_group, lanes), lambda g, j: (first_block + g, j)),
        out_shape=jax.ShapeDtypeStruct((t, d_pool), BF16),
        scratch_shapes=[pltpu.VMEM((blk, lanes), F32)],
        input_output_aliases=aliases,
        compiler_params=_cparams(("arbitrary", "arbitrary")),
        name="pool_grid" if grid_rows is not None else "pool_seq",
    )(*args)


def _route_kernel(lg_ref, oi_ref, of_ref, cnt_ref, tri_ref, carry_ref, *, n_groups, per_group):
    step = pl.program_id(0)
    tt = lg_ref.shape[1]
    n_exp = n_groups * per_group

    @pl.when(step == 0)
    def _():
        r = lax.broadcasted_iota(I32, tri_ref.shape, 0)
        c = lax.broadcasted_iota(I32, tri_ref.shape, 1)
        tri_ref[...] = (r <= c).astype(BF16)
        carry_ref[...] = jnp.zeros_like(carry_ref)

    row = lambda k: lg_ref[k:k + 1, :]
    cmax = row(0)
    gi = jnp.zeros((1, tt), I32)
    for g in range(1, n_groups):
        better = row(g) > cmax
        gi = jnp.where(better, g, gi)
        cmax = jnp.where(better, row(g), cmax)
    denom = jnp.zeros((1, tt), F32)
    for g in range(n_groups):
        denom = denom + jnp.exp(row(g) - cmax)
    pg = 1.0 / denom
    fine = []
    for j in range(per_group):
        f = row(n_groups + j)
        for g in range(1, n_groups):
            f = jnp.where(gi == g, row(n_groups + g * per_group + j), f)
        fine.append(f)
    v1 = fine[0]
    i1 = jnp.zeros((1, tt), I32)
    for j in range(1, per_group):
        better = fine[j] > v1
        i1 = jnp.where(better, j, i1)
        v1 = jnp.where(better, fine[j], v1)
    v2 = jnp.full((1, tt), -jnp.inf, F32)
    i2 = jnp.zeros((1, tt), I32)
    for j in range(per_group):
        better = jnp.logical_and(i1 != j, fine[j] > v2)
        i2 = jnp.where(better, j, i2)
        v2 = jnp.where(better, fine[j], v2)
    ex = jnp.exp(v2 - v1)
    w1 = (1.0 / (1.0 + ex)) * pg
    w2 = (ex / (1.0 + ex)) * pg
    e1 = gi * per_group + i1
    e2 = gi * per_group + i2
    eid = lax.broadcasted_iota(I32, (n_exp, tt), 0)
    hit1 = eid == e1
    hit2 = eid == e2
    member = jnp.logical_or(hit1, hit2).astype(BF16)
    cb = tri_ref.shape[0]
    carry = carry_ref[...]
    cums = []
    for s in range(tt // cb):
        c = jnp.dot(member[:, s * cb:(s + 1) * cb], tri_ref[...], preferred_element_type=F32) + carry
        carry = c[:, cb - 1:cb]
        cums.append(c)
    carry_ref[...] = carry
    cum = jnp.concatenate(cums, axis=1)
    rank1 = jnp.sum(jnp.where(hit1, cum, 0.0), axis=0, keepdims=True) - 1.0
    rank2 = jnp.sum(jnp.where(hit2, cum, 0.0), axis=0, keepdims=True) - 1.0
    zi = jnp.zeros((SUBLANES - 4, tt), I32)
    oi_ref[...] = jnp.concatenate([e1, e2, rank1.astype(I32), rank2.astype(I32), zi], axis=0)
    of_ref[...] = jnp.concatenate([w1, w2, jnp.zeros((of_ref.shape[0] - 2, tt), F32)], axis=0)
    cnt_ref[...] = jnp.broadcast_to(carry, cnt_ref.shape).astype(I32)


def _route(logits_t, n_groups, per_group):
    nr, t = logits_t.shape
    cb = min(CUMSUM_BLOCK, t)
    tt = max(k for k in range(cb, min(ROUTE_TILE, t) + 1, cb) if t % k == 0)
    n_exp = n_groups * per_group
    return pl.pallas_call(
        functools.partial(_route_kernel, n_groups=n_groups, per_group=per_group),
        grid=(t // tt,),
        in_specs=[pl.BlockSpec((nr, tt), lambda i: (0, i))],
        out_specs=[pl.BlockSpec((SUBLANES, tt), lambda i: (0, i)),
                   pl.BlockSpec((LANES, tt), lambda i: (0, i)),
                   pl.BlockSpec((n_exp, LANES), lambda i: (0, 0))],
        out_shape=[jax.ShapeDtypeStruct((SUBLANES, t), I32),
                   jax.ShapeDtypeStruct((LANES, t), F32),
                   jax.ShapeDtypeStruct((n_exp, LANES), I32)],
        scratch_shapes=[pltpu.VMEM((cb, cb), BF16), pltpu.VMEM((n_exp, 1), F32)],
        compiler_params=_cparams(("arbitrary",)),
        name="route",
    )(logits_t)


def _plan_kernel(ri_ref, cnt_ref, pos_ref, te_ref, nv_ref, off_ref, *, tmx):
    n_exp = cnt_ref.shape[0]
    tt = ri_ref.shape[1]
    shift = tmx.bit_length() - 1
    cnt = cnt_ref[:, 0:1]
    n_tile = jnp.right_shift(cnt + (tmx - 1), shift)
    run = jnp.zeros((1, 1), I32)
    offs = []
    for e in range(n_exp):
        offs.append(run)
        run = run + jnp.left_shift(n_tile[e:e + 1, :], shift)
    off = jnp.concatenate(offs, axis=0)
    eid = lax.broadcasted_iota(I32, (n_exp, tt), 0)
    place = lambda e_row, r_row: jnp.sum(jnp.where(eid == e_row, off, 0), axis=0, keepdims=True) + r_row
    p1 = place(ri_ref[0:1, :], ri_ref[2:3, :])
    p2 = place(ri_ref[1:2, :], ri_ref[3:4, :])
    pos_ref[...] = jnp.concatenate([p1, p2, jnp.zeros((SUBLANES - TOP_K_FINE, tt), I32)], axis=0)
    end_tile = jnp.right_shift(off, shift) + n_tile
    k = lax.broadcasted_iota(I32, (n_exp, te_ref.shape[1]), 1)
    te = jnp.sum((k >= end_tile).astype(I32), axis=0, keepdims=True)
    te_ref[...] = jnp.minimum(te, n_exp - 1)
    nv_ref[...] = jnp.broadcast_to(jnp.right_shift(run, shift), nv_ref.shape)
    off_ref[...] = jnp.broadcast_to(off, off_ref.shape)


def _plan(ri, counts, tmx, n_tiles):
    _, t = ri.shape
    n_exp = counts.shape[0]
    tt = max(k for k in range(LANES, min(ROUTE_TILE, t) + 1, LANES) if t % k == 0)
    ntp = -(-n_tiles // LANES) * LANES
    pos, te, nv, off = pl.pallas_call(
        functools.partial(_plan_kernel, tmx=tmx),
        grid=(t // tt,),
        in_specs=[pl.BlockSpec((SUBLANES, tt), lambda i: (0, i)),
                  pl.BlockSpec((n_exp, LANES), lambda i: (0, 0))],
        out_specs=[pl.BlockSpec((SUBLANES, tt), lambda i: (0, i)),
                   pl.BlockSpec((1, ntp), lambda i: (0, 0)),
                   pl.BlockSpec((1, LANES), lambda i: (0, 0)),
                   pl.BlockSpec((n_exp, LANES), lambda i: (0, 0))],
        out_shape=[jax.ShapeDtypeStruct((SUBLANES, t), I32),
                   jax.ShapeDtypeStruct((1, ntp), I32),
                   jax.ShapeDtypeStruct((1, LANES), I32),
                   jax.ShapeDtypeStruct((n_exp, LANES), I32)],
        compiler_params=_cparams(("arbitrary",)),
        name="plan",
    )(ri, counts)
    return pos[0], pos[1], te[0, :n_tiles], nv[0, :1], off[:, 0]


def _load_positions(pos1_hbm, pos2_hbm, p1_ref, p2_ref, isem, step):
    c1 = pltpu.make_async_copy(pos1_hbm.at[step], p1_ref, isem.at[0])
    c2 = pltpu.make_async_copy(pos2_hbm.at[step], p2_ref, isem.at[1])
    c1.start()
    c2.start()
    c1.wait()
    c2.wait()


def _dispatch_kernel(off_ref, cnt_ref, pos1_hbm, pos2_hbm, h2_ref, xs_hbm, p1_ref, p2_ref, zero_ref,
                     isem, sem, zsem, *, tmx):
    step = pl.program_id(0)
    n_blk = h2_ref.shape[0] // SUBLANES
    _load_positions(pos1_hbm, pos2_hbm, p1_ref, p2_ref, isem, step)

    def row_copy(j, p):
        return pltpu.make_async_copy(h2_ref.at[j], xs_hbm.at[p], sem)

    def send(blk, c):
        j0 = blk * SUBLANES
        for s in range(SUBLANES):
            row_copy(j0 + s, p1_ref[j0 + s]).start(priority=0)
            row_copy(j0 + s, p2_ref[j0 + s]).start(priority=1)
        return c
    lax.fori_loop(0, n_blk, send, 0, unroll=2)
    _wait_many(row_copy(0, 0), TOP_K_FINE * n_blk * SUBLANES)

    @pl.when(step == pl.num_programs(0) - 1)
    def _():
        zero_ref[...] = jnp.zeros_like(zero_ref)
        shift = tmx.bit_length() - 1

        def pad_expert(e, c):
            cnt = cnt_ref[e]
            n_pad = jnp.left_shift(jnp.right_shift(cnt + (tmx - 1), shift), shift) - cnt
            first = off_ref[e] + cnt
            zero_copy = lambda r: pltpu.make_async_copy(zero_ref, xs_hbm.at[first + r], zsem)

            def fill(r, c2):
                zero_copy(r).start()
                return c2
            lax.fori_loop(0, n_pad, fill, 0)

            def done(r, c2):
                zero_copy(r).wait()
                return c2
            lax.fori_loop(0, n_pad, done, 0)
            return c
        lax.fori_loop(0, off_ref.shape[0], pad_expert, 0)


def _dispatch(offsets, counts, pos1, pos2, h2p, t, tmx, ts):
    nc = h2p.shape[0] // t
    n_exp = offsets.shape[0]
    p_max = TOP_K_FINE * t + n_exp * tmx
    hbm = lambda: pl.BlockSpec(memory_space=pl.ANY)
    grid_spec = pltpu.PrefetchScalarGridSpec(
        num_scalar_prefetch=2,
        grid=(t // ts,),
        in_specs=[hbm(), hbm(),
                  pl.BlockSpec((ts, nc, LANES), lambda i, off, cnt: (i, 0, 0))],
        out_specs=hbm(),
        scratch_shapes=[pltpu.SMEM((ts,), I32),
                        pltpu.SMEM((ts,), I32),
                        pltpu.VMEM((nc, LANES), U32),
                        pltpu.SemaphoreType.DMA((2,)),
                        pltpu.SemaphoreType.DMA(()),
                        pltpu.SemaphoreType.DMA(())])
    xs = pl.pallas_call(
        functools.partial(_dispatch_kernel, tmx=tmx),
        grid_spec=grid_spec,
        out_shape=jax.ShapeDtypeStruct((p_max, nc, LANES), U32),
        compiler_params=_cparams(("arbitrary",)),
        name="dispatch",
    )(offsets, counts, pos1.reshape(t // ts, ts), pos2.reshape(t // ts, ts), h2p.reshape(t, nc, LANES))
    return xs.reshape(p_max * nc, LANES)


def _expert_kernel(te_ref, nv_ref, x_ref, w1_ref, w3_ref, w2_ref, y_ref, *, tmx):
    @pl.when(pl.program_id(0) < nv_ref[0])
    def _():
        lo, hi = _unpack_bf16_pair(_load_token_tiles(x_ref, tmx))
        lo = lo.astype(BF16)
        hi = hi.astype(BF16)
        half = lo.shape[1]
        mm = lambda w_ref: (jnp.dot(lo, w_ref[0, :half, :].astype(BF16), preferred_element_type=F32)
                            + jnp.dot(hi, w_ref[0, half:, :].astype(BF16), preferred_element_type=F32))
        h1 = mm(w1_ref)
        h3 = mm(w3_ref)
        hh = (h1 * jax.nn.sigmoid(h1) * h3).astype(BF16)
        y = jnp.dot(hh, w2_ref[0].astype(BF16), preferred_element_type=F32)
        _store_token_tiles(y_ref, _pack_bf16_pair(y[:, :half], y[:, half:]))


def _expert_mlp(xs, tile_expert, n_valid, w1, w3, w2, tmx):
    d, de = w1.shape[1], w1.shape[2]
    nc = d // 2 // LANES
    p = xs.shape[0] // nc
    tile = lambda i, te, nv: (jnp.minimum(i, nv[0] - 1), 0)
    grid_spec = pltpu.PrefetchScalarGridSpec(
        num_scalar_prefetch=2,
        grid=(p // tmx,),
        in_specs=[pl.BlockSpec((tmx * nc, LANES), tile),
                  pl.BlockSpec((1, d, de), lambda i, te, nv: (te[i], 0, 0)),
                  pl.BlockSpec((1, d, de), lambda i, te, nv: (te[i], 0, 0)),
                  pl.BlockSpec((1, de, d), lambda i, te, nv: (te[i], 0, 0))],
        out_specs=pl.BlockSpec((tmx * nc, LANES), tile))
    return pl.pallas_call(
        functools.partial(_expert_kernel, tmx=tmx),
        grid_spec=grid_spec,
        out_shape=jax.ShapeDtypeStruct((p * nc, LANES), U32),
        compiler_params=_cparams(("arbitrary",)),
        name="expert_mlp",
    )(tile_expert, n_valid, xs, w1, w3, w2)


def _combine_kernel(pos1_hbm, pos2_hbm, ys_hbm, x1_ref, rf_ref, g2_ref, fg_ref, ylat_hbm, yctx_hbm,
                    p1_ref, p2_ref, gbuf, obuf, isem, gsem, osem, *, geom):
    step = pl.program_id(0)
    n_steps = pl.num_programs(0)
    tm = geom.tm
    n_blk = tm // SUBLANES
    slot = lax.rem(step, 2)

    nc = ys_hbm.shape[1]

    def row_copy(gs, k, p, j):
        rows = pl.ds(pl.multiple_of(j * nc, nc), nc)
        return pltpu.make_async_copy(ys_hbm.at[p], gbuf.at[gs, k, rows, :], gsem.at[gs])

    def gather(tile, gs):
        _load_positions(pos1_hbm, pos2_hbm, p1_ref, p2_ref, isem, tile)

        def fetch(blk, c):
            j0 = blk * SUBLANES
            for s in range(SUBLANES):
                row_copy(gs, 0, p1_ref[j0 + s], j0 + s).start(priority=0)
                row_copy(gs, 1, p2_ref[j0 + s], j0 + s).start(priority=1)
            return c
        lax.fori_loop(0, n_blk, fetch, 0, unroll=2)

    @pl.when(step == 0)
    def _():
        gather(step, slot)

    @pl.when(step + 1 < n_steps)
    def _():
        gather(step + 1, 1 - slot)

    _wait_many(row_copy(slot, 0, 0, 0), TOP_K_FINE * tm)

    wt = rf_ref[...].T
    a_lo, a_hi = _unpack_bf16_pair(_load_token_tiles(gbuf.at[slot, 0], tm))
    b_lo, b_hi = _unpack_bf16_pair(_load_token_tiles(gbuf.at[slot, 1], tm))
    w1 = wt[:, 0:1]
    w2 = wt[:, 1:2]
    moe = jnp.concatenate([w1 * a_lo + w2 * b_lo, w1 * a_hi + w2 * b_hi], axis=1)
    x = x1_ref[...] + _gate(moe, g2_ref[0])
    out = _rms(x, fg_ref[...])

    start = lambda cp: cp.start()
    wait = lambda cp: cp.wait()
    put = functools.partial(_tile_copies, ylat_hbm, yctx_hbm, geom=geom, to_rows=False)

    @pl.when(step >= 2)
    def _():
        put(obuf.at[slot], osem.at[slot], step - 2, fn=wait)

    obuf[slot] = out.reshape(obuf.shape[1:])
    put(obuf.at[slot], osem.at[slot], step, fn=start)

    @pl.when(step == n_steps - 1)
    def _():
        @pl.when(n_steps > 1)
        def _():
            put(obuf.at[1 - slot], osem.at[1 - slot], step - 1, fn=wait)
        put(obuf.at[slot], osem.at[slot], step, fn=wait)


def _combine(pos1, pos2, ys, x1, rf, modtab, final_g, lat_shape, ctx_shape, geom):
    t, d = x1.shape
    tm = geom.tm
    nc = d // 2 // LANES
    ys = ys.reshape(ys.shape[0] // nc, nc, LANES)
    mset = lambda i: jnp.where(i < geom.n_lat_tiles, 0, 1)
    hbm = lambda: pl.BlockSpec(memory_space=pl.ANY)
    return pl.pallas_call(
        functools.partial(_combine_kernel, geom=geom),
        grid=(t // tm,),
        in_specs=[hbm(), hbm(), hbm(),
                  pl.BlockSpec((tm, d), lambda i: (i, 0)),
                  pl.BlockSpec((LANES, tm), lambda i: (0, i)),
                  pl.BlockSpec((1, SUBLANES, d), lambda i: (mset(i), 0, 5)),
                  pl.BlockSpec((1, d), lambda i: (0, 0))],
        out_specs=[hbm(), hbm()],
        out_shape=[jax.ShapeDtypeStruct(lat_shape, F32), jax.ShapeDtypeStruct(ctx_shape, F32)],
        scratch_shapes=[pltpu.SMEM((tm,), I32),
                        pltpu.SMEM((tm,), I32),
                        pltpu.VMEM((2, TOP_K_FINE, tm * nc, LANES), U32),
                        pltpu.VMEM((2, tm // SUBLANES, SUBLANES, d), F32),
                        pltpu.SemaphoreType.DMA((2,)),
                        pltpu.SemaphoreType.DMA((2,)),
                        pltpu.SemaphoreType.DMA((2,))],
        compiler_params=_cparams(("arbitrary",)),
        name="combine",
    )(pos1.reshape(t // tm, tm), pos2.reshape(t // tm, tm), ys, x1, rf, modtab, final_g.reshape(1, d))


def _tile_meta(groups, tm):
    rows, first, last, grp = [], [], [], []
    blk = 0
    for g, r in enumerate(groups):
        nc = r // tm
        for c in range(nc):
            rows.append(blk + c)
            first.append(int(c == 0))
            last.append(int(c == nc - 1))
            grp.append(g)
        blk += nc
    fwd = (np.array(rows, np.int32), np.array([first, last, grp], np.int32))
    order = []
    blk = 0
    for r in groups:
        nc = r // tm
        order.extend(range(blk + nc - 1, blk - 1, -1))
        blk += nc
    order = np.array(order)
    bwd = (fwd[0][order], fwd[1][:, order])
    return fwd, bwd


def kernel(x_prompt, x_sample, state_lru, c, c_ctx, w_mod, b_mod, norm1_g, w_in, conv_w, conv_b, lru_wa, lru_ba, lru_wx, lru_bx, lru_lambda, pool_w, pool_scale, w_out, norm2_g, router_coarse_w, router_coarse_b, router_fine_w, router_fine_b, exp_w1, exp_w3, exp_w2, final_norm_g):
    bp, sp, d = x_prompt.shape
    bs, ss, _ = x_sample.shape
    d_lru = lru_lambda.shape[-1]
    heads, bw = lru_wa.shape[2], lru_wa.shape[3]
    n_groups, per_group = router_fine_w.shape[2], router_fine_w.shape[3]
    n_exp = n_groups * per_group
    assert w_mod.shape[0] == 1 and bs == SUBLANES and bp % SUBLANES == 0 and ss % GRID_W == 0
    assert EXPERT_TILE & (EXPERT_TILE - 1) == 0
    n_lat_groups, n_ctx_groups = bs // SUBLANES, bp // SUBLANES
    lat_rows, ctx_rows = ss * SUBLANES, sp * SUBLANES
    assert (n_lat_groups * lat_rows) % ctx_rows == 0
    tm = min(TOKEN_TILE, ctx_rows, lat_rows)
    geom = _Geom(tm=tm, n_lat_tiles=n_lat_groups * lat_rows // tm, lat_chunks=lat_rows // tm,
                 ctx_chunks=ctx_rows // tm)
    n_rows = n_lat_groups * lat_rows + n_ctx_groups * ctx_rows
    groups = [lat_rows] * n_lat_groups + [ctx_rows] * n_ctx_groups
    (f_rows, f_flags), (b_rows, b_flags) = _tile_meta(groups, tm)

    cond = jnp.zeros((2 * SUBLANES, d), F32).at[:bs].set(c).at[bs].set(c_ctx)
    mod = _modulation(cond, w_mod[0], b_mod[0])
    modtab = jnp.stack([mod[:SUBLANES], jnp.broadcast_to(mod[SUBLANES], (SUBLANES, mod.shape[1]))])
    h0_lat = state_lru[:, 0].reshape(n_lat_groups, SUBLANES, 2, d_lru)
    h0 = jnp.concatenate([h0_lat, jnp.zeros((n_ctx_groups, SUBLANES, 2, d_lru), F32)], axis=0)
    h0 = h0.transpose(0, 2, 1, 3)

    xa, ga, z = _input_projection(x_sample, x_prompt, modtab, norm1_g[0], w_in[0],
                                  pool_w[0].astype(BF16), n_rows, geom)

    def gate_weights(direction):
        wg = jnp.concatenate([lru_wa[0, direction], lru_wx[0, direction]], axis=-1).astype(BF16)
        bg = jnp.concatenate([lru_ba[0, direction].reshape(heads, 1, bw),
                              lru_bx[0, direction].reshape(heads, 1, bw)], axis=-1)
        return wg, bg

    log_decay = jax.nn.log_sigmoid(lru_lambda[0])
    wg_f, bg_f = gate_weights(0)
    wg_b, bg_b = gate_weights(1)
    hf, hf_last = _forward_scan(xa, (jnp.asarray(f_rows), jnp.asarray(f_flags)), conv_w[0], conv_b[0],
                                wg_f, bg_f, log_decay[0], h0[:, 0], tm)

    yb = _pool(z, None, pool_scale[0], lat_rows, n_lat_groups, 0, lat_rows // (GRID_W * SUBLANES))
    yb = _pool(z, yb, pool_scale[0], ctx_rows, n_ctx_groups, n_lat_groups * lat_rows // ctx_rows, None)

    n_logits = n_groups + n_exp
    rw = jnp.concatenate([router_coarse_w[0], router_fine_w[0].reshape(d, n_exp)], axis=1)
    rwt = jnp.zeros((LANES, d), BF16).at[:n_logits].set(rw.T.astype(BF16))
    rb = jnp.zeros((LANES, 1), F32).at[:n_logits, 0].set(
        jnp.concatenate([router_coarse_b[0], router_fine_b[0].reshape(n_exp)]))
    x1, h2p, logits_t, hb_last = _backward_scan_mix(
        xa, ga, hf, yb, x_sample, x_prompt, modtab, (jnp.asarray(b_rows), jnp.asarray(b_flags)),
        conv_w[0], conv_b[0], wg_b, bg_b, log_decay[1], h0[:, 1], w_out[0].astype(BF16), norm2_g[0],
        rwt, rb, geom)

    ri, rf, counts = _route(logits_t, n_groups, per_group)
    n_tiles = (TOP_K_FINE * n_rows + n_exp * EXPERT_TILE) // EXPERT_TILE
    pos1, pos2, tile_expert, n_valid, offsets = _plan(ri, counts, EXPERT_TILE, n_tiles)
    xs = _dispatch(offsets, counts[:, 0], pos1, pos2, h2p, n_rows, EXPERT_TILE, tm)
    ys = _expert_mlp(xs, tile_expert, n_valid, exp_w1[0], exp_w3[0], exp_w2[0], EXPERT_TILE)
    y_sample, y_prompt = _combine(pos1, pos2, ys, x1, rf, modtab, final_norm_g, x_sample.shape,
                                  x_prompt.shape, geom)

    st = jnp.stack([hf_last[n_lat_groups:], hb_last[n_lat_groups:]], axis=2)
    state_new = st.reshape(bp, 1, 2, d_lru).astype(x_prompt.dtype)
    return (y_prompt, y_sample, state_new)
```

```python
import functools
from typing import NamedTuple

import numpy as np
import jax
import jax.numpy as jnp
from jax import lax
from jax.experimental import pallas as pl
from jax.experimental.pallas import tpu as pltpu

GRID_W = 64
CONV_W = 4
RG_C = 8.0
POOL_WINDOWS = (2, 4, 8, 16)
TOP_K_FINE = 2
EPS = 1e-6
EXPM1_SERIES_BELOW = 0.125

SUBLANES = 8
LANES = 128
BF16_ROWS = 16
TOKEN_TILE = 512
EXPERT_TILE = 512
ROUTE_TILE = 2048
SEG_SHIFT = 3
SEG_CHUNK = 1 << SEG_SHIFT
MOD_COL_TILE = 1024
WEIGHT_STAGE_ROWS = 256
VMEM_LIMIT = 60 * 1024 * 1024

F32 = jnp.float32
BF16 = jnp.bfloat16
U32 = jnp.uint32
I32 = jnp.int32
HIGH_HALF = 0xFFFF0000


class _Geom(NamedTuple):
    tm: int
    n_lat_tiles: int
    lat_chunks: int
    ctx_chunks: int


def _cparams(sem):
    return pltpu.CompilerParams(dimension_semantics=sem, vmem_limit_bytes=VMEM_LIMIT)


def _per_sequence(y, m):
    rows, d = y.shape
    return y.reshape(rows // SUBLANES, SUBLANES, d), m[None]


def _modulate(y, scale, shift):
    y3, sc = _per_sequence(y, scale)
    _, sh = _per_sequence(y, shift)
    return (y3 * (1.0 + sc) + sh).reshape(y.shape)


def _gate(y, g):
    y3, g3 = _per_sequence(y, g)
    return (y3 * g3).reshape(y.shape)


def _rms(x, g):
    ms = jnp.mean(x * x, axis=-1, keepdims=True)
    return x * lax.rsqrt(ms + EPS) * g


def _pack_bf16_pair(lo, hi):
    lo_bits = lax.shift_right_logical(lax.bitcast_convert_type(lo.astype(BF16).astype(F32), U32), U32(16))
    hi_bits = lax.bitcast_convert_type(hi.astype(BF16).astype(F32), U32) & U32(HIGH_HALF)
    return lo_bits | hi_bits


def _unpack_bf16_pair(w):
    lo = lax.bitcast_convert_type(lax.shift_left(w, U32(16)), F32)
    hi = lax.bitcast_convert_type(w & U32(HIGH_HALF), F32)
    return lo, hi


def _store_token_tiles(ref, w):
    n, words = w.shape
    nc = words // LANES
    for c in range(nc):
        ref[pl.ds(c, n, stride=nc), :] = w[:, c * LANES:(c + 1) * LANES]


def _load_token_tiles(ref, n):
    nc = ref.shape[0] // n
    return jnp.concatenate([ref[pl.ds(c, n, stride=nc), :] for c in range(nc)], axis=1)


def _tile_copies(lat_hbm, ctx_hbm, buf, sem, tile, geom, to_rows, fn):
    tt = geom.tm // SUBLANES

    def run(hbm, k, chunks):
        g = lax.div(k, jnp.int32(chunks))
        c = lax.rem(k, jnp.int32(chunks))
        for b in range(SUBLANES):
            h = hbm.at[g * SUBLANES + b, pl.ds(c * tt, tt), :]
            v = buf.at[:, b, :]
            fn(pltpu.make_async_copy(h, v, sem) if to_rows else pltpu.make_async_copy(v, h, sem))

    @pl.when(tile < geom.n_lat_tiles)
    def _():
        run(lat_hbm, tile, geom.lat_chunks)

    @pl.when(tile >= geom.n_lat_tiles)
    def _():
        run(ctx_hbm, tile - geom.n_lat_tiles, geom.ctx_chunks)


def _fetch_rows(lat_hbm, ctx_hbm, xbuf, sem, step, n_steps, tile_of, geom):
    slot = lax.rem(step, 2)
    start = lambda cp: cp.start()
    wait = lambda cp: cp.wait()

    @pl.when(step == 0)
    def _():
        _tile_copies(lat_hbm, ctx_hbm, xbuf.at[0], sem.at[0], tile_of(jnp.int32(0)), geom, True, start)

    @pl.when(step + 1 < n_steps)
    def _():
        nxt = 1 - slot
        _tile_copies(lat_hbm, ctx_hbm, xbuf.at[nxt], sem.at[nxt], tile_of(step + 1), geom, True, start)

    _tile_copies(lat_hbm, ctx_hbm, xbuf.at[slot], sem.at[slot], tile_of(step), geom, True, wait)
    tt, _, d = xbuf.shape[1:]
    return xbuf[slot].reshape(tt * SUBLANES, d)


def _mod_kernel(c_ref, w_ref, b_ref, o_ref):
    c = c_ref[...]
    s = c * jax.nn.sigmoid(c)
    o_ref[...] = jnp.dot(s.astype(BF16), w_ref[...].astype(BF16),
                         preferred_element_type=F32) + b_ref[...]


def _modulation(cond, w_mod, b_mod):
    rows, d = cond.shape
    n = w_mod.shape[1]
    tn = min(MOD_COL_TILE, n)
    return pl.pallas_call(
        _mod_kernel,
        grid=(n // tn,),
        in_specs=[pl.BlockSpec((rows, d), lambda j: (0, 0)),
                  pl.BlockSpec((d, tn), lambda j: (0, j)),
                  pl.BlockSpec((1, tn), lambda j: (0, j))],
        out_specs=pl.BlockSpec((rows, tn), lambda j: (0, j)),
        out_shape=jax.ShapeDtypeStruct((rows, n), F32),
        compiler_params=_cparams(("arbitrary",)),
        name="modulation",
    )(cond, w_mod, b_mod.reshape(1, n))


def _round_weights(w_hbm, wbf_ref, stage, sem):
    rows = stage.shape[1]
    n = w_hbm.shape[0] // rows
    copy = lambda r: pltpu.make_async_copy(w_hbm.at[pl.ds(r * rows, rows), :], stage.at[r % 2], sem.at[r % 2])
    copy(0).start()
    for r in range(n):
        if r + 1 < n:
            copy(r + 1).start()
        copy(r).wait()
        wbf_ref[r * rows:(r + 1) * rows, :] = stage[r % 2].astype(wbf_ref.dtype)


def _proj_kernel(lat_hbm, ctx_hbm, sh_ref, sc_ref, g_ref, win_hbm, pw_ref, xa_ref, ga_ref, z_ref,
                 xbuf, sem, wbf_ref, wstage, wsem, *, d_lru, gw, geom):
    step = pl.program_id(0)

    @pl.when(step == 0)
    def _():
        _round_weights(win_hbm, wbf_ref, wstage, wsem)

    x = _fetch_rows(lat_hbm, ctx_hbm, xbuf, sem, step, pl.num_programs(0), lambda s: s, geom)
    h = _modulate(_rms(x, g_ref[...]), sc_ref[0], sh_ref[0])
    proj = jnp.dot(h.astype(BF16), wbf_ref[...], preferred_element_type=F32)
    xa_ref[...] = proj[:, :d_lru].astype(BF16)
    ga_ref[...] = proj[:, d_lru:2 * d_lru].astype(BF16)
    for g in range(pw_ref.shape[0]):
        lo = 2 * d_lru + g * gw
        z_ref[:, g * gw:(g + 1) * gw] = jnp.dot(
            proj[:, lo:lo + gw].astype(BF16), pw_ref[g], preferred_element_type=F32).astype(BF16)


def _input_projection(x_lat, x_ctx, modtab, norm_g, w_in, pool_w, n_rows, geom):
    d = x_lat.shape[-1]
    tm = geom.tm
    d_pool = pool_w.shape[0] * pool_w.shape[1]
    d_lru = (w_in.shape[1] - d_pool) // 2
    mset = lambda i: jnp.where(i < geom.n_lat_tiles, 0, 1)
    const = dict(pipeline_mode=pl.Buffered(1))
    return pl.pallas_call(
        functools.partial(_proj_kernel, d_lru=d_lru, gw=pool_w.shape[1], geom=geom),
        grid=(n_rows // tm,),
        in_specs=[pl.BlockSpec(memory_space=pl.ANY),
                  pl.BlockSpec(memory_space=pl.ANY),
                  pl.BlockSpec((1, SUBLANES, d), lambda i: (mset(i), 0, 0)),
                  pl.BlockSpec((1, SUBLANES, d), lambda i: (mset(i), 0, 1)),
                  pl.BlockSpec((1, d), lambda i: (0, 0)),
                  pl.BlockSpec(memory_space=pl.ANY),
                  pl.BlockSpec(pool_w.shape, lambda i: (0, 0, 0), **const)],
        out_specs=[pl.BlockSpec((tm, d_lru), lambda i: (i, 0)),
                   pl.BlockSpec((tm, d_lru), lambda i: (i, 0)),
                   pl.BlockSpec((tm, d_pool), lambda i: (i, 0))],
        out_shape=[jax.ShapeDtypeStruct((n_rows, d_lru), BF16),
                   jax.ShapeDtypeStruct((n_rows, d_lru), BF16),
                   jax.ShapeDtypeStruct((n_rows, d_pool), BF16)],
        scratch_shapes=[pltpu.VMEM((2, tm // SUBLANES, SUBLANES, d), F32),
                        pltpu.SemaphoreType.DMA((2,)),
                        pltpu.VMEM(w_in.shape, BF16),
                        pltpu.VMEM((2, min(WEIGHT_STAGE_ROWS, d), w_in.shape[1]), F32),
                        pltpu.SemaphoreType.DMA((2,))],
        compiler_params=_cparams(("arbitrary",)),
        name="input_projection",
    )(x_lat, x_ctx, modtab, modtab, norm_g.reshape(1, d), w_in, pool_w)


def _fill_ext(ext_ref, prev_ref, main_ref, next_ref, first, last):
    tm = main_ref.shape[0]
    prev = prev_ref[...].astype(F32)
    nxt = next_ref[...].astype(F32)
    ext_ref[0:BF16_ROWS, :] = jnp.where(first, 0.0, prev)
    ext_ref[BF16_ROWS:BF16_ROWS + tm, :] = main_ref[...].astype(F32)
    ext_ref[BF16_ROWS + tm:, :] = jnp.where(last, 0.0, nxt)


def _one_minus_exp(y, exp_y):
    p = 1.0 / 120.0
    for c in (1.0 / 24.0, 1.0 / 6.0, 0.5, 1.0):
        p = p * y + c
    return jnp.where(y > -EXPM1_SERIES_BELOW, -y * p, 1.0 - exp_y)


def _sqrt_nonneg(q):
    return jnp.where(q > 0.0, q * lax.rsqrt(q), 0.0)


def _decay_and_input(ext_ref, cw_ref, cb_ref, wg_ref, bg_ref, lam_ref, a_ref, u_ref, tm):
    heads, bw = wg_ref.shape[0], wg_ref.shape[1]
    for hd in range(heads):
        sl = slice(hd * bw, (hd + 1) * bw)
        xc = cb_ref[:, sl]
        for k in range(CONV_W):
            xc = xc + cw_ref[k:k + 1, sl] * ext_ref[SUBLANES * k:SUBLANES * k + tm, sl]
        g = jnp.dot(xc.astype(BF16), wg_ref[hd], preferred_element_type=F32) + bg_ref[hd]
        r = jax.nn.sigmoid(g[:, :bw])
        ig = jax.nn.sigmoid(g[:, bw:])
        log_a = (RG_C * r) * lam_ref[:, sl]
        a = jnp.exp(log_a)
        a_ref[:, sl] = a
        u_ref[:, sl] = _sqrt_nonneg(_one_minus_exp(2.0 * log_a, a * a)) * (ig * xc)


def _scan(a_ref, u_ref, h, tm, reverse):
    nblk = tm // SUBLANES

    def body(s, h):
        j = (nblk - 1 - s) if reverse else s
        rows = pl.ds(pl.multiple_of(j * SUBLANES, SUBLANES), SUBLANES)
        h = a_ref[rows, :] * h + u_ref[rows, :]
        u_ref[rows, :] = h
        return h

    return lax.fori_loop(0, nblk, body, h, unroll=8)


def _halo_specs(tm, d_lru, n_rows):
    per = tm // BF16_ROWS
    last_blk = n_rows // BF16_ROWS - 1
    return [pl.BlockSpec((BF16_ROWS, d_lru), lambda i, tr, fl: (jnp.maximum(tr[i] * per - 1, 0), 0)),
            pl.BlockSpec((tm, d_lru), lambda i, tr, fl: (tr[i], 0)),
            pl.BlockSpec((BF16_ROWS, d_lru), lambda i, tr, fl: (jnp.minimum((tr[i] + 1) * per, last_blk), 0))]


def _fwd_kernel(tr_ref, fl_ref, prev_ref, main_ref, next_ref, cw_ref, cb_ref, wg_ref, bg_ref, lam_ref,
                h0_ref, hf_ref, hlast_ref, ext_ref, a_ref, u_ref, h_ref):
    i = pl.program_id(0)
    tm = main_ref.shape[0]
    first = fl_ref[0, i] == 1
    last = fl_ref[1, i] == 1
    _fill_ext(ext_ref, prev_ref, main_ref, next_ref, first, last)
    _decay_and_input(ext_ref, cw_ref, cb_ref, wg_ref, bg_ref, lam_ref, a_ref, u_ref, tm)

    @pl.when(first)
    def _():
        h_ref[...] = h0_ref[0]

    h = _scan(a_ref, u_ref, h_ref[...], tm, reverse=False)
    h_ref[...] = h
    hlast_ref[0] = h
    hf_ref[...] = u_ref[...].astype(BF16)


def _forward_scan(xa, meta, conv_w, conv_b, wg, bg, lam, h0, tm):
    t, d_lru = xa.shape
    tile_row, flags = meta
    ngrp = h0.shape[0]
    const = dict(pipeline_mode=pl.Buffered(1))
    grid_spec = pltpu.PrefetchScalarGridSpec(
        num_scalar_prefetch=2,
        grid=(t // tm,),
        in_specs=_halo_specs(tm, d_lru, t) + [
            pl.BlockSpec(conv_w.shape, lambda i, tr, fl: (0, 0)),
            pl.BlockSpec((1, d_lru), lambda i, tr, fl: (0, 0)),
            pl.BlockSpec(wg.shape, lambda i, tr, fl: (0, 0, 0), **const),
            pl.BlockSpec(bg.shape, lambda i, tr, fl: (0, 0, 0)),
            pl.BlockSpec((1, d_lru), lambda i, tr, fl: (0, 0)),
            pl.BlockSpec((1, SUBLANES, d_lru), lambda i, tr, fl: (fl[2, i], 0, 0))],
        out_specs=[pl.BlockSpec((tm, d_lru), lambda i, tr, fl: (tr[i], 0)),
                   pl.BlockSpec((1, SUBLANES, d_lru), lambda i, tr, fl: (fl[2, i], 0, 0))],
        scratch_shapes=[pltpu.VMEM((tm + 2 * BF16_ROWS, d_lru), F32),
                        pltpu.VMEM((tm, d_lru), F32),
                        pltpu.VMEM((tm, d_lru), F32),
                        pltpu.VMEM((SUBLANES, d_lru), F32)])
    return pl.pallas_call(
        _fwd_kernel,
        grid_spec=grid_spec,
        out_shape=[jax.ShapeDtypeStruct((t, d_lru), BF16),
                   jax.ShapeDtypeStruct((ngrp, SUBLANES, d_lru), F32)],
        compiler_params=_cparams(("arbitrary",)),
        name="forward_scan",
    )(tile_row, flags, xa, xa, xa, conv_w, conv_b.reshape(1, d_lru), wg, bg, lam.reshape(1, d_lru), h0)


def _bwd_kernel(tr_ref, fl_ref, prev_ref, main_ref, next_ref, cw_ref, cb_ref, wg_ref, bg_ref, lam_ref,
                h0_ref, hf_ref, ga_ref, yb_ref, lat_hbm, ctx_hbm, g1_ref, sh2_ref, sc2_ref, wout_ref,
                n2_ref, rwt_ref, rb_ref,
                x1_ref, h2_ref, lg_ref, hlast_ref,
                ext_ref, a_ref, u_ref, h_ref, cat_ref, xbuf, xsem, *, geom):
    i = pl.program_id(0)
    n_steps = pl.num_programs(0)
    tm, d_lru = main_ref.shape
    x = _fetch_rows(lat_hbm, ctx_hbm, xbuf, xsem, i, n_steps,
                    lambda s: tr_ref[jnp.minimum(s, n_steps - 1)], geom)
    first = fl_ref[0, i] == 1
    last = fl_ref[1, i] == 1
    _fill_ext(ext_ref, prev_ref, main_ref, next_ref, first, last)
    _decay_and_input(ext_ref, cw_ref, cb_ref, wg_ref, bg_ref, lam_ref, a_ref, u_ref, tm)

    @pl.when(last)
    def _():
        h_ref[...] = h0_ref[0]

    h = _scan(a_ref, u_ref, h_ref[...], tm, reverse=True)
    h_ref[...] = h
    hlast_ref[0] = h

    ga = ga_ref[...].astype(F32)
    ya = (hf_ref[...].astype(F32) + u_ref[...]) * jax.nn.gelu(ga)
    cat_ref[:, :d_lru] = ya.astype(BF16)
    cat_ref[:, d_lru:] = yb_ref[...]
    mix = jnp.dot(cat_ref[...], wout_ref[...], preferred_element_type=F32)
    x1 = x + _gate(mix, g1_ref[0])
    x1_ref[...] = x1
    h2 = _modulate(_rms(x1, n2_ref[...]), sc2_ref[0], sh2_ref[0])
    half = h2.shape[1] // 2
    _store_token_tiles(h2_ref, _pack_bf16_pair(h2[:, :half], h2[:, half:]))
    lg_ref[...] = lax.dot_general(rwt_ref[...], h2.astype(BF16), (((1,), (1,)), ((), ())),
                                  preferred_element_type=F32) + rb_ref[...]


def _backward_scan_mix(xa, ga, hf, yb, x_lat, x_ctx, modtab, meta, conv_w, conv_b, wg, bg, lam, h0, w_out,
                       norm2_g, rwt, rb, geom):
    t, d_lru = xa.shape
    d = x_lat.shape[-1]
    d_pool = yb.shape[1]
    tm = geom.tm
    tile_row, flags = meta
    ngrp = h0.shape[0]
    nr = rwt.shape[0]
    const = dict(pipeline_mode=pl.Buffered(1))
    mset = lambda i, tr, fl: jnp.where(tr[i] < geom.n_lat_tiles, 0, 1)
    row = lambda i, tr, fl: (tr[i], 0)
    mod = lambda col: pl.BlockSpec((1, SUBLANES, d), lambda i, tr, fl: (mset(i, tr, fl), 0, col))
    grid_spec = pltpu.PrefetchScalarGridSpec(
        num_scalar_prefetch=2,
        grid=(t // tm,),
        in_specs=_halo_specs(tm, d_lru, t) + [
            pl.BlockSpec(conv_w.shape, lambda i, tr, fl: (0, 0)),
            pl.BlockSpec((1, d_lru), lambda i, tr, fl: (0, 0)),
            pl.BlockSpec(wg.shape, lambda i, tr, fl: (0, 0, 0), **const),
            pl.BlockSpec(bg.shape, lambda i, tr, fl: (0, 0, 0)),
            pl.BlockSpec((1, d_lru), lambda i, tr, fl: (0, 0)),
            pl.BlockSpec((1, SUBLANES, d_lru), lambda i, tr, fl: (fl[2, i], 0, 0)),
            pl.BlockSpec((tm, d_lru), row),
            pl.BlockSpec((tm, d_lru), row),
            pl.BlockSpec((tm, d_pool), row),
            pl.BlockSpec(memory_space=pl.ANY),
            pl.BlockSpec(memory_space=pl.ANY),
            mod(2), mod(3), mod(4),
            pl.BlockSpec(w_out.shape, lambda i, tr, fl: (0, 0), **const),
            pl.BlockSpec((1, d), lambda i, tr, fl: (0, 0)),
            pl.BlockSpec(rwt.shape, lambda i, tr, fl: (0, 0)),
            pl.BlockSpec((nr, 1), lambda i, tr, fl: (0, 0))],
        out_specs=[pl.BlockSpec((tm, d), row),
                   pl.BlockSpec((tm * (d // 2) // LANES, LANES), row),
                   pl.BlockSpec((nr, tm), lambda i, tr, fl: (0, tr[i])),
                   pl.BlockSpec((1, SUBLANES, d_lru), lambda i, tr, fl: (fl[2, i], 0, 0))],
        scratch_shapes=[pltpu.VMEM((tm + 2 * BF16_ROWS, d_lru), F32),
                        pltpu.VMEM((tm, d_lru), F32),
                        pltpu.VMEM((tm, d_lru), F32),
                        pltpu.VMEM((SUBLANES, d_lru), F32),
                        pltpu.VMEM((tm, d_lru + d_pool), BF16),
                        pltpu.VMEM((2, tm // SUBLANES, SUBLANES, d), F32),
                        pltpu.SemaphoreType.DMA((2,))])
    return pl.pallas_call(
        functools.partial(_bwd_kernel, geom=geom),
        grid_spec=grid_spec,
        out_shape=[jax.ShapeDtypeStruct((t, d), F32),
                   jax.ShapeDtypeStruct((t * (d // 2) // LANES, LANES), U32),
                   jax.ShapeDtypeStruct((nr, t), F32),
                   jax.ShapeDtypeStruct((ngrp, SUBLANES, d_lru), F32)],
        compiler_params=_cparams(("arbitrary",)),
        name="backward_scan_mix",
    )(tile_row, flags, xa, xa, xa, conv_w, conv_b.reshape(1, d_lru), wg, bg, lam.reshape(1, d_lru), h0,
      hf, ga, yb, x_lat, x_ctx, modtab, modtab, modtab, w_out, norm2_g.reshape(1, d), rwt, rb)


def _shift_rows(v, k):
    if k == 0:
        return v
    z = jnp.zeros((abs(k) * SUBLANES, v.shape[1]), v.dtype)
    if k > 0:
        return jnp.concatenate([z, v[:-k * SUBLANES]], axis=0)
    return jnp.concatenate([v[-k * SUBLANES:], z], axis=0)


def _run_sum(v, m, direction):
    if m & (m - 1) == 0:
        k = 1
        while k < m:
            v = v + _shift_rows(v, -direction * k)
            k *= 2
        return v
    out = v
    for j in range(1, m):
        out = out + _shift_rows(v, -direction * j)
    return out


def _box_sum(v, w):
    lo = w // 2
    hi = w - 1 - lo
    s = _run_sum(v, hi + 1, +1)
    if lo:
        s = s + _shift_rows(_run_sum(v, lo, -1), 1)
    return s


def _window_count(n_rows, lanes, n_pos, w):
    lo = w // 2
    hi = w - 1 - lo
    p = lax.shift_right_logical(lax.broadcasted_iota(I32, (n_rows, lanes), 0), 3)
    return (jnp.minimum(p + hi + 1, n_pos) - jnp.maximum(p - lo, 0)).astype(F32)


def _pool_kernel(z_ref, ps_ref, o_ref, v_ref, *, tiles_per_group, grid_rows, grid_cols):
    group = pl.program_id(1) // tiles_per_group
    lanes = z_ref.shape[1]
    ps = ps_ref[...]

    def pool_1d(w):
        z = z_ref[...].astype(F32)
        n = z.shape[0] // SUBLANES
        mean = _box_sum(z, w) / _window_count(z.shape[0], lanes, n, w)
        o_ref[...] = ((mean - z) * ps).astype(o_ref.dtype)

    def pool_2d(w):
        lo = w // 2
        hi = w - 1 - lo
        blk = grid_cols * SUBLANES
        cw = _window_count(blk, lanes, grid_cols, w)

        def zrow(r):
            return z_ref[pl.ds(pl.multiple_of(r * blk, blk), blk), :].astype(F32)

        v = jnp.zeros((blk, lanes), F32)
        for r in range(hi):
            v = v + zrow(r)
        v_ref[...] = v

        def body(r, carry):
            add = r + hi
            sub = r - lo - 1
            v = v_ref[...]
            v = v + jnp.where(add < grid_rows, zrow(jnp.minimum(add, grid_rows - 1)), 0.0)
            v = v - jnp.where(sub >= 0, zrow(jnp.maximum(sub, 0)), 0.0)
            v_ref[...] = v
            ch = (jnp.minimum(r + hi + 1, grid_rows) - jnp.maximum(r - lo, 0)).astype(F32)
            mean = _box_sum(v, w) / (ch * cw)
            o_ref[pl.ds(pl.multiple_of(r * blk, blk), blk), :] = ((mean - zrow(r)) * ps).astype(o_ref.dtype)
            return carry

        lax.fori_loop(0, grid_rows, body, 0)

    for g, w in enumerate(POOL_WINDOWS):
        @pl.when(group == g)
        def _(w=w):
            if grid_rows is None:
                pool_1d(w)
            else:
                pool_2d(w)


def _pool(z, prev_out, pool_scale, rows_per_group, n_groups, first_block, grid_rows):
    t, d_pool = z.shape
    gw = d_pool // len(POOL_WINDOWS)
    lanes = LANES
    blk = GRID_W * SUBLANES
    kern = functools.partial(_pool_kernel, tiles_per_group=gw // lanes, grid_rows=grid_rows, grid_cols=GRID_W)
    in_specs = [pl.BlockSpec((rows_per_group, lanes), lambda g, j: (first_block + g, j)),
                pl.BlockSpec((1, lanes), lambda g, j: (0, j))]
    args = [z, pool_scale.reshape(1, d_pool)]
    aliases = {}
    if prev_out is not None:
        in_specs.append(pl.BlockSpec(memory_space=pl.ANY))
        args.append(prev_out)
        aliases = {2: 0}
        kern_fn = lambda z_ref, ps_ref, prev_ref, o_ref, v_ref: kern(z_ref, ps_ref, o_ref, v_ref)
    else:
        kern_fn = kern
    return pl.pallas_call(
        kern_fn,
        grid=(n_groups, d_pool // lanes),
        in_specs=in_specs,
        out_specs=pl.BlockSpec((rows_per_group, lanes), lambda g, j: (first_block + g, j)),
        out_shape=jax.ShapeDtypeStruct((t, d_pool), BF16),
        scratch_shapes=[pltpu.VMEM((blk, lanes), F32)],
        input_output_aliases=aliases,
        compiler_params=_cparams(("arbitrary", "arbitrary")),
        name="pool_grid" if grid_rows is not None else "pool_seq",
    )(*args)


def _route_kernel(lg_ref, oi_ref, of_ref, cnt_ref, bc_ref, tri_ref, carry_ref, *, n_groups, per_group):
    step = pl.program_id(0)
    tt = lg_ref.shape[1]
    n_exp = n_groups * per_group

    @pl.when(step == 0)
    def _():
        r = lax.broadcasted_iota(I32, tri_ref.shape, 0)
        c = lax.broadcasted_iota(I32, tri_ref.shape, 1)
        tri_ref[...] = (r <= c).astype(BF16)
        carry_ref[...] = jnp.zeros_like(carry_ref)

    row = lambda k: lg_ref[k:k + 1, :]
    cmax = row(0)
    gi = jnp.zeros((1, tt), I32)
    for g in range(1, n_groups):
        better = row(g) > cmax
        gi = jnp.where(better, g, gi)
        cmax = jnp.where(better, row(g), cmax)
    denom = jnp.zeros((1, tt), F32)
    for g in range(n_groups):
        denom = denom + jnp.exp(row(g) - cmax)
    pg = 1.0 / denom
    fine = []
    for j in range(per_group):
        f = row(n_groups + j)
        for g in range(1, n_groups):
            f = jnp.where(gi == g, row(n_groups + g * per_group + j), f)
        fine.append(f)
    v1 = fine[0]
    i1 = jnp.zeros((1, tt), I32)
    for j in range(1, per_group):
        better = fine[j] > v1
        i1 = jnp.where(better, j, i1)
        v1 = jnp.where(better, fine[j], v1)
    v2 = jnp.full((1, tt), -jnp.inf, F32)
    i2 = jnp.zeros((1, tt), I32)
    for j in range(per_group):
        better = jnp.logical_and(i1 != j, fine[j] > v2)
        i2 = jnp.where(better, j, i2)
        v2 = jnp.where(better, fine[j], v2)
    ex = jnp.exp(v2 - v1)
    w1 = (1.0 / (1.0 + ex)) * pg
    w2 = (ex / (1.0 + ex)) * pg
    e1 = gi * per_group + i1
    e2 = gi * per_group + i2
    eid = lax.broadcasted_iota(I32, (n_exp, tt), 0)
    hit1 = eid == e1
    hit2 = eid == e2
    member = jnp.logical_or(hit1, hit2).astype(BF16)
    cb = tri_ref.shape[0]
    carry = carry_ref[...]
    cums = []
    lane = lax.broadcasted_iota(I32, bc_ref.shape, 1)
    block_ends = jnp.zeros(bc_ref.shape, F32)
    for s in range(tt // cb):
        c = jnp.dot(member[:, s * cb:(s + 1) * cb], tri_ref[...], preferred_element_type=F32) + carry
        carry = c[:, cb - 1:cb]
        cums.append(c)
        block_ends = jnp.where(lane == s, carry, block_ends)
    bc_ref[...] = block_ends.astype(I32)
    carry_ref[...] = carry
    cum = jnp.concatenate(cums, axis=1)
    rank1 = jnp.sum(jnp.where(hit1, cum, 0.0), axis=0, keepdims=True) - 1.0
    rank2 = jnp.sum(jnp.where(hit2, cum, 0.0), axis=0, keepdims=True) - 1.0
    zi = jnp.zeros((SUBLANES - 4, tt), I32)
    oi_ref[...] = jnp.concatenate([e1, e2, rank1.astype(I32), rank2.astype(I32), zi], axis=0)
    of_ref[...] = jnp.concatenate([w1, w2, jnp.zeros((of_ref.shape[0] - 2, tt), F32)], axis=0)
    cnt_ref[...] = jnp.broadcast_to(carry, cnt_ref.shape).astype(I32)


def _route(logits_t, n_groups, per_group, cb):
    nr, t = logits_t.shape
    tt = max(k for k in range(cb, min(ROUTE_TILE, t) + 1, cb) if t % k == 0)
    n_exp = n_groups * per_group
    return pl.pallas_call(
        functools.partial(_route_kernel, n_groups=n_groups, per_group=per_group),
        grid=(t // tt,),
        in_specs=[pl.BlockSpec((nr, tt), lambda i: (0, i))],
        out_specs=[pl.BlockSpec((SUBLANES, tt), lambda i: (0, i)),
                   pl.BlockSpec((LANES, tt), lambda i: (0, i)),
                   pl.BlockSpec((n_exp, LANES), lambda i: (0, 0)),
                   pl.BlockSpec((n_exp, LANES), lambda i: (0, i))],
        out_shape=[jax.ShapeDtypeStruct((SUBLANES, t), I32),
                   jax.ShapeDtypeStruct((LANES, t), F32),
                   jax.ShapeDtypeStruct((n_exp, LANES), I32),
                   jax.ShapeDtypeStruct((n_exp, LANES * (t // tt)), I32)],
        scratch_shapes=[pltpu.VMEM((cb, cb), BF16), pltpu.VMEM((n_exp, 1), F32)],
        compiler_params=_cparams(("arbitrary",)),
        name="route",
    )(logits_t)


def _tiles_for(cnt, tmx):
    return jnp.right_shift(cnt + (SEG_CHUNK + tmx - 1), tmx.bit_length() - 1)


def _plan_kernel(ri_ref, cnt_ref, bc_ref, bcp_ref, lp_ref, dst_ref, src_ref, len_ref, te_ref, nv_ref,
                 off_ref, *, tmx, tb):
    step = pl.program_id(0)
    n_exp = cnt_ref.shape[0]
    tt = ri_ref.shape[1]
    shift = tmx.bit_length() - 1
    n_tile = _tiles_for(cnt_ref[:, 0:1], tmx)

    def exclusive_prefix(col):
        run = jnp.zeros((1, 1), I32)
        parts = []
        for e in range(n_exp):
            parts.append(run)
            run = run + col[e:e + 1, :]
        return jnp.concatenate(parts, axis=0), run

    off, total = exclusive_prefix(jnp.left_shift(n_tile, shift))
    before = jnp.where(step == 0, 0, bcp_ref[:, tt // tb - 1:tt // tb])
    lane = lax.broadcasted_iota(I32, dst_ref.shape, 1)
    eid = lax.broadcasted_iota(I32, (n_exp, tb), 0)
    dst = jnp.zeros(dst_ref.shape, I32)
    src = jnp.zeros(dst_ref.shape, I32)
    length = jnp.zeros(dst_ref.shape, I32)
    rows = []
    for k in range(tt // tb):
        end = bc_ref[:, k:k + 1]
        n = end - before
        aligned = jnp.left_shift(jnp.right_shift(n + (SEG_CHUNK - 1), SEG_SHIFT), SEG_SHIFT)
        start, _ = exclusive_prefix(aligned)
        base = start - before
        sl = slice(k * tb, (k + 1) * tb)
        place = lambda e_row, r_row: jnp.sum(jnp.where(eid == e_row, base, 0), axis=0, keepdims=True) + r_row
        rows.append(jnp.concatenate([place(ri_ref[0:1, sl], ri_ref[2:3, sl]),
                                     place(ri_ref[1:2, sl], ri_ref[3:4, sl])], axis=0))
        dst = jnp.where(lane == k, off + before, dst)
        src = jnp.where(lane == k, start, src)
        length = jnp.where(lane == k, n, length)
        before = end
    lp_ref[...] = jnp.concatenate([jnp.concatenate(rows, axis=1),
                                   jnp.zeros((SUBLANES - TOP_K_FINE, tt), I32)], axis=0)
    dst_ref[...] = dst
    src_ref[...] = src
    len_ref[...] = length
    end_tile = jnp.right_shift(off, shift) + n_tile
    k = lax.broadcasted_iota(I32, (n_exp, te_ref.shape[1]), 1)
    te = jnp.sum((k >= end_tile).astype(I32), axis=0, keepdims=True)
    te_ref[...] = jnp.minimum(te, n_exp - 1)
    nv_ref[...] = jnp.broadcast_to(jnp.right_shift(total, shift), nv_ref.shape)
    off_ref[...] = jnp.broadcast_to(off, off_ref.shape)


def _plan(ri, counts, block_counts, tmx, n_tiles, tb):
    _, t = ri.shape
    n_exp = counts.shape[0]
    steps = block_counts.shape[1] // LANES
    tt = t // steps
    ntp = -(-n_tiles // LANES) * LANES
    per_step = lambda: pl.BlockSpec((n_exp, LANES), lambda i: (0, i))
    whole = lambda w: pl.BlockSpec((n_exp, w), lambda i: (0, 0))
    tab = jax.ShapeDtypeStruct((n_exp, LANES * steps), I32)
    lp, dst, src, length, te, nv, off = pl.pallas_call(
        functools.partial(_plan_kernel, tmx=tmx, tb=tb),
        grid=(steps,),
        in_specs=[pl.BlockSpec((SUBLANES, tt), lambda i: (0, i)),
                  whole(LANES),
                  per_step(),
                  pl.BlockSpec((n_exp, LANES), lambda i: (0, jnp.maximum(i - 1, 0)))],
        out_specs=[pl.BlockSpec((SUBLANES, tt), lambda i: (0, i)),
                   per_step(), per_step(), per_step(),
                   pl.BlockSpec((1, ntp), lambda i: (0, 0)),
                   pl.BlockSpec((1, LANES), lambda i: (0, 0)),
                   whole(LANES)],
        out_shape=[jax.ShapeDtypeStruct((SUBLANES, t), I32), tab, tab, tab,
                   jax.ShapeDtypeStruct((1, ntp), I32),
                   jax.ShapeDtypeStruct((1, LANES), I32),
                   jax.ShapeDtypeStruct((n_exp, LANES), I32)],
        compiler_params=_cparams(("arbitrary",)),
        name="plan",
    )(ri, counts, block_counts, block_counts)
    flat = lambda a: a.reshape(n_exp, steps, LANES)[:, :, :tt // tb].transpose(1, 2, 0).reshape(-1)
    tables = (flat(dst), flat(src), flat(length))
    return (lp[0].reshape(t // tb, tb), lp[1].reshape(t // tb, tb), tables,
            te[0, :n_tiles], nv[0, :1], off[:, 0])


def _load_positions(pos1_hbm, pos2_hbm, p1_ref, p2_ref, isem, step):
    c1 = pltpu.make_async_copy(pos1_hbm.at[step], p1_ref, isem.at[0])
    c2 = pltpu.make_async_copy(pos2_hbm.at[step], p2_ref, isem.at[1])
    c1.start()
    c2.start()
    c1.wait()
    c2.wait()


def _segment_copy(stage_ref, far_hbm, near_row, far_row, sem, to_far):
    near = stage_ref.at[pl.ds(near_row, SEG_CHUNK)]
    far = far_hbm.at[pl.ds(far_row, SEG_CHUNK)]
    return pltpu.make_async_copy(near, far, sem) if to_far else pltpu.make_async_copy(far, near, sem)


def _start_segments(tables, blk, n_exp, stage_ref, far_hbm, sem, to_far):
    dst_ref, src_ref, len_ref = tables

    def per_expert(e, total):
        k = blk * n_exp + e
        n_chunks = jnp.right_shift(len_ref[k] + (SEG_CHUNK - 1), SEG_SHIFT)
        near0 = src_ref[k]
        far0 = dst_ref[k]

        def one(c, carry):
            _segment_copy(stage_ref, far_hbm, near0 + c * SEG_CHUNK, far0 + c * SEG_CHUNK, sem, to_far).start()
            return carry
        lax.fori_loop(0, n_chunks, one, 0)
        return total + n_chunks
    return lax.fori_loop(0, n_exp, per_expert, jnp.int32(0))


def _wait_segments(stage_ref, far_hbm, sem, n, to_far):
    copy = _segment_copy(stage_ref, far_hbm, 0, 0, sem, to_far)

    def body(_, c):
        copy.wait()
        return c
    lax.fori_loop(0, n, body, 0)


def _dispatch_kernel(dst_ref, src_ref, len_ref, off_ref, cnt_ref, lp1_hbm, lp2_hbm, h2_ref, xs_hbm,
                     p1_ref, p2_ref, stage, nd_ref, zero_ref, isem, ssem, zsem, *, tmx):
    step = pl.program_id(0)
    n_steps = pl.num_programs(0)
    slot = lax.rem(step, 2)
    n_exp = off_ref.shape[0]
    n_blk = h2_ref.shape[0] // SUBLANES
    tables = (dst_ref, src_ref, len_ref)

    @pl.when(step == 0)
    def _():
        stage[...] = jnp.zeros_like(stage)

    _load_positions(lp1_hbm, lp2_hbm, p1_ref, p2_ref, isem, step)

    def place(blk, c):
        j0 = blk * SUBLANES
        for s in range(SUBLANES):
            row = h2_ref[j0 + s]
            stage[slot, p1_ref[j0 + s]] = row
            stage[slot, p2_ref[j0 + s]] = row
        return c
    lax.fori_loop(0, n_blk, place, 0, unroll=2)

    @pl.when(step >= 1)
    def _():
        _wait_segments(stage.at[1 - slot], xs_hbm, ssem.at[1 - slot], nd_ref[1 - slot], True)

    nd_ref[slot] = _start_segments(tables, step, n_exp, stage.at[slot], xs_hbm, ssem.at[slot], True)

    @pl.when(step == n_steps - 1)
    def _():
        _wait_segments(stage.at[slot], xs_hbm, ssem.at[slot], nd_ref[slot], True)
        zero_ref[...] = jnp.zeros_like(zero_ref)
        shift = tmx.bit_length() - 1

        def pad_expert(e, c):
            cnt = cnt_ref[e]
            n_pad = jnp.left_shift(_tiles_for(cnt, tmx), shift) - cnt
            first = off_ref[e] + cnt
            zero_copy = lambda r: pltpu.make_async_copy(zero_ref, xs_hbm.at[first + r], zsem)

            def fill(r, c2):
                zero_copy(r).start()
                return c2
            lax.fori_loop(0, n_pad, fill, 0)

            def done(r, c2):
                zero_copy(r).wait()
                return c2
            lax.fori_loop(0, n_pad, done, 0)
            return c
        lax.fori_loop(0, off_ref.shape[0], pad_expert, 0)


def _stage_rows(tb, n_exp):
    return TOP_K_FINE * tb + n_exp * SEG_CHUNK


def _dispatch(tables, offsets, counts, lp1, lp2, h2p, p_max, tmx):
    nb, tb = lp1.shape
    nc = h2p.shape[0] // (nb * tb)
    n_exp = offsets.shape[0]
    hbm = lambda: pl.BlockSpec(memory_space=pl.ANY)
    grid_spec = pltpu.PrefetchScalarGridSpec(
        num_scalar_prefetch=5,
        grid=(nb,),
        in_specs=[hbm(), hbm(),
                  pl.BlockSpec((tb, nc, LANES), lambda i, *_: (i, 0, 0))],
        out_specs=hbm(),
        scratch_shapes=[pltpu.SMEM((tb,), I32),
                        pltpu.SMEM((tb,), I32),
                        pltpu.VMEM((2, _stage_rows(tb, n_exp), nc, LANES), U32),
                        pltpu.SMEM((2,), I32),
                        pltpu.VMEM((nc, LANES), U32),
                        pltpu.SemaphoreType.DMA((2,)),
                        pltpu.SemaphoreType.DMA((2,)),
                        pltpu.SemaphoreType.DMA(())])
    xs = pl.pallas_call(
        functools.partial(_dispatch_kernel, tmx=tmx),
        grid_spec=grid_spec,
        out_shape=jax.ShapeDtypeStruct((p_max, nc, LANES), U32),
        compiler_params=_cparams(("arbitrary",)),
        name="dispatch",
    )(*tables, offsets, counts, lp1, lp2, h2p.reshape(nb * tb, nc, LANES))
    return xs.reshape(p_max * nc, LANES)


def _expert_kernel(te_ref, nv_ref, x_ref, w1_ref, w3_ref, w2_ref, y_ref, *, tmx):
    @pl.when(pl.program_id(0) < nv_ref[0])
    def _():
        lo, hi = _unpack_bf16_pair(_load_token_tiles(x_ref, tmx))
        lo = lo.astype(BF16)
        hi = hi.astype(BF16)
        half = lo.shape[1]
        mm = lambda w_ref: (jnp.dot(lo, w_ref[0, :half, :].astype(BF16), preferred_element_type=F32)
                            + jnp.dot(hi, w_ref[0, half:, :].astype(BF16), preferred_element_type=F32))
        h1 = mm(w1_ref)
        h3 = mm(w3_ref)
        hh = (h1 * jax.nn.sigmoid(h1) * h3).astype(BF16)
        y = jnp.dot(hh, w2_ref[0].astype(BF16), preferred_element_type=F32)
        _store_token_tiles(y_ref, _pack_bf16_pair(y[:, :half], y[:, half:]))


def _expert_mlp(xs, tile_expert, n_valid, w1, w3, w2, tmx):
    d, de = w1.shape[1], w1.shape[2]
    nc = d // 2 // LANES
    p = xs.shape[0] // nc
    tile = lambda i, te, nv: (jnp.minimum(i, nv[0] - 1), 0)
    grid_spec = pltpu.PrefetchScalarGridSpec(
        num_scalar_prefetch=2,
        grid=(p // tmx,),
        in_specs=[pl.BlockSpec((tmx * nc, LANES), tile),
                  pl.BlockSpec((1, d, de), lambda i, te, nv: (te[i], 0, 0)),
                  pl.BlockSpec((1, d, de), lambda i, te, nv: (te[i], 0, 0)),
                  pl.BlockSpec((1, de, d), lambda i, te, nv: (te[i], 0, 0))],
        out_specs=pl.BlockSpec((tmx * nc, LANES), tile))
    return pl.pallas_call(
        functools.partial(_expert_kernel, tmx=tmx),
        grid_spec=grid_spec,
        out_shape=jax.ShapeDtypeStruct((p * nc, LANES), U32),
        compiler_params=_cparams(("arbitrary",)),
        name="expert_mlp",
    )(tile_expert, n_valid, xs, w1, w3, w2)


def _combine_kernel(dst_ref, src_ref, len_ref, lp1_hbm, lp2_hbm, ys_hbm, x1_ref, rf_ref, g2_ref, fg_ref,
                    ylat_hbm, yctx_hbm, p1_ref, p2_ref, stage, gbuf, obuf, nd_ref, isem, csem, osem,
                    *, geom, n_exp):
    step = pl.program_id(0)
    n_steps = pl.num_programs(0)
    tm = geom.tm
    n_blk = tm // SUBLANES
    slot = lax.rem(step, 2)
    nc = ys_hbm.shape[1]
    tables = (dst_ref, src_ref, len_ref)

    def fetch(blk, gs):
        nd_ref[gs] = _start_segments(tables, blk, n_exp, stage.at[gs], ys_hbm, csem.at[gs], False)

    @pl.when(step == 0)
    def _():
        fetch(step, slot)

    @pl.when(step + 1 < n_steps)
    def _():
        fetch(step + 1, 1 - slot)

    _wait_segments(stage.at[slot], ys_hbm, csem.at[slot], nd_ref[slot], False)
    _load_positions(lp1_hbm, lp2_hbm, p1_ref, p2_ref, isem, step)

    def pick(blk, c):
        j0 = blk * SUBLANES
        for s in range(SUBLANES):
            rows = pl.ds(pl.multiple_of((j0 + s) * nc, nc), nc)
            gbuf[0, rows, :] = stage[slot, p1_ref[j0 + s]]
            gbuf[1, rows, :] = stage[slot, p2_ref[j0 + s]]
        return c
    lax.fori_loop(0, n_blk, pick, 0, unroll=2)

    wt = rf_ref[...].T
    a_lo, a_hi = _unpack_bf16_pair(_load_token_tiles(gbuf.at[0], tm))
    b_lo, b_hi = _unpack_bf16_pair(_load_token_tiles(gbuf.at[1], tm))
    w1 = wt[:, 0:1]
    w2 = wt[:, 1:2]
    moe = jnp.concatenate([w1 * a_lo + w2 * b_lo, w1 * a_hi + w2 * b_hi], axis=1)
    x = x1_ref[...] + _gate(moe, g2_ref[0])
    out = _rms(x, fg_ref[...])

    start = lambda cp: cp.start()
    wait = lambda cp: cp.wait()
    put = functools.partial(_tile_copies, ylat_hbm, yctx_hbm, geom=geom, to_rows=False)

    @pl.when(step >= 2)
    def _():
        put(obuf.at[slot], osem.at[slot], step - 2, fn=wait)

    obuf[slot] = out.reshape(obuf.shape[1:])
    put(obuf.at[slot], osem.at[slot], step, fn=start)

    @pl.when(step == n_steps - 1)
    def _():
        @pl.when(n_steps > 1)
        def _():
            put(obuf.at[1 - slot], osem.at[1 - slot], step - 1, fn=wait)
        put(obuf.at[slot], osem.at[slot], step, fn=wait)


def _combine(tables, lp1, lp2, ys, x1, rf, modtab, final_g, lat_shape, ctx_shape, geom):
    t, d = x1.shape
    tm = geom.tm
    nc = d // 2 // LANES
    n_exp = tables[0].shape[0] // (t // tm)
    ys = ys.reshape(ys.shape[0] // nc, nc, LANES)
    mset = lambda i, *_: jnp.where(i < geom.n_lat_tiles, 0, 1)
    hbm = lambda: pl.BlockSpec(memory_space=pl.ANY)
    grid_spec = pltpu.PrefetchScalarGridSpec(
        num_scalar_prefetch=3,
        grid=(t // tm,),
        in_specs=[hbm(), hbm(), hbm(),
                  pl.BlockSpec((tm, d), lambda i, *_: (i, 0)),
                  pl.BlockSpec((LANES, tm), lambda i, *_: (0, i)),
                  pl.BlockSpec((1, SUBLANES, d), lambda i, *_: (mset(i), 0, 5)),
                  pl.BlockSpec((1, d), lambda i, *_: (0, 0))],
        out_specs=[hbm(), hbm()],
        scratch_shapes=[pltpu.SMEM((tm,), I32),
                        pltpu.SMEM((tm,), I32),
                        pltpu.VMEM((2, _stage_rows(tm, n_exp), nc, LANES), U32),
                        pltpu.VMEM((TOP_K_FINE, tm * nc, LANES), U32),
                        pltpu.VMEM((2, tm // SUBLANES, SUBLANES, d), F32),
                        pltpu.SMEM((2,), I32),
                        pltpu.SemaphoreType.DMA((2,)),
                        pltpu.SemaphoreType.DMA((2,)),
                        pltpu.SemaphoreType.DMA((2,))])
    return pl.pallas_call(
        functools.partial(_combine_kernel, geom=geom, n_exp=n_exp),
        grid_spec=grid_spec,
        out_shape=[jax.ShapeDtypeStruct(lat_shape, F32), jax.ShapeDtypeStruct(ctx_shape, F32)],
        compiler_params=_cparams(("arbitrary",)),
        name="combine",
    )(*tables, lp1, lp2, ys, x1, rf, modtab, final_g.reshape(1, d))


def _tile_meta(groups, tm):
    rows, first, last, grp = [], [], [], []
    blk = 0
    for g, r in enumerate(groups):
        nc = r // tm
        for c in range(nc):
            rows.append(blk + c)
            first.append(int(c == 0))
            last.append(int(c == nc - 1))
            grp.append(g)
        blk += nc
    fwd = (np.array(rows, np.int32), np.array([first, last, grp], np.int32))
    order = []
    blk = 0
    for r in groups:
        nc = r // tm
        order.extend(range(blk + nc - 1, blk - 1, -1))
        blk += nc
    order = np.array(order)
    bwd = (fwd[0][order], fwd[1][:, order])
    return fwd, bwd


def kernel(x_prompt, x_sample, state_lru, c, c_ctx, w_mod, b_mod, norm1_g, w_in, conv_w, conv_b, lru_wa, lru_ba, lru_wx, lru_bx, lru_lambda, pool_w, pool_scale, w_out, norm2_g, router_coarse_w, router_coarse_b, router_fine_w, router_fine_b, exp_w1, exp_w3, exp_w2, final_norm_g):
    bp, sp, d = x_prompt.shape
    bs, ss, _ = x_sample.shape
    d_lru = lru_lambda.shape[-1]
    heads, bw = lru_wa.shape[2], lru_wa.shape[3]
    n_groups, per_group = router_fine_w.shape[2], router_fine_w.shape[3]
    n_exp = n_groups * per_group
    assert w_mod.shape[0] == 1 and bs == SUBLANES and bp % SUBLANES == 0 and ss % GRID_W == 0
    assert EXPERT_TILE & (EXPERT_TILE - 1) == 0
    n_lat_groups, n_ctx_groups = bs // SUBLANES, bp // SUBLANES
    lat_rows, ctx_rows = ss * SUBLANES, sp * SUBLANES
    assert (n_lat_groups * lat_rows) % ctx_rows == 0
    tm = min(TOKEN_TILE, ctx_rows, lat_rows)
    geom = _Geom(tm=tm, n_lat_tiles=n_lat_groups * lat_rows // tm, lat_chunks=lat_rows // tm,
                 ctx_chunks=ctx_rows // tm)
    n_rows = n_lat_groups * lat_rows + n_ctx_groups * ctx_rows
    groups = [lat_rows] * n_lat_groups + [ctx_rows] * n_ctx_groups
    (f_rows, f_flags), (b_rows, b_flags) = _tile_meta(groups, tm)

    cond = jnp.zeros((2 * SUBLANES, d), F32).at[:bs].set(c).at[bs].set(c_ctx)
    mod = _modulation(cond, w_mod[0], b_mod[0])
    modtab = jnp.stack([mod[:SUBLANES], jnp.broadcast_to(mod[SUBLANES], (SUBLANES, mod.shape[1]))])
    h0_lat = state_lru[:, 0].reshape(n_lat_groups, SUBLANES, 2, d_lru)
    h0 = jnp.concatenate([h0_lat, jnp.zeros((n_ctx_groups, SUBLANES, 2, d_lru), F32)], axis=0)
    h0 = h0.transpose(0, 2, 1, 3)

    xa, ga, z = _input_projection(x_sample, x_prompt, modtab, norm1_g[0], w_in[0],
                                  pool_w[0].astype(BF16), n_rows, geom)

    def gate_weights(direction):
        wg = jnp.concatenate([lru_wa[0, direction], lru_wx[0, direction]], axis=-1).astype(BF16)
        bg = jnp.concatenate([lru_ba[0, direction].reshape(heads, 1, bw),
                              lru_bx[0, direction].reshape(heads, 1, bw)], axis=-1)
        return wg, bg

    log_decay = jax.nn.log_sigmoid(lru_lambda[0])
    wg_f, bg_f = gate_weights(0)
    wg_b, bg_b = gate_weights(1)
    hf, hf_last = _forward_scan(xa, (jnp.asarray(f_rows), jnp.asarray(f_flags)), conv_w[0], conv_b[0],
                                wg_f, bg_f, log_decay[0], h0[:, 0], tm)

    yb = _pool(z, None, pool_scale[0], lat_rows, n_lat_groups, 0, lat_rows // (GRID_W * SUBLANES))
    yb = _pool(z, yb, pool_scale[0], ctx_rows, n_ctx_groups, n_lat_groups * lat_rows // ctx_rows, None)

    n_logits = n_groups + n_exp
    rw = jnp.concatenate([router_coarse_w[0], router_fine_w[0].reshape(d, n_exp)], axis=1)
    rwt = jnp.zeros((LANES, d), BF16).at[:n_logits].set(rw.T.astype(BF16))
    rb = jnp.zeros((LANES, 1), F32).at[:n_logits, 0].set(
        jnp.concatenate([router_coarse_b[0], router_fine_b[0].reshape(n_exp)]))
    x1, h2p, logits_t, hb_last = _backward_scan_mix(
        xa, ga, hf, yb, x_sample, x_prompt, modtab, (jnp.asarray(b_rows), jnp.asarray(b_flags)),
        conv_w[0], conv_b[0], wg_b, bg_b, log_decay[1], h0[:, 1], w_out[0].astype(BF16), norm2_g[0],
        rwt, rb, geom)

    ri, rf, counts, block_counts = _route(logits_t, n_groups, per_group, tm)
    n_tiles = -(-(TOP_K_FINE * n_rows + n_exp * (EXPERT_TILE + SEG_CHUNK)) // EXPERT_TILE)
    lp1, lp2, tables, tile_expert, n_valid, offsets = _plan(ri, counts, block_counts, EXPERT_TILE, n_tiles, tm)
    xs = _dispatch(tables, offsets, counts[:, 0], lp1, lp2, h2p, n_tiles * EXPERT_TILE, EXPERT_TILE)
    ys = _expert_mlp(xs, tile_expert, n_valid, exp_w1[0], exp_w3[0], exp_w2[0], EXPERT_TILE)
    y_sample, y_prompt = _combine(tables, lp1, lp2, ys, x1, rf, modtab, final_norm_g, x_sample.shape,
                                  x_prompt.shape, geom)

    st = jnp.stack([hf_last[n_lat_groups:], hb_last[n_lat_groups:]], axis=2)
    state_new = st.reshape(bp, 1, 2, d_lru).astype(x_prompt.dtype)
    return (y_prompt, y_sample, state_new)
```

```python
import functools
from typing import NamedTuple

import numpy as np
import jax
import jax.numpy as jnp
from jax import lax
from jax.experimental import pallas as pl
from jax.experimental.pallas import tpu as pltpu

GRID_W = 64
CONV_W = 4
RG_C = 8.0
POOL_WINDOWS = (2, 4, 8, 16)
TOP_K_FINE = 2
EPS = 1e-6
EXPM1_SERIES_BELOW = 0.125

SUBLANES = 8
LANES = 128
BF16_ROWS = 16
TOKEN_TILE = 512
EXPERT_TILE = 512
ROUTE_TILE = 2048
SEG_SHIFT = 4
SEG_CHUNK = 1 << SEG_SHIFT
MOD_COL_TILE = 1024
WEIGHT_STAGE_ROWS = 256
VMEM_LIMIT = 60 * 1024 * 1024

F32 = jnp.float32
BF16 = jnp.bfloat16
U32 = jnp.uint32
I32 = jnp.int32
HIGH_HALF = 0xFFFF0000


class _Geom(NamedTuple):
    tm: int
    n_lat_tiles: int
    lat_chunks: int
    ctx_chunks: int


def _cparams(sem):
    return pltpu.CompilerParams(dimension_semantics=sem, vmem_limit_bytes=VMEM_LIMIT)


def _per_sequence(y, m):
    rows, d = y.shape
    return y.reshape(rows // SUBLANES, SUBLANES, d), m[None]


def _modulate(y, scale, shift):
    y3, sc = _per_sequence(y, scale)
    _, sh = _per_sequence(y, shift)
    return (y3 * (1.0 + sc) + sh).reshape(y.shape)


def _gate(y, g):
    y3, g3 = _per_sequence(y, g)
    return (y3 * g3).reshape(y.shape)


def _rms(x, g):
    ms = jnp.mean(x * x, axis=-1, keepdims=True)
    return x * lax.rsqrt(ms + EPS) * g


def _pack_bf16_pair(lo, hi):
    lo_bits = lax.shift_right_logical(lax.bitcast_convert_type(lo.astype(BF16).astype(F32), U32), U32(16))
    hi_bits = lax.bitcast_convert_type(hi.astype(BF16).astype(F32), U32) & U32(HIGH_HALF)
    return lo_bits | hi_bits


def _unpack_bf16_pair(w):
    lo = lax.bitcast_convert_type(lax.shift_left(w, U32(16)), F32)
    hi = lax.bitcast_convert_type(w & U32(HIGH_HALF), F32)
    return lo, hi


def _store_token_tiles(ref, w):
    n, words = w.shape
    nc = words // LANES
    for c in range(nc):
        ref[pl.ds(c, n, stride=nc), :] = w[:, c * LANES:(c + 1) * LANES]


def _load_token_tiles(ref, n):
    nc = ref.shape[0] // n
    return jnp.concatenate([ref[pl.ds(c, n, stride=nc), :] for c in range(nc)], axis=1)


def _tile_copies(lat_hbm, ctx_hbm, buf, sem, tile, geom, to_rows, fn):
    tt = geom.tm // SUBLANES

    def run(hbm, k, chunks):
        g = lax.div(k, jnp.int32(chunks))
        c = lax.rem(k, jnp.int32(chunks))
        for b in range(SUBLANES):
            h = hbm.at[g * SUBLANES + b, pl.ds(c * tt, tt), :]
            v = buf.at[:, b, :]
            fn(pltpu.make_async_copy(h, v, sem) if to_rows else pltpu.make_async_copy(v, h, sem))

    @pl.when(tile < geom.n_lat_tiles)
    def _():
        run(lat_hbm, tile, geom.lat_chunks)

    @pl.when(tile >= geom.n_lat_tiles)
    def _():
        run(ctx_hbm, tile - geom.n_lat_tiles, geom.ctx_chunks)


def _fetch_rows(lat_hbm, ctx_hbm, xbuf, sem, step, n_steps, tile_of, geom):
    slot = lax.rem(step, 2)
    start = lambda cp: cp.start()
    wait = lambda cp: cp.wait()

    @pl.when(step == 0)
    def _():
        _tile_copies(lat_hbm, ctx_hbm, xbuf.at[0], sem.at[0], tile_of(jnp.int32(0)), geom, True, start)

    @pl.when(step + 1 < n_steps)
    def _():
        nxt = 1 - slot
        _tile_copies(lat_hbm, ctx_hbm, xbuf.at[nxt], sem.at[nxt], tile_of(step + 1), geom, True, start)

    _tile_copies(lat_hbm, ctx_hbm, xbuf.at[slot], sem.at[slot], tile_of(step), geom, True, wait)
    tt, _, d = xbuf.shape[1:]
    return xbuf[slot].reshape(tt * SUBLANES, d)


def _mod_kernel(c_ref, w_ref, b_ref, o_ref):
    c = c_ref[...]
    s = c * jax.nn.sigmoid(c)
    o_ref[...] = jnp.dot(s.astype(BF16), w_ref[...].astype(BF16),
                         preferred_element_type=F32) + b_ref[...]


def _modulation(cond, w_mod, b_mod):
    rows, d = cond.shape
    n = w_mod.shape[1]
    tn = min(MOD_COL_TILE, n)
    return pl.pallas_call(
        _mod_kernel,
        grid=(n // tn,),
        in_specs=[pl.BlockSpec((rows, d), lambda j: (0, 0)),
                  pl.BlockSpec((d, tn), lambda j: (0, j)),
                  pl.BlockSpec((1, tn), lambda j: (0, j))],
        out_specs=pl.BlockSpec((rows, tn), lambda j: (0, j)),
        out_shape=jax.ShapeDtypeStruct((rows, n), F32),
        compiler_params=_cparams(("arbitrary",)),
        name="modulation",
    )(cond, w_mod, b_mod.reshape(1, n))


def _round_weights(w_hbm, wbf_ref, stage, sem):
    rows = stage.shape[1]
    n = w_hbm.shape[0] // rows
    copy = lambda r: pltpu.make_async_copy(w_hbm.at[pl.ds(r * rows, rows), :], stage.at[r % 2], sem.at[r % 2])
    copy(0).start()
    for r in range(n):
        if r + 1 < n:
            copy(r + 1).start()
        copy(r).wait()
        wbf_ref[r * rows:(r + 1) * rows, :] = stage[r % 2].astype(wbf_ref.dtype)


def _proj_kernel(lat_hbm, ctx_hbm, sh_ref, sc_ref, g_ref, win_hbm, pw_ref, xa_ref, ga_ref, z_ref,
                 xbuf, sem, wbf_ref, wstage, wsem, *, d_lru, gw, geom):
    step = pl.program_id(0)

    @pl.when(step == 0)
    def _():
        _round_weights(win_hbm, wbf_ref, wstage, wsem)

    x = _fetch_rows(lat_hbm, ctx_hbm, xbuf, sem, step, pl.num_programs(0), lambda s: s, geom)
    h = _modulate(_rms(x, g_ref[...]), sc_ref[0], sh_ref[0])
    proj = jnp.dot(h.astype(BF16), wbf_ref[...], preferred_element_type=F32)
    xa_ref[...] = proj[:, :d_lru].astype(BF16)
    ga_ref[...] = proj[:, d_lru:2 * d_lru].astype(BF16)
    for g in range(pw_ref.shape[0]):
        lo = 2 * d_lru + g * gw
        z_ref[:, g * gw:(g + 1) * gw] = jnp.dot(
            proj[:, lo:lo + gw].astype(BF16), pw_ref[g], preferred_element_type=F32).astype(BF16)


def _input_projection(x_lat, x_ctx, modtab, norm_g, w_in, pool_w, n_rows, geom):
    d = x_lat.shape[-1]
    tm = geom.tm
    d_pool = pool_w.shape[0] * pool_w.shape[1]
    d_lru = (w_in.shape[1] - d_pool) // 2
    mset = lambda i: jnp.where(i < geom.n_lat_tiles, 0, 1)
    const = dict(pipeline_mode=pl.Buffered(1))
    return pl.pallas_call(
        functools.partial(_proj_kernel, d_lru=d_lru, gw=pool_w.shape[1], geom=geom),
        grid=(n_rows // tm,),
        in_specs=[pl.BlockSpec(memory_space=pl.ANY),
                  pl.BlockSpec(memory_space=pl.ANY),
                  pl.BlockSpec((1, SUBLANES, d), lambda i: (mset(i), 0, 0)),
                  pl.BlockSpec((1, SUBLANES, d), lambda i: (mset(i), 0, 1)),
                  pl.BlockSpec((1, d), lambda i: (0, 0)),
                  pl.BlockSpec(memory_space=pl.ANY),
                  pl.BlockSpec(pool_w.shape, lambda i: (0, 0, 0), **const)],
        out_specs=[pl.BlockSpec((tm, d_lru), lambda i: (i, 0)),
                   pl.BlockSpec((tm, d_lru), lambda i: (i, 0)),
                   pl.BlockSpec((tm, d_pool), lambda i: (i, 0))],
        out_shape=[jax.ShapeDtypeStruct((n_rows, d_lru), BF16),
                   jax.ShapeDtypeStruct((n_rows, d_lru), BF16),
                   jax.ShapeDtypeStruct((n_rows, d_pool), BF16)],
        scratch_shapes=[pltpu.VMEM((2, tm // SUBLANES, SUBLANES, d), F32),
                        pltpu.SemaphoreType.DMA((2,)),
                        pltpu.VMEM(w_in.shape, BF16),
                        pltpu.VMEM((2, min(WEIGHT_STAGE_ROWS, d), w_in.shape[1]), F32),
                        pltpu.SemaphoreType.DMA((2,))],
        compiler_params=_cparams(("arbitrary",)),
        name="input_projection",
    )(x_lat, x_ctx, modtab, modtab, norm_g.reshape(1, d), w_in, pool_w)


def _fill_ext(ext_ref, prev_ref, main_ref, next_ref, first, last):
    tm = main_ref.shape[0]
    prev = prev_ref[...].astype(F32)
    nxt = next_ref[...].astype(F32)
    ext_ref[0:BF16_ROWS, :] = jnp.where(first, 0.0, prev)
    ext_ref[BF16_ROWS:BF16_ROWS + tm, :] = main_ref[...].astype(F32)
    ext_ref[BF16_ROWS + tm:, :] = jnp.where(last, 0.0, nxt)


def _one_minus_exp(y, exp_y):
    p = 1.0 / 120.0
    for c in (1.0 / 24.0, 1.0 / 6.0, 0.5, 1.0):
        p = p * y + c
    return jnp.where(y > -EXPM1_SERIES_BELOW, -y * p, 1.0 - exp_y)


def _sqrt_nonneg(q):
    return jnp.where(q > 0.0, q * lax.rsqrt(q), 0.0)


def _decay_and_input(ext_ref, cw_ref, cb_ref, wg_ref, bg_ref, lam_ref, a_ref, u_ref, tm):
    heads, bw = wg_ref.shape[0], wg_ref.shape[1]
    for hd in range(heads):
        sl = slice(hd * bw, (hd + 1) * bw)
        xc = cb_ref[:, sl]
        for k in range(CONV_W):
            xc = xc + cw_ref[k:k + 1, sl] * ext_ref[SUBLANES * k:SUBLANES * k + tm, sl]
        g = jnp.dot(xc.astype(BF16), wg_ref[hd], preferred_element_type=F32) + bg_ref[hd]
        r = jax.nn.sigmoid(g[:, :bw])
        ig = jax.nn.sigmoid(g[:, bw:])
        log_a = (RG_C * r) * lam_ref[:, sl]
        a = jnp.exp(log_a)
        a_ref[:, sl] = a
        u_ref[:, sl] = _sqrt_nonneg(_one_minus_exp(2.0 * log_a, a * a)) * (ig * xc)


def _scan(a_ref, u_ref, h, tm, reverse):
    nblk = tm // SUBLANES

    def body(s, h):
        j = (nblk - 1 - s) if reverse else s
        rows = pl.ds(pl.multiple_of(j * SUBLANES, SUBLANES), SUBLANES)
        h = a_ref[rows, :] * h + u_ref[rows, :]
        u_ref[rows, :] = h
        return h

    return lax.fori_loop(0, nblk, body, h, unroll=8)


def _halo_specs(tm, d_lru, n_rows):
    per = tm // BF16_ROWS
    last_blk = n_rows // BF16_ROWS - 1
    return [pl.BlockSpec((BF16_ROWS, d_lru), lambda i, tr, fl: (jnp.maximum(tr[i] * per - 1, 0), 0)),
            pl.BlockSpec((tm, d_lru), lambda i, tr, fl: (tr[i], 0)),
            pl.BlockSpec((BF16_ROWS, d_lru), lambda i, tr, fl: (jnp.minimum((tr[i] + 1) * per, last_blk), 0))]


def _fwd_kernel(tr_ref, fl_ref, prev_ref, main_ref, next_ref, cw_ref, cb_ref, wg_ref, bg_ref, lam_ref,
                h0_ref, hf_ref, hlast_ref, ext_ref, a_ref, u_ref, h_ref):
    i = pl.program_id(0)
    tm = main_ref.shape[0]
    first = fl_ref[0, i] == 1
    last = fl_ref[1, i] == 1
    _fill_ext(ext_ref, prev_ref, main_ref, next_ref, first, last)
    _decay_and_input(ext_ref, cw_ref, cb_ref, wg_ref, bg_ref, lam_ref, a_ref, u_ref, tm)

    @pl.when(first)
    def _():
        h_ref[...] = h0_ref[0]

    h = _scan(a_ref, u_ref, h_ref[...], tm, reverse=False)
    h_ref[...] = h
    hlast_ref[0] = h
    hf_ref[...] = u_ref[...].astype(BF16)


def _forward_scan(xa, meta, conv_w, conv_b, wg, bg, lam, h0, tm):
    t, d_lru = xa.shape
    tile_row, flags = meta
    ngrp = h0.shape[0]
    const = dict(pipeline_mode=pl.Buffered(1))
    grid_spec = pltpu.PrefetchScalarGridSpec(
        num_scalar_prefetch=2,
        grid=(t // tm,),
        in_specs=_halo_specs(tm, d_lru, t) + [
            pl.BlockSpec(conv_w.shape, lambda i, tr, fl: (0, 0)),
            pl.BlockSpec((1, d_lru), lambda i, tr, fl: (0, 0)),
            pl.BlockSpec(wg.shape, lambda i, tr, fl: (0, 0, 0), **const),
            pl.BlockSpec(bg.shape, lambda i, tr, fl: (0, 0, 0)),
            pl.BlockSpec((1, d_lru), lambda i, tr, fl: (0, 0)),
            pl.BlockSpec((1, SUBLANES, d_lru), lambda i, tr, fl: (fl[2, i], 0, 0))],
        out_specs=[pl.BlockSpec((tm, d_lru), lambda i, tr, fl: (tr[i], 0)),
                   pl.BlockSpec((1, SUBLANES, d_lru), lambda i, tr, fl: (fl[2, i], 0, 0))],
        scratch_shapes=[pltpu.VMEM((tm + 2 * BF16_ROWS, d_lru), F32),
                        pltpu.VMEM((tm, d_lru), F32),
                        pltpu.VMEM((tm, d_lru), F32),
                        pltpu.VMEM((SUBLANES, d_lru), F32)])
    return pl.pallas_call(
        _fwd_kernel,
        grid_spec=grid_spec,
        out_shape=[jax.ShapeDtypeStruct((t, d_lru), BF16),
                   jax.ShapeDtypeStruct((ngrp, SUBLANES, d_lru), F32)],
        compiler_params=_cparams(("arbitrary",)),
        name="forward_scan",
    )(tile_row, flags, xa, xa, xa, conv_w, conv_b.reshape(1, d_lru), wg, bg, lam.reshape(1, d_lru), h0)


def _bwd_kernel(tr_ref, fl_ref, prev_ref, main_ref, next_ref, cw_ref, cb_ref, wg_ref, bg_ref, lam_ref,
                h0_ref, hf_ref, ga_ref, yb_ref, lat_hbm, ctx_hbm, g1_ref, sh2_ref, sc2_ref, wout_ref,
                n2_ref, rwt_ref, rb_ref,
                x1_ref, h2_ref, lg_ref, hlast_ref,
                ext_ref, a_ref, u_ref, h_ref, cat_ref, xbuf, xsem, *, geom):
    i = pl.program_id(0)
    n_steps = pl.num_programs(0)
    tm, d_lru = main_ref.shape
    x = _fetch_rows(lat_hbm, ctx_hbm, xbuf, xsem, i, n_steps,
                    lambda s: tr_ref[jnp.minimum(s, n_steps - 1)], geom)
    first = fl_ref[0, i] == 1
    last = fl_ref[1, i] == 1
    _fill_ext(ext_ref, prev_ref, main_ref, next_ref, first, last)
    _decay_and_input(ext_ref, cw_ref, cb_ref, wg_ref, bg_ref, lam_ref, a_ref, u_ref, tm)

    @pl.when(last)
    def _():
        h_ref[...] = h0_ref[0]

    h = _scan(a_ref, u_ref, h_ref[...], tm, reverse=True)
    h_ref[...] = h
    hlast_ref[0] = h

    ga = ga_ref[...].astype(F32)
    ya = (hf_ref[...].astype(F32) + u_ref[...]) * jax.nn.gelu(ga)
    cat_ref[:, :d_lru] = ya.astype(BF16)
    cat_ref[:, d_lru:] = yb_ref[...]
    mix = jnp.dot(cat_ref[...], wout_ref[...], preferred_element_type=F32)
    x1 = x + _gate(mix, g1_ref[0])
    x1_ref[...] = x1
    h2 = _modulate(_rms(x1, n2_ref[...]), sc2_ref[0], sh2_ref[0])
    half = h2.shape[1] // 2
    _store_token_tiles(h2_ref, _pack_bf16_pair(h2[:, :half], h2[:, half:]))
    lg_ref[...] = lax.dot_general(rwt_ref[...], h2.astype(BF16), (((1,), (1,)), ((), ())),
                                  preferred_element_type=F32) + rb_ref[...]


def _backward_scan_mix(xa, ga, hf, yb, x_lat, x_ctx, modtab, meta, conv_w, conv_b, wg, bg, lam, h0, w_out,
                       norm2_g, rwt, rb, geom):
    t, d_lru = xa.shape
    d = x_lat.shape[-1]
    d_pool = yb.shape[1]
    tm = geom.tm
    tile_row, flags = meta
    ngrp = h0.shape[0]
    nr = rwt.shape[0]
    const = dict(pipeline_mode=pl.Buffered(1))
    mset = lambda i, tr, fl: jnp.where(tr[i] < geom.n_lat_tiles, 0, 1)
    row = lambda i, tr, fl: (tr[i], 0)
    mod = lambda col: pl.BlockSpec((1, SUBLANES, d), lambda i, tr, fl: (mset(i, tr, fl), 0, col))
    grid_spec = pltpu.PrefetchScalarGridSpec(
        num_scalar_prefetch=2,
        grid=(t // tm,),
        in_specs=_halo_specs(tm, d_lru, t) + [
            pl.BlockSpec(conv_w.shape, lambda i, tr, fl: (0, 0)),
            pl.BlockSpec((1, d_lru), lambda i, tr, fl: (0, 0)),
            pl.BlockSpec(wg.shape, lambda i, tr, fl: (0, 0, 0), **const),
            pl.BlockSpec(bg.shape, lambda i, tr, fl: (0, 0, 0)),
            pl.BlockSpec((1, d_lru), lambda i, tr, fl: (0, 0)),
            pl.BlockSpec((1, SUBLANES, d_lru), lambda i, tr, fl: (fl[2, i], 0, 0)),
            pl.BlockSpec((tm, d_lru), row),
            pl.BlockSpec((tm, d_lru), row),
            pl.BlockSpec((tm, d_pool), row),
            pl.BlockSpec(memory_space=pl.ANY),
            pl.BlockSpec(memory_space=pl.ANY),
            mod(2), mod(3), mod(4),
            pl.BlockSpec(w_out.shape, lambda i, tr, fl: (0, 0), **const),
            pl.BlockSpec((1, d), lambda i, tr, fl: (0, 0)),
            pl.BlockSpec(rwt.shape, lambda i, tr, fl: (0, 0)),
            pl.BlockSpec((nr, 1), lambda i, tr, fl: (0, 0))],
        out_specs=[pl.BlockSpec((tm, d), row),
                   pl.BlockSpec((tm * (d // 2) // LANES, LANES), row),
                   pl.BlockSpec((nr, tm), lambda i, tr, fl: (0, tr[i])),
                   pl.BlockSpec((1, SUBLANES, d_lru), lambda i, tr, fl: (fl[2, i], 0, 0))],
        scratch_shapes=[pltpu.VMEM((tm + 2 * BF16_ROWS, d_lru), F32),
                        pltpu.VMEM((tm, d_lru), F32),
                        pltpu.VMEM((tm, d_lru), F32),
                        pltpu.VMEM((SUBLANES, d_lru), F32),
                        pltpu.VMEM((tm, d_lru + d_pool), BF16),
                        pltpu.VMEM((2, tm // SUBLANES, SUBLANES, d), F32),
                        pltpu.SemaphoreType.DMA((2,))])
    return pl.pallas_call(
        functools.partial(_bwd_kernel, geom=geom),
        grid_spec=grid_spec,
        out_shape=[jax.ShapeDtypeStruct((t, d), F32),
                   jax.ShapeDtypeStruct((t * (d // 2) // LANES, LANES), U32),
                   jax.ShapeDtypeStruct((nr, t), F32),
                   jax.ShapeDtypeStruct((ngrp, SUBLANES, d_lru), F32)],
        compiler_params=_cparams(("arbitrary",)),
        name="backward_scan_mix",
    )(tile_row, flags, xa, xa, xa, conv_w, conv_b.reshape(1, d_lru), wg, bg, lam.reshape(1, d_lru), h0,
      hf, ga, yb, x_lat, x_ctx, modtab, modtab, modtab, w_out, norm2_g.reshape(1, d), rwt, rb)


def _shift_rows(v, k):
    if k == 0:
        return v
    z = jnp.zeros((abs(k) * SUBLANES, v.shape[1]), v.dtype)
    if k > 0:
        return jnp.concatenate([z, v[:-k * SUBLANES]], axis=0)
    return jnp.concatenate([v[-k * SUBLANES:], z], axis=0)


def _run_sum(v, m, direction):
    if m & (m - 1) == 0:
        k = 1
        while k < m:
            v = v + _shift_rows(v, -direction * k)
            k *= 2
        return v
    out = v
    for j in range(1, m):
        out = out + _shift_rows(v, -direction * j)
    return out


def _box_sum(v, w):
    lo = w // 2
    hi = w - 1 - lo
    s = _run_sum(v, hi + 1, +1)
    if lo:
        s = s + _shift_rows(_run_sum(v, lo, -1), 1)
    return s


def _window_count(n_rows, lanes, n_pos, w):
    lo = w // 2
    hi = w - 1 - lo
    p = lax.shift_right_logical(lax.broadcasted_iota(I32, (n_rows, lanes), 0), 3)
    return (jnp.minimum(p + hi + 1, n_pos) - jnp.maximum(p - lo, 0)).astype(F32)


def _pool_kernel(z_ref, ps_ref, o_ref, v_ref, *, tiles_per_group, grid_rows, grid_cols):
    group = pl.program_id(1) // tiles_per_group
    lanes = z_ref.shape[1]
    ps = ps_ref[...]

    def pool_1d(w):
        z = z_ref[...].astype(F32)
        n = z.shape[0] // SUBLANES
        mean = _box_sum(z, w) / _window_count(z.shape[0], lanes, n, w)
        o_ref[...] = ((mean - z) * ps).astype(o_ref.dtype)

    def pool_2d(w):
        lo = w // 2
        hi = w - 1 - lo
        blk = grid_cols * SUBLANES
        cw = _window_count(blk, lanes, grid_cols, w)

        def zrow(r):
            return z_ref[pl.ds(pl.multiple_of(r * blk, blk), blk), :].astype(F32)

        v = jnp.zeros((blk, lanes), F32)
        for r in range(hi):
            v = v + zrow(r)
        v_ref[...] = v

        def body(r, carry):
            add = r + hi
            sub = r - lo - 1
            v = v_ref[...]
            v = v + jnp.where(add < grid_rows, zrow(jnp.minimum(add, grid_rows - 1)), 0.0)
            v = v - jnp.where(sub >= 0, zrow(jnp.maximum(sub, 0)), 0.0)
            v_ref[...] = v
            ch = (jnp.minimum(r + hi + 1, grid_rows) - jnp.maximum(r - lo, 0)).astype(F32)
            mean = _box_sum(v, w) / (ch * cw)
            o_ref[pl.ds(pl.multiple_of(r * blk, blk), blk), :] = ((mean - zrow(r)) * ps).astype(o_ref.dtype)
            return carry

        lax.fori_loop(0, grid_rows, body, 0)

    for g, w in enumerate(POOL_WINDOWS):
        @pl.when(group == g)
        def _(w=w):
            if grid_rows is None:
                pool_1d(w)
            else:
                pool_2d(w)


def _pool(z, prev_out, pool_scale, rows_per_group, n_groups, first_block, grid_rows):
    t, d_pool = z.shape
    gw = d_pool // len(POOL_WINDOWS)
    lanes = LANES
    blk = GRID_W * SUBLANES
    kern = functools.partial(_pool_kernel, tiles_per_group=gw // lanes, grid_rows=grid_rows, grid_cols=GRID_W)
    in_specs = [pl.BlockSpec((rows_per_group, lanes), lambda g, j: (first_block + g, j)),
                pl.BlockSpec((1, lanes), lambda g, j: (0, j))]
    args = [z, pool_scale.reshape(1, d_pool)]
    aliases = {}
    if prev_out is not None:
        in_specs.append(pl.BlockSpec(memory_space=pl.ANY))
        args.append(prev_out)
        aliases = {2: 0}
        kern_fn = lambda z_ref, ps_ref, prev_ref, o_ref, v_ref: kern(z_ref, ps_ref, o_ref, v_ref)
    else:
        kern_fn = kern
    return pl.pallas_call(
        kern_fn,
        grid=(n_groups, d_pool // lanes),
        in_specs=in_specs,
        out_specs=pl.BlockSpec((rows_per_group, lanes), lambda g, j: (first_block + g, j)),
        out_shape=jax.ShapeDtypeStruct((t, d_pool), BF16),
        scratch_shapes=[pltpu.VMEM((blk, lanes), F32)],
        input_output_aliases=aliases,
        compiler_params=_cparams(("arbitrary", "arbitrary")),
        name="pool_grid" if grid_rows is not None else "pool_seq",
    )(*args)


def _route_kernel(lg_ref, oi_ref, of_ref, cnt_ref, bc_ref, tri_ref, carry_ref, *, n_groups, per_group):
    step = pl.program_id(0)
    tt = lg_ref.shape[1]
    n_exp = n_groups * per_group

    @pl.when(step == 0)
    def _():
        r = lax.broadcasted_iota(I32, tri_ref.shape, 0)
        c = lax.broadcasted_iota(I32, tri_ref.shape, 1)
        tri_ref[...] = (r <= c).astype(BF16)
        carry_ref[...] = jnp.zeros_like(carry_ref)

    row = lambda k: lg_ref[k:k + 1, :]
    cmax = row(0)
    gi = jnp.zeros((1, tt), I32)
    for g in range(1, n_groups):
        better = row(g) > cmax
        gi = jnp.where(better, g, gi)
        cmax = jnp.where(better, row(g), cmax)
    denom = jnp.zeros((1, tt), F32)
    for g in range(n_groups):
        denom = denom + jnp.exp(row(g) - cmax)
    pg = 1.0 / denom
    fine = []
    for j in range(per_group):
        f = row(n_groups + j)
        for g in range(1, n_groups):
            f = jnp.where(gi == g, row(n_groups + g * per_group + j), f)
        fine.append(f)
    v1 = fine[0]
    i1 = jnp.zeros((1, tt), I32)
    for j in range(1, per_group):
        better = fine[j] > v1
        i1 = jnp.where(better, j, i1)
        v1 = jnp.where(better, fine[j], v1)
    v2 = jnp.full((1, tt), -jnp.inf, F32)
    i2 = jnp.zeros((1, tt), I32)
    for j in range(per_group):
        better = jnp.logical_and(i1 != j, fine[j] > v2)
        i2 = jnp.where(better, j, i2)
        v2 = jnp.where(better, fine[j], v2)
    ex = jnp.exp(v2 - v1)
    w1 = (1.0 / (1.0 + ex)) * pg
    w2 = (ex / (1.0 + ex)) * pg
    e1 = gi * per_group + i1
    e2 = gi * per_group + i2
    eid = lax.broadcasted_iota(I32, (n_exp, tt), 0)
    hit1 = eid == e1
    hit2 = eid == e2
    member = jnp.logical_or(hit1, hit2).astype(BF16)
    cb = tri_ref.shape[0]
    carry = carry_ref[...]
    cums = []
    lane = lax.broadcasted_iota(I32, bc_ref.shape, 1)
    block_ends = jnp.zeros(bc_ref.shape, F32)
    for s in range(tt // cb):
        c = jnp.dot(member[:, s * cb:(s + 1) * cb], tri_ref[...], preferred_element_type=F32) + carry
        carry = c[:, cb - 1:cb]
        cums.append(c)
        block_ends = jnp.where(lane == s, carry, block_ends)
    bc_ref[...] = block_ends.astype(I32)
    carry_ref[...] = carry
    cum = jnp.concatenate(cums, axis=1)
    rank1 = jnp.sum(jnp.where(hit1, cum, 0.0), axis=0, keepdims=True) - 1.0
    rank2 = jnp.sum(jnp.where(hit2, cum, 0.0), axis=0, keepdims=True) - 1.0
    zi = jnp.zeros((SUBLANES - 4, tt), I32)
    oi_ref[...] = jnp.concatenate([e1, e2, rank1.astype(I32), rank2.astype(I32), zi], axis=0)
    of_ref[...] = jnp.concatenate([w1, w2, jnp.zeros((of_ref.shape[0] - 2, tt), F32)], axis=0)
    cnt_ref[...] = jnp.broadcast_to(carry, cnt_ref.shape).astype(I32)


def _route(logits_t, n_groups, per_group, cb):
    nr, t = logits_t.shape
    tt = max(k for k in range(cb, min(ROUTE_TILE, t) + 1, cb) if t % k == 0)
    n_exp = n_groups * per_group
    return pl.pallas_call(
        functools.partial(_route_kernel, n_groups=n_groups, per_group=per_group),
        grid=(t // tt,),
        in_specs=[pl.BlockSpec((nr, tt), lambda i: (0, i))],
        out_specs=[pl.BlockSpec((SUBLANES, tt), lambda i: (0, i)),
                   pl.BlockSpec((LANES, tt), lambda i: (0, i)),
                   pl.BlockSpec((n_exp, LANES), lambda i: (0, 0)),
                   pl.BlockSpec((n_exp, LANES), lambda i: (0, i))],
        out_shape=[jax.ShapeDtypeStruct((SUBLANES, t), I32),
                   jax.ShapeDtypeStruct((LANES, t), F32),
                   jax.ShapeDtypeStruct((n_exp, LANES), I32),
                   jax.ShapeDtypeStruct((n_exp, LANES * (t // tt)), I32)],
        scratch_shapes=[pltpu.VMEM((cb, cb), BF16), pltpu.VMEM((n_exp, 1), F32)],
        compiler_params=_cparams(("arbitrary",)),
        name="route",
    )(logits_t)


def _tiles_for(cnt, tmx):
    return jnp.right_shift(cnt + (SEG_CHUNK + tmx - 1), tmx.bit_length() - 1)


def _plan_kernel(ri_ref, cnt_ref, bc_ref, bcp_ref, lp_ref, dst_ref, src_ref, len_ref, te_ref, nv_ref,
                 off_ref, *, tmx, tb):
    step = pl.program_id(0)
    n_exp = cnt_ref.shape[0]
    tt = ri_ref.shape[1]
    shift = tmx.bit_length() - 1
    n_tile = _tiles_for(cnt_ref[:, 0:1], tmx)

    def exclusive_prefix(col):
        run = jnp.zeros((1, 1), I32)
        parts = []
        for e in range(n_exp):
            parts.append(run)
            run = run + col[e:e + 1, :]
        return jnp.concatenate(parts, axis=0), run

    off, total = exclusive_prefix(jnp.left_shift(n_tile, shift))
    before = jnp.where(step == 0, 0, bcp_ref[:, tt // tb - 1:tt // tb])
    lane = lax.broadcasted_iota(I32, dst_ref.shape, 1)
    eid = lax.broadcasted_iota(I32, (n_exp, tb), 0)
    dst = jnp.zeros(dst_ref.shape, I32)
    src = jnp.zeros(dst_ref.shape, I32)
    length = jnp.zeros(dst_ref.shape, I32)
    rows = []
    for k in range(tt // tb):
        end = bc_ref[:, k:k + 1]
        n = end - before
        aligned = jnp.left_shift(jnp.right_shift(n + (SEG_CHUNK - 1), SEG_SHIFT), SEG_SHIFT)
        start, _ = exclusive_prefix(aligned)
        base = start - before
        sl = slice(k * tb, (k + 1) * tb)
        place = lambda e_row, r_row: jnp.sum(jnp.where(eid == e_row, base, 0), axis=0, keepdims=True) + r_row
        rows.append(jnp.concatenate([place(ri_ref[0:1, sl], ri_ref[2:3, sl]),
                                     place(ri_ref[1:2, sl], ri_ref[3:4, sl])], axis=0))
        dst = jnp.where(lane == k, off + before, dst)
        src = jnp.where(lane == k, start, src)
        length = jnp.where(lane == k, n, length)
        before = end
    lp_ref[...] = jnp.concatenate([jnp.concatenate(rows, axis=1),
                                   jnp.zeros((SUBLANES - TOP_K_FINE, tt), I32)], axis=0)
    dst_ref[...] = dst
    src_ref[...] = src
    len_ref[...] = length
    end_tile = jnp.right_shift(off, shift) + n_tile
    k = lax.broadcasted_iota(I32, (n_exp, te_ref.shape[1]), 1)
    te = jnp.sum((k >= end_tile).astype(I32), axis=0, keepdims=True)
    te_ref[...] = jnp.minimum(te, n_exp - 1)
    nv_ref[...] = jnp.broadcast_to(jnp.right_shift(total, shift), nv_ref.shape)
    off_ref[...] = jnp.broadcast_to(off, off_ref.shape)


def _plan(ri, counts, block_counts, tmx, n_tiles, tb):
    _, t = ri.shape
    n_exp = counts.shape[0]
    steps = block_counts.shape[1] // LANES
    tt = t // steps
    ntp = -(-n_tiles // LANES) * LANES
    per_step = lambda: pl.BlockSpec((n_exp, LANES), lambda i: (0, i))
    whole = lambda w: pl.BlockSpec((n_exp, w), lambda i: (0, 0))
    tab = jax.ShapeDtypeStruct((n_exp, LANES * steps), I32)
    lp, dst, src, length, te, nv, off = pl.pallas_call(
        functools.partial(_plan_kernel, tmx=tmx, tb=tb),
        grid=(steps,),
        in_specs=[pl.BlockSpec((SUBLANES, tt), lambda i: (0, i)),
                  whole(LANES),
                  per_step(),
                  pl.BlockSpec((n_exp, LANES), lambda i: (0, jnp.maximum(i - 1, 0)))],
        out_specs=[pl.BlockSpec((SUBLANES, tt), lambda i: (0, i)),
                   per_step(), per_step(), per_step(),
                   pl.BlockSpec((1, ntp), lambda i: (0, 0)),
                   pl.BlockSpec((1, LANES), lambda i: (0, 0)),
                   whole(LANES)],
        out_shape=[jax.ShapeDtypeStruct((SUBLANES, t), I32), tab, tab, tab,
                   jax.ShapeDtypeStruct((1, ntp), I32),
                   jax.ShapeDtypeStruct((1, LANES), I32),
                   jax.ShapeDtypeStruct((n_exp, LANES), I32)],
        compiler_params=_cparams(("arbitrary",)),
        name="plan",
    )(ri, counts, block_counts, block_counts)
    flat = lambda a: a.reshape(n_exp, steps, LANES)[:, :, :tt // tb].transpose(1, 2, 0).reshape(-1)
    tables = (flat(dst), flat(src), flat(length))
    return (lp[0].reshape(t // tb, tb), lp[1].reshape(t // tb, tb), tables,
            te[0, :n_tiles], nv[0, :1], off[:, 0])


def _load_positions(pos1_hbm, pos2_hbm, p1_ref, p2_ref, isem, step):
    c1 = pltpu.make_async_copy(pos1_hbm.at[step], p1_ref, isem.at[0])
    c2 = pltpu.make_async_copy(pos2_hbm.at[step], p2_ref, isem.at[1])
    c1.start()
    c2.start()
    c1.wait()
    c2.wait()


def _segment_copy(stage_ref, far_hbm, near_row, far_row, sem, to_far):
    near = stage_ref.at[pl.ds(near_row, SEG_CHUNK)]
    far = far_hbm.at[pl.ds(far_row, SEG_CHUNK)]
    return pltpu.make_async_copy(near, far, sem) if to_far else pltpu.make_async_copy(far, near, sem)


def _start_segments(tables, blk, n_exp, stage_ref, far_hbm, sem, to_far):
    dst_ref, src_ref, len_ref = tables

    total = jnp.int32(0)
    for e in range(n_exp):
        k = blk * n_exp + e
        n_chunks = jnp.right_shift(len_ref[k] + (SEG_CHUNK - 1), SEG_SHIFT)
        near0 = src_ref[k]
        far0 = dst_ref[k]

        def one(c, carry, near0=near0, far0=far0, queue=e % 2):
            _segment_copy(stage_ref, far_hbm, near0 + c * SEG_CHUNK, far0 + c * SEG_CHUNK, sem,
                          to_far).start(priority=queue)
            return carry
        lax.fori_loop(0, n_chunks, one, 0)
        total = total + n_chunks
    return total


def _wait_segments(stage_ref, far_hbm, sem, n, to_far):
    copy = _segment_copy(stage_ref, far_hbm, 0, 0, sem, to_far)

    def body(_, c):
        copy.wait()
        return c
    lax.fori_loop(0, n, body, 0)


def _dispatch_kernel(dst_ref, src_ref, len_ref, off_ref, cnt_ref, lp1_hbm, lp2_hbm, h2_ref, xs_hbm,
                     p1_ref, p2_ref, stage, nd_ref, zero_ref, isem, ssem, zsem, *, tmx):
    step = pl.program_id(0)
    n_steps = pl.num_programs(0)
    slot = lax.rem(step, 2)
    n_exp = off_ref.shape[0]
    n_blk = h2_ref.shape[0] // SUBLANES
    tables = (dst_ref, src_ref, len_ref)

    @pl.when(step == 0)
    def _():
        stage[...] = jnp.zeros_like(stage)

    _load_positions(lp1_hbm, lp2_hbm, p1_ref, p2_ref, isem, step)

    def place(blk, c):
        j0 = blk * SUBLANES
        for s in range(SUBLANES):
            row = h2_ref[j0 + s]
            stage[slot, p1_ref[j0 + s]] = row
            stage[slot, p2_ref[j0 + s]] = row
        return c
    lax.fori_loop(0, n_blk, place, 0, unroll=2)

    @pl.when(step >= 1)
    def _():
        _wait_segments(stage.at[1 - slot], xs_hbm, ssem.at[1 - slot], nd_ref[1 - slot], True)

    nd_ref[slot] = _start_segments(tables, step, n_exp, stage.at[slot], xs_hbm, ssem.at[slot], True)

    @pl.when(step == n_steps - 1)
    def _():
        _wait_segments(stage.at[slot], xs_hbm, ssem.at[slot], nd_ref[slot], True)
        zero_ref[...] = jnp.zeros_like(zero_ref)
        shift = tmx.bit_length() - 1

        def pad_expert(e, c):
            cnt = cnt_ref[e]
            n_pad = jnp.left_shift(_tiles_for(cnt, tmx), shift) - cnt
            first = off_ref[e] + cnt
            zero_copy = lambda r: pltpu.make_async_copy(zero_ref, xs_hbm.at[first + r], zsem)

            def fill(r, c2):
                zero_copy(r).start()
                return c2
            lax.fori_loop(0, n_pad, fill, 0)

            def done(r, c2):
                zero_copy(r).wait()
                return c2
            lax.fori_loop(0, n_pad, done, 0)
            return c
        lax.fori_loop(0, off_ref.shape[0], pad_expert, 0)


def _stage_rows(tb, n_exp):
    return TOP_K_FINE * tb + n_exp * SEG_CHUNK


def _dispatch(tables, offsets, counts, lp1, lp2, h2p, p_max, tmx):
    nb, tb = lp1.shape
    nc = h2p.shape[0] // (nb * tb)
    n_exp = offsets.shape[0]
    hbm = lambda: pl.BlockSpec(memory_space=pl.ANY)
    grid_spec = pltpu.PrefetchScalarGridSpec(
        num_scalar_prefetch=5,
        grid=(nb,),
        in_specs=[hbm(), hbm(),
                  pl.BlockSpec((tb, nc, LANES), lambda i, *_: (i, 0, 0))],
        out_specs=hbm(),
        scratch_shapes=[pltpu.SMEM((tb,), I32),
                        pltpu.SMEM((tb,), I32),
                        pltpu.VMEM((2, _stage_rows(tb, n_exp), nc, LANES), U32),
                        pltpu.SMEM((2,), I32),
                        pltpu.VMEM((nc, LANES), U32),
                        pltpu.SemaphoreType.DMA((2,)),
                        pltpu.SemaphoreType.DMA((2,)),
                        pltpu.SemaphoreType.DMA(())])
    xs = pl.pallas_call(
        functools.partial(_dispatch_kernel, tmx=tmx),
        grid_spec=grid_spec,
        out_shape=jax.ShapeDtypeStruct((p_max, nc, LANES), U32),
        compiler_params=_cparams(("arbitrary",)),
        name="dispatch",
    )(*tables, offsets, counts, lp1, lp2, h2p.reshape(nb * tb, nc, LANES))
    return xs.reshape(p_max * nc, LANES)


def _expert_kernel(te_ref, nv_ref, x_ref, w1_ref, w3_ref, w2_ref, y_ref, *, tmx):
    @pl.when(pl.program_id(0) < nv_ref[0])
    def _():
        lo, hi = _unpack_bf16_pair(_load_token_tiles(x_ref, tmx))
        lo = lo.astype(BF16)
        hi = hi.astype(BF16)
        half = lo.shape[1]
        mm = lambda w_ref: (jnp.dot(lo, w_ref[0, :half, :].astype(BF16), preferred_element_type=F32)
                            + jnp.dot(hi, w_ref[0, half:, :].astype(BF16), preferred_element_type=F32))
        h1 = mm(w1_ref)
        h3 = mm(w3_ref)
        hh = (h1 * jax.nn.sigmoid(h1) * h3).astype(BF16)
        y = jnp.dot(hh, w2_ref[0].astype(BF16), preferred_element_type=F32)
        _store_token_tiles(y_ref, _pack_bf16_pair(y[:, :half], y[:, half:]))


def _expert_mlp(xs, tile_expert, n_valid, w1, w3, w2, tmx):
    d, de = w1.shape[1], w1.shape[2]
    nc = d // 2 // LANES
    p = xs.shape[0] // nc
    tile = lambda i, te, nv: (jnp.minimum(i, nv[0] - 1), 0)
    grid_spec = pltpu.PrefetchScalarGridSpec(
        num_scalar_prefetch=2,
        grid=(p // tmx,),
        in_specs=[pl.BlockSpec((tmx * nc, LANES), tile),
                  pl.BlockSpec((1, d, de), lambda i, te, nv: (te[i], 0, 0)),
                  pl.BlockSpec((1, d, de), lambda i, te, nv: (te[i], 0, 0)),
                  pl.BlockSpec((1, de, d), lambda i, te, nv: (te[i], 0, 0))],
        out_specs=pl.BlockSpec((tmx * nc, LANES), tile))
    return pl.pallas_call(
        functools.partial(_expert_kernel, tmx=tmx),
        grid_spec=grid_spec,
        out_shape=jax.ShapeDtypeStruct((p * nc, LANES), U32),
        compiler_params=_cparams(("arbitrary",)),
        name="expert_mlp",
    )(tile_expert, n_valid, xs, w1, w3, w2)


def _combine_kernel(dst_ref, src_ref, len_ref, lp1_hbm, lp2_hbm, ys_hbm, x1_ref, rf_ref, g2_ref, fg_ref,
                    ylat_hbm, yctx_hbm, p1_ref, p2_ref, stage, gbuf, obuf, nd_ref, isem, csem, osem,
                    *, geom, n_exp):
    step = pl.program_id(0)
    n_steps = pl.num_programs(0)
    tm = geom.tm
    n_blk = tm // SUBLANES
    slot = lax.rem(step, 2)
    nc = ys_hbm.shape[1]
    tables = (dst_ref, src_ref, len_ref)

    def fetch(blk, gs):
        nd_ref[gs] = _start_segments(tables, blk, n_exp, stage.at[gs], ys_hbm, csem.at[gs], False)

    @pl.when(step == 0)
    def _():
        fetch(step, slot)

    @pl.when(step + 1 < n_steps)
    def _():
        fetch(step + 1, 1 - slot)

    _wait_segments(stage.at[slot], ys_hbm, csem.at[slot], nd_ref[slot], False)
    _load_positions(lp1_hbm, lp2_hbm, p1_ref, p2_ref, isem, step)

    def pick(blk, c):
        j0 = blk * SUBLANES
        for s in range(SUBLANES):
            rows = pl.ds(pl.multiple_of((j0 + s) * nc, nc), nc)
            gbuf[0, rows, :] = stage[slot, p1_ref[j0 + s]]
            gbuf[1, rows, :] = stage[slot, p2_ref[j0 + s]]
        return c
    lax.fori_loop(0, n_blk, pick, 0, unroll=2)

    wt = rf_ref[...].T
    a_lo, a_hi = _unpack_bf16_pair(_load_token_tiles(gbuf.at[0], tm))
    b_lo, b_hi = _unpack_bf16_pair(_load_token_tiles(gbuf.at[1], tm))
    w1 = wt[:, 0:1]
    w2 = wt[:, 1:2]
    moe = jnp.concatenate([w1 * a_lo + w2 * b_lo, w1 * a_hi + w2 * b_hi], axis=1)
    x = x1_ref[...] + _gate(moe, g2_ref[0])
    out = _rms(x, fg_ref[...])

    start = lambda cp: cp.start()
    wait = lambda cp: cp.wait()
    put = functools.partial(_tile_copies, ylat_hbm, yctx_hbm, geom=geom, to_rows=False)

    @pl.when(step >= 2)
    def _():
        put(obuf.at[slot], osem.at[slot], step - 2, fn=wait)

    obuf[slot] = out.reshape(obuf.shape[1:])
    put(obuf.at[slot], osem.at[slot], step, fn=start)

    @pl.when(step == n_steps - 1)
    def _():
        @pl.when(n_steps > 1)
        def _():
            put(obuf.at[1 - slot], osem.at[1 - slot], step - 1, fn=wait)
        put(obuf.at[slot], osem.at[slot], step, fn=wait)


def _combine(tables, lp1, lp2, ys, x1, rf, modtab, final_g, lat_shape, ctx_shape, geom):
    t, d = x1.shape
    tm = geom.tm
    nc = d // 2 // LANES
    n_exp = tables[0].shape[0] // (t // tm)
    ys = ys.reshape(ys.shape[0] // nc, nc, LANES)
    mset = lambda i, *_: jnp.where(i < geom.n_lat_tiles, 0, 1)
    hbm = lambda: pl.BlockSpec(memory_space=pl.ANY)
    grid_spec = pltpu.PrefetchScalarGridSpec(
        num_scalar_prefetch=3,
        grid=(t // tm,),
        in_specs=[hbm(), hbm(), hbm(),
                  pl.BlockSpec((tm, d), lambda i, *_: (i, 0)),
                  pl.BlockSpec((LANES, tm), lambda i, *_: (0, i)),
                  pl.BlockSpec((1, SUBLANES, d), lambda i, *_: (mset(i), 0, 5)),
                  pl.BlockSpec((1, d), lambda i, *_: (0, 0))],
        out_specs=[hbm(), hbm()],
        scratch_shapes=[pltpu.SMEM((tm,), I32),
                        pltpu.SMEM((tm,), I32),
                        pltpu.VMEM((2, _stage_rows(tm, n_exp), nc, LANES), U32),
                        pltpu.VMEM((TOP_K_FINE, tm * nc, LANES), U32),
                        pltpu.VMEM((2, tm // SUBLANES, SUBLANES, d), F32),
                        pltpu.SMEM((2,), I32),
                        pltpu.SemaphoreType.DMA((2,)),
                        pltpu.SemaphoreType.DMA((2,)),
                        pltpu.SemaphoreType.DMA((2,))])
    return pl.pallas_call(
        functools.partial(_combine_kernel, geom=geom, n_exp=n_exp),
        grid_spec=grid_spec,
        out_shape=[jax.ShapeDtypeStruct(lat_shape, F32), jax.ShapeDtypeStruct(ctx_shape, F32)],
        compiler_params=_cparams(("arbitrary",)),
        name="combine",
    )(*tables, lp1, lp2, ys, x1, rf, modtab, final_g.reshape(1, d))


def _tile_meta(groups, tm):
    rows, first, last, grp = [], [], [], []
    blk = 0
    for g, r in enumerate(groups):
        nc = r // tm
        for c in range(nc):
            rows.append(blk + c)
            first.append(int(c == 0))
            last.append(int(c == nc - 1))
            grp.append(g)
        blk += nc
    fwd = (np.array(rows, np.int32), np.array([first, last, grp], np.int32))
    order = []
    blk = 0
    for r in groups:
        nc = r // tm
        order.extend(range(blk + nc - 1, blk - 1, -1))
        blk += nc
    order = np.array(order)
    bwd = (fwd[0][order], fwd[1][:, order])
    return fwd, bwd


def kernel(x_prompt, x_sample, state_lru, c, c_ctx, w_mod, b_mod, norm1_g, w_in, conv_w, conv_b, lru_wa, lru_ba, lru_wx, lru_bx, lru_lambda, pool_w, pool_scale, w_out, norm2_g, router_coarse_w, router_coarse_b, router_fine_w, router_fine_b, exp_w1, exp_w3, exp_w2, final_norm_g):
    bp, sp, d = x_prompt.shape
    bs, ss, _ = x_sample.shape
    d_lru = lru_lambda.shape[-1]
    heads, bw = lru_wa.shape[2], lru_wa.shape[3]
    n_groups, per_group = router_fine_w.shape[2], router_fine_w.shape[3]
    n_exp = n_groups * per_group
    assert w_mod.shape[0] == 1 and bs == SUBLANES and bp % SUBLANES == 0 and ss % GRID_W == 0
    assert EXPERT_TILE & (EXPERT_TILE - 1) == 0
    n_lat_groups, n_ctx_groups = bs // SUBLANES, bp // SUBLANES
    lat_rows, ctx_rows = ss * SUBLANES, sp * SUBLANES
    assert (n_lat_groups * lat_rows) % ctx_rows == 0
    tm = min(TOKEN_TILE, ctx_rows, lat_rows)
    geom = _Geom(tm=tm, n_lat_tiles=n_lat_groups * lat_rows // tm, lat_chunks=lat_rows // tm,
                 ctx_chunks=ctx_rows // tm)
    n_rows = n_lat_groups * lat_rows + n_ctx_groups * ctx_rows
    groups = [lat_rows] * n_lat_groups + [ctx_rows] * n_ctx_groups
    (f_rows, f_flags), (b_rows, b_flags) = _tile_meta(groups, tm)

    cond = jnp.zeros((2 * SUBLANES, d), F32).at[:bs].set(c).at[bs].set(c_ctx)
    mod = _modulation(cond, w_mod[0], b_mod[0])
    modtab = jnp.stack([mod[:SUBLANES], jnp.broadcast_to(mod[SUBLANES], (SUBLANES, mod.shape[1]))])
    h0_lat = state_lru[:, 0].reshape(n_lat_groups, SUBLANES, 2, d_lru)
    h0 = jnp.concatenate([h0_lat, jnp.zeros((n_ctx_groups, SUBLANES, 2, d_lru), F32)], axis=0)
    h0 = h0.transpose(0, 2, 1, 3)

    xa, ga, z = _input_projection(x_sample, x_prompt, modtab, norm1_g[0], w_in[0],
                                  pool_w[0].astype(BF16), n_rows, geom)

    def gate_weights(direction):
        wg = jnp.concatenate([lru_wa[0, direction], lru_wx[0, direction]], axis=-1).astype(BF16)
        bg = jnp.concatenate([lru_ba[0, direction].reshape(heads, 1, bw),
                              lru_bx[0, direction].reshape(heads, 1, bw)], axis=-1)
        return wg, bg

    log_decay = jax.nn.log_sigmoid(lru_lambda[0])
    wg_f, bg_f = gate_weights(0)
    wg_b, bg_b = gate_weights(1)
    hf, hf_last = _forward_scan(xa, (jnp.asarray(f_rows), jnp.asarray(f_flags)), conv_w[0], conv_b[0],
                                wg_f, bg_f, log_decay[0], h0[:, 0], tm)

    yb = _pool(z, None, pool_scale[0], lat_rows, n_lat_groups, 0, lat_rows // (GRID_W * SUBLANES))
    yb = _pool(z, yb, pool_scale[0], ctx_rows, n_ctx_groups, n_lat_groups * lat_rows // ctx_rows, None)

    n_logits = n_groups + n_exp
    rw = jnp.concatenate([router_coarse_w[0], router_fine_w[0].reshape(d, n_exp)], axis=1)
    rwt = jnp.zeros((LANES, d), BF16).at[:n_logits].set(rw.T.astype(BF16))
    rb = jnp.zeros((LANES, 1), F32).at[:n_logits, 0].set(
        jnp.concatenate([router_coarse_b[0], router_fine_b[0].reshape(n_exp)]))
    x1, h2p, logits_t, hb_last = _backward_scan_mix(
        xa, ga, hf, yb, x_sample, x_prompt, modtab, (jnp.asarray(b_rows), jnp.asarray(b_flags)),
        conv_w[0], conv_b[0], wg_b, bg_b, log_decay[1], h0[:, 1], w_out[0].astype(BF16), norm2_g[0],
        rwt, rb, geom)

    ri, rf, counts, block_counts = _route(logits_t, n_groups, per_group, tm)
    n_tiles = -(-(TOP_K_FINE * n_rows + n_exp * (EXPERT_TILE + SEG_CHUNK)) // EXPERT_TILE)
    lp1, lp2, tables, tile_expert, n_valid, offsets = _plan(ri, counts, block_counts, EXPERT_TILE, n_tiles, tm)
    xs = _dispatch(tables, offsets, counts[:, 0], lp1, lp2, h2p, n_tiles * EXPERT_TILE, EXPERT_TILE)
    ys = _expert_mlp(xs, tile_expert, n_valid, exp_w1[0], exp_w3[0], exp_w2[0], EXPERT_TILE)
    y_sample, y_prompt = _combine(tables, lp1, lp2, ys, x1, rf, modtab, final_norm_g, x_sample.shape,
                                  x_prompt.shape, geom)

    st = jnp.stack([hf_last[n_lat_groups:], hb_last[n_lat_groups:]], axis=2)
    state_new = st.reshape(bp, 1, 2, d_lru).astype(x_prompt.dtype)
    return (y_prompt, y_sample, state_new)
```

```python
import functools
from typing import NamedTuple

import numpy as np
import jax
import jax.numpy as jnp
from jax import lax
from jax.experimental import pallas as pl
from jax.experimental.pallas import tpu as pltpu

GRID_W = 64
CONV_W = 4
RG_C = 8.0
POOL_WINDOWS = (2, 4, 8, 16)
TOP_K_FINE = 2
EPS = 1e-6
EXPM1_SERIES_BELOW = 0.125

SUBLANES = 8
LANES = 128
BF16_ROWS = 16
TOKEN_TILE = 512
EXPERT_TILE = 512
ROUTE_TILE = 2048
SEG_SHIFT = 4
SEG_CHUNK = 1 << SEG_SHIFT
MOD_COL_TILE = 1024
WEIGHT_STAGE_ROWS = 256
VMEM_LIMIT = 60 * 1024 * 1024

F32 = jnp.float32
BF16 = jnp.bfloat16
U32 = jnp.uint32
I32 = jnp.int32


class _Geom(NamedTuple):
    tm: int
    n_lat_tiles: int
    lat_chunks: int
    ctx_chunks: int


def _cparams(sem):
    return pltpu.CompilerParams(dimension_semantics=sem, vmem_limit_bytes=VMEM_LIMIT)


def _per_sequence(y, m):
    rows, d = y.shape
    return y.reshape(rows // SUBLANES, SUBLANES, d), m[None]


def _modulate(y, scale, shift):
    y3, sc = _per_sequence(y, scale)
    _, sh = _per_sequence(y, shift)
    return (y3 * (1.0 + sc) + sh).reshape(y.shape)


def _gate(y, g):
    y3, g3 = _per_sequence(y, g)
    return (y3 * g3).reshape(y.shape)


def _rms(x, g):
    ms = jnp.mean(x * x, axis=-1, keepdims=True)
    return x * lax.rsqrt(ms + EPS) * g


def _store_packed_tokens(ref, lo, hi, scratch):
    n, words = lo.shape
    nc = words // LANES
    for c in range(nc):
        sl = slice(c * LANES, (c + 1) * LANES)
        scratch[c, pl.ds(0, n, stride=2), :] = lo[:, sl]
        scratch[c, pl.ds(1, n, stride=2), :] = hi[:, sl]
        ref[pl.ds(c, n, stride=nc), :] = pltpu.bitcast(scratch[c].astype(BF16), U32)


def _packed_zeros(rows):
    return pltpu.bitcast(jnp.zeros((2 * rows, LANES), BF16), U32)


def _load_packed_tokens(ref, n, scratch):
    nc = ref.shape[0] // n
    los, his = [], []
    for c in range(nc):
        scratch[c] = pltpu.bitcast(ref[pl.ds(c, n, stride=nc), :], BF16).astype(F32)
        los.append(scratch[c, pl.ds(0, n, stride=2), :])
        his.append(scratch[c, pl.ds(1, n, stride=2), :])
    return jnp.concatenate(los, axis=1), jnp.concatenate(his, axis=1)


def _tile_copies(lat_hbm, ctx_hbm, buf, sem, tile, geom, to_rows, fn):
    tt = geom.tm // SUBLANES

    def run(hbm, k, chunks):
        g = lax.div(k, jnp.int32(chunks))
        c = lax.rem(k, jnp.int32(chunks))
        for b in range(SUBLANES):
            h = hbm.at[g * SUBLANES + b, pl.ds(c * tt, tt), :]
            v = buf.at[:, b, :]
            fn(pltpu.make_async_copy(h, v, sem) if to_rows else pltpu.make_async_copy(v, h, sem))

    @pl.when(tile < geom.n_lat_tiles)
    def _():
        run(lat_hbm, tile, geom.lat_chunks)

    @pl.when(tile >= geom.n_lat_tiles)
    def _():
        run(ctx_hbm, tile - geom.n_lat_tiles, geom.ctx_chunks)


def _fetch_rows(lat_hbm, ctx_hbm, xbuf, sem, step, n_steps, tile_of, geom):
    slot = lax.rem(step, 2)
    start = lambda cp: cp.start()
    wait = lambda cp: cp.wait()

    @pl.when(step == 0)
    def _():
        _tile_copies(lat_hbm, ctx_hbm, xbuf.at[0], sem.at[0], tile_of(jnp.int32(0)), geom, True, start)

    @pl.when(step + 1 < n_steps)
    def _():
        nxt = 1 - slot
        _tile_copies(lat_hbm, ctx_hbm, xbuf.at[nxt], sem.at[nxt], tile_of(step + 1), geom, True, start)

    _tile_copies(lat_hbm, ctx_hbm, xbuf.at[slot], sem.at[slot], tile_of(step), geom, True, wait)
    tt, _, d = xbuf.shape[1:]
    return xbuf[slot].reshape(tt * SUBLANES, d)


def _mod_kernel(c_ref, w_ref, b_ref, o_ref):
    c = c_ref[...]
    s = c * jax.nn.sigmoid(c)
    o_ref[...] = jnp.dot(s.astype(BF16), w_ref[...].astype(BF16),
                         preferred_element_type=F32) + b_ref[...]


def _modulation(cond, w_mod, b_mod):
    rows, d = cond.shape
    n = w_mod.shape[1]
    tn = min(MOD_COL_TILE, n)
    return pl.pallas_call(
        _mod_kernel,
        grid=(n // tn,),
        in_specs=[pl.BlockSpec((rows, d), lambda j: (0, 0)),
                  pl.BlockSpec((d, tn), lambda j: (0, j)),
                  pl.BlockSpec((1, tn), lambda j: (0, j))],
        out_specs=pl.BlockSpec((rows, tn), lambda j: (0, j)),
        out_shape=jax.ShapeDtypeStruct((rows, n), F32),
        compiler_params=_cparams(("arbitrary",)),
        name="modulation",
    )(cond, w_mod, b_mod.reshape(1, n))


def _round_weights(w_hbm, wbf_ref, stage, sem):
    rows = stage.shape[1]
    n = w_hbm.shape[0] // rows
    copy = lambda r: pltpu.make_async_copy(w_hbm.at[pl.ds(r * rows, rows), :], stage.at[r % 2], sem.at[r % 2])
    copy(0).start()
    for r in range(n):
        if r + 1 < n:
            copy(r + 1).start()
        copy(r).wait()
        wbf_ref[r * rows:(r + 1) * rows, :] = stage[r % 2].astype(wbf_ref.dtype)


def _proj_kernel(lat_hbm, ctx_hbm, sh_ref, sc_ref, g_ref, win_hbm, pw_ref, xa_ref, ga_ref, z_ref,
                 xbuf, sem, wbf_ref, wstage, wsem, *, d_lru, gw, geom):
    step = pl.program_id(0)

    @pl.when(step == 0)
    def _():
        _round_weights(win_hbm, wbf_ref, wstage, wsem)

    x = _fetch_rows(lat_hbm, ctx_hbm, xbuf, sem, step, pl.num_programs(0), lambda s: s, geom)
    h = _modulate(_rms(x, g_ref[...]), sc_ref[0], sh_ref[0])
    proj = jnp.dot(h.astype(BF16), wbf_ref[...], preferred_element_type=F32)
    xa_ref[...] = proj[:, :d_lru].astype(BF16)
    ga_ref[...] = proj[:, d_lru:2 * d_lru].astype(BF16)
    for g in range(pw_ref.shape[0]):
        lo = 2 * d_lru + g * gw
        z_ref[:, g * gw:(g + 1) * gw] = jnp.dot(
            proj[:, lo:lo + gw].astype(BF16), pw_ref[g], preferred_element_type=F32).astype(BF16)


def _input_projection(x_lat, x_ctx, modtab, norm_g, w_in, pool_w, n_rows, geom):
    d = x_lat.shape[-1]
    tm = geom.tm
    d_pool = pool_w.shape[0] * pool_w.shape[1]
    d_lru = (w_in.shape[1] - d_pool) // 2
    mset = lambda i: jnp.where(i < geom.n_lat_tiles, 0, 1)
    const = dict(pipeline_mode=pl.Buffered(1))
    return pl.pallas_call(
        functools.partial(_proj_kernel, d_lru=d_lru, gw=pool_w.shape[1], geom=geom),
        grid=(n_rows // tm,),
        in_specs=[pl.BlockSpec(memory_space=pl.ANY),
                  pl.BlockSpec(memory_space=pl.ANY),
                  pl.BlockSpec((1, SUBLANES, d), lambda i: (mset(i), 0, 0)),
                  pl.BlockSpec((1, SUBLANES, d), lambda i: (mset(i), 0, 1)),
                  pl.BlockSpec((1, d), lambda i: (0, 0)),
                  pl.BlockSpec(memory_space=pl.ANY),
                  pl.BlockSpec(pool_w.shape, lambda i: (0, 0, 0), **const)],
        out_specs=[pl.BlockSpec((tm, d_lru), lambda i: (i, 0)),
                   pl.BlockSpec((tm, d_lru), lambda i: (i, 0)),
                   pl.BlockSpec((tm, d_pool), lambda i: (i, 0))],
        out_shape=[jax.ShapeDtypeStruct((n_rows, d_lru), BF16),
                   jax.ShapeDtypeStruct((n_rows, d_lru), BF16),
                   jax.ShapeDtypeStruct((n_rows, d_pool), BF16)],
        scratch_shapes=[pltpu.VMEM((2, tm // SUBLANES, SUBLANES, d), F32),
                        pltpu.SemaphoreType.DMA((2,)),
                        pltpu.VMEM(w_in.shape, BF16),
                        pltpu.VMEM((2, min(WEIGHT_STAGE_ROWS, d), w_in.shape[1]), F32),
                        pltpu.SemaphoreType.DMA((2,))],
        compiler_params=_cparams(("arbitrary",)),
        name="input_projection",
    )(x_lat, x_ctx, modtab, modtab, norm_g.reshape(1, d), w_in, pool_w)


def _fill_ext(ext_ref, prev_ref, main_ref, next_ref, first, last):
    tm = main_ref.shape[0]
    prev = prev_ref[...].astype(F32)
    nxt = next_ref[...].astype(F32)
    ext_ref[0:BF16_ROWS, :] = jnp.where(first, 0.0, prev)
    ext_ref[BF16_ROWS:BF16_ROWS + tm, :] = main_ref[...].astype(F32)
    ext_ref[BF16_ROWS + tm:, :] = jnp.where(last, 0.0, nxt)


def _one_minus_exp(y, exp_y):
    p = 1.0 / 120.0
    for c in (1.0 / 24.0, 1.0 / 6.0, 0.5, 1.0):
        p = p * y + c
    return jnp.where(y > -EXPM1_SERIES_BELOW, -y * p, 1.0 - exp_y)


def _sqrt_nonneg(q):
    return jnp.where(q > 0.0, q * lax.rsqrt(q), 0.0)


def _decay_and_input(ext_ref, cw_ref, cb_ref, wg_ref, bg_ref, lam_ref, a_ref, u_ref, tm):
    heads, bw = wg_ref.shape[0], wg_ref.shape[1]
    for hd in range(heads):
        sl = slice(hd * bw, (hd + 1) * bw)
        xc = cb_ref[:, sl]
        for k in range(CONV_W):
            xc = xc + cw_ref[k:k + 1, sl] * ext_ref[SUBLANES * k:SUBLANES * k + tm, sl]
        g = jnp.dot(xc.astype(BF16), wg_ref[hd], preferred_element_type=F32) + bg_ref[hd]
        r = jax.nn.sigmoid(g[:, :bw])
        ig = jax.nn.sigmoid(g[:, bw:])
        log_a = (RG_C * r) * lam_ref[:, sl]
        a = jnp.exp(log_a)
        a_ref[:, sl] = a
        u_ref[:, sl] = _sqrt_nonneg(_one_minus_exp(2.0 * log_a, a * a)) * (ig * xc)


def _scan(a_ref, u_ref, h, tm, reverse):
    nblk = tm // SUBLANES

    def body(s, h):
        j = (nblk - 1 - s) if reverse else s
        rows = pl.ds(pl.multiple_of(j * SUBLANES, SUBLANES), SUBLANES)
        h = a_ref[rows, :] * h + u_ref[rows, :]
        u_ref[rows, :] = h
        return h

    return lax.fori_loop(0, nblk, body, h, unroll=8)


def _halo_specs(tm, d_lru, n_rows):
    per = tm // BF16_ROWS
    last_blk = n_rows // BF16_ROWS - 1
    return [pl.BlockSpec((BF16_ROWS, d_lru), lambda i, tr, fl: (jnp.maximum(tr[i] * per - 1, 0), 0)),
            pl.BlockSpec((tm, d_lru), lambda i, tr, fl: (tr[i], 0)),
            pl.BlockSpec((BF16_ROWS, d_lru), lambda i, tr, fl: (jnp.minimum((tr[i] + 1) * per, last_blk), 0))]


def _fwd_kernel(tr_ref, fl_ref, prev_ref, main_ref, next_ref, cw_ref, cb_ref, wg_ref, bg_ref, lam_ref,
                h0_ref, hf_ref, hlast_ref, ext_ref, a_ref, u_ref, h_ref):
    i = pl.program_id(0)
    tm = main_ref.shape[0]
    first = fl_ref[0, i] == 1
    last = fl_ref[1, i] == 1
    _fill_ext(ext_ref, prev_ref, main_ref, next_ref, first, last)
    _decay_and_input(ext_ref, cw_ref, cb_ref, wg_ref, bg_ref, lam_ref, a_ref, u_ref, tm)

    @pl.when(first)
    def _():
        h_ref[...] = h0_ref[0]

    h = _scan(a_ref, u_ref, h_ref[...], tm, reverse=False)
    h_ref[...] = h
    hlast_ref[0] = h
    hf_ref[...] = u_ref[...].astype(BF16)


def _forward_scan(xa, meta, conv_w, conv_b, wg, bg, lam, h0, tm):
    t, d_lru = xa.shape
    tile_row, flags = meta
    ngrp = h0.shape[0]
    const = dict(pipeline_mode=pl.Buffered(1))
    grid_spec = pltpu.PrefetchScalarGridSpec(
        num_scalar_prefetch=2,
        grid=(t // tm,),
        in_specs=_halo_specs(tm, d_lru, t) + [
            pl.BlockSpec(conv_w.shape, lambda i, tr, fl: (0, 0)),
            pl.BlockSpec((1, d_lru), lambda i, tr, fl: (0, 0)),
            pl.BlockSpec(wg.shape, lambda i, tr, fl: (0, 0, 0), **const),
            pl.BlockSpec(bg.shape, lambda i, tr, fl: (0, 0, 0)),
            pl.BlockSpec((1, d_lru), lambda i, tr, fl: (0, 0)),
            pl.BlockSpec((1, SUBLANES, d_lru), lambda i, tr, fl: (fl[2, i], 0, 0))],
        out_specs=[pl.BlockSpec((tm, d_lru), lambda i, tr, fl: (tr[i], 0)),
                   pl.BlockSpec((1, SUBLANES, d_lru), lambda i, tr, fl: (fl[2, i], 0, 0))],
        scratch_shapes=[pltpu.VMEM((tm + 2 * BF16_ROWS, d_lru), F32),
                        pltpu.VMEM((tm, d_lru), F32),
                        pltpu.VMEM((tm, d_lru), F32),
                        pltpu.VMEM((SUBLANES, d_lru), F32)])
    return pl.pallas_call(
        _fwd_kernel,
        grid_spec=grid_spec,
        out_shape=[jax.ShapeDtypeStruct((t, d_lru), BF16),
                   jax.ShapeDtypeStruct((ngrp, SUBLANES, d_lru), F32)],
        compiler_params=_cparams(("arbitrary",)),
        name="forward_scan",
    )(tile_row, flags, xa, xa, xa, conv_w, conv_b.reshape(1, d_lru), wg, bg, lam.reshape(1, d_lru), h0)


def _bwd_kernel(tr_ref, fl_ref, prev_ref, main_ref, next_ref, cw_ref, cb_ref, wg_ref, bg_ref, lam_ref,
                h0_ref, hf_ref, ga_ref, ybl_ref, ybc_ref, lat_hbm, ctx_hbm, g1_ref, sh2_ref, sc2_ref, wout_ref,
                n2_ref, rwt_ref, rb_ref,
                x1_ref, h2_ref, lg_ref, hlast_ref,
                ext_ref, a_ref, u_ref, h_ref, cat_ref, xbuf, xsem, pack_ref, *, geom):
    i = pl.program_id(0)
    n_steps = pl.num_programs(0)
    tm, d_lru = main_ref.shape
    x = _fetch_rows(lat_hbm, ctx_hbm, xbuf, xsem, i, n_steps,
                    lambda s: tr_ref[jnp.minimum(s, n_steps - 1)], geom)
    first = fl_ref[0, i] == 1
    last = fl_ref[1, i] == 1
    _fill_ext(ext_ref, prev_ref, main_ref, next_ref, first, last)
    _decay_and_input(ext_ref, cw_ref, cb_ref, wg_ref, bg_ref, lam_ref, a_ref, u_ref, tm)

    @pl.when(last)
    def _():
        h_ref[...] = h0_ref[0]

    h = _scan(a_ref, u_ref, h_ref[...], tm, reverse=True)
    h_ref[...] = h
    hlast_ref[0] = h

    ga = ga_ref[...].astype(F32)
    ya = (hf_ref[...].astype(F32) + u_ref[...]) * jax.nn.gelu(ga)
    cat_ref[:, :d_lru] = ya.astype(BF16)
    cat_ref[:, d_lru:] = jnp.where(tr_ref[i] < geom.n_lat_tiles, ybl_ref[...], ybc_ref[...])
    mix = jnp.dot(cat_ref[...], wout_ref[...], preferred_element_type=F32)
    x1 = x + _gate(mix, g1_ref[0])
    x1_ref[...] = x1
    h2 = _modulate(_rms(x1, n2_ref[...]), sc2_ref[0], sh2_ref[0])
    half = h2.shape[1] // 2
    _store_packed_tokens(h2_ref, h2[:, :half], h2[:, half:], pack_ref)
    lg_ref[...] = lax.dot_general(rwt_ref[...], h2.astype(BF16), (((1,), (1,)), ((), ())),
                                  preferred_element_type=F32) + rb_ref[...]


def _backward_scan_mix(xa, ga, hf, yb_lat, yb_ctx, x_lat, x_ctx, modtab, meta, conv_w, conv_b, wg, bg, lam, h0, w_out,
                       norm2_g, rwt, rb, geom):
    t, d_lru = xa.shape
    d = x_lat.shape[-1]
    d_pool = yb_lat.shape[1]
    n_lat, n_ctx = yb_lat.shape[0] // geom.tm, yb_ctx.shape[0] // geom.tm
    tm = geom.tm
    tile_row, flags = meta
    ngrp = h0.shape[0]
    nr = rwt.shape[0]
    const = dict(pipeline_mode=pl.Buffered(1))
    mset = lambda i, tr, fl: jnp.where(tr[i] < geom.n_lat_tiles, 0, 1)
    row = lambda i, tr, fl: (tr[i], 0)
    mod = lambda col: pl.BlockSpec((1, SUBLANES, d), lambda i, tr, fl: (mset(i, tr, fl), 0, col))
    grid_spec = pltpu.PrefetchScalarGridSpec(
        num_scalar_prefetch=2,
        grid=(t // tm,),
        in_specs=_halo_specs(tm, d_lru, t) + [
            pl.BlockSpec(conv_w.shape, lambda i, tr, fl: (0, 0)),
            pl.BlockSpec((1, d_lru), lambda i, tr, fl: (0, 0)),
            pl.BlockSpec(wg.shape, lambda i, tr, fl: (0, 0, 0), **const),
            pl.BlockSpec(bg.shape, lambda i, tr, fl: (0, 0, 0)),
            pl.BlockSpec((1, d_lru), lambda i, tr, fl: (0, 0)),
            pl.BlockSpec((1, SUBLANES, d_lru), lambda i, tr, fl: (fl[2, i], 0, 0)),
            pl.BlockSpec((tm, d_lru), row),
            pl.BlockSpec((tm, d_lru), row),
            pl.BlockSpec((tm, d_pool), lambda i, tr, fl: (jnp.minimum(tr[i], n_lat - 1), 0)),
            pl.BlockSpec((tm, d_pool), lambda i, tr, fl: (jnp.clip(tr[i] - n_lat, 0, n_ctx - 1), 0)),
            pl.BlockSpec(memory_space=pl.ANY),
            pl.BlockSpec(memory_space=pl.ANY),
            mod(2), mod(3), mod(4),
            pl.BlockSpec(w_out.shape, lambda i, tr, fl: (0, 0), **const),
            pl.BlockSpec((1, d), lambda i, tr, fl: (0, 0)),
            pl.BlockSpec(rwt.shape, lambda i, tr, fl: (0, 0)),
            pl.BlockSpec((nr, 1), lambda i, tr, fl: (0, 0))],
        out_specs=[pl.BlockSpec((tm, d), row),
                   pl.BlockSpec((tm * (d // 2) // LANES, LANES), row),
                   pl.BlockSpec((nr, tm), lambda i, tr, fl: (0, tr[i])),
                   pl.BlockSpec((1, SUBLANES, d_lru), lambda i, tr, fl: (fl[2, i], 0, 0))],
        scratch_shapes=[pltpu.VMEM((tm + 2 * BF16_ROWS, d_lru), F32),
                        pltpu.VMEM((tm, d_lru), F32),
                        pltpu.VMEM((tm, d_lru), F32),
                        pltpu.VMEM((SUBLANES, d_lru), F32),
                        pltpu.VMEM((tm, d_lru + d_pool), BF16),
                        pltpu.VMEM((2, tm // SUBLANES, SUBLANES, d), F32),
                        pltpu.SemaphoreType.DMA((2,)),
                        pltpu.VMEM((d // 2 // LANES, 2 * tm, LANES), F32)])
    return pl.pallas_call(
        functools.partial(_bwd_kernel, geom=geom),
        grid_spec=grid_spec,
        out_shape=[jax.ShapeDtypeStruct((t, d), F32),
                   jax.ShapeDtypeStruct((t * (d // 2) // LANES, LANES), U32),
                   jax.ShapeDtypeStruct((nr, t), F32),
                   jax.ShapeDtypeStruct((ngrp, SUBLANES, d_lru), F32)],
        compiler_params=_cparams(("arbitrary",)),
        name="backward_scan_mix",
    )(tile_row, flags, xa, xa, xa, conv_w, conv_b.reshape(1, d_lru), wg, bg, lam.reshape(1, d_lru), h0,
      hf, ga, yb_lat, yb_ctx, x_lat, x_ctx, modtab, modtab, modtab, w_out, norm2_g.reshape(1, d), rwt, rb)


def _shift_rows(v, k):
    if k == 0:
        return v
    z = jnp.zeros((abs(k) * SUBLANES, v.shape[1]), v.dtype)
    if k > 0:
        return jnp.concatenate([z, v[:-k * SUBLANES]], axis=0)
    return jnp.concatenate([v[-k * SUBLANES:], z], axis=0)


def _run_sum(v, m, direction):
    if m & (m - 1) == 0:
        k = 1
        while k < m:
            v = v + _shift_rows(v, -direction * k)
            k *= 2
        return v
    out = v
    for j in range(1, m):
        out = out + _shift_rows(v, -direction * j)
    return out


def _box_sum(v, w):
    lo = w // 2
    hi = w - 1 - lo
    s = _run_sum(v, hi + 1, +1)
    if lo:
        s = s + _shift_rows(_run_sum(v, lo, -1), 1)
    return s


def _window_count(n_rows, lanes, n_pos, w):
    lo = w // 2
    hi = w - 1 - lo
    p = lax.shift_right_logical(lax.broadcasted_iota(I32, (n_rows, lanes), 0), 3)
    return (jnp.minimum(p + hi + 1, n_pos) - jnp.maximum(p - lo, 0)).astype(F32)


def _pool_kernel(z_ref, ps_ref, o_ref, v_ref, *, tiles_per_group, grid_rows, grid_cols):
    group = pl.program_id(1) // tiles_per_group
    lanes = z_ref.shape[1]
    ps = ps_ref[...]

    def pool_1d(w):
        z = z_ref[...].astype(F32)
        n = z.shape[0] // SUBLANES
        mean = _box_sum(z, w) / _window_count(z.shape[0], lanes, n, w)
        o_ref[...] = ((mean - z) * ps).astype(o_ref.dtype)

    def pool_2d(w):
        lo = w // 2
        hi = w - 1 - lo
        blk = grid_cols * SUBLANES
        cw = _window_count(blk, lanes, grid_cols, w)

        def zrow(r):
            return z_ref[pl.ds(pl.multiple_of(r * blk, blk), blk), :].astype(F32)

        v = jnp.zeros((blk, lanes), F32)
        for r in range(hi):
            v = v + zrow(r)
        v_ref[...] = v

        def body(r, carry):
            add = r + hi
            sub = r - lo - 1
            v = v_ref[...]
            v = v + jnp.where(add < grid_rows, zrow(jnp.minimum(add, grid_rows - 1)), 0.0)
            v = v - jnp.where(sub >= 0, zrow(jnp.maximum(sub, 0)), 0.0)
            v_ref[...] = v
            ch = (jnp.minimum(r + hi + 1, grid_rows) - jnp.maximum(r - lo, 0)).astype(F32)
            mean = _box_sum(v, w) / (ch * cw)
            o_ref[pl.ds(pl.multiple_of(r * blk, blk), blk), :] = ((mean - zrow(r)) * ps).astype(o_ref.dtype)
            return carry

        lax.fori_loop(0, grid_rows, body, 0)

    for g, w in enumerate(POOL_WINDOWS):
        @pl.when(group == g)
        def _(w=w):
            if grid_rows is None:
                pool_1d(w)
            else:
                pool_2d(w)


def _pool(z, pool_scale, rows_per_group, n_groups, first_block, grid_rows):
    d_pool = z.shape[1]
    gw = d_pool // len(POOL_WINDOWS)
    lanes = LANES
    blk = GRID_W * SUBLANES
    return pl.pallas_call(
        functools.partial(_pool_kernel, tiles_per_group=gw // lanes, grid_rows=grid_rows, grid_cols=GRID_W),
        grid=(n_groups, d_pool // lanes),
        in_specs=[pl.BlockSpec((rows_per_group, lanes), lambda g, j: (first_block + g, j)),
                  pl.BlockSpec((1, lanes), lambda g, j: (0, j))],
        out_specs=pl.BlockSpec((rows_per_group, lanes), lambda g, j: (g, j)),
        out_shape=jax.ShapeDtypeStruct((n_groups * rows_per_group, d_pool), BF16),
        scratch_shapes=[pltpu.VMEM((blk, lanes), F32)],
        compiler_params=_cparams(("arbitrary", "arbitrary")),
        name="pool_grid" if grid_rows is not None else "pool_seq",
    )(z, pool_scale.reshape(1, d_pool))


def _route_kernel(lg_ref, oi_ref, of_ref, cnt_ref, bc_ref, tri_ref, carry_ref, *, n_groups, per_group):
    step = pl.program_id(0)
    tt = lg_ref.shape[1]
    n_exp = n_groups * per_group

    @pl.when(step == 0)
    def _():
        r = lax.broadcasted_iota(I32, tri_ref.shape, 0)
        c = lax.broadcasted_iota(I32, tri_ref.shape, 1)
        tri_ref[...] = (r <= c).astype(BF16)
        carry_ref[...] = jnp.zeros_like(carry_ref)

    row = lambda k: lg_ref[k:k + 1, :]
    cmax = row(0)
    gi = jnp.zeros((1, tt), I32)
    for g in range(1, n_groups):
        better = row(g) > cmax
        gi = jnp.where(better, g, gi)
        cmax = jnp.where(better, row(g), cmax)
    denom = jnp.zeros((1, tt), F32)
    for g in range(n_groups):
        denom = denom + jnp.exp(row(g) - cmax)
    pg = 1.0 / denom
    fine = []
    for j in range(per_group):
        f = row(n_groups + j)
        for g in range(1, n_groups):
            f = jnp.where(gi == g, row(n_groups + g * per_group + j), f)
        fine.append(f)
    v1 = fine[0]
    i1 = jnp.zeros((1, tt), I32)
    for j in range(1, per_group):
        better = fine[j] > v1
        i1 = jnp.where(better, j, i1)
        v1 = jnp.where(better, fine[j], v1)
    v2 = jnp.full((1, tt), -jnp.inf, F32)
    i2 = jnp.zeros((1, tt), I32)
    for j in range(per_group):
        better = jnp.logical_and(i1 != j, fine[j] > v2)
        i2 = jnp.where(better, j, i2)
        v2 = jnp.where(better, fine[j], v2)
    ex = jnp.exp(v2 - v1)
    w1 = (1.0 / (1.0 + ex)) * pg
    w2 = (ex / (1.0 + ex)) * pg
    e1 = gi * per_group + i1
    e2 = gi * per_group + i2
    eid = lax.broadcasted_iota(I32, (n_exp, tt), 0)
    hit1 = eid == e1
    hit2 = eid == e2
    member = jnp.logical_or(hit1, hit2).astype(BF16)
    cb = tri_ref.shape[0]
    carry = carry_ref[...]
    cums = []
    lane = lax.broadcasted_iota(I32, bc_ref.shape, 1)
    block_ends = jnp.zeros(bc_ref.shape, F32)
    for s in range(tt // cb):
        c = jnp.dot(member[:, s * cb:(s + 1) * cb], tri_ref[...], preferred_element_type=F32) + carry
        carry = c[:, cb - 1:cb]
        cums.append(c)
        block_ends = jnp.where(lane == s, carry, block_ends)
    bc_ref[...] = block_ends.astype(I32)
    carry_ref[...] = carry
    cum = jnp.concatenate(cums, axis=1)
    rank1 = jnp.sum(jnp.where(hit1, cum, 0.0), axis=0, keepdims=True) - 1.0
    rank2 = jnp.sum(jnp.where(hit2, cum, 0.0), axis=0, keepdims=True) - 1.0
    zi = jnp.zeros((SUBLANES - 4, tt), I32)
    oi_ref[...] = jnp.concatenate([e1, e2, rank1.astype(I32), rank2.astype(I32), zi], axis=0)
    of_ref[...] = jnp.concatenate([w1, w2, jnp.zeros((of_ref.shape[0] - 2, tt), F32)], axis=0)
    cnt_ref[...] = jnp.broadcast_to(carry, cnt_ref.shape).astype(I32)


def _route(logits_t, n_groups, per_group, cb):
    nr, t = logits_t.shape
    tt = max(k for k in range(cb, min(ROUTE_TILE, t) + 1, cb) if t % k == 0)
    n_exp = n_groups * per_group
    return pl.pallas_call(
        functools.partial(_route_kernel, n_groups=n_groups, per_group=per_group),
        grid=(t // tt,),
        in_specs=[pl.BlockSpec((nr, tt), lambda i: (0, i))],
        out_specs=[pl.BlockSpec((SUBLANES, tt), lambda i: (0, i)),
                   pl.BlockSpec((LANES, tt), lambda i: (0, i)),
                   pl.BlockSpec((n_exp, LANES), lambda i: (0, 0)),
                   pl.BlockSpec((n_exp, LANES), lambda i: (0, i))],
        out_shape=[jax.ShapeDtypeStruct((SUBLANES, t), I32),
                   jax.ShapeDtypeStruct((LANES, t), F32),
                   jax.ShapeDtypeStruct((n_exp, LANES), I32),
                   jax.ShapeDtypeStruct((n_exp, LANES * (t // tt)), I32)],
        scratch_shapes=[pltpu.VMEM((cb, cb), BF16), pltpu.VMEM((n_exp, 1), F32)],
        compiler_params=_cparams(("arbitrary",)),
        name="route",
    )(logits_t)


def _tiles_for(cnt, tmx):
    return jnp.right_shift(cnt + (SEG_CHUNK + tmx - 1), tmx.bit_length() - 1)


def _plan_kernel(ri_ref, cnt_ref, bc_ref, bcp_ref, lp_ref, dst_ref, src_ref, len_ref, te_ref, nv_ref,
                 off_ref, *, tmx, tb):
    step = pl.program_id(0)
    n_exp = cnt_ref.shape[0]
    tt = ri_ref.shape[1]
    shift = tmx.bit_length() - 1
    n_tile = _tiles_for(cnt_ref[:, 0:1], tmx)

    def exclusive_prefix(col):
        run = jnp.zeros((1, 1), I32)
        parts = []
        for e in range(n_exp):
            parts.append(run)
            run = run + col[e:e + 1, :]
        return jnp.concatenate(parts, axis=0), run

    off, total = exclusive_prefix(jnp.left_shift(n_tile, shift))
    before = jnp.where(step == 0, 0, bcp_ref[:, tt // tb - 1:tt // tb])
    lane = lax.broadcasted_iota(I32, dst_ref.shape, 1)
    eid = lax.broadcasted_iota(I32, (n_exp, tb), 0)
    dst = jnp.zeros(dst_ref.shape, I32)
    src = jnp.zeros(dst_ref.shape, I32)
    length = jnp.zeros(dst_ref.shape, I32)
    rows = []
    for k in range(tt // tb):
        end = bc_ref[:, k:k + 1]
        n = end - before
        aligned = jnp.left_shift(jnp.right_shift(n + (SEG_CHUNK - 1), SEG_SHIFT), SEG_SHIFT)
        start, _ = exclusive_prefix(aligned)
        base = start - before
        sl = slice(k * tb, (k + 1) * tb)
        place = lambda e_row, r_row: jnp.sum(jnp.where(eid == e_row, base, 0), axis=0, keepdims=True) + r_row
        rows.append(jnp.concatenate([place(ri_ref[0:1, sl], ri_ref[2:3, sl]),
                                     place(ri_ref[1:2, sl], ri_ref[3:4, sl])], axis=0))
        dst = jnp.where(lane == k, off + before, dst)
        src = jnp.where(lane == k, start, src)
        length = jnp.where(lane == k, n, length)
        before = end
    lp_ref[...] = jnp.concatenate([jnp.concatenate(rows, axis=1),
                                   jnp.zeros((SUBLANES - TOP_K_FINE, tt), I32)], axis=0)
    dst_ref[...] = dst
    src_ref[...] = src
    len_ref[...] = length
    end_tile = jnp.right_shift(off, shift) + n_tile
    k = lax.broadcasted_iota(I32, (n_exp, te_ref.shape[1]), 1)
    te = jnp.sum((k >= end_tile).astype(I32), axis=0, keepdims=True)
    te_ref[...] = jnp.minimum(te, n_exp - 1)
    nv_ref[...] = jnp.broadcast_to(jnp.right_shift(total, shift), nv_ref.shape)
    off_ref[...] = jnp.broadcast_to(off, off_ref.shape)


def _plan(ri, counts, block_counts, tmx, n_tiles, tb):
    _, t = ri.shape
    n_exp = counts.shape[0]
    steps = block_counts.shape[1] // LANES
    tt = t // steps
    ntp = -(-n_tiles // LANES) * LANES
    per_step = lambda: pl.BlockSpec((n_exp, LANES), lambda i: (0, i))
    whole = lambda w: pl.BlockSpec((n_exp, w), lambda i: (0, 0))
    tab = jax.ShapeDtypeStruct((n_exp, LANES * steps), I32)
    lp, dst, src, length, te, nv, off = pl.pallas_call(
        functools.partial(_plan_kernel, tmx=tmx, tb=tb),
        grid=(steps,),
        in_specs=[pl.BlockSpec((SUBLANES, tt), lambda i: (0, i)),
                  whole(LANES),
                  per_step(),
                  pl.BlockSpec((n_exp, LANES), lambda i: (0, jnp.maximum(i - 1, 0)))],
        out_specs=[pl.BlockSpec((SUBLANES, tt), lambda i: (0, i)),
                   per_step(), per_step(), per_step(),
                   pl.BlockSpec((1, ntp), lambda i: (0, 0)),
                   pl.BlockSpec((1, LANES), lambda i: (0, 0)),
                   whole(LANES)],
        out_shape=[jax.ShapeDtypeStruct((SUBLANES, t), I32), tab, tab, tab,
                   jax.ShapeDtypeStruct((1, ntp), I32),
                   jax.ShapeDtypeStruct((1, LANES), I32),
                   jax.ShapeDtypeStruct((n_exp, LANES), I32)],
        compiler_params=_cparams(("arbitrary",)),
        name="plan",
    )(ri, counts, block_counts, block_counts)
    flat = lambda a: a.reshape(n_exp, steps, LANES)[:, :, :tt // tb].transpose(1, 2, 0).reshape(-1)
    tables = (flat(dst), flat(src), flat(length))
    return (lp[0].reshape(t // tb, tb), lp[1].reshape(t // tb, tb), tables,
            te[0, :n_tiles], nv[0, :1], off[:, 0])


def _load_positions(pos1_hbm, pos2_hbm, p1_ref, p2_ref, isem, step, n_steps):
    tb = p1_ref.shape[0] // 2
    slot = lax.rem(step, 2)

    def copies(blk, half):
        rows = pl.ds(pl.multiple_of(half * tb, tb), tb)
        return (pltpu.make_async_copy(pos1_hbm.at[blk], p1_ref.at[rows], isem.at[0, half]),
                pltpu.make_async_copy(pos2_hbm.at[blk], p2_ref.at[rows], isem.at[1, half]))

    @pl.when(step == 0)
    def _():
        for cp in copies(step, slot):
            cp.start()

    @pl.when(step + 1 < n_steps)
    def _():
        for cp in copies(step + 1, 1 - slot):
            cp.start()

    for cp in copies(step, slot):
        cp.wait()
    return slot * tb


def _token_rows(token, count, nc):
    return pl.ds(pl.multiple_of(token * nc, nc), count * nc)


def _segment_copy(stage_ref, far_hbm, near_tok, far_tok, sem, to_far, nc):
    near = stage_ref.at[_token_rows(near_tok, SEG_CHUNK, nc), :]
    far = far_hbm.at[_token_rows(far_tok, SEG_CHUNK, nc), :]
    return pltpu.make_async_copy(near, far, sem) if to_far else pltpu.make_async_copy(far, near, sem)


def _start_segments(tables, blk, n_exp, stage_ref, far_hbm, sem, to_far, nc):
    dst_ref, src_ref, len_ref = tables

    total = jnp.int32(0)
    for e in range(n_exp):
        k = blk * n_exp + e
        n_chunks = jnp.right_shift(len_ref[k] + (SEG_CHUNK - 1), SEG_SHIFT)
        near0 = src_ref[k]
        far0 = dst_ref[k]

        def one(c, carry, near0=near0, far0=far0, queue=e % 2):
            _segment_copy(stage_ref, far_hbm, near0 + c * SEG_CHUNK, far0 + c * SEG_CHUNK, sem,
                          to_far, nc).start(priority=queue)
            return carry
        lax.fori_loop(0, n_chunks, one, 0)
        total = total + n_chunks
    return total


def _wait_segments(stage_ref, far_hbm, sem, n, to_far, nc):
    copy = _segment_copy(stage_ref, far_hbm, 0, 0, sem, to_far, nc)

    def body(_, c):
        copy.wait()
        return c
    lax.fori_loop(0, n, body, 0)


def _dispatch_kernel(dst_ref, src_ref, len_ref, off_ref, cnt_ref, lp1_hbm, lp2_hbm, h2_ref, xs_hbm,
                     p1_ref, p2_ref, stage, nd_ref, zero_ref, isem, ssem, zsem, *, tmx):
    step = pl.program_id(0)
    n_steps = pl.num_programs(0)
    slot = lax.rem(step, 2)
    n_exp = off_ref.shape[0]
    nc = zero_ref.shape[0]
    n_blk = h2_ref.shape[0] // nc // SUBLANES
    tables = (dst_ref, src_ref, len_ref)

    @pl.when(step == 0)
    def _():
        for s in range(stage.shape[0]):
            stage[s] = _packed_zeros(stage.shape[1])

    p0 = _load_positions(lp1_hbm, lp2_hbm, p1_ref, p2_ref, isem, step, n_steps)

    def place(blk, c):
        j0 = blk * SUBLANES
        for s in range(SUBLANES):
            row = h2_ref[_token_rows(j0 + s, 1, nc), :]
            stage[slot, _token_rows(p1_ref[p0 + j0 + s], 1, nc), :] = row
            stage[slot, _token_rows(p2_ref[p0 + j0 + s], 1, nc), :] = row
        return c
    lax.fori_loop(0, n_blk, place, 0, unroll=2)

    @pl.when(step >= 1)
    def _():
        _wait_segments(stage.at[1 - slot], xs_hbm, ssem.at[1 - slot], nd_ref[1 - slot], True, nc)

    nd_ref[slot] = _start_segments(tables, step, n_exp, stage.at[slot], xs_hbm, ssem.at[slot], True, nc)

    @pl.when(step == n_steps - 1)
    def _():
        _wait_segments(stage.at[slot], xs_hbm, ssem.at[slot], nd_ref[slot], True, nc)
        zero_ref[...] = _packed_zeros(nc)
        shift = tmx.bit_length() - 1

        def pad_expert(e, c):
            cnt = cnt_ref[e]
            n_pad = jnp.left_shift(_tiles_for(cnt, tmx), shift) - cnt
            first = off_ref[e] + cnt
            zero_copy = lambda r: pltpu.make_async_copy(
                zero_ref, xs_hbm.at[_token_rows(first + r, 1, nc), :], zsem)

            def fill(r, c2):
                zero_copy(r).start()
                return c2
            lax.fori_loop(0, n_pad, fill, 0)

            def done(r, c2):
                zero_copy(r).wait()
                return c2
            lax.fori_loop(0, n_pad, done, 0)
            return c
        lax.fori_loop(0, off_ref.shape[0], pad_expert, 0)


def _stage_rows(tb, n_exp):
    return TOP_K_FINE * tb + n_exp * SEG_CHUNK


def _dispatch(tables, offsets, counts, lp1, lp2, h2p, p_max, tmx):
    nb, tb = lp1.shape
    nc = h2p.shape[0] // (nb * tb)
    n_exp = offsets.shape[0]
    hbm = lambda: pl.BlockSpec(memory_space=pl.ANY)
    grid_spec = pltpu.PrefetchScalarGridSpec(
        num_scalar_prefetch=5,
        grid=(nb,),
        in_specs=[hbm(), hbm(),
                  pl.BlockSpec((tb * nc, LANES), lambda i, *_: (i, 0))],
        out_specs=hbm(),
        scratch_shapes=[pltpu.SMEM((2 * tb,), I32),
                        pltpu.SMEM((2 * tb,), I32),
                        pltpu.VMEM((2, _stage_rows(tb, n_exp) * nc, LANES), U32),
                        pltpu.SMEM((2,), I32),
                        pltpu.VMEM((nc, LANES), U32),
                        pltpu.SemaphoreType.DMA((2, 2)),
                        pltpu.SemaphoreType.DMA((2,)),
                        pltpu.SemaphoreType.DMA(())])
    return pl.pallas_call(
        functools.partial(_dispatch_kernel, tmx=tmx),
        grid_spec=grid_spec,
        out_shape=jax.ShapeDtypeStruct((p_max * nc, LANES), U32),
        compiler_params=_cparams(("arbitrary",)),
        name="dispatch",
    )(*tables, offsets, counts, lp1, lp2, h2p)


def _expert_kernel(te_ref, nv_ref, x_ref, w1_ref, w3_ref, w2_ref, y_ref, pack_ref, *, tmx):
    @pl.when(pl.program_id(0) < nv_ref[0])
    def _():
        lo, hi = _load_packed_tokens(x_ref, tmx, pack_ref)
        lo = lo.astype(BF16)
        hi = hi.astype(BF16)
        half = lo.shape[1]
        mm = lambda w_ref: (jnp.dot(lo, w_ref[0, :half, :].astype(BF16), preferred_element_type=F32)
                            + jnp.dot(hi, w_ref[0, half:, :].astype(BF16), preferred_element_type=F32))
        h1 = mm(w1_ref)
        h3 = mm(w3_ref)
        hh = (h1 * jax.nn.sigmoid(h1) * h3).astype(BF16)
        y = jnp.dot(hh, w2_ref[0].astype(BF16), preferred_element_type=F32)
        _store_packed_tokens(y_ref, y[:, :half], y[:, half:], pack_ref)


def _expert_mlp(xs, tile_expert, n_valid, w1, w3, w2, tmx):
    d, de = w1.shape[1], w1.shape[2]
    nc = d // 2 // LANES
    p = xs.shape[0] // nc
    tile = lambda i, te, nv: (jnp.minimum(i, nv[0] - 1), 0)
    grid_spec = pltpu.PrefetchScalarGridSpec(
        num_scalar_prefetch=2,
        grid=(p // tmx,),
        in_specs=[pl.BlockSpec((tmx * nc, LANES), tile),
                  pl.BlockSpec((1, d, de), lambda i, te, nv: (te[i], 0, 0)),
                  pl.BlockSpec((1, d, de), lambda i, te, nv: (te[i], 0, 0)),
                  pl.BlockSpec((1, de, d), lambda i, te, nv: (te[i], 0, 0))],
        out_specs=pl.BlockSpec((tmx * nc, LANES), tile),
        scratch_shapes=[pltpu.VMEM((nc, 2 * tmx, LANES), F32)])
    return pl.pallas_call(
        functools.partial(_expert_kernel, tmx=tmx),
        grid_spec=grid_spec,
        out_shape=jax.ShapeDtypeStruct((p * nc, LANES), U32),
        compiler_params=_cparams(("arbitrary",)),
        name="expert_mlp",
    )(tile_expert, n_valid, xs, w1, w3, w2)


def _combine_kernel(dst_ref, src_ref, len_ref, lp1_hbm, lp2_hbm, ys_hbm, x1_ref, rf_ref, g2_ref, fg_ref,
                    ylat_hbm, yctx_hbm, p1_ref, p2_ref, stage, gbuf, pack_ref, obuf, nd_ref, isem, csem, osem,
                    *, geom, n_exp):
    step = pl.program_id(0)
    n_steps = pl.num_programs(0)
    tm = geom.tm
    n_blk = tm // SUBLANES
    slot = lax.rem(step, 2)
    nc = gbuf.shape[1] // tm
    tables = (dst_ref, src_ref, len_ref)

    def fetch(blk, gs):
        nd_ref[gs] = _start_segments(tables, blk, n_exp, stage.at[gs], ys_hbm, csem.at[gs], False, nc)

    @pl.when(step == 0)
    def _():
        fetch(step, slot)

    @pl.when(step + 1 < n_steps)
    def _():
        fetch(step + 1, 1 - slot)

    _wait_segments(stage.at[slot], ys_hbm, csem.at[slot], nd_ref[slot], False, nc)
    p0 = _load_positions(lp1_hbm, lp2_hbm, p1_ref, p2_ref, isem, step, n_steps)

    def pick(blk, c):
        j0 = blk * SUBLANES
        for s in range(SUBLANES):
            rows = _token_rows(j0 + s, 1, nc)
            gbuf[0, rows, :] = stage[slot, _token_rows(p1_ref[p0 + j0 + s], 1, nc), :]
            gbuf[1, rows, :] = stage[slot, _token_rows(p2_ref[p0 + j0 + s], 1, nc), :]
        return c
    lax.fori_loop(0, n_blk, pick, 0, unroll=2)

    wt = rf_ref[...].T
    a_lo, a_hi = _load_packed_tokens(gbuf.at[0], tm, pack_ref.at[0])
    b_lo, b_hi = _load_packed_tokens(gbuf.at[1], tm, pack_ref.at[1])
    w1 = wt[:, 0:1]
    w2 = wt[:, 1:2]
    moe = jnp.concatenate([w1 * a_lo + w2 * b_lo, w1 * a_hi + w2 * b_hi], axis=1)
    x = x1_ref[...] + _gate(moe, g2_ref[0])
    out = _rms(x, fg_ref[...])

    start = lambda cp: cp.start()
    wait = lambda cp: cp.wait()
    put = functools.partial(_tile_copies, ylat_hbm, yctx_hbm, geom=geom, to_rows=False)

    @pl.when(step >= 2)
    def _():
        put(obuf.at[slot], osem.at[slot], step - 2, fn=wait)

    obuf[slot] = out.reshape(obuf.shape[1:])
    put(obuf.at[slot], osem.at[slot], step, fn=start)

    @pl.when(step == n_steps - 1)
    def _():
        @pl.when(n_steps > 1)
        def _():
            put(obuf.at[1 - slot], osem.at[1 - slot], step - 1, fn=wait)
        put(obuf.at[slot], osem.at[slot], step, fn=wait)


def _combine(tables, lp1, lp2, ys, x1, rf, modtab, final_g, lat_shape, ctx_shape, geom):
    t, d = x1.shape
    tm = geom.tm
    nc = d // 2 // LANES
    n_exp = tables[0].shape[0] // (t // tm)
    mset = lambda i, *_: jnp.where(i < geom.n_lat_tiles, 0, 1)
    hbm = lambda: pl.BlockSpec(memory_space=pl.ANY)
    grid_spec = pltpu.PrefetchScalarGridSpec(
        num_scalar_prefetch=3,
        grid=(t // tm,),
        in_specs=[hbm(), hbm(), hbm(),
                  pl.BlockSpec((tm, d), lambda i, *_: (i, 0)),
                  pl.BlockSpec((LANES, tm), lambda i, *_: (0, i)),
                  pl.BlockSpec((1, SUBLANES, d), lambda i, *_: (mset(i), 0, 5)),
                  pl.BlockSpec((1, d), lambda i, *_: (0, 0))],
        out_specs=[hbm(), hbm()],
        scratch_shapes=[pltpu.SMEM((2 * tm,), I32),
                        pltpu.SMEM((2 * tm,), I32),
                        pltpu.VMEM((2, _stage_rows(tm, n_exp) * nc, LANES), U32),
                        pltpu.VMEM((TOP_K_FINE, tm * nc, LANES), U32),
                        pltpu.VMEM((TOP_K_FINE, nc, 2 * tm, LANES), F32),
                        pltpu.VMEM((2, tm // SUBLANES, SUBLANES, d), F32),
                        pltpu.SMEM((2,), I32),
                        pltpu.SemaphoreType.DMA((2, 2)),
                        pltpu.SemaphoreType.DMA((2,)),
                        pltpu.SemaphoreType.DMA((2,))])
    return pl.pallas_call(
        functools.partial(_combine_kernel, geom=geom, n_exp=n_exp),
        grid_spec=grid_spec,
        out_shape=[jax.ShapeDtypeStruct(lat_shape, F32), jax.ShapeDtypeStruct(ctx_shape, F32)],
        compiler_params=_cparams(("arbitrary",)),
        name="combine",
    )(*tables, lp1, lp2, ys, x1, rf, modtab, final_g.reshape(1, d))


def _tile_meta(groups, tm):
    rows, first, last, grp = [], [], [], []
    blk = 0
    for g, r in enumerate(groups):
        nc = r // tm
        for c in range(nc):
            rows.append(blk + c)
            first.append(int(c == 0))
            last.append(int(c == nc - 1))
            grp.append(g)
        blk += nc
    fwd = (np.array(rows, np.int32), np.array([first, last, grp], np.int32))
    order = []
    blk = 0
    for r in groups:
        nc = r // tm
        order.extend(range(blk + nc - 1, blk - 1, -1))
        blk += nc
    order = np.array(order)
    bwd = (fwd[0][order], fwd[1][:, order])
    return fwd, bwd


def kernel(x_prompt, x_sample, state_lru, c, c_ctx, w_mod, b_mod, norm1_g, w_in, conv_w, conv_b, lru_wa, lru_ba, lru_wx, lru_bx, lru_lambda, pool_w, pool_scale, w_out, norm2_g, router_coarse_w, router_coarse_b, router_fine_w, router_fine_b, exp_w1, exp_w3, exp_w2, final_norm_g):
    bp, sp, d = x_prompt.shape
    bs, ss, _ = x_sample.shape
    d_lru = lru_lambda.shape[-1]
    heads, bw = lru_wa.shape[2], lru_wa.shape[3]
    n_groups, per_group = router_fine_w.shape[2], router_fine_w.shape[3]
    n_exp = n_groups * per_group
    assert w_mod.shape[0] == 1 and bs == SUBLANES and bp % SUBLANES == 0 and ss % GRID_W == 0
    assert EXPERT_TILE & (EXPERT_TILE - 1) == 0
    n_lat_groups, n_ctx_groups = bs // SUBLANES, bp // SUBLANES
    lat_rows, ctx_rows = ss * SUBLANES, sp * SUBLANES
    assert (n_lat_groups * lat_rows) % ctx_rows == 0
    tm = min(TOKEN_TILE, ctx_rows, lat_rows)
    geom = _Geom(tm=tm, n_lat_tiles=n_lat_groups * lat_rows // tm, lat_chunks=lat_rows // tm,
                 ctx_chunks=ctx_rows // tm)
    n_rows = n_lat_groups * lat_rows + n_ctx_groups * ctx_rows
    groups = [lat_rows] * n_lat_groups + [ctx_rows] * n_ctx_groups
    (f_rows, f_flags), (b_rows, b_flags) = _tile_meta(groups, tm)

    cond = jnp.zeros((2 * SUBLANES, d), F32).at[:bs].set(c).at[bs].set(c_ctx)
    mod = _modulation(cond, w_mod[0], b_mod[0])
    modtab = jnp.stack([mod[:SUBLANES], jnp.broadcast_to(mod[SUBLANES], (SUBLANES, mod.shape[1]))])
    h0_lat = state_lru[:, 0].reshape(n_lat_groups, SUBLANES, 2, d_lru)
    h0 = jnp.concatenate([h0_lat, jnp.zeros((n_ctx_groups, SUBLANES, 2, d_lru), F32)], axis=0)
    h0 = h0.transpose(0, 2, 1, 3)

    xa, ga, z = _input_projection(x_sample, x_prompt, modtab, norm1_g[0], w_in[0],
                                  pool_w[0].astype(BF16), n_rows, geom)

    def gate_weights(direction):
        wg = jnp.concatenate([lru_wa[0, direction], lru_wx[0, direction]], axis=-1).astype(BF16)
        bg = jnp.concatenate([lru_ba[0, direction].reshape(heads, 1, bw),
                              lru_bx[0, direction].reshape(heads, 1, bw)], axis=-1)
        return wg, bg

    log_decay = jax.nn.log_sigmoid(lru_lambda[0])
    wg_f, bg_f = gate_weights(0)
    wg_b, bg_b = gate_weights(1)
    hf, hf_last = _forward_scan(xa, (jnp.asarray(f_rows), jnp.asarray(f_flags)), conv_w[0], conv_b[0],
                                wg_f, bg_f, log_decay[0], h0[:, 0], tm)

    yb_lat = _pool(z, pool_scale[0], lat_rows, n_lat_groups, 0, lat_rows // (GRID_W * SUBLANES))
    yb_ctx = _pool(z, pool_scale[0], ctx_rows, n_ctx_groups, n_lat_groups * lat_rows // ctx_rows, None)

    n_logits = n_groups + n_exp
    rw = jnp.concatenate([router_coarse_w[0], router_fine_w[0].reshape(d, n_exp)], axis=1)
    rwt = jnp.zeros((LANES, d), BF16).at[:n_logits].set(rw.T.astype(BF16))
    rb = jnp.zeros((LANES, 1), F32).at[:n_logits, 0].set(
        jnp.concatenate([router_coarse_b[0], router_fine_b[0].reshape(n_exp)]))
    x1, h2p, logits_t, hb_last = _backward_scan_mix(
        xa, ga, hf, yb_lat, yb_ctx, x_sample, x_prompt, modtab, (jnp.asarray(b_rows), jnp.asarray(b_flags)),
        conv_w[0], conv_b[0], wg_b, bg_b, log_decay[1], h0[:, 1], w_out[0].astype(BF16), norm2_g[0],
        rwt, rb, geom)

    ri, rf, counts, block_counts = _route(logits_t, n_groups, per_group, tm)
    n_tiles = -(-(TOP_K_FINE * n_rows + n_exp * (EXPERT_TILE + SEG_CHUNK)) // EXPERT_TILE)
    lp1, lp2, tables, tile_expert, n_valid, offsets = _plan(ri, counts, block_counts, EXPERT_TILE, n_tiles, tm)
    xs = _dispatch(tables, offsets, counts[:, 0], lp1, lp2, h2p, n_tiles * EXPERT_TILE, EXPERT_TILE)
    ys = _expert_mlp(xs, tile_expert, n_valid, exp_w1[0], exp_w3[0], exp_w2[0], EXPERT_TILE)
    y_sample, y_prompt = _combine(tables, lp1, lp2, ys, x1, rf, modtab, final_norm_g, x_sample.shape,
                                  x_prompt.shape, geom)

    st = jnp.stack([hf_last[n_lat_groups:], hb_last[n_lat_groups:]], axis=2)
    state_new = st.reshape(bp, 1, 2, d_lru).astype(x_prompt.dtype)
    return (y_prompt, y_sample, state_new)
```

```python
import functools
from typing import NamedTuple

import numpy as np
import jax
import jax.numpy as jnp
from jax import lax
from jax.experimental import pallas as pl
from jax.experimental.pallas import tpu as pltpu

GRID_W = 64
CONV_W = 4
RG_C = 8.0
POOL_WINDOWS = (2, 4, 8, 16)
TOP_K_FINE = 2
EPS = 1e-6
EXPM1_SERIES_BELOW = 0.125

SUBLANES = 8
LANES = 128
BF16_ROWS = 16
TOKEN_TILE = 512
EXPERT_TILE = 512
ROUTE_TILE = 2048
SEG_SHIFT = 4
SEG_CHUNK = 1 << SEG_SHIFT
MOD_COL_TILE = 1024
WEIGHT_STAGE_ROWS = 256
VMEM_LIMIT = 60 * 1024 * 1024

F32 = jnp.float32
BF16 = jnp.bfloat16
U32 = jnp.uint32
I32 = jnp.int32


class _Geom(NamedTuple):
    tm: int
    n_lat_tiles: int
    lat_chunks: int
    ctx_chunks: int


def _cparams(sem):
    return pltpu.CompilerParams(dimension_semantics=sem, vmem_limit_bytes=VMEM_LIMIT)


def _per_sequence(y, m):
    rows, d = y.shape
    return y.reshape(rows // SUBLANES, SUBLANES, d), m[None]


def _modulate(y, scale, shift):
    y3, sc = _per_sequence(y, scale)
    _, sh = _per_sequence(y, shift)
    return (y3 * (1.0 + sc) + sh).reshape(y.shape)


def _gate(y, g):
    y3, g3 = _per_sequence(y, g)
    return (y3 * g3).reshape(y.shape)


def _rms(x, g):
    ms = jnp.mean(x * x, axis=-1, keepdims=True)
    return x * lax.rsqrt(ms + EPS) * g


def _store_packed_tokens(ref, lo, hi, scratch):
    n, words = lo.shape
    nc = words // LANES
    for c in range(nc):
        sl = slice(c * LANES, (c + 1) * LANES)
        scratch[c, pl.ds(0, n, stride=2), :] = lo[:, sl]
        scratch[c, pl.ds(1, n, stride=2), :] = hi[:, sl]
        ref[pl.ds(c, n, stride=nc), :] = pltpu.bitcast(scratch[c].astype(BF16), U32)


def _packed_zeros(rows):
    return pltpu.bitcast(jnp.zeros((2 * rows, LANES), BF16), U32)


def _load_packed_tokens(ref, n, scratch):
    nc = ref.shape[0] // n
    los, his = [], []
    for c in range(nc):
        scratch[c] = pltpu.bitcast(ref[pl.ds(c, n, stride=nc), :], BF16).astype(F32)
        los.append(scratch[c, pl.ds(0, n, stride=2), :])
        his.append(scratch[c, pl.ds(1, n, stride=2), :])
    return jnp.concatenate(los, axis=1), jnp.concatenate(his, axis=1)


def _tile_copies(lat_hbm, ctx_hbm, buf, sem, tile, geom, to_rows, fn):
    tt = geom.tm // SUBLANES

    def run(hbm, k, chunks):
        g = lax.div(k, jnp.int32(chunks))
        c = lax.rem(k, jnp.int32(chunks))
        for b in range(SUBLANES):
            h = hbm.at[g * SUBLANES + b, pl.ds(c * tt, tt), :]
            v = buf.at[:, b, :]
            fn(pltpu.make_async_copy(h, v, sem) if to_rows else pltpu.make_async_copy(v, h, sem))

    @pl.when(tile < geom.n_lat_tiles)
    def _():
        run(lat_hbm, tile, geom.lat_chunks)

    @pl.when(tile >= geom.n_lat_tiles)
    def _():
        run(ctx_hbm, tile - geom.n_lat_tiles, geom.ctx_chunks)


def _fetch_rows(lat_hbm, ctx_hbm, xbuf, sem, step, n_steps, tile_of, geom):
    slot = lax.rem(step, 2)
    start = lambda cp: cp.start()
    wait = lambda cp: cp.wait()

    @pl.when(step == 0)
    def _():
        _tile_copies(lat_hbm, ctx_hbm, xbuf.at[0], sem.at[0], tile_of(jnp.int32(0)), geom, True, start)

    @pl.when(step + 1 < n_steps)
    def _():
        nxt = 1 - slot
        _tile_copies(lat_hbm, ctx_hbm, xbuf.at[nxt], sem.at[nxt], tile_of(step + 1), geom, True, start)

    _tile_copies(lat_hbm, ctx_hbm, xbuf.at[slot], sem.at[slot], tile_of(step), geom, True, wait)
    tt, _, d = xbuf.shape[1:]
    return xbuf[slot].reshape(tt * SUBLANES, d)


def _mod_kernel(c_ref, w_ref, b_ref, o_ref):
    c = c_ref[...]
    s = c * jax.nn.sigmoid(c)
    o_ref[...] = jnp.dot(s.astype(BF16), w_ref[...].astype(BF16),
                         preferred_element_type=F32) + b_ref[...]


def _modulation(cond, w_mod, b_mod):
    rows, d = cond.shape
    n = w_mod.shape[1]
    tn = min(MOD_COL_TILE, n)
    return pl.pallas_call(
        _mod_kernel,
        grid=(n // tn,),
        in_specs=[pl.BlockSpec((rows, d), lambda j: (0, 0)),
                  pl.BlockSpec((d, tn), lambda j: (0, j)),
                  pl.BlockSpec((1, tn), lambda j: (0, j))],
        out_specs=pl.BlockSpec((rows, tn), lambda j: (0, j)),
        out_shape=jax.ShapeDtypeStruct((rows, n), F32),
        compiler_params=_cparams(("arbitrary",)),
        name="modulation",
    )(cond, w_mod, b_mod.reshape(1, n))


def _round_weights(w_hbm, wbf_ref, stage, sem):
    rows = stage.shape[1]
    n = w_hbm.shape[0] // rows
    copy = lambda r: pltpu.make_async_copy(w_hbm.at[pl.ds(r * rows, rows), :], stage.at[r % 2], sem.at[r % 2])
    copy(0).start()
    for r in range(n):
        if r + 1 < n:
            copy(r + 1).start()
        copy(r).wait()
        wbf_ref[r * rows:(r + 1) * rows, :] = stage[r % 2].astype(wbf_ref.dtype)


def _proj_kernel(lat_hbm, ctx_hbm, sh_ref, sc_ref, g_ref, win_hbm, pw_ref, xa_ref, ga_ref, z_ref,
                 xbuf, sem, wbf_ref, wstage, wsem, *, d_lru, gw, geom):
    step = pl.program_id(0)

    @pl.when(step == 0)
    def _():
        _round_weights(win_hbm, wbf_ref, wstage, wsem)

    x = _fetch_rows(lat_hbm, ctx_hbm, xbuf, sem, step, pl.num_programs(0), lambda s: s, geom)
    h = _modulate(_rms(x, g_ref[...]), sc_ref[0], sh_ref[0])
    proj = jnp.dot(h.astype(BF16), wbf_ref[...], preferred_element_type=F32)
    xa_ref[...] = proj[:, :d_lru].astype(BF16)
    ga_ref[...] = proj[:, d_lru:2 * d_lru].astype(BF16)
    for g in range(pw_ref.shape[0]):
        lo = 2 * d_lru + g * gw
        z_ref[:, g * gw:(g + 1) * gw] = jnp.dot(
            proj[:, lo:lo + gw].astype(BF16), pw_ref[g], preferred_element_type=F32).astype(BF16)


def _input_projection(x_lat, x_ctx, modtab, norm_g, w_in, pool_w, n_rows, geom):
    d = x_lat.shape[-1]
    tm = geom.tm
    d_pool = pool_w.shape[0] * pool_w.shape[1]
    d_lru = (w_in.shape[1] - d_pool) // 2
    mset = lambda i: jnp.where(i < geom.n_lat_tiles, 0, 1)
    const = dict(pipeline_mode=pl.Buffered(1))
    return pl.pallas_call(
        functools.partial(_proj_kernel, d_lru=d_lru, gw=pool_w.shape[1], geom=geom),
        grid=(n_rows // tm,),
        in_specs=[pl.BlockSpec(memory_space=pl.ANY),
                  pl.BlockSpec(memory_space=pl.ANY),
                  pl.BlockSpec((1, SUBLANES, d), lambda i: (mset(i), 0, 0)),
                  pl.BlockSpec((1, SUBLANES, d), lambda i: (mset(i), 0, 1)),
                  pl.BlockSpec((1, d), lambda i: (0, 0)),
                  pl.BlockSpec(memory_space=pl.ANY),
                  pl.BlockSpec(pool_w.shape, lambda i: (0, 0, 0), **const)],
        out_specs=[pl.BlockSpec((tm, d_lru), lambda i: (i, 0)),
                   pl.BlockSpec((tm, d_lru), lambda i: (i, 0)),
                   pl.BlockSpec((tm, d_pool), lambda i: (i, 0))],
        out_shape=[jax.ShapeDtypeStruct((n_rows, d_lru), BF16),
                   jax.ShapeDtypeStruct((n_rows, d_lru), BF16),
                   jax.ShapeDtypeStruct((n_rows, d_pool), BF16)],
        scratch_shapes=[pltpu.VMEM((2, tm // SUBLANES, SUBLANES, d), F32),
                        pltpu.SemaphoreType.DMA((2,)),
                        pltpu.VMEM(w_in.shape, BF16),
                        pltpu.VMEM((2, min(WEIGHT_STAGE_ROWS, d), w_in.shape[1]), F32),
                        pltpu.SemaphoreType.DMA((2,))],
        compiler_params=_cparams(("arbitrary",)),
        name="input_projection",
    )(x_lat, x_ctx, modtab, modtab, norm_g.reshape(1, d), w_in, pool_w)


def _fill_ext(ext_ref, prev_ref, main_ref, next_ref, first, last):
    tm = main_ref.shape[0]
    prev = prev_ref[...].astype(F32)
    nxt = next_ref[...].astype(F32)
    ext_ref[0:BF16_ROWS, :] = jnp.where(first, 0.0, prev)
    ext_ref[BF16_ROWS:BF16_ROWS + tm, :] = main_ref[...].astype(F32)
    ext_ref[BF16_ROWS + tm:, :] = jnp.where(last, 0.0, nxt)


def _one_minus_exp(y, exp_y):
    p = 1.0 / 120.0
    for c in (1.0 / 24.0, 1.0 / 6.0, 0.5, 1.0):
        p = p * y + c
    return jnp.where(y > -EXPM1_SERIES_BELOW, -y * p, 1.0 - exp_y)


def _sqrt_nonneg(q):
    return jnp.where(q > 0.0, q * lax.rsqrt(q), 0.0)


def _decay_and_input(ext_ref, cw_ref, cb_ref, wg_ref, bg_ref, lam_ref, a_ref, u_ref, tm):
    heads, bw = wg_ref.shape[0], wg_ref.shape[1]
    for hd in range(heads):
        sl = slice(hd * bw, (hd + 1) * bw)
        xc = cb_ref[:, sl]
        for k in range(CONV_W):
            xc = xc + cw_ref[k:k + 1, sl] * ext_ref[SUBLANES * k:SUBLANES * k + tm, sl]
        g = jnp.dot(xc.astype(BF16), wg_ref[hd], preferred_element_type=F32) + bg_ref[hd]
        r = jax.nn.sigmoid(g[:, :bw])
        ig = jax.nn.sigmoid(g[:, bw:])
        log_a = (RG_C * r) * lam_ref[:, sl]
        a = jnp.exp(log_a)
        a_ref[:, sl] = a
        u_ref[:, sl] = _sqrt_nonneg(_one_minus_exp(2.0 * log_a, a * a)) * (ig * xc)


def _scan(a_ref, u_ref, h, tm, reverse):
    nblk = tm // SUBLANES

    def body(s, h):
        j = (nblk - 1 - s) if reverse else s
        rows = pl.ds(pl.multiple_of(j * SUBLANES, SUBLANES), SUBLANES)
        h = a_ref[rows, :] * h + u_ref[rows, :]
        u_ref[rows, :] = h
        return h

    return lax.fori_loop(0, nblk, body, h, unroll=8)


def _halo_specs(tm, d_lru, n_rows):
    per = tm // BF16_ROWS
    last_blk = n_rows // BF16_ROWS - 1
    return [pl.BlockSpec((BF16_ROWS, d_lru), lambda i, tr, fl: (jnp.maximum(tr[i] * per - 1, 0), 0)),
            pl.BlockSpec((tm, d_lru), lambda i, tr, fl: (tr[i], 0)),
            pl.BlockSpec((BF16_ROWS, d_lru), lambda i, tr, fl: (jnp.minimum((tr[i] + 1) * per, last_blk), 0))]


def _fwd_kernel(tr_ref, fl_ref, prev_ref, main_ref, next_ref, cw_ref, cb_ref, wg_ref, bg_ref, lam_ref,
                h0_ref, hf_ref, hlast_ref, ext_ref, a_ref, u_ref, h_ref):
    i = pl.program_id(0)
    tm = main_ref.shape[0]
    first = fl_ref[0, i] == 1
    last = fl_ref[1, i] == 1
    _fill_ext(ext_ref, prev_ref, main_ref, next_ref, first, last)
    _decay_and_input(ext_ref, cw_ref, cb_ref, wg_ref, bg_ref, lam_ref, a_ref, u_ref, tm)

    @pl.when(first)
    def _():
        h_ref[...] = h0_ref[0]

    h = _scan(a_ref, u_ref, h_ref[...], tm, reverse=False)
    h_ref[...] = h
    hlast_ref[0] = h
    hf_ref[...] = u_ref[...].astype(BF16)


def _forward_scan(xa, meta, conv_w, conv_b, wg, bg, lam, h0, tm):
    t, d_lru = xa.shape
    tile_row, flags = meta
    ngrp = h0.shape[0]
    const = dict(pipeline_mode=pl.Buffered(1))
    grid_spec = pltpu.PrefetchScalarGridSpec(
        num_scalar_prefetch=2,
        grid=(t // tm,),
        in_specs=_halo_specs(tm, d_lru, t) + [
            pl.BlockSpec(conv_w.shape, lambda i, tr, fl: (0, 0)),
            pl.BlockSpec((1, d_lru), lambda i, tr, fl: (0, 0)),
            pl.BlockSpec(wg.shape, lambda i, tr, fl: (0, 0, 0), **const),
            pl.BlockSpec(bg.shape, lambda i, tr, fl: (0, 0, 0)),
            pl.BlockSpec((1, d_lru), lambda i, tr, fl: (0, 0)),
            pl.BlockSpec((1, SUBLANES, d_lru), lambda i, tr, fl: (fl[2, i], 0, 0))],
        out_specs=[pl.BlockSpec((tm, d_lru), lambda i, tr, fl: (tr[i], 0)),
                   pl.BlockSpec((1, SUBLANES, d_lru), lambda i, tr, fl: (fl[2, i], 0, 0))],
        scratch_shapes=[pltpu.VMEM((tm + 2 * BF16_ROWS, d_lru), F32),
                        pltpu.VMEM((tm, d_lru), F32),
                        pltpu.VMEM((tm, d_lru), F32),
                        pltpu.VMEM((SUBLANES, d_lru), F32)])
    return pl.pallas_call(
        _fwd_kernel,
        grid_spec=grid_spec,
        out_shape=[jax.ShapeDtypeStruct((t, d_lru), BF16),
                   jax.ShapeDtypeStruct((ngrp, SUBLANES, d_lru), F32)],
        compiler_params=_cparams(("arbitrary",)),
        name="forward_scan",
    )(tile_row, flags, xa, xa, xa, conv_w, conv_b.reshape(1, d_lru), wg, bg, lam.reshape(1, d_lru), h0)


def _bwd_kernel(tr_ref, fl_ref, prev_ref, main_ref, next_ref, cw_ref, cb_ref, wg_ref, bg_ref, lam_ref,
                h0_ref, hf_ref, ga_ref, ybl_ref, ybc_ref, lat_hbm, ctx_hbm, g1_ref, sh2_ref, sc2_ref, wout_ref,
                n2_ref, rwt_ref, rb_ref,
                x1_ref, h2_ref, lg_ref, hlast_ref,
                ext_ref, a_ref, u_ref, h_ref, cat_ref, xbuf, xsem, pack_ref, *, geom):
    i = pl.program_id(0)
    n_steps = pl.num_programs(0)
    tm, d_lru = main_ref.shape
    x = _fetch_rows(lat_hbm, ctx_hbm, xbuf, xsem, i, n_steps,
                    lambda s: tr_ref[jnp.minimum(s, n_steps - 1)], geom)
    first = fl_ref[0, i] == 1
    last = fl_ref[1, i] == 1
    _fill_ext(ext_ref, prev_ref, main_ref, next_ref, first, last)
    _decay_and_input(ext_ref, cw_ref, cb_ref, wg_ref, bg_ref, lam_ref, a_ref, u_ref, tm)

    @pl.when(last)
    def _():
        h_ref[...] = h0_ref[0]

    h = _scan(a_ref, u_ref, h_ref[...], tm, reverse=True)
    h_ref[...] = h
    hlast_ref[0] = h

    ga = ga_ref[...].astype(F32)
    ya = (hf_ref[...].astype(F32) + u_ref[...]) * jax.nn.gelu(ga)
    cat_ref[:, :d_lru] = ya.astype(BF16)
    cat_ref[:, d_lru:] = jnp.where(tr_ref[i] < geom.n_lat_tiles, ybl_ref[...], ybc_ref[...])
    mix = jnp.dot(cat_ref[...], wout_ref[...], preferred_element_type=F32)
    x1 = x + _gate(mix, g1_ref[0])
    x1_ref[...] = x1
    h2 = _modulate(_rms(x1, n2_ref[...]), sc2_ref[0], sh2_ref[0])
    half = h2.shape[1] // 2
    _store_packed_tokens(h2_ref, h2[:, :half], h2[:, half:], pack_ref)
    lg_ref[...] = lax.dot_general(rwt_ref[...], h2.astype(BF16), (((1,), (1,)), ((), ())),
                                  preferred_element_type=F32) + rb_ref[...]


def _backward_scan_mix(xa, ga, hf, yb_lat, yb_ctx, x_lat, x_ctx, modtab, meta, conv_w, conv_b, wg, bg, lam, h0, w_out,
                       norm2_g, rwt, rb, geom):
    t, d_lru = xa.shape
    d = x_lat.shape[-1]
    d_pool = yb_lat.shape[1]
    n_lat, n_ctx = yb_lat.shape[0] // geom.tm, yb_ctx.shape[0] // geom.tm
    tm = geom.tm
    tile_row, flags = meta
    ngrp = h0.shape[0]
    nr = rwt.shape[0]
    const = dict(pipeline_mode=pl.Buffered(1))
    mset = lambda i, tr, fl: jnp.where(tr[i] < geom.n_lat_tiles, 0, 1)
    row = lambda i, tr, fl: (tr[i], 0)
    mod = lambda col: pl.BlockSpec((1, SUBLANES, d), lambda i, tr, fl: (mset(i, tr, fl), 0, col))
    grid_spec = pltpu.PrefetchScalarGridSpec(
        num_scalar_prefetch=2,
        grid=(t // tm,),
        in_specs=_halo_specs(tm, d_lru, t) + [
            pl.BlockSpec(conv_w.shape, lambda i, tr, fl: (0, 0)),
            pl.BlockSpec((1, d_lru), lambda i, tr, fl: (0, 0)),
            pl.BlockSpec(wg.shape, lambda i, tr, fl: (0, 0, 0), **const),
            pl.BlockSpec(bg.shape, lambda i, tr, fl: (0, 0, 0)),
            pl.BlockSpec((1, d_lru), lambda i, tr, fl: (0, 0)),
            pl.BlockSpec((1, SUBLANES, d_lru), lambda i, tr, fl: (fl[2, i], 0, 0)),
            pl.BlockSpec((tm, d_lru), row),
            pl.BlockSpec((tm, d_lru), row),
            pl.BlockSpec((tm, d_pool), lambda i, tr, fl: (jnp.minimum(tr[i], n_lat - 1), 0)),
            pl.BlockSpec((tm, d_pool), lambda i, tr, fl: (jnp.clip(tr[i] - n_lat, 0, n_ctx - 1), 0)),
            pl.BlockSpec(memory_space=pl.ANY),
            pl.BlockSpec(memory_space=pl.ANY),
            mod(2), mod(3), mod(4),
            pl.BlockSpec(w_out.shape, lambda i, tr, fl: (0, 0), **const),
            pl.BlockSpec((1, d), lambda i, tr, fl: (0, 0)),
            pl.BlockSpec(rwt.shape, lambda i, tr, fl: (0, 0)),
            pl.BlockSpec((nr, 1), lambda i, tr, fl: (0, 0))],
        out_specs=[pl.BlockSpec((tm, d), row),
                   pl.BlockSpec((tm * (d // 2) // LANES, LANES), row),
                   pl.BlockSpec((nr, tm), lambda i, tr, fl: (0, tr[i])),
                   pl.BlockSpec((1, SUBLANES, d_lru), lambda i, tr, fl: (fl[2, i], 0, 0))],
        scratch_shapes=[pltpu.VMEM((tm + 2 * BF16_ROWS, d_lru), F32),
                        pltpu.VMEM((tm, d_lru), F32),
                        pltpu.VMEM((tm, d_lru), F32),
                        pltpu.VMEM((SUBLANES, d_lru), F32),
                        pltpu.VMEM((tm, d_lru + d_pool), BF16),
                        pltpu.VMEM((2, tm // SUBLANES, SUBLANES, d), F32),
                        pltpu.SemaphoreType.DMA((2,)),
                        pltpu.VMEM((d // 2 // LANES, 2 * tm, LANES), F32)])
    return pl.pallas_call(
        functools.partial(_bwd_kernel, geom=geom),
        grid_spec=grid_spec,
        out_shape=[jax.ShapeDtypeStruct((t, d), F32),
                   jax.ShapeDtypeStruct((t * (d // 2) // LANES, LANES), U32),
                   jax.ShapeDtypeStruct((nr, t), F32),
                   jax.ShapeDtypeStruct((ngrp, SUBLANES, d_lru), F32)],
        compiler_params=_cparams(("arbitrary",)),
        name="backward_scan_mix",
    )(tile_row, flags, xa, xa, xa, conv_w, conv_b.reshape(1, d_lru), wg, bg, lam.reshape(1, d_lru), h0,
      hf, ga, yb_lat, yb_ctx, x_lat, x_ctx, modtab, modtab, modtab, w_out, norm2_g.reshape(1, d), rwt, rb)


def _shift_rows(v, k):
    if k == 0:
        return v
    z = jnp.zeros((abs(k) * SUBLANES, v.shape[1]), v.dtype)
    if k > 0:
        return jnp.concatenate([z, v[:-k * SUBLANES]], axis=0)
    return jnp.concatenate([v[-k * SUBLANES:], z], axis=0)


def _run_sum(v, m, direction):
    if m & (m - 1) == 0:
        k = 1
        while k < m:
            v = v + _shift_rows(v, -direction * k)
            k *= 2
        return v
    out = v
    for j in range(1, m):
        out = out + _shift_rows(v, -direction * j)
    return out


def _box_sum(v, w):
    lo = w // 2
    hi = w - 1 - lo
    s = _run_sum(v, hi + 1, +1)
    if lo:
        s = s + _shift_rows(_run_sum(v, lo, -1), 1)
    return s


def _window_count(n_rows, lanes, n_pos, w):
    lo = w // 2
    hi = w - 1 - lo
    p = lax.shift_right_logical(lax.broadcasted_iota(I32, (n_rows, lanes), 0), 3)
    return (jnp.minimum(p + hi + 1, n_pos) - jnp.maximum(p - lo, 0)).astype(F32)


def _pool_kernel(z_ref, ps_ref, o_ref, v_ref, *, tiles_per_group, grid_rows, grid_cols):
    group = pl.program_id(1) // tiles_per_group
    lanes = z_ref.shape[1]
    ps = ps_ref[...]

    def pool_1d(w):
        z = z_ref[...].astype(F32)
        n = z.shape[0] // SUBLANES
        mean = _box_sum(z, w) / _window_count(z.shape[0], lanes, n, w)
        o_ref[...] = ((mean - z) * ps).astype(o_ref.dtype)

    def pool_2d(w):
        lo = w // 2
        hi = w - 1 - lo
        blk = grid_cols * SUBLANES
        cw = _window_count(blk, lanes, grid_cols, w)

        def zrow(r):
            return z_ref[pl.ds(pl.multiple_of(r * blk, blk), blk), :].astype(F32)

        v = jnp.zeros((blk, lanes), F32)
        for r in range(hi):
            v = v + zrow(r)
        v_ref[...] = v

        def body(r, carry):
            add = r + hi
            sub = r - lo - 1
            v = v_ref[...]
            v = v + jnp.where(add < grid_rows, zrow(jnp.minimum(add, grid_rows - 1)), 0.0)
            v = v - jnp.where(sub >= 0, zrow(jnp.maximum(sub, 0)), 0.0)
            v_ref[...] = v
            ch = (jnp.minimum(r + hi + 1, grid_rows) - jnp.maximum(r - lo, 0)).astype(F32)
            mean = _box_sum(v, w) / (ch * cw)
            o_ref[pl.ds(pl.multiple_of(r * blk, blk), blk), :] = ((mean - zrow(r)) * ps).astype(o_ref.dtype)
            return carry

        lax.fori_loop(0, grid_rows, body, 0)

    for g, w in enumerate(POOL_WINDOWS):
        @pl.when(group == g)
        def _(w=w):
            if grid_rows is None:
                pool_1d(w)
            else:
                pool_2d(w)


def _pool(z, pool_scale, rows_per_group, n_groups, first_block, grid_rows):
    d_pool = z.shape[1]
    gw = d_pool // len(POOL_WINDOWS)
    lanes = LANES
    blk = GRID_W * SUBLANES
    return pl.pallas_call(
        functools.partial(_pool_kernel, tiles_per_group=gw // lanes, grid_rows=grid_rows, grid_cols=GRID_W),
        grid=(n_groups, d_pool // lanes),
        in_specs=[pl.BlockSpec((rows_per_group, lanes), lambda g, j: (first_block + g, j)),
                  pl.BlockSpec((1, lanes), lambda g, j: (0, j))],
        out_specs=pl.BlockSpec((rows_per_group, lanes), lambda g, j: (g, j)),
        out_shape=jax.ShapeDtypeStruct((n_groups * rows_per_group, d_pool), BF16),
        scratch_shapes=[pltpu.VMEM((blk, lanes), F32)],
        compiler_params=_cparams(("arbitrary", "arbitrary")),
        name="pool_grid" if grid_rows is not None else "pool_seq",
    )(z, pool_scale.reshape(1, d_pool))


def _route_kernel(lg_ref, oi_ref, of_ref, cnt_ref, bc_ref, tri_ref, carry_ref, *, n_groups, per_group):
    step = pl.program_id(0)
    tt = lg_ref.shape[1]
    n_exp = n_groups * per_group

    @pl.when(step == 0)
    def _():
        r = lax.broadcasted_iota(I32, tri_ref.shape, 0)
        c = lax.broadcasted_iota(I32, tri_ref.shape, 1)
        tri_ref[...] = (r <= c).astype(BF16)
        carry_ref[...] = jnp.zeros_like(carry_ref)

    row = lambda k: lg_ref[k:k + 1, :]
    cmax = row(0)
    gi = jnp.zeros((1, tt), I32)
    for g in range(1, n_groups):
        better = row(g) > cmax
        gi = jnp.where(better, g, gi)
        cmax = jnp.where(better, row(g), cmax)
    denom = jnp.zeros((1, tt), F32)
    for g in range(n_groups):
        denom = denom + jnp.exp(row(g) - cmax)
    pg = 1.0 / denom
    fine = []
    for j in range(per_group):
        f = row(n_groups + j)
        for g in range(1, n_groups):
            f = jnp.where(gi == g, row(n_groups + g * per_group + j), f)
        fine.append(f)
    v1 = fine[0]
    i1 = jnp.zeros((1, tt), I32)
    for j in range(1, per_group):
        better = fine[j] > v1
        i1 = jnp.where(better, j, i1)
        v1 = jnp.where(better, fine[j], v1)
    v2 = jnp.full((1, tt), -jnp.inf, F32)
    i2 = jnp.zeros((1, tt), I32)
    for j in range(per_group):
        better = jnp.logical_and(i1 != j, fine[j] > v2)
        i2 = jnp.where(better, j, i2)
        v2 = jnp.where(better, fine[j], v2)
    ex = jnp.exp(v2 - v1)
    w1 = (1.0 / (1.0 + ex)) * pg
    w2 = (ex / (1.0 + ex)) * pg
    e1 = gi * per_group + i1
    e2 = gi * per_group + i2
    eid = lax.broadcasted_iota(I32, (n_exp, tt), 0)
    hit1 = eid == e1
    hit2 = eid == e2
    member = jnp.logical_or(hit1, hit2).astype(BF16)
    cb = tri_ref.shape[0]
    carry = carry_ref[...]
    cums = []
    lane = lax.broadcasted_iota(I32, bc_ref.shape, 1)
    block_ends = jnp.zeros(bc_ref.shape, F32)
    for s in range(tt // cb):
        c = jnp.dot(member[:, s * cb:(s + 1) * cb], tri_ref[...], preferred_element_type=F32) + carry
        carry = c[:, cb - 1:cb]
        cums.append(c)
        block_ends = jnp.where(lane == s, carry, block_ends)
    bc_ref[...] = block_ends.astype(I32)
    carry_ref[...] = carry
    cum = jnp.concatenate(cums, axis=1)
    rank1 = jnp.sum(jnp.where(hit1, cum, 0.0), axis=0, keepdims=True) - 1.0
    rank2 = jnp.sum(jnp.where(hit2, cum, 0.0), axis=0, keepdims=True) - 1.0
    zi = jnp.zeros((SUBLANES - 4, tt), I32)
    oi_ref[...] = jnp.concatenate([e1, e2, rank1.astype(I32), rank2.astype(I32), zi], axis=0)
    of_ref[...] = jnp.concatenate([w1, w2, jnp.zeros((of_ref.shape[0] - 2, tt), F32)], axis=0)
    cnt_ref[...] = jnp.broadcast_to(carry, cnt_ref.shape).astype(I32)


def _route(logits_t, n_groups, per_group, cb):
    nr, t = logits_t.shape
    tt = max(k for k in range(cb, min(ROUTE_TILE, t) + 1, cb) if t % k == 0)
    n_exp = n_groups * per_group
    return pl.pallas_call(
        functools.partial(_route_kernel, n_groups=n_groups, per_group=per_group),
        grid=(t // tt,),
        in_specs=[pl.BlockSpec((nr, tt), lambda i: (0, i))],
        out_specs=[pl.BlockSpec((SUBLANES, tt), lambda i: (0, i)),
                   pl.BlockSpec((LANES, tt), lambda i: (0, i)),
                   pl.BlockSpec((n_exp, LANES), lambda i: (0, 0)),
                   pl.BlockSpec((n_exp, LANES), lambda i: (0, i))],
        out_shape=[jax.ShapeDtypeStruct((SUBLANES, t), I32),
                   jax.ShapeDtypeStruct((LANES, t), F32),
                   jax.ShapeDtypeStruct((n_exp, LANES), I32),
                   jax.ShapeDtypeStruct((n_exp, LANES * (t // tt)), I32)],
        scratch_shapes=[pltpu.VMEM((cb, cb), BF16), pltpu.VMEM((n_exp, 1), F32)],
        compiler_params=_cparams(("arbitrary",)),
        name="route",
    )(logits_t)


def _tiles_for(cnt, tmx):
    return jnp.right_shift(cnt + (SEG_CHUNK + tmx - 1), tmx.bit_length() - 1)


def _plan_kernel(ri_ref, cnt_ref, bc_ref, bcp_ref, lp_ref, dst_ref, src_ref, len_ref, te_ref, nv_ref,
                 off_ref, *, tmx, tb):
    step = pl.program_id(0)
    n_exp = cnt_ref.shape[0]
    tt = ri_ref.shape[1]
    shift = tmx.bit_length() - 1
    n_tile = _tiles_for(cnt_ref[:, 0:1], tmx)

    def exclusive_prefix(col):
        run = jnp.zeros((1, 1), I32)
        parts = []
        for e in range(n_exp):
            parts.append(run)
            run = run + col[e:e + 1, :]
        return jnp.concatenate(parts, axis=0), run

    off, total = exclusive_prefix(jnp.left_shift(n_tile, shift))
    before = jnp.where(step == 0, 0, bcp_ref[:, tt // tb - 1:tt // tb])
    lane = lax.broadcasted_iota(I32, dst_ref.shape, 1)
    eid = lax.broadcasted_iota(I32, (n_exp, tb), 0)
    dst = jnp.zeros(dst_ref.shape, I32)
    src = jnp.zeros(dst_ref.shape, I32)
    length = jnp.zeros(dst_ref.shape, I32)
    rows = []
    for k in range(tt // tb):
        end = bc_ref[:, k:k + 1]
        n = end - before
        aligned = jnp.left_shift(jnp.right_shift(n + (SEG_CHUNK - 1), SEG_SHIFT), SEG_SHIFT)
        start, _ = exclusive_prefix(aligned)
        base = start - before
        sl = slice(k * tb, (k + 1) * tb)
        place = lambda e_row, r_row: jnp.sum(jnp.where(eid == e_row, base, 0), axis=0, keepdims=True) + r_row
        rows.append(jnp.concatenate([place(ri_ref[0:1, sl], ri_ref[2:3, sl]),
                                     place(ri_ref[1:2, sl], ri_ref[3:4, sl])], axis=0))
        dst = jnp.where(lane == k, off + before, dst)
        src = jnp.where(lane == k, start, src)
        length = jnp.where(lane == k, n, length)
        before = end
    lp_ref[...] = jnp.concatenate([jnp.concatenate(rows, axis=1),
                                   jnp.zeros((SUBLANES - TOP_K_FINE, tt), I32)], axis=0)
    dst_ref[...] = dst
    src_ref[...] = src
    len_ref[...] = length
    end_tile = jnp.right_shift(off, shift) + n_tile
    k = lax.broadcasted_iota(I32, (n_exp, te_ref.shape[1]), 1)
    te = jnp.sum((k >= end_tile).astype(I32), axis=0, keepdims=True)
    te_ref[...] = jnp.minimum(te, n_exp - 1)
    nv_ref[...] = jnp.broadcast_to(jnp.right_shift(total, shift), nv_ref.shape)
    off_ref[...] = jnp.broadcast_to(off, off_ref.shape)


def _plan(ri, counts, block_counts, tmx, n_tiles, tb):
    _, t = ri.shape
    n_exp = counts.shape[0]
    steps = block_counts.shape[1] // LANES
    tt = t // steps
    ntp = -(-n_tiles // LANES) * LANES
    per_step = lambda: pl.BlockSpec((n_exp, LANES), lambda i: (0, i))
    whole = lambda w: pl.BlockSpec((n_exp, w), lambda i: (0, 0))
    tab = jax.ShapeDtypeStruct((n_exp, LANES * steps), I32)
    lp, dst, src, length, te, nv, off = pl.pallas_call(
        functools.partial(_plan_kernel, tmx=tmx, tb=tb),
        grid=(steps,),
        in_specs=[pl.BlockSpec((SUBLANES, tt), lambda i: (0, i)),
                  whole(LANES),
                  per_step(),
                  pl.BlockSpec((n_exp, LANES), lambda i: (0, jnp.maximum(i - 1, 0)))],
        out_specs=[pl.BlockSpec((SUBLANES, tt), lambda i: (0, i)),
                   per_step(), per_step(), per_step(),
                   pl.BlockSpec((1, ntp), lambda i: (0, 0)),
                   pl.BlockSpec((1, LANES), lambda i: (0, 0)),
                   whole(LANES)],
        out_shape=[jax.ShapeDtypeStruct((SUBLANES, t), I32), tab, tab, tab,
                   jax.ShapeDtypeStruct((1, ntp), I32),
                   jax.ShapeDtypeStruct((1, LANES), I32),
                   jax.ShapeDtypeStruct((n_exp, LANES), I32)],
        compiler_params=_cparams(("arbitrary",)),
        name="plan",
    )(ri, counts, block_counts, block_counts)
    flat = lambda a: a.reshape(n_exp, steps, LANES)[:, :, :tt // tb].transpose(1, 2, 0).reshape(-1)
    tables = (flat(dst), flat(src), flat(length))
    return (lp[0].reshape(t // tb, tb), lp[1].reshape(t // tb, tb), tables,
            te[0, :n_tiles], nv[0, :1], off[:, 0])


def _load_positions(pos1_hbm, pos2_hbm, p1_ref, p2_ref, isem, step, n_steps):
    tb = p1_ref.shape[0] // 2
    slot = lax.rem(step, 2)

    def copies(blk, half):
        rows = pl.ds(pl.multiple_of(half * tb, tb), tb)
        return (pltpu.make_async_copy(pos1_hbm.at[blk], p1_ref.at[rows], isem.at[0, half]),
                pltpu.make_async_copy(pos2_hbm.at[blk], p2_ref.at[rows], isem.at[1, half]))

    @pl.when(step == 0)
    def _():
        for cp in copies(step, slot):
            cp.start()

    @pl.when(step + 1 < n_steps)
    def _():
        for cp in copies(step + 1, 1 - slot):
            cp.start()

    for cp in copies(step, slot):
        cp.wait()
    return slot * tb


def _token_rows(token, count, nc):
    return pl.ds(pl.multiple_of(token * nc, nc), count * nc)


def _segment_copy(stage_ref, far_hbm, near_tok, far_tok, sem, to_far, nc):
    near = stage_ref.at[_token_rows(near_tok, SEG_CHUNK, nc), :]
    far = far_hbm.at[_token_rows(far_tok, SEG_CHUNK, nc), :]
    return pltpu.make_async_copy(near, far, sem) if to_far else pltpu.make_async_copy(far, near, sem)


def _start_segments(tables, blk, n_exp, stage_ref, far_hbm, sem, to_far, nc, first=True, rest=True):
    dst_ref, src_ref, len_ref = tables

    total = jnp.int32(0)
    for e in range(n_exp):
        k = blk * n_exp + e
        n_chunks = jnp.right_shift(len_ref[k] + (SEG_CHUNK - 1), SEG_SHIFT)
        near0 = src_ref[k]
        far0 = dst_ref[k]
        lo = jnp.int32(0 if first else 1)
        hi = n_chunks if rest else jnp.minimum(n_chunks, 1)

        def one(c, carry, near0=near0, far0=far0, queue=e % 2):
            _segment_copy(stage_ref, far_hbm, near0 + c * SEG_CHUNK, far0 + c * SEG_CHUNK, sem,
                          to_far, nc).start(priority=queue)
            return carry
        lax.fori_loop(lo, hi, one, 0)
        total = total + jnp.maximum(hi - lo, 0)
    return total


def _wait_segments(stage_ref, far_hbm, sem, n, to_far, nc):
    copy = _segment_copy(stage_ref, far_hbm, 0, 0, sem, to_far, nc)

    def body(_, c):
        copy.wait()
        return c
    lax.fori_loop(0, n, body, 0)


def _dispatch_kernel(dst_ref, src_ref, len_ref, off_ref, cnt_ref, lp1_hbm, lp2_hbm, h2_ref, xs_hbm,
                     p1_ref, p2_ref, stage, nd_ref, zero_ref, isem, ssem, zsem, *, tmx):
    step = pl.program_id(0)
    n_steps = pl.num_programs(0)
    slot = lax.rem(step, 2)
    n_exp = off_ref.shape[0]
    nc = zero_ref.shape[0]
    n_blk = h2_ref.shape[0] // nc // SUBLANES
    tables = (dst_ref, src_ref, len_ref)

    @pl.when(step == 0)
    def _():
        for s in range(stage.shape[0]):
            stage[s] = _packed_zeros(stage.shape[1])

    p0 = _load_positions(lp1_hbm, lp2_hbm, p1_ref, p2_ref, isem, step, n_steps)

    def place(blk, c):
        j0 = blk * SUBLANES
        for s in range(SUBLANES):
            row = h2_ref[_token_rows(j0 + s, 1, nc), :]
            stage[slot, _token_rows(p1_ref[p0 + j0 + s], 1, nc), :] = row
            stage[slot, _token_rows(p2_ref[p0 + j0 + s], 1, nc), :] = row
        return c
    lax.fori_loop(0, n_blk, place, 0, unroll=2)

    send = functools.partial(_start_segments, tables, step, n_exp, stage.at[slot], xs_hbm, ssem.at[slot],
                             True, nc)
    n_rest = send(first=False)

    @pl.when(step >= 1)
    def _():
        _wait_segments(stage.at[1 - slot], xs_hbm, ssem.at[1 - slot], nd_ref[1 - slot], True, nc)

    nd_ref[slot] = n_rest + send(rest=False)

    @pl.when(step == n_steps - 1)
    def _():
        _wait_segments(stage.at[slot], xs_hbm, ssem.at[slot], nd_ref[slot], True, nc)
        zero_ref[...] = _packed_zeros(nc)
        shift = tmx.bit_length() - 1

        def pad_expert(e, c):
            cnt = cnt_ref[e]
            n_pad = jnp.left_shift(_tiles_for(cnt, tmx), shift) - cnt
            first = off_ref[e] + cnt
            zero_copy = lambda r: pltpu.make_async_copy(
                zero_ref, xs_hbm.at[_token_rows(first + r, 1, nc), :], zsem)

            def fill(r, c2):
                zero_copy(r).start()
                return c2
            lax.fori_loop(0, n_pad, fill, 0)

            def done(r, c2):
                zero_copy(r).wait()
                return c2
            lax.fori_loop(0, n_pad, done, 0)
            return c
        lax.fori_loop(0, off_ref.shape[0], pad_expert, 0)


def _stage_rows(tb, n_exp):
    return TOP_K_FINE * tb + n_exp * SEG_CHUNK


def _dispatch(tables, offsets, counts, lp1, lp2, h2p, p_max, tmx):
    nb, tb = lp1.shape
    nc = h2p.shape[0] // (nb * tb)
    n_exp = offsets.shape[0]
    hbm = lambda: pl.BlockSpec(memory_space=pl.ANY)
    grid_spec = pltpu.PrefetchScalarGridSpec(
        num_scalar_prefetch=5,
        grid=(nb,),
        in_specs=[hbm(), hbm(),
                  pl.BlockSpec((tb * nc, LANES), lambda i, *_: (i, 0))],
        out_specs=hbm(),
        scratch_shapes=[pltpu.SMEM((2 * tb,), I32),
                        pltpu.SMEM((2 * tb,), I32),
                        pltpu.VMEM((2, _stage_rows(tb, n_exp) * nc, LANES), U32),
                        pltpu.SMEM((2,), I32),
                        pltpu.VMEM((nc, LANES), U32),
                        pltpu.SemaphoreType.DMA((2, 2)),
                        pltpu.SemaphoreType.DMA((2,)),
                        pltpu.SemaphoreType.DMA(())])
    return pl.pallas_call(
        functools.partial(_dispatch_kernel, tmx=tmx),
        grid_spec=grid_spec,
        out_shape=jax.ShapeDtypeStruct((p_max * nc, LANES), U32),
        compiler_params=_cparams(("arbitrary",)),
        name="dispatch",
    )(*tables, offsets, counts, lp1, lp2, h2p)


def _expert_kernel(te_ref, nv_ref, x_ref, w1_ref, w3_ref, w2_ref, y_ref, pack_ref, *, tmx):
    @pl.when(pl.program_id(0) < nv_ref[0])
    def _():
        lo, hi = _load_packed_tokens(x_ref, tmx, pack_ref)
        lo = lo.astype(BF16)
        hi = hi.astype(BF16)
        half = lo.shape[1]
        mm = lambda w_ref: (jnp.dot(lo, w_ref[0, :half, :].astype(BF16), preferred_element_type=F32)
                            + jnp.dot(hi, w_ref[0, half:, :].astype(BF16), preferred_element_type=F32))
        h1 = mm(w1_ref)
        h3 = mm(w3_ref)
        hh = (h1 * jax.nn.sigmoid(h1) * h3).astype(BF16)
        y = jnp.dot(hh, w2_ref[0].astype(BF16), preferred_element_type=F32)
        _store_packed_tokens(y_ref, y[:, :half], y[:, half:], pack_ref)


def _expert_mlp(xs, tile_expert, n_valid, w1, w3, w2, tmx):
    d, de = w1.shape[1], w1.shape[2]
    nc = d // 2 // LANES
    p = xs.shape[0] // nc
    tile = lambda i, te, nv: (jnp.minimum(i, nv[0] - 1), 0)
    grid_spec = pltpu.PrefetchScalarGridSpec(
        num_scalar_prefetch=2,
        grid=(p // tmx,),
        in_specs=[pl.BlockSpec((tmx * nc, LANES), tile),
                  pl.BlockSpec((1, d, de), lambda i, te, nv: (te[i], 0, 0)),
                  pl.BlockSpec((1, d, de), lambda i, te, nv: (te[i], 0, 0)),
                  pl.BlockSpec((1, de, d), lambda i, te, nv: (te[i], 0, 0))],
        out_specs=pl.BlockSpec((tmx * nc, LANES), tile),
        scratch_shapes=[pltpu.VMEM((nc, 2 * tmx, LANES), F32)])
    return pl.pallas_call(
        functools.partial(_expert_kernel, tmx=tmx),
        grid_spec=grid_spec,
        out_shape=jax.ShapeDtypeStruct((p * nc, LANES), U32),
        compiler_params=_cparams(("arbitrary",)),
        name="expert_mlp",
    )(tile_expert, n_valid, xs, w1, w3, w2)


def _combine_kernel(dst_ref, src_ref, len_ref, lp1_hbm, lp2_hbm, ys_hbm, x1_ref, rf_ref, g2_ref, fg_ref,
                    ylat_hbm, yctx_hbm, p1_ref, p2_ref, stage, gbuf, pack_ref, obuf, nd_ref, isem, csem, osem,
                    *, geom, n_exp):
    step = pl.program_id(0)
    n_steps = pl.num_programs(0)
    tm = geom.tm
    n_blk = tm // SUBLANES
    slot = lax.rem(step, 2)
    nc = gbuf.shape[1] // tm
    tables = (dst_ref, src_ref, len_ref)

    def fetch(blk, gs):
        nd_ref[gs] = _start_segments(tables, blk, n_exp, stage.at[gs], ys_hbm, csem.at[gs], False, nc)

    @pl.when(step == 0)
    def _():
        fetch(step, slot)

    @pl.when(step + 1 < n_steps)
    def _():
        fetch(step + 1, 1 - slot)

    _wait_segments(stage.at[slot], ys_hbm, csem.at[slot], nd_ref[slot], False, nc)
    p0 = _load_positions(lp1_hbm, lp2_hbm, p1_ref, p2_ref, isem, step, n_steps)

    def pick(blk, c):
        j0 = blk * SUBLANES
        for s in range(SUBLANES):
            rows = _token_rows(j0 + s, 1, nc)
            gbuf[0, rows, :] = stage[slot, _token_rows(p1_ref[p0 + j0 + s], 1, nc), :]
            gbuf[1, rows, :] = stage[slot, _token_rows(p2_ref[p0 + j0 + s], 1, nc), :]
        return c
    lax.fori_loop(0, n_blk, pick, 0, unroll=2)

    wt = rf_ref[...].T
    a_lo, a_hi = _load_packed_tokens(gbuf.at[0], tm, pack_ref.at[0])
    b_lo, b_hi = _load_packed_tokens(gbuf.at[1], tm, pack_ref.at[1])
    w1 = wt[:, 0:1]
    w2 = wt[:, 1:2]
    moe = jnp.concatenate([w1 * a_lo + w2 * b_lo, w1 * a_hi + w2 * b_hi], axis=1)
    x = x1_ref[...] + _gate(moe, g2_ref[0])
    out = _rms(x, fg_ref[...])

    start = lambda cp: cp.start()
    wait = lambda cp: cp.wait()
    put = functools.partial(_tile_copies, ylat_hbm, yctx_hbm, geom=geom, to_rows=False)

    @pl.when(step >= 2)
    def _():
        put(obuf.at[slot], osem.at[slot], step - 2, fn=wait)

    obuf[slot] = out.reshape(obuf.shape[1:])
    put(obuf.at[slot], osem.at[slot], step, fn=start)

    @pl.when(step == n_steps - 1)
    def _():
        @pl.when(n_steps > 1)
        def _():
            put(obuf.at[1 - slot], osem.at[1 - slot], step - 1, fn=wait)
        put(obuf.at[slot], osem.at[slot], step, fn=wait)


def _combine(tables, lp1, lp2, ys, x1, rf, modtab, final_g, lat_shape, ctx_shape, geom):
    t, d = x1.shape
    tm = geom.tm
    nc = d // 2 // LANES
    n_exp = tables[0].shape[0] // (t // tm)
    mset = lambda i, *_: jnp.where(i < geom.n_lat_tiles, 0, 1)
    hbm = lambda: pl.BlockSpec(memory_space=pl.ANY)
    grid_spec = pltpu.PrefetchScalarGridSpec(
        num_scalar_prefetch=3,
        grid=(t // tm,),
        in_specs=[hbm(), hbm(), hbm(),
                  pl.BlockSpec((tm, d), lambda i, *_: (i, 0)),
                  pl.BlockSpec((LANES, tm), lambda i, *_: (0, i)),
                  pl.BlockSpec((1, SUBLANES, d), lambda i, *_: (mset(i), 0, 5)),
                  pl.BlockSpec((1, d), lambda i, *_: (0, 0))],
        out_specs=[hbm(), hbm()],
        scratch_shapes=[pltpu.SMEM((2 * tm,), I32),
                        pltpu.SMEM((2 * tm,), I32),
                        pltpu.VMEM((2, _stage_rows(tm, n_exp) * nc, LANES), U32),
                        pltpu.VMEM((TOP_K_FINE, tm * nc, LANES), U32),
                        pltpu.VMEM((TOP_K_FINE, nc, 2 * tm, LANES), F32),
                        pltpu.VMEM((2, tm // SUBLANES, SUBLANES, d), F32),
                        pltpu.SMEM((2,), I32),
                        pltpu.SemaphoreType.DMA((2, 2)),
                        pltpu.SemaphoreType.DMA((2,)),
                        pltpu.SemaphoreType.DMA((2,))])
    return pl.pallas_call(
        functools.partial(_combine_kernel, geom=geom, n_exp=n_exp),
        grid_spec=grid_spec,
        out_shape=[jax.ShapeDtypeStruct(lat_shape, F32), jax.ShapeDtypeStruct(ctx_shape, F32)],
        compiler_params=_cparams(("arbitrary",)),
        name="combine",
    )(*tables, lp1, lp2, ys, x1, rf, modtab, final_g.reshape(1, d))


def _tile_meta(groups, tm):
    rows, first, last, grp = [], [], [], []
    blk = 0
    for g, r in enumerate(groups):
        nc = r // tm
        for c in range(nc):
            rows.append(blk + c)
            first.append(int(c == 0))
            last.append(int(c == nc - 1))
            grp.append(g)
        blk += nc
    fwd = (np.array(rows, np.int32), np.array([first, last, grp], np.int32))
    order = []
    blk = 0
    for r in groups:
        nc = r // tm
        order.extend(range(blk + nc - 1, blk - 1, -1))
        blk += nc
    order = np.array(order)
    bwd = (fwd[0][order], fwd[1][:, order])
    return fwd, bwd


def kernel(x_prompt, x_sample, state_lru, c, c_ctx, w_mod, b_mod, norm1_g, w_in, conv_w, conv_b, lru_wa, lru_ba, lru_wx, lru_bx, lru_lambda, pool_w, pool_scale, w_out, norm2_g, router_coarse_w, router_coarse_b, router_fine_w, router_fine_b, exp_w1, exp_w3, exp_w2, final_norm_g):
    bp, sp, d = x_prompt.shape
    bs, ss, _ = x_sample.shape
    d_lru = lru_lambda.shape[-1]
    heads, bw = lru_wa.shape[2], lru_wa.shape[3]
    n_groups, per_group = router_fine_w.shape[2], router_fine_w.shape[3]
    n_exp = n_groups * per_group
    assert w_mod.shape[0] == 1 and bs == SUBLANES and bp % SUBLANES == 0 and ss % GRID_W == 0
    assert EXPERT_TILE & (EXPERT_TILE - 1) == 0
    n_lat_groups, n_ctx_groups = bs // SUBLANES, bp // SUBLANES
    lat_rows, ctx_rows = ss * SUBLANES, sp * SUBLANES
    assert (n_lat_groups * lat_rows) % ctx_rows == 0
    tm = min(TOKEN_TILE, ctx_rows, lat_rows)
    geom = _Geom(tm=tm, n_lat_tiles=n_lat_groups * lat_rows // tm, lat_chunks=lat_rows // tm,
                 ctx_chunks=ctx_rows // tm)
    n_rows = n_lat_groups * lat_rows + n_ctx_groups * ctx_rows
    groups = [lat_rows] * n_lat_groups + [ctx_rows] * n_ctx_groups
    (f_rows, f_flags), (b_rows, b_flags) = _tile_meta(groups, tm)

    cond = jnp.zeros((2 * SUBLANES, d), F32).at[:bs].set(c).at[bs].set(c_ctx)
    mod = _modulation(cond, w_mod[0], b_mod[0])
    modtab = jnp.stack([mod[:SUBLANES], jnp.broadcast_to(mod[SUBLANES], (SUBLANES, mod.shape[1]))])
    h0_lat = state_lru[:, 0].reshape(n_lat_groups, SUBLANES, 2, d_lru)
    h0 = jnp.concatenate([h0_lat, jnp.zeros((n_ctx_groups, SUBLANES, 2, d_lru), F32)], axis=0)
    h0 = h0.transpose(0, 2, 1, 3)

    xa, ga, z = _input_projection(x_sample, x_prompt, modtab, norm1_g[0], w_in[0],
                                  pool_w[0].astype(BF16), n_rows, geom)

    def gate_weights(direction):
        wg = jnp.concatenate([lru_wa[0, direction], lru_wx[0, direction]], axis=-1).astype(BF16)
        bg = jnp.concatenate([lru_ba[0, direction].reshape(heads, 1, bw),
                              lru_bx[0, direction].reshape(heads, 1, bw)], axis=-1)
        return wg, bg

    log_decay = jax.nn.log_sigmoid(lru_lambda[0])
    wg_f, bg_f = gate_weights(0)
    wg_b, bg_b = gate_weights(1)
    hf, hf_last = _forward_scan(xa, (jnp.asarray(f_rows), jnp.asarray(f_flags)), conv_w[0], conv_b[0],
                                wg_f, bg_f, log_decay[0], h0[:, 0], tm)

    yb_lat = _pool(z, pool_scale[0], lat_rows, n_lat_groups, 0, lat_rows // (GRID_W * SUBLANES))
    yb_ctx = _pool(z, pool_scale[0], ctx_rows, n_ctx_groups, n_lat_groups * lat_rows // ctx_rows, None)

    n_logits = n_groups + n_exp
    rw = jnp.concatenate([router_coarse_w[0], router_fine_w[0].reshape(d, n_exp)], axis=1)
    rwt = jnp.zeros((LANES, d), BF16).at[:n_logits].set(rw.T.astype(BF16))
    rb = jnp.zeros((LANES, 1), F32).at[:n_logits, 0].set(
        jnp.concatenate([router_coarse_b[0], router_fine_b[0].reshape(n_exp)]))
    x1, h2p, logits_t, hb_last = _backward_scan_mix(
        xa, ga, hf, yb_lat, yb_ctx, x_sample, x_prompt, modtab, (jnp.asarray(b_rows), jnp.asarray(b_flags)),
        conv_w[0], conv_b[0], wg_b, bg_b, log_decay[1], h0[:, 1], w_out[0].astype(BF16), norm2_g[0],
        rwt, rb, geom)

    ri, rf, counts, block_counts = _route(logits_t, n_groups, per_group, tm)
    n_tiles = -(-(TOP_K_FINE * n_rows + n_exp * (EXPERT_TILE + SEG_CHUNK)) // EXPERT_TILE)
    lp1, lp2, tables, tile_expert, n_valid, offsets = _plan(ri, counts, block_counts, EXPERT_TILE, n_tiles, tm)
    xs = _dispatch(tables, offsets, counts[:, 0], lp1, lp2, h2p, n_tiles * EXPERT_TILE, EXPERT_TILE)
    ys = _expert_mlp(xs, tile_expert, n_valid, exp_w1[0], exp_w3[0], exp_w2[0], EXPERT_TILE)
    y_sample, y_prompt = _combine(tables, lp1, lp2, ys, x1, rf, modtab, final_norm_g, x_sample.shape,
                                  x_prompt.shape, geom)

    st = jnp.stack([hf_last[n_lat_groups:], hb_last[n_lat_groups:]], axis=2)
    state_new = st.reshape(bp, 1, 2, d_lru).astype(x_prompt.dtype)
    return (y_prompt, y_sample, state_new)
```

```python
import functools
from typing import NamedTuple

import numpy as np
import jax
import jax.numpy as jnp
from jax import lax
from jax.experimental import pallas as pl
from jax.experimental.pallas import tpu as pltpu

GRID_W = 64
CONV_W = 4
RG_C = 8.0
POOL_WINDOWS = (2, 4, 8, 16)
TOP_K_FINE = 2
EPS = 1e-6
EXPM1_SERIES_BELOW = 0.125

SUBLANES = 8
LANES = 128
BF16_ROWS = 16
TOKEN_TILE = 512
EXPERT_TILE = 512
ROUTE_TILE = 2048
SEG_SHIFT = 4
SEG_CHUNK = 1 << SEG_SHIFT
MOD_COL_TILE = 1024
WEIGHT_STAGE_ROWS = 256
VMEM_LIMIT = 60 * 1024 * 1024

F32 = jnp.float32
BF16 = jnp.bfloat16
U32 = jnp.uint32
I32 = jnp.int32


class _Geom(NamedTuple):
    tm: int
    n_lat_tiles: int
    lat_chunks: int
    ctx_chunks: int


def _cparams(sem):
    return pltpu.CompilerParams(dimension_semantics=sem, vmem_limit_bytes=VMEM_LIMIT)


def _per_sequence(y, m):
    rows, d = y.shape
    return y.reshape(rows // SUBLANES, SUBLANES, d), m[None]


def _modulate(y, scale, shift):
    y3, sc = _per_sequence(y, scale)
    _, sh = _per_sequence(y, shift)
    return (y3 * (1.0 + sc) + sh).reshape(y.shape)


def _gate(y, g):
    y3, g3 = _per_sequence(y, g)
    return (y3 * g3).reshape(y.shape)


def _rms(x, g):
    ms = jnp.mean(x * x, axis=-1, keepdims=True)
    return x * lax.rsqrt(ms + EPS) * g


def _store_packed_tokens(ref, lo, hi, scratch):
    n, words = lo.shape
    nc = words // LANES
    for c in range(nc):
        sl = slice(c * LANES, (c + 1) * LANES)
        scratch[c, pl.ds(0, n, stride=2), :] = lo[:, sl]
        scratch[c, pl.ds(1, n, stride=2), :] = hi[:, sl]
        ref[pl.ds(c, n, stride=nc), :] = pltpu.bitcast(scratch[c].astype(BF16), U32)


def _packed_zeros(rows):
    return pltpu.bitcast(jnp.zeros((2 * rows, LANES), BF16), U32)


def _load_packed_tokens(ref, n, scratch):
    nc = ref.shape[0] // n
    los, his = [], []
    for c in range(nc):
        scratch[c] = pltpu.bitcast(ref[pl.ds(c, n, stride=nc), :], BF16).astype(F32)
        los.append(scratch[c, pl.ds(0, n, stride=2), :])
        his.append(scratch[c, pl.ds(1, n, stride=2), :])
    return jnp.concatenate(los, axis=1), jnp.concatenate(his, axis=1)


def _tile_copies(lat_hbm, ctx_hbm, buf, sem, tile, geom, to_rows, fn):
    tt = geom.tm // SUBLANES

    def run(hbm, k, chunks):
        g = lax.div(k, jnp.int32(chunks))
        c = lax.rem(k, jnp.int32(chunks))
        for b in range(SUBLANES):
            h = hbm.at[g * SUBLANES + b, pl.ds(c * tt, tt), :]
            v = buf.at[:, b, :]
            fn(pltpu.make_async_copy(h, v, sem) if to_rows else pltpu.make_async_copy(v, h, sem))

    @pl.when(tile < geom.n_lat_tiles)
    def _():
        run(lat_hbm, tile, geom.lat_chunks)

    @pl.when(tile >= geom.n_lat_tiles)
    def _():
        run(ctx_hbm, tile - geom.n_lat_tiles, geom.ctx_chunks)


def _fetch_rows(lat_hbm, ctx_hbm, xbuf, sem, step, n_steps, tile_of, geom):
    slot = lax.rem(step, 2)
    start = lambda cp: cp.start()
    wait = lambda cp: cp.wait()

    @pl.when(step == 0)
    def _():
        _tile_copies(lat_hbm, ctx_hbm, xbuf.at[0], sem.at[0], tile_of(jnp.int32(0)), geom, True, start)

    @pl.when(step + 1 < n_steps)
    def _():
        nxt = 1 - slot
        _tile_copies(lat_hbm, ctx_hbm, xbuf.at[nxt], sem.at[nxt], tile_of(step + 1), geom, True, start)

    _tile_copies(lat_hbm, ctx_hbm, xbuf.at[slot], sem.at[slot], tile_of(step), geom, True, wait)
    tt, _, d = xbuf.shape[1:]
    return xbuf[slot].reshape(tt * SUBLANES, d)


def _mod_kernel(c_ref, w_ref, b_ref, o_ref):
    c = c_ref[...]
    s = c * jax.nn.sigmoid(c)
    o_ref[...] = jnp.dot(s.astype(BF16), w_ref[...].astype(BF16),
                         preferred_element_type=F32) + b_ref[...]


def _modulation(cond, w_mod, b_mod):
    rows, d = cond.shape
    n = w_mod.shape[1]
    tn = min(MOD_COL_TILE, n)
    return pl.pallas_call(
        _mod_kernel,
        grid=(n // tn,),
        in_specs=[pl.BlockSpec((rows, d), lambda j: (0, 0)),
                  pl.BlockSpec((d, tn), lambda j: (0, j)),
                  pl.BlockSpec((1, tn), lambda j: (0, j))],
        out_specs=pl.BlockSpec((rows, tn), lambda j: (0, j)),
        out_shape=jax.ShapeDtypeStruct((rows, n), F32),
        compiler_params=_cparams(("arbitrary",)),
        name="modulation",
    )(cond, w_mod, b_mod.reshape(1, n))


def _round_weights(w_hbm, wbf_ref, stage, sem):
    rows = stage.shape[1]
    n = w_hbm.shape[0] // rows
    copy = lambda r: pltpu.make_async_copy(w_hbm.at[pl.ds(r * rows, rows), :], stage.at[r % 2], sem.at[r % 2])
    copy(0).start()
    for r in range(n):
        if r + 1 < n:
            copy(r + 1).start()
        copy(r).wait()
        wbf_ref[r * rows:(r + 1) * rows, :] = stage[r % 2].astype(wbf_ref.dtype)


def _proj_kernel(lat_hbm, ctx_hbm, sh_ref, sc_ref, g_ref, win_hbm, pw_ref, xa_ref, ga_ref, z_ref,
                 xbuf, sem, wbf_ref, wstage, wsem, *, d_lru, gw, geom):
    step = pl.program_id(0)

    @pl.when(step == 0)
    def _():
        _round_weights(win_hbm, wbf_ref, wstage, wsem)

    x = _fetch_rows(lat_hbm, ctx_hbm, xbuf, sem, step, pl.num_programs(0), lambda s: s, geom)
    h = _modulate(_rms(x, g_ref[...]), sc_ref[0], sh_ref[0])
    proj = jnp.dot(h.astype(BF16), wbf_ref[...], preferred_element_type=F32)
    xa_ref[...] = proj[:, :d_lru].astype(BF16)
    ga_ref[...] = proj[:, d_lru:2 * d_lru].astype(BF16)
    for g in range(pw_ref.shape[0]):
        lo = 2 * d_lru + g * gw
        z_ref[:, g * gw:(g + 1) * gw] = jnp.dot(
            proj[:, lo:lo + gw].astype(BF16), pw_ref[g], preferred_element_type=F32).astype(BF16)


def _input_projection(x_lat, x_ctx, modtab, norm_g, w_in, pool_w, n_rows, geom):
    d = x_lat.shape[-1]
    tm = geom.tm
    d_pool = pool_w.shape[0] * pool_w.shape[1]
    d_lru = (w_in.shape[1] - d_pool) // 2
    mset = lambda i: jnp.where(i < geom.n_lat_tiles, 0, 1)
    const = dict(pipeline_mode=pl.Buffered(1))
    return pl.pallas_call(
        functools.partial(_proj_kernel, d_lru=d_lru, gw=pool_w.shape[1], geom=geom),
        grid=(n_rows // tm,),
        in_specs=[pl.BlockSpec(memory_space=pl.ANY),
                  pl.BlockSpec(memory_space=pl.ANY),
                  pl.BlockSpec((1, SUBLANES, d), lambda i: (mset(i), 0, 0)),
                  pl.BlockSpec((1, SUBLANES, d), lambda i: (mset(i), 0, 1)),
                  pl.BlockSpec((1, d), lambda i: (0, 0)),
                  pl.BlockSpec(memory_space=pl.ANY),
                  pl.BlockSpec(pool_w.shape, lambda i: (0, 0, 0), **const)],
        out_specs=[pl.BlockSpec((tm, d_lru), lambda i: (i, 0)),
                   pl.BlockSpec((tm, d_lru), lambda i: (i, 0)),
                   pl.BlockSpec((tm, d_pool), lambda i: (i, 0))],
        out_shape=[jax.ShapeDtypeStruct((n_rows, d_lru), BF16),
                   jax.ShapeDtypeStruct((n_rows, d_lru), BF16),
                   jax.ShapeDtypeStruct((n_rows, d_pool), BF16)],
        scratch_shapes=[pltpu.VMEM((2, tm // SUBLANES, SUBLANES, d), F32),
                        pltpu.SemaphoreType.DMA((2,)),
                        pltpu.VMEM(w_in.shape, BF16),
                        pltpu.VMEM((2, min(WEIGHT_STAGE_ROWS, d), w_in.shape[1]), F32),
                        pltpu.SemaphoreType.DMA((2,))],
        compiler_params=_cparams(("arbitrary",)),
        name="input_projection",
    )(x_lat, x_ctx, modtab, modtab, norm_g.reshape(1, d), w_in, pool_w)


def _fill_ext(ext_ref, prev_ref, main_ref, next_ref, first, last):
    tm = main_ref.shape[0]
    prev = prev_ref[...].astype(F32)
    nxt = next_ref[...].astype(F32)
    ext_ref[0:BF16_ROWS, :] = jnp.where(first, 0.0, prev)
    ext_ref[BF16_ROWS:BF16_ROWS + tm, :] = main_ref[...].astype(F32)
    ext_ref[BF16_ROWS + tm:, :] = jnp.where(last, 0.0, nxt)


def _one_minus_exp(y, exp_y):
    p = 1.0 / 120.0
    for c in (1.0 / 24.0, 1.0 / 6.0, 0.5, 1.0):
        p = p * y + c
    return jnp.where(y > -EXPM1_SERIES_BELOW, -y * p, 1.0 - exp_y)


def _sqrt_nonneg(q):
    return jnp.where(q > 0.0, q * lax.rsqrt(q), 0.0)


def _decay_and_input(ext_ref, cw_ref, cb_ref, wg_ref, bg_ref, lam_ref, a_ref, u_ref, tm):
    heads, bw = wg_ref.shape[0], wg_ref.shape[1]
    for hd in range(heads):
        sl = slice(hd * bw, (hd + 1) * bw)
        xc = cb_ref[:, sl]
        for k in range(CONV_W):
            xc = xc + cw_ref[k:k + 1, sl] * ext_ref[SUBLANES * k:SUBLANES * k + tm, sl]
        g = jnp.dot(xc.astype(BF16), wg_ref[hd], preferred_element_type=F32) + bg_ref[hd]
        r = jax.nn.sigmoid(g[:, :bw])
        ig = jax.nn.sigmoid(g[:, bw:])
        log_a = (RG_C * r) * lam_ref[:, sl]
        a = jnp.exp(log_a)
        a_ref[:, sl] = a
        u_ref[:, sl] = _sqrt_nonneg(_one_minus_exp(2.0 * log_a, a * a)) * (ig * xc)


def _scan(a_ref, u_ref, h, tm, reverse):
    nblk = tm // SUBLANES

    def body(s, h):
        j = (nblk - 1 - s) if reverse else s
        rows = pl.ds(pl.multiple_of(j * SUBLANES, SUBLANES), SUBLANES)
        h = a_ref[rows, :] * h + u_ref[rows, :]
        u_ref[rows, :] = h
        return h

    return lax.fori_loop(0, nblk, body, h, unroll=8)


def _halo_specs(tm, d_lru, n_rows):
    per = tm // BF16_ROWS
    last_blk = n_rows // BF16_ROWS - 1
    return [pl.BlockSpec((BF16_ROWS, d_lru), lambda i, tr, fl: (jnp.maximum(tr[i] * per - 1, 0), 0)),
            pl.BlockSpec((tm, d_lru), lambda i, tr, fl: (tr[i], 0)),
            pl.BlockSpec((BF16_ROWS, d_lru), lambda i, tr, fl: (jnp.minimum((tr[i] + 1) * per, last_blk), 0))]


def _fwd_kernel(tr_ref, fl_ref, prev_ref, main_ref, next_ref, cw_ref, cb_ref, wg_ref, bg_ref, lam_ref,
                h0_ref, hf_ref, hlast_ref, ext_ref, a_ref, u_ref, h_ref):
    i = pl.program_id(0)
    tm = main_ref.shape[0]
    first = fl_ref[0, i] == 1
    last = fl_ref[1, i] == 1
    _fill_ext(ext_ref, prev_ref, main_ref, next_ref, first, last)
    _decay_and_input(ext_ref, cw_ref, cb_ref, wg_ref, bg_ref, lam_ref, a_ref, u_ref, tm)

    @pl.when(first)
    def _():
        h_ref[...] = h0_ref[0]

    h = _scan(a_ref, u_ref, h_ref[...], tm, reverse=False)
    h_ref[...] = h
    hlast_ref[0] = h
    hf_ref[...] = u_ref[...].astype(BF16)


def _forward_scan(xa, meta, conv_w, conv_b, wg, bg, lam, h0, tm):
    t, d_lru = xa.shape
    tile_row, flags = meta
    ngrp = h0.shape[0]
    const = dict(pipeline_mode=pl.Buffered(1))
    grid_spec = pltpu.PrefetchScalarGridSpec(
        num_scalar_prefetch=2,
        grid=(t // tm,),
        in_specs=_halo_specs(tm, d_lru, t) + [
            pl.BlockSpec(conv_w.shape, lambda i, tr, fl: (0, 0)),
            pl.BlockSpec((1, d_lru), lambda i, tr, fl: (0, 0)),
            pl.BlockSpec(wg.shape, lambda i, tr, fl: (0, 0, 0), **const),
            pl.BlockSpec(bg.shape, lambda i, tr, fl: (0, 0, 0)),
            pl.BlockSpec((1, d_lru), lambda i, tr, fl: (0, 0)),
            pl.BlockSpec((1, SUBLANES, d_lru), lambda i, tr, fl: (fl[2, i], 0, 0))],
        out_specs=[pl.BlockSpec((tm, d_lru), lambda i, tr, fl: (tr[i], 0)),
                   pl.BlockSpec((1, SUBLANES, d_lru), lambda i, tr, fl: (fl[2, i], 0, 0))],
        scratch_shapes=[pltpu.VMEM((tm + 2 * BF16_ROWS, d_lru), F32),
                        pltpu.VMEM((tm, d_lru), F32),
                        pltpu.VMEM((tm, d_lru), F32),
                        pltpu.VMEM((SUBLANES, d_lru), F32)])
    return pl.pallas_call(
        _fwd_kernel,
        grid_spec=grid_spec,
        out_shape=[jax.ShapeDtypeStruct((t, d_lru), BF16),
                   jax.ShapeDtypeStruct((ngrp, SUBLANES, d_lru), F32)],
        compiler_params=_cparams(("arbitrary",)),
        name="forward_scan",
    )(tile_row, flags, xa, xa, xa, conv_w, conv_b.reshape(1, d_lru), wg, bg, lam.reshape(1, d_lru), h0)


def _bwd_kernel(tr_ref, fl_ref, prev_ref, main_ref, next_ref, cw_ref, cb_ref, wg_ref, bg_ref, lam_ref,
                h0_ref, hf_ref, ga_ref, ybl_ref, ybc_ref, lat_hbm, ctx_hbm, g1_ref, sh2_ref, sc2_ref, wout_ref,
                n2_ref, rwt_ref, rb_ref,
                x1_ref, h2_ref, lg_ref, hlast_ref,
                ext_ref, a_ref, u_ref, h_ref, cat_ref, xbuf, xsem, pack_ref, *, geom):
    i = pl.program_id(0)
    n_steps = pl.num_programs(0)
    tm, d_lru = main_ref.shape
    x = _fetch_rows(lat_hbm, ctx_hbm, xbuf, xsem, i, n_steps,
                    lambda s: tr_ref[jnp.minimum(s, n_steps - 1)], geom)
    first = fl_ref[0, i] == 1
    last = fl_ref[1, i] == 1
    _fill_ext(ext_ref, prev_ref, main_ref, next_ref, first, last)
    _decay_and_input(ext_ref, cw_ref, cb_ref, wg_ref, bg_ref, lam_ref, a_ref, u_ref, tm)

    @pl.when(last)
    def _():
        h_ref[...] = h0_ref[0]

    h = _scan(a_ref, u_ref, h_ref[...], tm, reverse=True)
    h_ref[...] = h
    hlast_ref[0] = h

    ga = ga_ref[...].astype(F32)
    ya = (hf_ref[...].astype(F32) + u_ref[...]) * jax.nn.gelu(ga)
    cat_ref[:, :d_lru] = ya.astype(BF16)
    cat_ref[:, d_lru:] = jnp.where(tr_ref[i] < geom.n_lat_tiles, ybl_ref[...], ybc_ref[...])
    mix = jnp.dot(cat_ref[...], wout_ref[...], preferred_element_type=F32)
    x1 = x + _gate(mix, g1_ref[0])
    x1_ref[...] = x1
    h2 = _modulate(_rms(x1, n2_ref[...]), sc2_ref[0], sh2_ref[0])
    half = h2.shape[1] // 2
    _store_packed_tokens(h2_ref, h2[:, :half], h2[:, half:], pack_ref)
    lg_ref[...] = lax.dot_general(rwt_ref[...], h2.astype(BF16), (((1,), (1,)), ((), ())),
                                  preferred_element_type=F32) + rb_ref[...]


def _backward_scan_mix(xa, ga, hf, yb_lat, yb_ctx, x_lat, x_ctx, modtab, meta, conv_w, conv_b, wg, bg, lam, h0, w_out,
                       norm2_g, rwt, rb, geom):
    t, d_lru = xa.shape
    d = x_lat.shape[-1]
    d_pool = yb_lat.shape[1]
    n_lat, n_ctx = yb_lat.shape[0] // geom.tm, yb_ctx.shape[0] // geom.tm
    tm = geom.tm
    tile_row, flags = meta
    ngrp = h0.shape[0]
    nr = rwt.shape[0]
    const = dict(pipeline_mode=pl.Buffered(1))
    mset = lambda i, tr, fl: jnp.where(tr[i] < geom.n_lat_tiles, 0, 1)
    row = lambda i, tr, fl: (tr[i], 0)
    mod = lambda col: pl.BlockSpec((1, SUBLANES, d), lambda i, tr, fl: (mset(i, tr, fl), 0, col))
    grid_spec = pltpu.PrefetchScalarGridSpec(
        num_scalar_prefetch=2,
        grid=(t // tm,),
        in_specs=_halo_specs(tm, d_lru, t) + [
            pl.BlockSpec(conv_w.shape, lambda i, tr, fl: (0, 0)),
            pl.BlockSpec((1, d_lru), lambda i, tr, fl: (0, 0)),
            pl.BlockSpec(wg.shape, lambda i, tr, fl: (0, 0, 0), **const),
            pl.BlockSpec(bg.shape, lambda i, tr, fl: (0, 0, 0)),
            pl.BlockSpec((1, d_lru), lambda i, tr, fl: (0, 0)),
            pl.BlockSpec((1, SUBLANES, d_lru), lambda i, tr, fl: (fl[2, i], 0, 0)),
            pl.BlockSpec((tm, d_lru), row),
            pl.BlockSpec((tm, d_lru), row),
            pl.BlockSpec((tm, d_pool), lambda i, tr, fl: (jnp.minimum(tr[i], n_lat - 1), 0)),
            pl.BlockSpec((tm, d_pool), lambda i, tr, fl: (jnp.clip(tr[i] - n_lat, 0, n_ctx - 1), 0)),
            pl.BlockSpec(memory_space=pl.ANY),
            pl.BlockSpec(memory_space=pl.ANY),
            mod(2), mod(3), mod(4),
            pl.BlockSpec(w_out.shape, lambda i, tr, fl: (0, 0), **const),
            pl.BlockSpec((1, d), lambda i, tr, fl: (0, 0)),
            pl.BlockSpec(rwt.shape, lambda i, tr, fl: (0, 0)),
            pl.BlockSpec((nr, 1), lambda i, tr, fl: (0, 0))],
        out_specs=[pl.BlockSpec((tm, d), row),
                   pl.BlockSpec((tm * (d // 2) // LANES, LANES), row),
                   pl.BlockSpec((nr, tm), lambda i, tr, fl: (0, tr[i])),
                   pl.BlockSpec((1, SUBLANES, d_lru), lambda i, tr, fl: (fl[2, i], 0, 0))],
        scratch_shapes=[pltpu.VMEM((tm + 2 * BF16_ROWS, d_lru), F32),
                        pltpu.VMEM((tm, d_lru), F32),
                        pltpu.VMEM((tm, d_lru), F32),
                        pltpu.VMEM((SUBLANES, d_lru), F32),
                        pltpu.VMEM((tm, d_lru + d_pool), BF16),
                        pltpu.VMEM((2, tm // SUBLANES, SUBLANES, d), F32),
                        pltpu.SemaphoreType.DMA((2,)),
                        pltpu.VMEM((d // 2 // LANES, 2 * tm, LANES), F32)])
    return pl.pallas_call(
        functools.partial(_bwd_kernel, geom=geom),
        grid_spec=grid_spec,
        out_shape=[jax.ShapeDtypeStruct((t, d), F32),
                   jax.ShapeDtypeStruct((t * (d // 2) // LANES, LANES), U32),
                   jax.ShapeDtypeStruct((nr, t), F32),
                   jax.ShapeDtypeStruct((ngrp, SUBLANES, d_lru), F32)],
        compiler_params=_cparams(("arbitrary",)),
        name="backward_scan_mix",
    )(tile_row, flags, xa, xa, xa, conv_w, conv_b.reshape(1, d_lru), wg, bg, lam.reshape(1, d_lru), h0,
      hf, ga, yb_lat, yb_ctx, x_lat, x_ctx, modtab, modtab, modtab, w_out, norm2_g.reshape(1, d), rwt, rb)


def _shift_rows(v, k):
    if k == 0:
        return v
    z = jnp.zeros((abs(k) * SUBLANES, v.shape[1]), v.dtype)
    if k > 0:
        return jnp.concatenate([z, v[:-k * SUBLANES]], axis=0)
    return jnp.concatenate([v[-k * SUBLANES:], z], axis=0)


def _run_sum(v, m, direction):
    if m & (m - 1) == 0:
        k = 1
        while k < m:
            v = v + _shift_rows(v, -direction * k)
            k *= 2
        return v
    out = v
    for j in range(1, m):
        out = out + _shift_rows(v, -direction * j)
    return out


def _box_sum(v, w):
    lo = w // 2
    hi = w - 1 - lo
    s = _run_sum(v, hi + 1, +1)
    if lo:
        s = s + _shift_rows(_run_sum(v, lo, -1), 1)
    return s


def _window_count(n_rows, lanes, n_pos, w):
    lo = w // 2
    hi = w - 1 - lo
    p = lax.shift_right_logical(lax.broadcasted_iota(I32, (n_rows, lanes), 0), 3)
    return (jnp.minimum(p + hi + 1, n_pos) - jnp.maximum(p - lo, 0)).astype(F32)


def _pool_kernel(z_ref, ps_ref, o_ref, v_ref, *, tiles_per_group, grid_rows, grid_cols):
    group = pl.program_id(1) // tiles_per_group
    lanes = z_ref.shape[1]
    ps = ps_ref[...]

    def pool_1d(w):
        z = z_ref[...].astype(F32)
        n = z.shape[0] // SUBLANES
        mean = _box_sum(z, w) / _window_count(z.shape[0], lanes, n, w)
        o_ref[...] = ((mean - z) * ps).astype(o_ref.dtype)

    def pool_2d(w):
        lo = w // 2
        hi = w - 1 - lo
        blk = grid_cols * SUBLANES
        cw = _window_count(blk, lanes, grid_cols, w)

        def zrow(r):
            return z_ref[pl.ds(pl.multiple_of(r * blk, blk), blk), :].astype(F32)

        v = jnp.zeros((blk, lanes), F32)
        for r in range(hi):
            v = v + zrow(r)
        v_ref[...] = v

        def body(r, carry):
            add = r + hi
            sub = r - lo - 1
            v = v_ref[...]
            v = v + jnp.where(add < grid_rows, zrow(jnp.minimum(add, grid_rows - 1)), 0.0)
            v = v - jnp.where(sub >= 0, zrow(jnp.maximum(sub, 0)), 0.0)
            v_ref[...] = v
            ch = (jnp.minimum(r + hi + 1, grid_rows) - jnp.maximum(r - lo, 0)).astype(F32)
            mean = _box_sum(v, w) / (ch * cw)
            o_ref[pl.ds(pl.multiple_of(r * blk, blk), blk), :] = ((mean - zrow(r)) * ps).astype(o_ref.dtype)
            return carry

        lax.fori_loop(0, grid_rows, body, 0)

    for g, w in enumerate(POOL_WINDOWS):
        @pl.when(group == g)
        def _(w=w):
            if grid_rows is None:
                pool_1d(w)
            else:
                pool_2d(w)


def _pool(z, pool_scale, rows_per_group, n_groups, first_block, grid_rows):
    d_pool = z.shape[1]
    gw = d_pool // len(POOL_WINDOWS)
    lanes = LANES
    blk = GRID_W * SUBLANES
    return pl.pallas_call(
        functools.partial(_pool_kernel, tiles_per_group=gw // lanes, grid_rows=grid_rows, grid_cols=GRID_W),
        grid=(n_groups, d_pool // lanes),
        in_specs=[pl.BlockSpec((rows_per_group, lanes), lambda g, j: (first_block + g, j)),
                  pl.BlockSpec((1, lanes), lambda g, j: (0, j))],
        out_specs=pl.BlockSpec((rows_per_group, lanes), lambda g, j: (g, j)),
        out_shape=jax.ShapeDtypeStruct((n_groups * rows_per_group, d_pool), BF16),
        scratch_shapes=[pltpu.VMEM((blk, lanes), F32)],
        compiler_params=_cparams(("arbitrary", "arbitrary")),
        name="pool_grid" if grid_rows is not None else "pool_seq",
    )(z, pool_scale.reshape(1, d_pool))


def _route_kernel(lg_ref, oi_ref, of_ref, cnt_ref, bc_ref, tri_ref, carry_ref, *, n_groups, per_group):
    step = pl.program_id(0)
    tt = lg_ref.shape[1]
    n_exp = n_groups * per_group

    @pl.when(step == 0)
    def _():
        r = lax.broadcasted_iota(I32, tri_ref.shape, 0)
        c = lax.broadcasted_iota(I32, tri_ref.shape, 1)
        tri_ref[...] = (r <= c).astype(BF16)
        carry_ref[...] = jnp.zeros_like(carry_ref)

    row = lambda k: lg_ref[k:k + 1, :]
    cmax = row(0)
    gi = jnp.zeros((1, tt), I32)
    for g in range(1, n_groups):
        better = row(g) > cmax
        gi = jnp.where(better, g, gi)
        cmax = jnp.where(better, row(g), cmax)
    denom = jnp.zeros((1, tt), F32)
    for g in range(n_groups):
        denom = denom + jnp.exp(row(g) - cmax)
    pg = 1.0 / denom
    fine = []
    for j in range(per_group):
        f = row(n_groups + j)
        for g in range(1, n_groups):
            f = jnp.where(gi == g, row(n_groups + g * per_group + j), f)
        fine.append(f)
    v1 = fine[0]
    i1 = jnp.zeros((1, tt), I32)
    for j in range(1, per_group):
        better = fine[j] > v1
        i1 = jnp.where(better, j, i1)
        v1 = jnp.where(better, fine[j], v1)
    v2 = jnp.full((1, tt), -jnp.inf, F32)
    i2 = jnp.zeros((1, tt), I32)
    for j in range(per_group):
        better = jnp.logical_and(i1 != j, fine[j] > v2)
        i2 = jnp.where(better, j, i2)
        v2 = jnp.where(better, fine[j], v2)
    ex = jnp.exp(v2 - v1)
    w1 = (1.0 / (1.0 + ex)) * pg
    w2 = (ex / (1.0 + ex)) * pg
    e1 = gi * per_group + i1
    e2 = gi * per_group + i2
    eid = lax.broadcasted_iota(I32, (n_exp, tt), 0)
    hit1 = eid == e1
    hit2 = eid == e2
    member = jnp.logical_or(hit1, hit2).astype(BF16)
    cb = tri_ref.shape[0]
    carry = carry_ref[...]
    cums = []
    lane = lax.broadcasted_iota(I32, bc_ref.shape, 1)
    block_ends = jnp.zeros(bc_ref.shape, F32)
    for s in range(tt // cb):
        c = jnp.dot(member[:, s * cb:(s + 1) * cb], tri_ref[...], preferred_element_type=F32) + carry
        carry = c[:, cb - 1:cb]
        cums.append(c)
        block_ends = jnp.where(lane == s, carry, block_ends)
    bc_ref[...] = block_ends.astype(I32)
    carry_ref[...] = carry
    cum = jnp.concatenate(cums, axis=1)
    rank1 = jnp.sum(jnp.where(hit1, cum, 0.0), axis=0, keepdims=True) - 1.0
    rank2 = jnp.sum(jnp.where(hit2, cum, 0.0), axis=0, keepdims=True) - 1.0
    zi = jnp.zeros((SUBLANES - 4, tt), I32)
    oi_ref[...] = jnp.concatenate([e1, e2, rank1.astype(I32), rank2.astype(I32), zi], axis=0)
    of_ref[...] = jnp.concatenate([w1, w2, jnp.zeros((of_ref.shape[0] - 2, tt), F32)], axis=0)
    cnt_ref[...] = jnp.broadcast_to(carry, cnt_ref.shape).astype(I32)


def _route(logits_t, n_groups, per_group, cb):
    nr, t = logits_t.shape
    tt = max(k for k in range(cb, min(ROUTE_TILE, t) + 1, cb) if t % k == 0)
    n_exp = n_groups * per_group
    return pl.pallas_call(
        functools.partial(_route_kernel, n_groups=n_groups, per_group=per_group),
        grid=(t // tt,),
        in_specs=[pl.BlockSpec((nr, tt), lambda i: (0, i))],
        out_specs=[pl.BlockSpec((SUBLANES, tt), lambda i: (0, i)),
                   pl.BlockSpec((LANES, tt), lambda i: (0, i)),
                   pl.BlockSpec((n_exp, LANES), lambda i: (0, 0)),
                   pl.BlockSpec((n_exp, LANES), lambda i: (0, i))],
        out_shape=[jax.ShapeDtypeStruct((SUBLANES, t), I32),
                   jax.ShapeDtypeStruct((LANES, t), F32),
                   jax.ShapeDtypeStruct((n_exp, LANES), I32),
                   jax.ShapeDtypeStruct((n_exp, LANES * (t // tt)), I32)],
        scratch_shapes=[pltpu.VMEM((cb, cb), BF16), pltpu.VMEM((n_exp, 1), F32)],
        compiler_params=_cparams(("arbitrary",)),
        name="route",
    )(logits_t)


def _tiles_for(cnt, tmx):
    return jnp.right_shift(cnt + (SEG_CHUNK + tmx - 1), tmx.bit_length() - 1)


def _plan_kernel(ri_ref, cnt_ref, bc_ref, bcp_ref, lp_ref, dst_ref, src_ref, len_ref, te_ref, nv_ref,
                 off_ref, *, tmx, tb):
    step = pl.program_id(0)
    n_exp = cnt_ref.shape[0]
    tt = ri_ref.shape[1]
    shift = tmx.bit_length() - 1
    n_tile = _tiles_for(cnt_ref[:, 0:1], tmx)

    def exclusive_prefix(col):
        run = jnp.zeros((1, 1), I32)
        parts = []
        for e in range(n_exp):
            parts.append(run)
            run = run + col[e:e + 1, :]
        return jnp.concatenate(parts, axis=0), run

    off, total = exclusive_prefix(jnp.left_shift(n_tile, shift))
    before = jnp.where(step == 0, 0, bcp_ref[:, tt // tb - 1:tt // tb])
    lane = lax.broadcasted_iota(I32, dst_ref.shape, 1)
    eid = lax.broadcasted_iota(I32, (n_exp, tb), 0)
    dst = jnp.zeros(dst_ref.shape, I32)
    src = jnp.zeros(dst_ref.shape, I32)
    length = jnp.zeros(dst_ref.shape, I32)
    rows = []
    for k in range(tt // tb):
        end = bc_ref[:, k:k + 1]
        n = end - before
        aligned = jnp.left_shift(jnp.right_shift(n + (SEG_CHUNK - 1), SEG_SHIFT), SEG_SHIFT)
        start, _ = exclusive_prefix(aligned)
        base = start - before
        sl = slice(k * tb, (k + 1) * tb)
        place = lambda e_row, r_row: jnp.sum(jnp.where(eid == e_row, base, 0), axis=0, keepdims=True) + r_row
        rows.append(jnp.concatenate([place(ri_ref[0:1, sl], ri_ref[2:3, sl]),
                                     place(ri_ref[1:2, sl], ri_ref[3:4, sl])], axis=0))
        dst = jnp.where(lane == k, off + before, dst)
        src = jnp.where(lane == k, start, src)
        length = jnp.where(lane == k, n, length)
        before = end
    lp_ref[...] = jnp.concatenate([jnp.concatenate(rows, axis=1),
                                   jnp.zeros((SUBLANES - TOP_K_FINE, tt), I32)], axis=0)
    dst_ref[...] = dst
    src_ref[...] = src
    len_ref[...] = length
    end_tile = jnp.right_shift(off, shift) + n_tile
    k = lax.broadcasted_iota(I32, (n_exp, te_ref.shape[1]), 1)
    te = jnp.sum((k >= end_tile).astype(I32), axis=0, keepdims=True)
    te_ref[...] = jnp.minimum(te, n_exp - 1)
    nv_ref[...] = jnp.broadcast_to(jnp.right_shift(total, shift), nv_ref.shape)
    off_ref[...] = jnp.broadcast_to(off, off_ref.shape)


def _plan(ri, counts, block_counts, tmx, n_tiles, tb):
    _, t = ri.shape
    n_exp = counts.shape[0]
    steps = block_counts.shape[1] // LANES
    tt = t // steps
    ntp = -(-n_tiles // LANES) * LANES
    per_step = lambda: pl.BlockSpec((n_exp, LANES), lambda i: (0, i))
    whole = lambda w: pl.BlockSpec((n_exp, w), lambda i: (0, 0))
    tab = jax.ShapeDtypeStruct((n_exp, LANES * steps), I32)
    lp, dst, src, length, te, nv, off = pl.pallas_call(
        functools.partial(_plan_kernel, tmx=tmx, tb=tb),
        grid=(steps,),
        in_specs=[pl.BlockSpec((SUBLANES, tt), lambda i: (0, i)),
                  whole(LANES),
                  per_step(),
                  pl.BlockSpec((n_exp, LANES), lambda i: (0, jnp.maximum(i - 1, 0)))],
        out_specs=[pl.BlockSpec((SUBLANES, tt), lambda i: (0, i)),
                   per_step(), per_step(), per_step(),
                   pl.BlockSpec((1, ntp), lambda i: (0, 0)),
                   pl.BlockSpec((1, LANES), lambda i: (0, 0)),
                   whole(LANES)],
        out_shape=[jax.ShapeDtypeStruct((SUBLANES, t), I32), tab, tab, tab,
                   jax.ShapeDtypeStruct((1, ntp), I32),
                   jax.ShapeDtypeStruct((1, LANES), I32),
                   jax.ShapeDtypeStruct((n_exp, LANES), I32)],
        compiler_params=_cparams(("arbitrary",)),
        name="plan",
    )(ri, counts, block_counts, block_counts)
    flat = lambda a: a.reshape(n_exp, steps, LANES)[:, :, :tt // tb].transpose(1, 2, 0).reshape(-1)
    tables = (flat(dst), flat(src), flat(length))
    return (lp[0].reshape(t // tb, tb), lp[1].reshape(t // tb, tb), tables,
            te[0, :n_tiles], nv[0, :1], off[:, 0])


def _load_positions(pos1_hbm, pos2_hbm, p1_ref, p2_ref, isem, step, n_steps):
    tb = p1_ref.shape[0] // 2
    slot = lax.rem(step, 2)

    def copies(blk, half):
        rows = pl.ds(pl.multiple_of(half * tb, tb), tb)
        return (pltpu.make_async_copy(pos1_hbm.at[blk], p1_ref.at[rows], isem.at[0, half]),
                pltpu.make_async_copy(pos2_hbm.at[blk], p2_ref.at[rows], isem.at[1, half]))

    @pl.when(step == 0)
    def _():
        for cp in copies(step, slot):
            cp.start()

    @pl.when(step + 1 < n_steps)
    def _():
        for cp in copies(step + 1, 1 - slot):
            cp.start()

    for cp in copies(step, slot):
        cp.wait()
    return slot * tb


def _token_rows(token, count, nc):
    return pl.ds(pl.multiple_of(token * nc, nc), count * nc)


def _segment_copy(stage_ref, far_hbm, near_tok, far_tok, sem, to_far, nc):
    near = stage_ref.at[_token_rows(near_tok, SEG_CHUNK, nc), :]
    far = far_hbm.at[_token_rows(far_tok, SEG_CHUNK, nc), :]
    return pltpu.make_async_copy(near, far, sem) if to_far else pltpu.make_async_copy(far, near, sem)


def _start_segments(tables, blk, n_exp, stage_ref, far_hbm, sem, to_far, nc):
    dst_ref, src_ref, len_ref = tables

    total = jnp.int32(0)
    for e in range(n_exp):
        k = blk * n_exp + e
        n_chunks = jnp.right_shift(len_ref[k] + (SEG_CHUNK - 1), SEG_SHIFT)
        near0 = src_ref[k]
        far0 = dst_ref[k]

        def one(c, carry, near0=near0, far0=far0, queue=e % 2):
            _segment_copy(stage_ref, far_hbm, near0 + c * SEG_CHUNK, far0 + c * SEG_CHUNK, sem,
                          to_far, nc).start(priority=queue)
            return carry
        lax.fori_loop(0, n_chunks, one, 0)
        total = total + n_chunks
    return total


def _wait_segments(stage_ref, far_hbm, sem, n, to_far, nc):
    copy = _segment_copy(stage_ref, far_hbm, 0, 0, sem, to_far, nc)

    def body(_, c):
        copy.wait()
        return c
    lax.fori_loop(0, n, body, 0)


def _dispatch_kernel(dst_ref, src_ref, len_ref, off_ref, cnt_ref, lp1_hbm, lp2_hbm, h2_ref, xs_hbm,
                     p1_ref, p2_ref, stage, nd_ref, zero_ref, isem, ssem, zsem, *, tmx):
    step = pl.program_id(0)
    n_steps = pl.num_programs(0)
    slot = lax.rem(step, 2)
    n_exp = off_ref.shape[0]
    nc = zero_ref.shape[0]
    n_blk = h2_ref.shape[0] // nc // SUBLANES
    tables = (dst_ref, src_ref, len_ref)

    @pl.when(step == 0)
    def _():
        for s in range(stage.shape[0]):
            stage[s] = _packed_zeros(stage.shape[1])

    p0 = _load_positions(lp1_hbm, lp2_hbm, p1_ref, p2_ref, isem, step, n_steps)

    def place(blk, c):
        j0 = blk * SUBLANES
        for s in range(SUBLANES):
            row = h2_ref[_token_rows(j0 + s, 1, nc), :]
            stage[slot, _token_rows(p1_ref[p0 + j0 + s], 1, nc), :] = row
            stage[slot, _token_rows(p2_ref[p0 + j0 + s], 1, nc), :] = row
        return c
    lax.fori_loop(0, n_blk, place, 0, unroll=2)

    @pl.when(step >= 1)
    def _():
        _wait_segments(stage.at[1 - slot], xs_hbm, ssem.at[1 - slot], nd_ref[1 - slot], True, nc)

    nd_ref[slot] = _start_segments(tables, step, n_exp, stage.at[slot], xs_hbm, ssem.at[slot], True, nc)

    @pl.when(step == n_steps - 1)
    def _():
        _wait_segments(stage.at[slot], xs_hbm, ssem.at[slot], nd_ref[slot], True, nc)
        zero_ref[...] = _packed_zeros(nc)
        shift = tmx.bit_length() - 1

        def pad_expert(e, c):
            cnt = cnt_ref[e]
            n_pad = jnp.left_shift(_tiles_for(cnt, tmx), shift) - cnt
            first = off_ref[e] + cnt
            zero_copy = lambda r: pltpu.make_async_copy(
                zero_ref, xs_hbm.at[_token_rows(first + r, 1, nc), :], zsem)

            def fill(r, c2):
                zero_copy(r).start()
                return c2
            lax.fori_loop(0, n_pad, fill, 0)

            def done(r, c2):
                zero_copy(r).wait()
                return c2
            lax.fori_loop(0, n_pad, done, 0)
            return c
        lax.fori_loop(0, off_ref.shape[0], pad_expert, 0)


def _stage_rows(tb, n_exp):
    return TOP_K_FINE * tb + n_exp * SEG_CHUNK


def _dispatch(tables, offsets, counts, lp1, lp2, h2p, p_max, tmx):
    nb, tb = lp1.shape
    nc = h2p.shape[0] // (nb * tb)
    n_exp = offsets.shape[0]
    hbm = lambda: pl.BlockSpec(memory_space=pl.ANY)
    grid_spec = pltpu.PrefetchScalarGridSpec(
        num_scalar_prefetch=5,
        grid=(nb,),
        in_specs=[hbm(), hbm(),
                  pl.BlockSpec((tb * nc, LANES), lambda i, *_: (i, 0))],
        out_specs=hbm(),
        scratch_shapes=[pltpu.SMEM((2 * tb,), I32),
                        pltpu.SMEM((2 * tb,), I32),
                        pltpu.VMEM((2, _stage_rows(tb, n_exp) * nc, LANES), U32),
                        pltpu.SMEM((2,), I32),
                        pltpu.VMEM((nc, LANES), U32),
                        pltpu.SemaphoreType.DMA((2, 2)),
                        pltpu.SemaphoreType.DMA((2,)),
                        pltpu.SemaphoreType.DMA(())])
    return pl.pallas_call(
        functools.partial(_dispatch_kernel, tmx=tmx),
        grid_spec=grid_spec,
        out_shape=jax.ShapeDtypeStruct((p_max * nc, LANES), U32),
        compiler_params=_cparams(("arbitrary",)),
        name="dispatch",
    )(*tables, offsets, counts, lp1, lp2, h2p)


def _expert_kernel(te_ref, nv_ref, x_ref, w1_ref, w3_ref, w2_ref, y_ref, pack_ref, w1b, w3b, w2b, *, tmx):
    i = pl.program_id(0)

    @pl.when(jnp.logical_and(i < nv_ref[0],
                             jnp.logical_or(i == 0, te_ref[i] != te_ref[jnp.maximum(i - 1, 0)])))
    def _():
        w1b[...] = w1_ref[0].astype(BF16)
        w3b[...] = w3_ref[0].astype(BF16)
        w2b[...] = w2_ref[0].astype(BF16)

    @pl.when(i < nv_ref[0])
    def _():
        lo, hi = _load_packed_tokens(x_ref, tmx, pack_ref)
        lo = lo.astype(BF16)
        hi = hi.astype(BF16)
        half = lo.shape[1]
        mm = lambda w: (jnp.dot(lo, w[:half, :], preferred_element_type=F32)
                        + jnp.dot(hi, w[half:, :], preferred_element_type=F32))
        h1 = mm(w1b)
        h3 = mm(w3b)
        hh = (h1 * jax.nn.sigmoid(h1) * h3).astype(BF16)
        y = jnp.dot(hh, w2b[...], preferred_element_type=F32)
        _store_packed_tokens(y_ref, y[:, :half], y[:, half:], pack_ref)


def _expert_mlp(xs, tile_expert, n_valid, w1, w3, w2, tmx):
    d, de = w1.shape[1], w1.shape[2]
    nc = d // 2 // LANES
    p = xs.shape[0] // nc
    tile = lambda i, te, nv: (jnp.minimum(i, nv[0] - 1), 0)
    grid_spec = pltpu.PrefetchScalarGridSpec(
        num_scalar_prefetch=2,
        grid=(p // tmx,),
        in_specs=[pl.BlockSpec((tmx * nc, LANES), tile),
                  pl.BlockSpec((1, d, de), lambda i, te, nv: (te[i], 0, 0)),
                  pl.BlockSpec((1, d, de), lambda i, te, nv: (te[i], 0, 0)),
                  pl.BlockSpec((1, de, d), lambda i, te, nv: (te[i], 0, 0))],
        out_specs=pl.BlockSpec((tmx * nc, LANES), tile),
        scratch_shapes=[pltpu.VMEM((nc, 2 * tmx, LANES), F32),
                        pltpu.VMEM((d, de), BF16),
                        pltpu.VMEM((d, de), BF16),
                        pltpu.VMEM((de, d), BF16)])
    return pl.pallas_call(
        functools.partial(_expert_kernel, tmx=tmx),
        grid_spec=grid_spec,
        out_shape=jax.ShapeDtypeStruct((p * nc, LANES), U32),
        compiler_params=_cparams(("arbitrary",)),
        name="expert_mlp",
    )(tile_expert, n_valid, xs, w1, w3, w2)


def _combine_kernel(dst_ref, src_ref, len_ref, lp1_hbm, lp2_hbm, ys_hbm, x1_ref, rf_ref, g2_ref, fg_ref,
                    ylat_hbm, yctx_hbm, p1_ref, p2_ref, stage, gbuf, pack_ref, obuf, nd_ref, isem, csem, osem,
                    *, geom, n_exp):
    step = pl.program_id(0)
    n_steps = pl.num_programs(0)
    tm = geom.tm
    n_blk = tm // SUBLANES
    slot = lax.rem(step, 2)
    nc = gbuf.shape[1] // tm
    tables = (dst_ref, src_ref, len_ref)

    def fetch(blk, gs):
        nd_ref[gs] = _start_segments(tables, blk, n_exp, stage.at[gs], ys_hbm, csem.at[gs], False, nc)

    @pl.when(step == 0)
    def _():
        fetch(step, slot)

    @pl.when(step + 1 < n_steps)
    def _():
        fetch(step + 1, 1 - slot)

    _wait_segments(stage.at[slot], ys_hbm, csem.at[slot], nd_ref[slot], False, nc)
    p0 = _load_positions(lp1_hbm, lp2_hbm, p1_ref, p2_ref, isem, step, n_steps)

    def pick(blk, c):
        j0 = blk * SUBLANES
        for s in range(SUBLANES):
            rows = _token_rows(j0 + s, 1, nc)
            gbuf[0, rows, :] = stage[slot, _token_rows(p1_ref[p0 + j0 + s], 1, nc), :]
            gbuf[1, rows, :] = stage[slot, _token_rows(p2_ref[p0 + j0 + s], 1, nc), :]
        return c
    lax.fori_loop(0, n_blk, pick, 0, unroll=2)

    wt = rf_ref[...].T
    a_lo, a_hi = _load_packed_tokens(gbuf.at[0], tm, pack_ref.at[0])
    b_lo, b_hi = _load_packed_tokens(gbuf.at[1], tm, pack_ref.at[1])
    w1 = wt[:, 0:1]
    w2 = wt[:, 1:2]
    moe = jnp.concatenate([w1 * a_lo + w2 * b_lo, w1 * a_hi + w2 * b_hi], axis=1)
    x = x1_ref[...] + _gate(moe, g2_ref[0])
    out = _rms(x, fg_ref[...])

    start = lambda cp: cp.start()
    wait = lambda cp: cp.wait()
    put = functools.partial(_tile_copies, ylat_hbm, yctx_hbm, geom=geom, to_rows=False)

    @pl.when(step >= 2)
    def _():
        put(obuf.at[slot], osem.at[slot], step - 2, fn=wait)

    obuf[slot] = out.reshape(obuf.shape[1:])
    put(obuf.at[slot], osem.at[slot], step, fn=start)

    @pl.when(step == n_steps - 1)
    def _():
        @pl.when(n_steps > 1)
        def _():
            put(obuf.at[1 - slot], osem.at[1 - slot], step - 1, fn=wait)
        put(obuf.at[slot], osem.at[slot], step, fn=wait)


def _combine(tables, lp1, lp2, ys, x1, rf, modtab, final_g, lat_shape, ctx_shape, geom):
    t, d = x1.shape
    tm = geom.tm
    nc = d // 2 // LANES
    n_exp = tables[0].shape[0] // (t // tm)
    mset = lambda i, *_: jnp.where(i < geom.n_lat_tiles, 0, 1)
    hbm = lambda: pl.BlockSpec(memory_space=pl.ANY)
    grid_spec = pltpu.PrefetchScalarGridSpec(
        num_scalar_prefetch=3,
        grid=(t // tm,),
        in_specs=[hbm(), hbm(), hbm(),
                  pl.BlockSpec((tm, d), lambda i, *_: (i, 0)),
                  pl.BlockSpec((LANES, tm), lambda i, *_: (0, i)),
                  pl.BlockSpec((1, SUBLANES, d), lambda i, *_: (mset(i), 0, 5)),
                  pl.BlockSpec((1, d), lambda i, *_: (0, 0))],
        out_specs=[hbm(), hbm()],
        scratch_shapes=[pltpu.SMEM((2 * tm,), I32),
                        pltpu.SMEM((2 * tm,), I32),
                        pltpu.VMEM((2, _stage_rows(tm, n_exp) * nc, LANES), U32),
                        pltpu.VMEM((TOP_K_FINE, tm * nc, LANES), U32),
                        pltpu.VMEM((TOP_K_FINE, nc, 2 * tm, LANES), F32),
                        pltpu.VMEM((2, tm // SUBLANES, SUBLANES, d), F32),
                        pltpu.SMEM((2,), I32),
                        pltpu.SemaphoreType.DMA((2, 2)),
                        pltpu.SemaphoreType.DMA((2,)),
                        pltpu.SemaphoreType.DMA((2,))])
    return pl.pallas_call(
        functools.partial(_combine_kernel, geom=geom, n_exp=n_exp),
        grid_spec=grid_spec,
        out_shape=[jax.ShapeDtypeStruct(lat_shape, F32), jax.ShapeDtypeStruct(ctx_shape, F32)],
        compiler_params=_cparams(("arbitrary",)),
        name="combine",
    )(*tables, lp1, lp2, ys, x1, rf, modtab, final_g.reshape(1, d))


def _tile_meta(groups, tm):
    rows, first, last, grp = [], [], [], []
    blk = 0
    for g, r in enumerate(groups):
        nc = r // tm
        for c in range(nc):
            rows.append(blk + c)
            first.append(int(c == 0))
            last.append(int(c == nc - 1))
            grp.append(g)
        blk += nc
    fwd = (np.array(rows, np.int32), np.array([first, last, grp], np.int32))
    order = []
    blk = 0
    for r in groups:
        nc = r // tm
        order.extend(range(blk + nc - 1, blk - 1, -1))
        blk += nc
    order = np.array(order)
    bwd = (fwd[0][order], fwd[1][:, order])
    return fwd, bwd


def kernel(x_prompt, x_sample, state_lru, c, c_ctx, w_mod, b_mod, norm1_g, w_in, conv_w, conv_b, lru_wa, lru_ba, lru_wx, lru_bx, lru_lambda, pool_w, pool_scale, w_out, norm2_g, router_coarse_w, router_coarse_b, router_fine_w, router_fine_b, exp_w1, exp_w3, exp_w2, final_norm_g):
    bp, sp, d = x_prompt.shape
    bs, ss, _ = x_sample.shape
    d_lru = lru_lambda.shape[-1]
    heads, bw = lru_wa.shape[2], lru_wa.shape[3]
    n_groups, per_group = router_fine_w.shape[2], router_fine_w.shape[3]
    n_exp = n_groups * per_group
    assert w_mod.shape[0] == 1 and bs == SUBLANES and bp % SUBLANES == 0 and ss % GRID_W == 0
    assert EXPERT_TILE & (EXPERT_TILE - 1) == 0
    n_lat_groups, n_ctx_groups = bs // SUBLANES, bp // SUBLANES
    lat_rows, ctx_rows = ss * SUBLANES, sp * SUBLANES
    assert (n_lat_groups * lat_rows) % ctx_rows == 0
    tm = min(TOKEN_TILE, ctx_rows, lat_rows)
    geom = _Geom(tm=tm, n_lat_tiles=n_lat_groups * lat_rows // tm, lat_chunks=lat_rows // tm,
                 ctx_chunks=ctx_rows // tm)
    n_rows = n_lat_groups * lat_rows + n_ctx_groups * ctx_rows
    groups = [lat_rows] * n_lat_groups + [ctx_rows] * n_ctx_groups
    (f_rows, f_flags), (b_rows, b_flags) = _tile_meta(groups, tm)

    cond = jnp.zeros((2 * SUBLANES, d), F32).at[:bs].set(c).at[bs].set(c_ctx)
    mod = _modulation(cond, w_mod[0], b_mod[0])
    modtab = jnp.stack([mod[:SUBLANES], jnp.broadcast_to(mod[SUBLANES], (SUBLANES, mod.shape[1]))])
    h0_lat = state_lru[:, 0].reshape(n_lat_groups, SUBLANES, 2, d_lru)
    h0 = jnp.concatenate([h0_lat, jnp.zeros((n_ctx_groups, SUBLANES, 2, d_lru), F32)], axis=0)
    h0 = h0.transpose(0, 2, 1, 3)

    xa, ga, z = _input_projection(x_sample, x_prompt, modtab, norm1_g[0], w_in[0],
                                  pool_w[0].astype(BF16), n_rows, geom)

    def gate_weights(direction):
        wg = jnp.concatenate([lru_wa[0, direction], lru_wx[0, direction]], axis=-1).astype(BF16)
        bg = jnp.concatenate([lru_ba[0, direction].reshape(heads, 1, bw),
                              lru_bx[0, direction].reshape(heads, 1, bw)], axis=-1)
        return wg, bg

    log_decay = jax.nn.log_sigmoid(lru_lambda[0])
    wg_f, bg_f = gate_weights(0)
    wg_b, bg_b = gate_weights(1)
    hf, hf_last = _forward_scan(xa, (jnp.asarray(f_rows), jnp.asarray(f_flags)), conv_w[0], conv_b[0],
                                wg_f, bg_f, log_decay[0], h0[:, 0], tm)

    yb_lat = _pool(z, pool_scale[0], lat_rows, n_lat_groups, 0, lat_rows // (GRID_W * SUBLANES))
    yb_ctx = _pool(z, pool_scale[0], ctx_rows, n_ctx_groups, n_lat_groups * lat_rows // ctx_rows, None)

    n_logits = n_groups + n_exp
    rw = jnp.concatenate([router_coarse_w[0], router_fine_w[0].reshape(d, n_exp)], axis=1)
    rwt = jnp.zeros((LANES, d), BF16).at[:n_logits].set(rw.T.astype(BF16))
    rb = jnp.zeros((LANES, 1), F32).at[:n_logits, 0].set(
        jnp.concatenate([router_coarse_b[0], router_fine_b[0].reshape(n_exp)]))
    x1, h2p, logits_t, hb_last = _backward_scan_mix(
        xa, ga, hf, yb_lat, yb_ctx, x_sample, x_prompt, modtab, (jnp.asarray(b_rows), jnp.asarray(b_flags)),
        conv_w[0], conv_b[0], wg_b, bg_b, log_decay[1], h0[:, 1], w_out[0].astype(BF16), norm2_g[0],
        rwt, rb, geom)

    ri, rf, counts, block_counts = _route(logits_t, n_groups, per_group, tm)
    n_tiles = -(-(TOP_K_FINE * n_rows + n_exp * (EXPERT_TILE + SEG_CHUNK)) // EXPERT_TILE)
    lp1, lp2, tables, tile_expert, n_valid, offsets = _plan(ri, counts, block_counts, EXPERT_TILE, n_tiles, tm)
    xs = _dispatch(tables, offsets, counts[:, 0], lp1, lp2, h2p, n_tiles * EXPERT_TILE, EXPERT_TILE)
    ys = _expert_mlp(xs, tile_expert, n_valid, exp_w1[0], exp_w3[0], exp_w2[0], EXPERT_TILE)
    y_sample, y_prompt = _combine(tables, lp1, lp2, ys, x1, rf, modtab, final_norm_g, x_sample.shape,
                                  x_prompt.shape, geom)

    st = jnp.stack([hf_last[n_lat_groups:], hb_last[n_lat_groups:]], axis=2)
    state_new = st.reshape(bp, 1, 2, d_lru).astype(x_prompt.dtype)
    return (y_prompt, y_sample, state_new)
```

```python
import functools
from typing import NamedTuple

import numpy as np
import jax
import jax.numpy as jnp
from jax import lax
from jax.experimental import pallas as pl
from jax.experimental.pallas import tpu as pltpu

GRID_W = 64
CONV_W = 4
RG_C = 8.0
POOL_WINDOWS = (2, 4, 8, 16)
TOP_K_FINE = 2
EPS = 1e-6
EXPM1_SERIES_BELOW = 0.125

SUBLANES = 8
LANES = 128
BF16_ROWS = 16
TOKEN_TILE = 512
EXPERT_TILE = 512
ROUTE_TILE = 2048
SEG_SHIFT = 4
SEG_CHUNK = 1 << SEG_SHIFT
MOD_COL_TILE = 1024
WEIGHT_STAGE_ROWS = 256
VMEM_LIMIT = 60 * 1024 * 1024

F32 = jnp.float32
BF16 = jnp.bfloat16
U32 = jnp.uint32
I32 = jnp.int32


class _Geom(NamedTuple):
    tm: int
    n_lat_tiles: int
    lat_chunks: int
    ctx_chunks: int


def _cparams(sem):
    return pltpu.CompilerParams(dimension_semantics=sem, vmem_limit_bytes=VMEM_LIMIT)


def _per_sequence(y, m):
    rows, d = y.shape
    return y.reshape(rows // SUBLANES, SUBLANES, d), m[None]


def _modulate(y, scale, shift):
    y3, sc = _per_sequence(y, scale)
    _, sh = _per_sequence(y, shift)
    return (y3 * (1.0 + sc) + sh).reshape(y.shape)


def _gate(y, g):
    y3, g3 = _per_sequence(y, g)
    return (y3 * g3).reshape(y.shape)


def _rms(x, g):
    ms = jnp.mean(x * x, axis=-1, keepdims=True)
    return x * lax.rsqrt(ms + EPS) * g


def _store_packed_tokens(ref, lo, hi, scratch):
    n, words = lo.shape
    nc = words // LANES
    for c in range(nc):
        sl = slice(c * LANES, (c + 1) * LANES)
        scratch[c, pl.ds(0, n, stride=2), :] = lo[:, sl]
        scratch[c, pl.ds(1, n, stride=2), :] = hi[:, sl]
        ref[pl.ds(c, n, stride=nc), :] = pltpu.bitcast(scratch[c].astype(BF16), U32)


def _packed_zeros(rows):
    return pltpu.bitcast(jnp.zeros((2 * rows, LANES), BF16), U32)


def _load_packed_tokens(ref, n, scratch):
    nc = ref.shape[0] // n
    los, his = [], []
    for c in range(nc):
        scratch[c] = pltpu.bitcast(ref[pl.ds(c, n, stride=nc), :], BF16).astype(F32)
        los.append(scratch[c, pl.ds(0, n, stride=2), :])
        his.append(scratch[c, pl.ds(1, n, stride=2), :])
    return jnp.concatenate(los, axis=1), jnp.concatenate(his, axis=1)


def _tile_copies(lat_hbm, ctx_hbm, buf, sem, tile, geom, to_rows, fn, by_sequence=False):
    tt = geom.tm // SUBLANES

    def run(hbm, k, chunks):
        g = lax.div(k, jnp.int32(chunks))
        c = lax.rem(k, jnp.int32(chunks))
        for b in range(SUBLANES):
            h = hbm.at[g * SUBLANES + b, pl.ds(c * tt, tt), :]
            v = buf.at[b] if by_sequence else buf.at[:, b, :]
            fn(pltpu.make_async_copy(h, v, sem) if to_rows else pltpu.make_async_copy(v, h, sem))

    @pl.when(tile < geom.n_lat_tiles)
    def _():
        run(lat_hbm, tile, geom.lat_chunks)

    @pl.when(tile >= geom.n_lat_tiles)
    def _():
        run(ctx_hbm, tile - geom.n_lat_tiles, geom.ctx_chunks)


def _fetch_rows(lat_hbm, ctx_hbm, xbuf, sem, step, n_steps, tile_of, geom):
    slot = lax.rem(step, 2)
    start = lambda cp: cp.start()
    wait = lambda cp: cp.wait()

    @pl.when(step == 0)
    def _():
        _tile_copies(lat_hbm, ctx_hbm, xbuf.at[0], sem.at[0], tile_of(jnp.int32(0)), geom, True, start)

    @pl.when(step + 1 < n_steps)
    def _():
        nxt = 1 - slot
        _tile_copies(lat_hbm, ctx_hbm, xbuf.at[nxt], sem.at[nxt], tile_of(step + 1), geom, True, start)

    _tile_copies(lat_hbm, ctx_hbm, xbuf.at[slot], sem.at[slot], tile_of(step), geom, True, wait)
    tt, _, d = xbuf.shape[1:]
    return xbuf[slot].reshape(tt * SUBLANES, d)


def _mod_kernel(c_ref, w_ref, b_ref, o_ref):
    c = c_ref[...]
    s = c * jax.nn.sigmoid(c)
    o_ref[...] = jnp.dot(s.astype(BF16), w_ref[...].astype(BF16),
                         preferred_element_type=F32) + b_ref[...]


def _modulation(cond, w_mod, b_mod):
    rows, d = cond.shape
    n = w_mod.shape[1]
    tn = min(MOD_COL_TILE, n)
    return pl.pallas_call(
        _mod_kernel,
        grid=(n // tn,),
        in_specs=[pl.BlockSpec((rows, d), lambda j: (0, 0)),
                  pl.BlockSpec((d, tn), lambda j: (0, j)),
                  pl.BlockSpec((1, tn), lambda j: (0, j))],
        out_specs=pl.BlockSpec((rows, tn), lambda j: (0, j)),
        out_shape=jax.ShapeDtypeStruct((rows, n), F32),
        compiler_params=_cparams(("arbitrary",)),
        name="modulation",
    )(cond, w_mod, b_mod.reshape(1, n))


def _round_weights(w_hbm, wbf_ref, stage, sem):
    rows = stage.shape[1]
    n = w_hbm.shape[0] // rows
    copy = lambda r: pltpu.make_async_copy(w_hbm.at[pl.ds(r * rows, rows), :], stage.at[r % 2], sem.at[r % 2])
    copy(0).start()
    for r in range(n):
        if r + 1 < n:
            copy(r + 1).start()
        copy(r).wait()
        wbf_ref[r * rows:(r + 1) * rows, :] = stage[r % 2].astype(wbf_ref.dtype)


def _proj_kernel(lat_hbm, ctx_hbm, sh_ref, sc_ref, g_ref, win_hbm, pw_ref, xa_ref, ga_ref, z_ref,
                 xbuf, sem, wbf_ref, wstage, wsem, *, d_lru, gw, geom):
    step = pl.program_id(0)

    @pl.when(step == 0)
    def _():
        _round_weights(win_hbm, wbf_ref, wstage, wsem)

    x = _fetch_rows(lat_hbm, ctx_hbm, xbuf, sem, step, pl.num_programs(0), lambda s: s, geom)
    h = _modulate(_rms(x, g_ref[...]), sc_ref[0], sh_ref[0])
    proj = jnp.dot(h.astype(BF16), wbf_ref[...], preferred_element_type=F32)
    xa_ref[...] = proj[:, :d_lru].astype(BF16)
    ga_ref[...] = proj[:, d_lru:2 * d_lru].astype(BF16)
    for g in range(pw_ref.shape[0]):
        lo = 2 * d_lru + g * gw
        z_ref[:, g * gw:(g + 1) * gw] = jnp.dot(
            proj[:, lo:lo + gw].astype(BF16), pw_ref[g], preferred_element_type=F32).astype(BF16)


def _input_projection(x_lat, x_ctx, modtab, norm_g, w_in, pool_w, n_rows, geom):
    d = x_lat.shape[-1]
    tm = geom.tm
    d_pool = pool_w.shape[0] * pool_w.shape[1]
    d_lru = (w_in.shape[1] - d_pool) // 2
    mset = lambda i: jnp.where(i < geom.n_lat_tiles, 0, 1)
    const = dict(pipeline_mode=pl.Buffered(1))
    return pl.pallas_call(
        functools.partial(_proj_kernel, d_lru=d_lru, gw=pool_w.shape[1], geom=geom),
        grid=(n_rows // tm,),
        in_specs=[pl.BlockSpec(memory_space=pl.ANY),
                  pl.BlockSpec(memory_space=pl.ANY),
                  pl.BlockSpec((1, SUBLANES, d), lambda i: (mset(i), 0, 0)),
                  pl.BlockSpec((1, SUBLANES, d), lambda i: (mset(i), 0, 1)),
                  pl.BlockSpec((1, d), lambda i: (0, 0)),
                  pl.BlockSpec(memory_space=pl.ANY),
                  pl.BlockSpec(pool_w.shape, lambda i: (0, 0, 0), **const)],
        out_specs=[pl.BlockSpec((tm, d_lru), lambda i: (i, 0)),
                   pl.BlockSpec((tm, d_lru), lambda i: (i, 0)),
                   pl.BlockSpec((tm, d_pool), lambda i: (i, 0))],
        out_shape=[jax.ShapeDtypeStruct((n_rows, d_lru), BF16),
                   jax.ShapeDtypeStruct((n_rows, d_lru), BF16),
                   jax.ShapeDtypeStruct((n_rows, d_pool), BF16)],
        scratch_shapes=[pltpu.VMEM((2, tm // SUBLANES, SUBLANES, d), F32),
                        pltpu.SemaphoreType.DMA((2,)),
                        pltpu.VMEM(w_in.shape, BF16),
                        pltpu.VMEM((2, min(WEIGHT_STAGE_ROWS, d), w_in.shape[1]), F32),
                        pltpu.SemaphoreType.DMA((2,))],
        compiler_params=_cparams(("arbitrary",)),
        name="input_projection",
    )(x_lat, x_ctx, modtab, modtab, norm_g.reshape(1, d), w_in, pool_w)


def _fill_ext(ext_ref, prev_ref, main_ref, next_ref, first, last):
    tm = main_ref.shape[0]
    prev = prev_ref[...].astype(F32)
    nxt = next_ref[...].astype(F32)
    ext_ref[0:BF16_ROWS, :] = jnp.where(first, 0.0, prev)
    ext_ref[BF16_ROWS:BF16_ROWS + tm, :] = main_ref[...].astype(F32)
    ext_ref[BF16_ROWS + tm:, :] = jnp.where(last, 0.0, nxt)


def _one_minus_exp(y, exp_y):
    p = 1.0 / 120.0
    for c in (1.0 / 24.0, 1.0 / 6.0, 0.5, 1.0):
        p = p * y + c
    return jnp.where(y > -EXPM1_SERIES_BELOW, -y * p, 1.0 - exp_y)


def _sqrt_nonneg(q):
    return jnp.where(q > 0.0, q * lax.rsqrt(q), 0.0)


def _decay_and_input(ext_ref, cw_ref, cb_ref, wg_ref, bg_ref, lam_ref, a_ref, u_ref, tm):
    heads, bw = wg_ref.shape[0], wg_ref.shape[1]
    for hd in range(heads):
        sl = slice(hd * bw, (hd + 1) * bw)
        xc = cb_ref[:, sl]
        for k in range(CONV_W):
            xc = xc + cw_ref[k:k + 1, sl] * ext_ref[SUBLANES * k:SUBLANES * k + tm, sl]
        g = jnp.dot(xc.astype(BF16), wg_ref[hd], preferred_element_type=F32) + bg_ref[hd]
        r = jax.nn.sigmoid(g[:, :bw])
        ig = jax.nn.sigmoid(g[:, bw:])
        log_a = (RG_C * r) * lam_ref[:, sl]
        a = jnp.exp(log_a)
        a_ref[:, sl] = a
        u_ref[:, sl] = _sqrt_nonneg(_one_minus_exp(2.0 * log_a, a * a)) * (ig * xc)


def _scan(a_ref, u_ref, h, tm, reverse):
    nblk = tm // SUBLANES

    def body(s, h):
        j = (nblk - 1 - s) if reverse else s
        rows = pl.ds(pl.multiple_of(j * SUBLANES, SUBLANES), SUBLANES)
        h = a_ref[rows, :] * h + u_ref[rows, :]
        u_ref[rows, :] = h
        return h

    return lax.fori_loop(0, nblk, body, h, unroll=8)


def _halo_specs(tm, d_lru, n_rows):
    per = tm // BF16_ROWS
    last_blk = n_rows // BF16_ROWS - 1
    return [pl.BlockSpec((BF16_ROWS, d_lru), lambda i, tr, fl: (jnp.maximum(tr[i] * per - 1, 0), 0)),
            pl.BlockSpec((tm, d_lru), lambda i, tr, fl: (tr[i], 0)),
            pl.BlockSpec((BF16_ROWS, d_lru), lambda i, tr, fl: (jnp.minimum((tr[i] + 1) * per, last_blk), 0))]


def _fwd_kernel(tr_ref, fl_ref, prev_ref, main_ref, next_ref, cw_ref, cb_ref, wg_ref, bg_ref, lam_ref,
                h0_ref, hf_ref, hlast_ref, ext_ref, a_ref, u_ref, h_ref):
    i = pl.program_id(0)
    tm = main_ref.shape[0]
    first = fl_ref[0, i] == 1
    last = fl_ref[1, i] == 1
    _fill_ext(ext_ref, prev_ref, main_ref, next_ref, first, last)
    _decay_and_input(ext_ref, cw_ref, cb_ref, wg_ref, bg_ref, lam_ref, a_ref, u_ref, tm)

    @pl.when(first)
    def _():
        h_ref[...] = h0_ref[0]

    h = _scan(a_ref, u_ref, h_ref[...], tm, reverse=False)
    h_ref[...] = h
    hlast_ref[0] = h
    hf_ref[...] = u_ref[...].astype(BF16)


def _forward_scan(xa, meta, conv_w, conv_b, wg, bg, lam, h0, tm):
    t, d_lru = xa.shape
    tile_row, flags = meta
    ngrp = h0.shape[0]
    const = dict(pipeline_mode=pl.Buffered(1))
    grid_spec = pltpu.PrefetchScalarGridSpec(
        num_scalar_prefetch=2,
        grid=(t // tm,),
        in_specs=_halo_specs(tm, d_lru, t) + [
            pl.BlockSpec(conv_w.shape, lambda i, tr, fl: (0, 0)),
            pl.BlockSpec((1, d_lru), lambda i, tr, fl: (0, 0)),
            pl.BlockSpec(wg.shape, lambda i, tr, fl: (0, 0, 0), **const),
            pl.BlockSpec(bg.shape, lambda i, tr, fl: (0, 0, 0)),
            pl.BlockSpec((1, d_lru), lambda i, tr, fl: (0, 0)),
            pl.BlockSpec((1, SUBLANES, d_lru), lambda i, tr, fl: (fl[2, i], 0, 0))],
        out_specs=[pl.BlockSpec((tm, d_lru), lambda i, tr, fl: (tr[i], 0)),
                   pl.BlockSpec((1, SUBLANES, d_lru), lambda i, tr, fl: (fl[2, i], 0, 0))],
        scratch_shapes=[pltpu.VMEM((tm + 2 * BF16_ROWS, d_lru), F32),
                        pltpu.VMEM((tm, d_lru), F32),
                        pltpu.VMEM((tm, d_lru), F32),
                        pltpu.VMEM((SUBLANES, d_lru), F32)])
    return pl.pallas_call(
        _fwd_kernel,
        grid_spec=grid_spec,
        out_shape=[jax.ShapeDtypeStruct((t, d_lru), BF16),
                   jax.ShapeDtypeStruct((ngrp, SUBLANES, d_lru), F32)],
        compiler_params=_cparams(("arbitrary",)),
        name="forward_scan",
    )(tile_row, flags, xa, xa, xa, conv_w, conv_b.reshape(1, d_lru), wg, bg, lam.reshape(1, d_lru), h0)


def _bwd_kernel(tr_ref, fl_ref, prev_ref, main_ref, next_ref, cw_ref, cb_ref, wg_ref, bg_ref, lam_ref,
                h0_ref, hf_ref, ga_ref, ybl_ref, ybc_ref, lat_hbm, ctx_hbm, g1_ref, sh2_ref, sc2_ref, wout_ref,
                n2_ref, rwt_ref, rb_ref,
                x1_ref, h2_ref, lg_ref, hlast_ref,
                ext_ref, a_ref, u_ref, h_ref, cat_ref, xbuf, xsem, pack_ref, *, geom):
    i = pl.program_id(0)
    n_steps = pl.num_programs(0)
    tm, d_lru = main_ref.shape
    x = _fetch_rows(lat_hbm, ctx_hbm, xbuf, xsem, i, n_steps,
                    lambda s: tr_ref[jnp.minimum(s, n_steps - 1)], geom)
    first = fl_ref[0, i] == 1
    last = fl_ref[1, i] == 1
    _fill_ext(ext_ref, prev_ref, main_ref, next_ref, first, last)
    _decay_and_input(ext_ref, cw_ref, cb_ref, wg_ref, bg_ref, lam_ref, a_ref, u_ref, tm)

    @pl.when(last)
    def _():
        h_ref[...] = h0_ref[0]

    h = _scan(a_ref, u_ref, h_ref[...], tm, reverse=True)
    h_ref[...] = h
    hlast_ref[0] = h

    ga = ga_ref[...].astype(F32)
    ya = (hf_ref[...].astype(F32) + u_ref[...]) * jax.nn.gelu(ga)
    cat_ref[:, :d_lru] = ya.astype(BF16)
    cat_ref[:, d_lru:] = jnp.where(tr_ref[i] < geom.n_lat_tiles, ybl_ref[...], ybc_ref[...])
    mix = jnp.dot(cat_ref[...], wout_ref[...], preferred_element_type=F32)
    x1 = x + _gate(mix, g1_ref[0])
    x1_ref[...] = x1
    h2 = _modulate(_rms(x1, n2_ref[...]), sc2_ref[0], sh2_ref[0])
    half = h2.shape[1] // 2
    _store_packed_tokens(h2_ref, h2[:, :half], h2[:, half:], pack_ref)
    lg_ref[...] = lax.dot_general(rwt_ref[...], h2.astype(BF16), (((1,), (1,)), ((), ())),
                                  preferred_element_type=F32) + rb_ref[...]


def _backward_scan_mix(xa, ga, hf, yb_lat, yb_ctx, x_lat, x_ctx, modtab, meta, conv_w, conv_b, wg, bg, lam, h0, w_out,
                       norm2_g, rwt, rb, geom):
    t, d_lru = xa.shape
    d = x_lat.shape[-1]
    d_pool = yb_lat.shape[1]
    n_lat, n_ctx = yb_lat.shape[0] // geom.tm, yb_ctx.shape[0] // geom.tm
    tm = geom.tm
    tile_row, flags = meta
    ngrp = h0.shape[0]
    nr = rwt.shape[0]
    const = dict(pipeline_mode=pl.Buffered(1))
    mset = lambda i, tr, fl: jnp.where(tr[i] < geom.n_lat_tiles, 0, 1)
    row = lambda i, tr, fl: (tr[i], 0)
    mod = lambda col: pl.BlockSpec((1, SUBLANES, d), lambda i, tr, fl: (mset(i, tr, fl), 0, col))
    grid_spec = pltpu.PrefetchScalarGridSpec(
        num_scalar_prefetch=2,
        grid=(t // tm,),
        in_specs=_halo_specs(tm, d_lru, t) + [
            pl.BlockSpec(conv_w.shape, lambda i, tr, fl: (0, 0)),
            pl.BlockSpec((1, d_lru), lambda i, tr, fl: (0, 0)),
            pl.BlockSpec(wg.shape, lambda i, tr, fl: (0, 0, 0), **const),
            pl.BlockSpec(bg.shape, lambda i, tr, fl: (0, 0, 0)),
            pl.BlockSpec((1, d_lru), lambda i, tr, fl: (0, 0)),
            pl.BlockSpec((1, SUBLANES, d_lru), lambda i, tr, fl: (fl[2, i], 0, 0)),
            pl.BlockSpec((tm, d_lru), row),
            pl.BlockSpec((tm, d_lru), row),
            pl.BlockSpec((tm, d_pool), lambda i, tr, fl: (jnp.minimum(tr[i], n_lat - 1), 0)),
            pl.BlockSpec((tm, d_pool), lambda i, tr, fl: (jnp.clip(tr[i] - n_lat, 0, n_ctx - 1), 0)),
            pl.BlockSpec(memory_space=pl.ANY),
            pl.BlockSpec(memory_space=pl.ANY),
            mod(2), mod(3), mod(4),
            pl.BlockSpec(w_out.shape, lambda i, tr, fl: (0, 0), **const),
            pl.BlockSpec((1, d), lambda i, tr, fl: (0, 0)),
            pl.BlockSpec(rwt.shape, lambda i, tr, fl: (0, 0)),
            pl.BlockSpec((nr, 1), lambda i, tr, fl: (0, 0))],
        out_specs=[pl.BlockSpec((tm, d), row),
                   pl.BlockSpec((tm * (d // 2) // LANES, LANES), row),
                   pl.BlockSpec((nr, tm), lambda i, tr, fl: (0, tr[i])),
                   pl.BlockSpec((1, SUBLANES, d_lru), lambda i, tr, fl: (fl[2, i], 0, 0))],
        scratch_shapes=[pltpu.VMEM((tm + 2 * BF16_ROWS, d_lru), F32),
                        pltpu.VMEM((tm, d_lru), F32),
                        pltpu.VMEM((tm, d_lru), F32),
                        pltpu.VMEM((SUBLANES, d_lru), F32),
                        pltpu.VMEM((tm, d_lru + d_pool), BF16),
                        pltpu.VMEM((2, tm // SUBLANES, SUBLANES, d), F32),
                        pltpu.SemaphoreType.DMA((2,)),
                        pltpu.VMEM((d // 2 // LANES, 2 * tm, LANES), F32)])
    return pl.pallas_call(
        functools.partial(_bwd_kernel, geom=geom),
        grid_spec=grid_spec,
        out_shape=[jax.ShapeDtypeStruct((t, d), F32),
                   jax.ShapeDtypeStruct((t * (d // 2) // LANES, LANES), U32),
                   jax.ShapeDtypeStruct((nr, t), F32),
                   jax.ShapeDtypeStruct((ngrp, SUBLANES, d_lru), F32)],
        compiler_params=_cparams(("arbitrary",)),
        name="backward_scan_mix",
    )(tile_row, flags, xa, xa, xa, conv_w, conv_b.reshape(1, d_lru), wg, bg, lam.reshape(1, d_lru), h0,
      hf, ga, yb_lat, yb_ctx, x_lat, x_ctx, modtab, modtab, modtab, w_out, norm2_g.reshape(1, d), rwt, rb)


def _shift_rows(v, k):
    if k == 0:
        return v
    z = jnp.zeros((abs(k) * SUBLANES, v.shape[1]), v.dtype)
    if k > 0:
        return jnp.concatenate([z, v[:-k * SUBLANES]], axis=0)
    return jnp.concatenate([v[-k * SUBLANES:], z], axis=0)


def _run_sum(v, m, direction):
    if m & (m - 1) == 0:
        k = 1
        while k < m:
            v = v + _shift_rows(v, -direction * k)
            k *= 2
        return v
    out = v
    for j in range(1, m):
        out = out + _shift_rows(v, -direction * j)
    return out


def _box_sum(v, w):
    lo = w // 2
    hi = w - 1 - lo
    s = _run_sum(v, hi + 1, +1)
    if lo:
        s = s + _shift_rows(_run_sum(v, lo, -1), 1)
    return s


def _window_count(n_rows, lanes, n_pos, w):
    lo = w // 2
    hi = w - 1 - lo
    p = lax.shift_right_logical(lax.broadcasted_iota(I32, (n_rows, lanes), 0), 3)
    return (jnp.minimum(p + hi + 1, n_pos) - jnp.maximum(p - lo, 0)).astype(F32)


def _pool_kernel(z_ref, ps_ref, o_ref, v_ref, *, tiles_per_group, grid_rows, grid_cols):
    group = pl.program_id(1) // tiles_per_group
    lanes = z_ref.shape[1]
    ps = ps_ref[...]

    def pool_1d(w):
        z = z_ref[...].astype(F32)
        n = z.shape[0] // SUBLANES
        mean = _box_sum(z, w) / _window_count(z.shape[0], lanes, n, w)
        o_ref[...] = ((mean - z) * ps).astype(o_ref.dtype)

    def pool_2d(w):
        lo = w // 2
        hi = w - 1 - lo
        blk = grid_cols * SUBLANES
        cw = _window_count(blk, lanes, grid_cols, w)

        def zrow(r):
            return z_ref[pl.ds(pl.multiple_of(r * blk, blk), blk), :].astype(F32)

        v = jnp.zeros((blk, lanes), F32)
        for r in range(hi):
            v = v + zrow(r)
        v_ref[...] = v

        def body(r, carry):
            add = r + hi
            sub = r - lo - 1
            v = v_ref[...]
            v = v + jnp.where(add < grid_rows, zrow(jnp.minimum(add, grid_rows - 1)), 0.0)
            v = v - jnp.where(sub >= 0, zrow(jnp.maximum(sub, 0)), 0.0)
            v_ref[...] = v
            ch = (jnp.minimum(r + hi + 1, grid_rows) - jnp.maximum(r - lo, 0)).astype(F32)
            mean = _box_sum(v, w) / (ch * cw)
            o_ref[pl.ds(pl.multiple_of(r * blk, blk), blk), :] = ((mean - zrow(r)) * ps).astype(o_ref.dtype)
            return carry

        lax.fori_loop(0, grid_rows, body, 0)

    for g, w in enumerate(POOL_WINDOWS):
        @pl.when(group == g)
        def _(w=w):
            if grid_rows is None:
                pool_1d(w)
            else:
                pool_2d(w)


def _pool(z, pool_scale, rows_per_group, n_groups, first_block, grid_rows):
    d_pool = z.shape[1]
    gw = d_pool // len(POOL_WINDOWS)
    lanes = LANES
    blk = GRID_W * SUBLANES
    return pl.pallas_call(
        functools.partial(_pool_kernel, tiles_per_group=gw // lanes, grid_rows=grid_rows, grid_cols=GRID_W),
        grid=(n_groups, d_pool // lanes),
        in_specs=[pl.BlockSpec((rows_per_group, lanes), lambda g, j: (first_block + g, j)),
                  pl.BlockSpec((1, lanes), lambda g, j: (0, j))],
        out_specs=pl.BlockSpec((rows_per_group, lanes), lambda g, j: (g, j)),
        out_shape=jax.ShapeDtypeStruct((n_groups * rows_per_group, d_pool), BF16),
        scratch_shapes=[pltpu.VMEM((blk, lanes), F32)],
        compiler_params=_cparams(("arbitrary", "arbitrary")),
        name="pool_grid" if grid_rows is not None else "pool_seq",
    )(z, pool_scale.reshape(1, d_pool))


def _route_kernel(lg_ref, oi_ref, of_ref, cnt_ref, bc_ref, tri_ref, carry_ref, *, n_groups, per_group):
    step = pl.program_id(0)
    tt = lg_ref.shape[1]
    n_exp = n_groups * per_group

    @pl.when(step == 0)
    def _():
        r = lax.broadcasted_iota(I32, tri_ref.shape, 0)
        c = lax.broadcasted_iota(I32, tri_ref.shape, 1)
        tri_ref[...] = (r <= c).astype(BF16)
        carry_ref[...] = jnp.zeros_like(carry_ref)

    row = lambda k: lg_ref[k:k + 1, :]
    cmax = row(0)
    gi = jnp.zeros((1, tt), I32)
    for g in range(1, n_groups):
        better = row(g) > cmax
        gi = jnp.where(better, g, gi)
        cmax = jnp.where(better, row(g), cmax)
    denom = jnp.zeros((1, tt), F32)
    for g in range(n_groups):
        denom = denom + jnp.exp(row(g) - cmax)
    pg = 1.0 / denom
    fine = []
    for j in range(per_group):
        f = row(n_groups + j)
        for g in range(1, n_groups):
            f = jnp.where(gi == g, row(n_groups + g * per_group + j), f)
        fine.append(f)
    v1 = fine[0]
    i1 = jnp.zeros((1, tt), I32)
    for j in range(1, per_group):
        better = fine[j] > v1
        i1 = jnp.where(better, j, i1)
        v1 = jnp.where(better, fine[j], v1)
    v2 = jnp.full((1, tt), -jnp.inf, F32)
    i2 = jnp.zeros((1, tt), I32)
    for j in range(per_group):
        better = jnp.logical_and(i1 != j, fine[j] > v2)
        i2 = jnp.where(better, j, i2)
        v2 = jnp.where(better, fine[j], v2)
    ex = jnp.exp(v2 - v1)
    w1 = (1.0 / (1.0 + ex)) * pg
    w2 = (ex / (1.0 + ex)) * pg
    e1 = gi * per_group + i1
    e2 = gi * per_group + i2
    eid = lax.broadcasted_iota(I32, (n_exp, tt), 0)
    hit1 = eid == e1
    hit2 = eid == e2
    member = jnp.logical_or(hit1, hit2).astype(BF16)
    cb = tri_ref.shape[0]
    carry = carry_ref[...]
    cums = []
    lane = lax.broadcasted_iota(I32, bc_ref.shape, 1)
    block_ends = jnp.zeros(bc_ref.shape, F32)
    for s in range(tt // cb):
        c = jnp.dot(member[:, s * cb:(s + 1) * cb], tri_ref[...], preferred_element_type=F32) + carry
        carry = c[:, cb - 1:cb]
        cums.append(c)
        block_ends = jnp.where(lane == s, carry, block_ends)
    bc_ref[...] = block_ends.astype(I32)
    carry_ref[...] = carry
    cum = jnp.concatenate(cums, axis=1)
    rank1 = jnp.sum(jnp.where(hit1, cum, 0.0), axis=0, keepdims=True) - 1.0
    rank2 = jnp.sum(jnp.where(hit2, cum, 0.0), axis=0, keepdims=True) - 1.0
    zi = jnp.zeros((SUBLANES - 4, tt), I32)
    oi_ref[...] = jnp.concatenate([e1, e2, rank1.astype(I32), rank2.astype(I32), zi], axis=0)
    of_ref[...] = jnp.concatenate([w1, w2, jnp.zeros((of_ref.shape[0] - 2, tt), F32)], axis=0)
    cnt_ref[...] = jnp.broadcast_to(carry, cnt_ref.shape).astype(I32)


def _route(logits_t, n_groups, per_group, cb):
    nr, t = logits_t.shape
    tt = max(k for k in range(cb, min(ROUTE_TILE, t) + 1, cb) if t % k == 0)
    n_exp = n_groups * per_group
    return pl.pallas_call(
        functools.partial(_route_kernel, n_groups=n_groups, per_group=per_group),
        grid=(t // tt,),
        in_specs=[pl.BlockSpec((nr, tt), lambda i: (0, i))],
        out_specs=[pl.BlockSpec((SUBLANES, tt), lambda i: (0, i)),
                   pl.BlockSpec((LANES, tt), lambda i: (0, i)),
                   pl.BlockSpec((n_exp, LANES), lambda i: (0, 0)),
                   pl.BlockSpec((n_exp, LANES), lambda i: (0, i))],
        out_shape=[jax.ShapeDtypeStruct((SUBLANES, t), I32),
                   jax.ShapeDtypeStruct((LANES, t), F32),
                   jax.ShapeDtypeStruct((n_exp, LANES), I32),
                   jax.ShapeDtypeStruct((n_exp, LANES * (t // tt)), I32)],
        scratch_shapes=[pltpu.VMEM((cb, cb), BF16), pltpu.VMEM((n_exp, 1), F32)],
        compiler_params=_cparams(("arbitrary",)),
        name="route",
    )(logits_t)


def _tiles_for(cnt, tmx):
    return jnp.right_shift(cnt + (SEG_CHUNK + tmx - 1), tmx.bit_length() - 1)


def _plan_kernel(ri_ref, cnt_ref, bc_ref, bcp_ref, lp_ref, dst_ref, src_ref, len_ref, te_ref, nv_ref,
                 off_ref, *, tmx, tb):
    step = pl.program_id(0)
    n_exp = cnt_ref.shape[0]
    tt = ri_ref.shape[1]
    shift = tmx.bit_length() - 1
    n_tile = _tiles_for(cnt_ref[:, 0:1], tmx)

    def exclusive_prefix(col):
        run = jnp.zeros((1, 1), I32)
        parts = []
        for e in range(n_exp):
            parts.append(run)
            run = run + col[e:e + 1, :]
        return jnp.concatenate(parts, axis=0), run

    off, total = exclusive_prefix(jnp.left_shift(n_tile, shift))
    before = jnp.where(step == 0, 0, bcp_ref[:, tt // tb - 1:tt // tb])
    lane = lax.broadcasted_iota(I32, dst_ref.shape, 1)
    eid = lax.broadcasted_iota(I32, (n_exp, tb), 0)
    dst = jnp.zeros(dst_ref.shape, I32)
    src = jnp.zeros(dst_ref.shape, I32)
    length = jnp.zeros(dst_ref.shape, I32)
    rows = []
    for k in range(tt // tb):
        end = bc_ref[:, k:k + 1]
        n = end - before
        aligned = jnp.left_shift(jnp.right_shift(n + (SEG_CHUNK - 1), SEG_SHIFT), SEG_SHIFT)
        start, _ = exclusive_prefix(aligned)
        base = start - before
        sl = slice(k * tb, (k + 1) * tb)
        place = lambda e_row, r_row: jnp.sum(jnp.where(eid == e_row, base, 0), axis=0, keepdims=True) + r_row
        rows.append(jnp.concatenate([place(ri_ref[0:1, sl], ri_ref[2:3, sl]),
                                     place(ri_ref[1:2, sl], ri_ref[3:4, sl])], axis=0))
        dst = jnp.where(lane == k, off + before, dst)
        src = jnp.where(lane == k, start, src)
        length = jnp.where(lane == k, n, length)
        before = end
    lp_ref[...] = jnp.concatenate([jnp.concatenate(rows, axis=1),
                                   jnp.zeros((SUBLANES - TOP_K_FINE, tt), I32)], axis=0)
    dst_ref[...] = dst
    src_ref[...] = src
    len_ref[...] = length
    end_tile = jnp.right_shift(off, shift) + n_tile
    k = lax.broadcasted_iota(I32, (n_exp, te_ref.shape[1]), 1)
    te = jnp.sum((k >= end_tile).astype(I32), axis=0, keepdims=True)
    te_ref[...] = jnp.minimum(te, n_exp - 1)
    nv_ref[...] = jnp.broadcast_to(jnp.right_shift(total, shift), nv_ref.shape)
    off_ref[...] = jnp.broadcast_to(off, off_ref.shape)


def _plan(ri, counts, block_counts, tmx, n_tiles, tb):
    _, t = ri.shape
    n_exp = counts.shape[0]
    steps = block_counts.shape[1] // LANES
    tt = t // steps
    ntp = -(-n_tiles // LANES) * LANES
    per_step = lambda: pl.BlockSpec((n_exp, LANES), lambda i: (0, i))
    whole = lambda w: pl.BlockSpec((n_exp, w), lambda i: (0, 0))
    tab = jax.ShapeDtypeStruct((n_exp, LANES * steps), I32)
    lp, dst, src, length, te, nv, off = pl.pallas_call(
        functools.partial(_plan_kernel, tmx=tmx, tb=tb),
        grid=(steps,),
        in_specs=[pl.BlockSpec((SUBLANES, tt), lambda i: (0, i)),
                  whole(LANES),
                  per_step(),
                  pl.BlockSpec((n_exp, LANES), lambda i: (0, jnp.maximum(i - 1, 0)))],
        out_specs=[pl.BlockSpec((SUBLANES, tt), lambda i: (0, i)),
                   per_step(), per_step(), per_step(),
                   pl.BlockSpec((1, ntp), lambda i: (0, 0)),
                   pl.BlockSpec((1, LANES), lambda i: (0, 0)),
                   whole(LANES)],
        out_shape=[jax.ShapeDtypeStruct((SUBLANES, t), I32), tab, tab, tab,
                   jax.ShapeDtypeStruct((1, ntp), I32),
                   jax.ShapeDtypeStruct((1, LANES), I32),
                   jax.ShapeDtypeStruct((n_exp, LANES), I32)],
        compiler_params=_cparams(("arbitrary",)),
        name="plan",
    )(ri, counts, block_counts, block_counts)
    flat = lambda a: a.reshape(n_exp, steps, LANES)[:, :, :tt // tb].transpose(1, 2, 0).reshape(-1)
    tables = (flat(dst), flat(src), flat(length))
    return (lp[0].reshape(t // tb, tb), lp[1].reshape(t // tb, tb), tables,
            te[0, :n_tiles], nv[0, :1], off[:, 0])


def _load_positions(pos1_hbm, pos2_hbm, p1_ref, p2_ref, isem, step, n_steps):
    tb = p1_ref.shape[0] // 2
    slot = lax.rem(step, 2)

    def copies(blk, half):
        rows = pl.ds(pl.multiple_of(half * tb, tb), tb)
        return (pltpu.make_async_copy(pos1_hbm.at[blk], p1_ref.at[rows], isem.at[0, half]),
                pltpu.make_async_copy(pos2_hbm.at[blk], p2_ref.at[rows], isem.at[1, half]))

    @pl.when(step == 0)
    def _():
        for cp in copies(step, slot):
            cp.start()

    @pl.when(step + 1 < n_steps)
    def _():
        for cp in copies(step + 1, 1 - slot):
            cp.start()

    for cp in copies(step, slot):
        cp.wait()
    return slot * tb


def _token_rows(token, count, nc):
    return pl.ds(pl.multiple_of(token * nc, nc), count * nc)


def _segment_copy(stage_ref, far_hbm, near_tok, far_tok, sem, to_far, nc):
    near = stage_ref.at[_token_rows(near_tok, SEG_CHUNK, nc), :]
    far = far_hbm.at[_token_rows(far_tok, SEG_CHUNK, nc), :]
    return pltpu.make_async_copy(near, far, sem) if to_far else pltpu.make_async_copy(far, near, sem)


def _start_segments(tables, blk, n_exp, stage_ref, far_hbm, sem, to_far, nc):
    dst_ref, src_ref, len_ref = tables

    total = jnp.int32(0)
    for e in range(n_exp):
        k = blk * n_exp + e
        n_chunks = jnp.right_shift(len_ref[k] + (SEG_CHUNK - 1), SEG_SHIFT)
        near0 = src_ref[k]
        far0 = dst_ref[k]

        def one(c, carry, near0=near0, far0=far0, queue=e % 2):
            _segment_copy(stage_ref, far_hbm, near0 + c * SEG_CHUNK, far0 + c * SEG_CHUNK, sem,
                          to_far, nc).start(priority=queue)
            return carry
        lax.fori_loop(0, n_chunks, one, 0)
        total = total + n_chunks
    return total


def _wait_segments(stage_ref, far_hbm, sem, n, to_far, nc):
    copy = _segment_copy(stage_ref, far_hbm, 0, 0, sem, to_far, nc)

    def body(_, c):
        copy.wait()
        return c
    lax.fori_loop(0, n, body, 0)


def _dispatch_kernel(dst_ref, src_ref, len_ref, off_ref, cnt_ref, lp1_hbm, lp2_hbm, h2_ref, xs_hbm,
                     p1_ref, p2_ref, stage, nd_ref, zero_ref, isem, ssem, zsem, *, tmx):
    step = pl.program_id(0)
    n_steps = pl.num_programs(0)
    slot = lax.rem(step, 2)
    n_exp = off_ref.shape[0]
    nc = zero_ref.shape[0]
    n_blk = h2_ref.shape[0] // nc // SUBLANES
    tables = (dst_ref, src_ref, len_ref)

    @pl.when(step == 0)
    def _():
        for s in range(stage.shape[0]):
            stage[s] = _packed_zeros(stage.shape[1])

    p0 = _load_positions(lp1_hbm, lp2_hbm, p1_ref, p2_ref, isem, step, n_steps)

    def place(blk, c):
        j0 = blk * SUBLANES
        for s in range(SUBLANES):
            row = h2_ref[_token_rows(j0 + s, 1, nc), :]
            stage[slot, _token_rows(p1_ref[p0 + j0 + s], 1, nc), :] = row
            stage[slot, _token_rows(p2_ref[p0 + j0 + s], 1, nc), :] = row
        return c
    lax.fori_loop(0, n_blk, place, 0, unroll=2)

    @pl.when(step >= 1)
    def _():
        _wait_segments(stage.at[1 - slot], xs_hbm, ssem.at[1 - slot], nd_ref[1 - slot], True, nc)

    nd_ref[slot] = _start_segments(tables, step, n_exp, stage.at[slot], xs_hbm, ssem.at[slot], True, nc)

    @pl.when(step == n_steps - 1)
    def _():
        _wait_segments(stage.at[slot], xs_hbm, ssem.at[slot], nd_ref[slot], True, nc)
        zero_ref[...] = _packed_zeros(nc)
        shift = tmx.bit_length() - 1

        def pad_expert(e, c):
            cnt = cnt_ref[e]
            n_pad = jnp.left_shift(_tiles_for(cnt, tmx), shift) - cnt
            first = off_ref[e] + cnt
            zero_copy = lambda r: pltpu.make_async_copy(
                zero_ref, xs_hbm.at[_token_rows(first + r, 1, nc), :], zsem)

            def fill(r, c2):
                zero_copy(r).start()
                return c2
            lax.fori_loop(0, n_pad, fill, 0)

            def done(r, c2):
                zero_copy(r).wait()
                return c2
            lax.fori_loop(0, n_pad, done, 0)
            return c
        lax.fori_loop(0, off_ref.shape[0], pad_expert, 0)


def _stage_rows(tb, n_exp):
    return TOP_K_FINE * tb + n_exp * SEG_CHUNK


def _dispatch(tables, offsets, counts, lp1, lp2, h2p, p_max, tmx):
    nb, tb = lp1.shape
    nc = h2p.shape[0] // (nb * tb)
    n_exp = offsets.shape[0]
    hbm = lambda: pl.BlockSpec(memory_space=pl.ANY)
    grid_spec = pltpu.PrefetchScalarGridSpec(
        num_scalar_prefetch=5,
        grid=(nb,),
        in_specs=[hbm(), hbm(),
                  pl.BlockSpec((tb * nc, LANES), lambda i, *_: (i, 0))],
        out_specs=hbm(),
        scratch_shapes=[pltpu.SMEM((2 * tb,), I32),
                        pltpu.SMEM((2 * tb,), I32),
                        pltpu.VMEM((2, _stage_rows(tb, n_exp) * nc, LANES), U32),
                        pltpu.SMEM((2,), I32),
                        pltpu.VMEM((nc, LANES), U32),
                        pltpu.SemaphoreType.DMA((2, 2)),
                        pltpu.SemaphoreType.DMA((2,)),
                        pltpu.SemaphoreType.DMA(())])
    return pl.pallas_call(
        functools.partial(_dispatch_kernel, tmx=tmx),
        grid_spec=grid_spec,
        out_shape=jax.ShapeDtypeStruct((p_max * nc, LANES), U32),
        compiler_params=_cparams(("arbitrary",)),
        name="dispatch",
    )(*tables, offsets, counts, lp1, lp2, h2p)


def _expert_kernel(te_ref, nv_ref, x_ref, w1_ref, w3_ref, w2_ref, y_ref, pack_ref, *, tmx):
    @pl.when(pl.program_id(0) < nv_ref[0])
    def _():
        lo, hi = _load_packed_tokens(x_ref, tmx, pack_ref)
        lo = lo.astype(BF16)
        hi = hi.astype(BF16)
        half = lo.shape[1]
        mm = lambda w_ref: (jnp.dot(lo, w_ref[0, :half, :].astype(BF16), preferred_element_type=F32)
                            + jnp.dot(hi, w_ref[0, half:, :].astype(BF16), preferred_element_type=F32))
        h1 = mm(w1_ref)
        h3 = mm(w3_ref)
        hh = (h1 * jax.nn.sigmoid(h1) * h3).astype(BF16)
        y = jnp.dot(hh, w2_ref[0].astype(BF16), preferred_element_type=F32)
        _store_packed_tokens(y_ref, y[:, :half], y[:, half:], pack_ref)


def _expert_mlp(xs, tile_expert, n_valid, w1, w3, w2, tmx):
    d, de = w1.shape[1], w1.shape[2]
    nc = d // 2 // LANES
    p = xs.shape[0] // nc
    tile = lambda i, te, nv: (jnp.minimum(i, nv[0] - 1), 0)
    grid_spec = pltpu.PrefetchScalarGridSpec(
        num_scalar_prefetch=2,
        grid=(p // tmx,),
        in_specs=[pl.BlockSpec((tmx * nc, LANES), tile),
                  pl.BlockSpec((1, d, de), lambda i, te, nv: (te[i], 0, 0)),
                  pl.BlockSpec((1, d, de), lambda i, te, nv: (te[i], 0, 0)),
                  pl.BlockSpec((1, de, d), lambda i, te, nv: (te[i], 0, 0))],
        out_specs=pl.BlockSpec((tmx * nc, LANES), tile),
        scratch_shapes=[pltpu.VMEM((nc, 2 * tmx, LANES), F32)])
    return pl.pallas_call(
        functools.partial(_expert_kernel, tmx=tmx),
        grid_spec=grid_spec,
        out_shape=jax.ShapeDtypeStruct((p * nc, LANES), U32),
        compiler_params=_cparams(("arbitrary",)),
        name="expert_mlp",
    )(tile_expert, n_valid, xs, w1, w3, w2)


def _combine_kernel(dst_ref, src_ref, len_ref, lp1_hbm, lp2_hbm, ys_hbm, x1_ref, rf_ref, g2_ref, fg_ref,
                    ylat_hbm, yctx_hbm, p1_ref, p2_ref, stage, gbuf, pack_ref, obuf, oslab, nd_ref, isem, csem, osem,
                    *, geom, n_exp):
    step = pl.program_id(0)
    n_steps = pl.num_programs(0)
    tm = geom.tm
    n_blk = tm // SUBLANES
    slot = lax.rem(step, 2)
    nc = gbuf.shape[1] // tm
    tables = (dst_ref, src_ref, len_ref)

    def fetch(blk, gs):
        nd_ref[gs] = _start_segments(tables, blk, n_exp, stage.at[gs], ys_hbm, csem.at[gs], False, nc)

    @pl.when(step == 0)
    def _():
        fetch(step, slot)

    @pl.when(step + 1 < n_steps)
    def _():
        fetch(step + 1, 1 - slot)

    _wait_segments(stage.at[slot], ys_hbm, csem.at[slot], nd_ref[slot], False, nc)
    p0 = _load_positions(lp1_hbm, lp2_hbm, p1_ref, p2_ref, isem, step, n_steps)

    def pick(blk, c):
        j0 = blk * SUBLANES
        for s in range(SUBLANES):
            rows = _token_rows(j0 + s, 1, nc)
            gbuf[0, rows, :] = stage[slot, _token_rows(p1_ref[p0 + j0 + s], 1, nc), :]
            gbuf[1, rows, :] = stage[slot, _token_rows(p2_ref[p0 + j0 + s], 1, nc), :]
        return c
    lax.fori_loop(0, n_blk, pick, 0, unroll=2)

    wt = rf_ref[...].T
    a_lo, a_hi = _load_packed_tokens(gbuf.at[0], tm, pack_ref.at[0])
    b_lo, b_hi = _load_packed_tokens(gbuf.at[1], tm, pack_ref.at[1])
    w1 = wt[:, 0:1]
    w2 = wt[:, 1:2]
    moe = jnp.concatenate([w1 * a_lo + w2 * b_lo, w1 * a_hi + w2 * b_hi], axis=1)
    x = x1_ref[...] + _gate(moe, g2_ref[0])
    out = _rms(x, fg_ref[...])

    start = lambda cp: cp.start()
    wait = lambda cp: cp.wait()
    put = functools.partial(_tile_copies, ylat_hbm, yctx_hbm, geom=geom, to_rows=False, by_sequence=True)

    @pl.when(step >= 2)
    def _():
        put(obuf.at[slot], osem.at[slot], step - 2, fn=wait)

    tt = tm // SUBLANES
    for c in range(oslab.shape[0]):
        oslab[c] = out[:, c * LANES:(c + 1) * LANES]
    for b in range(SUBLANES):
        for c in range(oslab.shape[0]):
            obuf[slot, b, :, c * LANES:(c + 1) * LANES] = oslab[c, pl.ds(b, tt, stride=SUBLANES), :]
    put(obuf.at[slot], osem.at[slot], step, fn=start)

    @pl.when(step == n_steps - 1)
    def _():
        @pl.when(n_steps > 1)
        def _():
            put(obuf.at[1 - slot], osem.at[1 - slot], step - 1, fn=wait)
        put(obuf.at[slot], osem.at[slot], step, fn=wait)


def _combine(tables, lp1, lp2, ys, x1, rf, modtab, final_g, lat_shape, ctx_shape, geom):
    t, d = x1.shape
    tm = geom.tm
    nc = d // 2 // LANES
    n_exp = tables[0].shape[0] // (t // tm)
    mset = lambda i, *_: jnp.where(i < geom.n_lat_tiles, 0, 1)
    hbm = lambda: pl.BlockSpec(memory_space=pl.ANY)
    grid_spec = pltpu.PrefetchScalarGridSpec(
        num_scalar_prefetch=3,
        grid=(t // tm,),
        in_specs=[hbm(), hbm(), hbm(),
                  pl.BlockSpec((tm, d), lambda i, *_: (i, 0)),
                  pl.BlockSpec((LANES, tm), lambda i, *_: (0, i)),
                  pl.BlockSpec((1, SUBLANES, d), lambda i, *_: (mset(i), 0, 5)),
                  pl.BlockSpec((1, d), lambda i, *_: (0, 0))],
        out_specs=[hbm(), hbm()],
        scratch_shapes=[pltpu.SMEM((2 * tm,), I32),
                        pltpu.SMEM((2 * tm,), I32),
                        pltpu.VMEM((2, _stage_rows(tm, n_exp) * nc, LANES), U32),
                        pltpu.VMEM((TOP_K_FINE, tm * nc, LANES), U32),
                        pltpu.VMEM((TOP_K_FINE, nc, 2 * tm, LANES), F32),
                        pltpu.VMEM((2, SUBLANES, tm // SUBLANES, d), F32),
                        pltpu.VMEM((d // LANES, tm, LANES), F32),
                        pltpu.SMEM((2,), I32),
                        pltpu.SemaphoreType.DMA((2, 2)),
                        pltpu.SemaphoreType.DMA((2,)),
                        pltpu.SemaphoreType.DMA((2,))])
    return pl.pallas_call(
        functools.partial(_combine_kernel, geom=geom, n_exp=n_exp),
        grid_spec=grid_spec,
        out_shape=[jax.ShapeDtypeStruct(lat_shape, F32), jax.ShapeDtypeStruct(ctx_shape, F32)],
        compiler_params=_cparams(("arbitrary",)),
        name="combine",
    )(*tables, lp1, lp2, ys, x1, rf, modtab, final_g.reshape(1, d))


def _tile_meta(groups, tm):
    rows, first, last, grp = [], [], [], []
    blk = 0
    for g, r in enumerate(groups):
        nc = r // tm
        for c in range(nc):
            rows.append(blk + c)
            first.append(int(c == 0))
            last.append(int(c == nc - 1))
            grp.append(g)
        blk += nc
    fwd = (np.array(rows, np.int32), np.array([first, last, grp], np.int32))
    order = []
    blk = 0
    for r in groups:
        nc = r // tm
        order.extend(range(blk + nc - 1, blk - 1, -1))
        blk += nc
    order = np.array(order)
    bwd = (fwd[0][order], fwd[1][:, order])
    return fwd, bwd


def kernel(x_prompt, x_sample, state_lru, c, c_ctx, w_mod, b_mod, norm1_g, w_in, conv_w, conv_b, lru_wa, lru_ba, lru_wx, lru_bx, lru_lambda, pool_w, pool_scale, w_out, norm2_g, router_coarse_w, router_coarse_b, router_fine_w, router_fine_b, exp_w1, exp_w3, exp_w2, final_norm_g):
    bp, sp, d = x_prompt.shape
    bs, ss, _ = x_sample.shape
    d_lru = lru_lambda.shape[-1]
    heads, bw = lru_wa.shape[2], lru_wa.shape[3]
    n_groups, per_group = router_fine_w.shape[2], router_fine_w.shape[3]
    n_exp = n_groups * per_group
    assert w_mod.shape[0] == 1 and bs == SUBLANES and bp % SUBLANES == 0 and ss % GRID_W == 0
    assert EXPERT_TILE & (EXPERT_TILE - 1) == 0
    n_lat_groups, n_ctx_groups = bs // SUBLANES, bp // SUBLANES
    lat_rows, ctx_rows = ss * SUBLANES, sp * SUBLANES
    assert (n_lat_groups * lat_rows) % ctx_rows == 0
    tm = min(TOKEN_TILE, ctx_rows, lat_rows)
    geom = _Geom(tm=tm, n_lat_tiles=n_lat_groups * lat_rows // tm, lat_chunks=lat_rows // tm,
                 ctx_chunks=ctx_rows // tm)
    n_rows = n_lat_groups * lat_rows + n_ctx_groups * ctx_rows
    groups = [lat_rows] * n_lat_groups + [ctx_rows] * n_ctx_groups
    (f_rows, f_flags), (b_rows, b_flags) = _tile_meta(groups, tm)

    cond = jnp.zeros((2 * SUBLANES, d), F32).at[:bs].set(c).at[bs].set(c_ctx)
    mod = _modulation(cond, w_mod[0], b_mod[0])
    modtab = jnp.stack([mod[:SUBLANES], jnp.broadcast_to(mod[SUBLANES], (SUBLANES, mod.shape[1]))])
    h0_lat = state_lru[:, 0].reshape(n_lat_groups, SUBLANES, 2, d_lru)
    h0 = jnp.concatenate([h0_lat, jnp.zeros((n_ctx_groups, SUBLANES, 2, d_lru), F32)], axis=0)
    h0 = h0.transpose(0, 2, 1, 3)

    xa, ga, z = _input_projection(x_sample, x_prompt, modtab, norm1_g[0], w_in[0],
                                  pool_w[0].astype(BF16), n_rows, geom)

    def gate_weights(direction):
        wg = jnp.concatenate([lru_wa[0, direction], lru_wx[0, direction]], axis=-1).astype(BF16)
        bg = jnp.concatenate([lru_ba[0, direction].reshape(heads, 1, bw),
                              lru_bx[0, direction].reshape(heads, 1, bw)], axis=-1)
        return wg, bg

    log_decay = jax.nn.log_sigmoid(lru_lambda[0])
    wg_f, bg_f = gate_weights(0)
    wg_b, bg_b = gate_weights(1)
    hf, hf_last = _forward_scan(xa, (jnp.asarray(f_rows), jnp.asarray(f_flags)), conv_w[0], conv_b[0],
                                wg_f, bg_f, log_decay[0], h0[:, 0], tm)

    yb_lat = _pool(z, pool_scale[0], lat_rows, n_lat_groups, 0, lat_rows // (GRID_W * SUBLANES))
    yb_ctx = _pool(z, pool_scale[0], ctx_rows, n_ctx_groups, n_lat_groups * lat_rows // ctx_rows, None)

    n_logits = n_groups + n_exp
    rw = jnp.concatenate([router_coarse_w[0], router_fine_w[0].reshape(d, n_exp)], axis=1)
    rwt = jnp.zeros((LANES, d), BF16).at[:n_logits].set(rw.T.astype(BF16))
    rb = jnp.zeros((LANES, 1), F32).at[:n_logits, 0].set(
        jnp.concatenate([router_coarse_b[0], router_fine_b[0].reshape(n_exp)]))
    x1, h2p, logits_t, hb_last = _backward_scan_mix(
        xa, ga, hf, yb_lat, yb_ctx, x_sample, x_prompt, modtab, (jnp.asarray(b_rows), jnp.asarray(b_flags)),
        conv_w[0], conv_b[0], wg_b, bg_b, log_decay[1], h0[:, 1], w_out[0].astype(BF16), norm2_g[0],
        rwt, rb, geom)

    ri, rf, counts, block_counts = _route(logits_t, n_groups, per_group, tm)
    n_tiles = -(-(TOP_K_FINE * n_rows + n_exp * (EXPERT_TILE + SEG_CHUNK)) // EXPERT_TILE)
    lp1, lp2, tables, tile_expert, n_valid, offsets = _plan(ri, counts, block_counts, EXPERT_TILE, n_tiles, tm)
    xs = _dispatch(tables, offsets, counts[:, 0], lp1, lp2, h2p, n_tiles * EXPERT_TILE, EXPERT_TILE)
    ys = _expert_mlp(xs, tile_expert, n_valid, exp_w1[0], exp_w3[0], exp_w2[0], EXPERT_TILE)
    y_sample, y_prompt = _combine(tables, lp1, lp2, ys, x1, rf, modtab, final_norm_g, x_sample.shape,
                                  x_prompt.shape, geom)

    st = jnp.stack([hf_last[n_lat_groups:], hb_last[n_lat_groups:]], axis=2)
    state_new = st.reshape(bp, 1, 2, d_lru).astype(x_prompt.dtype)
    return (y_prompt, y_sample, state_new)
```

```python
import functools
from typing import NamedTuple

import numpy as np
import jax
import jax.numpy as jnp
from jax import lax
from jax.experimental import pallas as pl
from jax.experimental.pallas import tpu as pltpu

GRID_W = 64
CONV_W = 4
RG_C = 8.0
POOL_WINDOWS = (2, 4, 8, 16)
TOP_K_FINE = 2
EPS = 1e-6
EXPM1_SERIES_BELOW = 0.125

SUBLANES = 8
LANES = 128
BF16_ROWS = 16
TOKEN_TILE = 512
EXPERT_TILE = 512
ROUTE_TILE = 2048
SEG_SHIFT = 3
SEG_CHUNK = 1 << SEG_SHIFT
MOD_COL_TILE = 1024
WEIGHT_STAGE_ROWS = 256
VMEM_LIMIT = 60 * 1024 * 1024

F32 = jnp.float32
BF16 = jnp.bfloat16
U32 = jnp.uint32
I32 = jnp.int32


class _Geom(NamedTuple):
    tm: int
    n_lat_tiles: int
    lat_chunks: int
    ctx_chunks: int


def _cparams(sem):
    return pltpu.CompilerParams(dimension_semantics=sem, vmem_limit_bytes=VMEM_LIMIT)


def _per_sequence(y, m):
    rows, d = y.shape
    return y.reshape(rows // SUBLANES, SUBLANES, d), m[None]


def _modulate(y, scale, shift):
    y3, sc = _per_sequence(y, scale)
    _, sh = _per_sequence(y, shift)
    return (y3 * (1.0 + sc) + sh).reshape(y.shape)


def _gate(y, g):
    y3, g3 = _per_sequence(y, g)
    return (y3 * g3).reshape(y.shape)


def _rms(x, g):
    ms = jnp.mean(x * x, axis=-1, keepdims=True)
    return x * lax.rsqrt(ms + EPS) * g


def _store_packed_tokens(ref, lo, hi, scratch):
    n, words = lo.shape
    nc = words // LANES
    for c in range(nc):
        sl = slice(c * LANES, (c + 1) * LANES)
        scratch[c, pl.ds(0, n, stride=2), :] = lo[:, sl]
        scratch[c, pl.ds(1, n, stride=2), :] = hi[:, sl]
        ref[pl.ds(c, n, stride=nc), :] = pltpu.bitcast(scratch[c].astype(BF16), U32)


def _packed_zeros(rows):
    return pltpu.bitcast(jnp.zeros((2 * rows, LANES), BF16), U32)


def _load_packed_tokens(ref, n, scratch):
    nc = ref.shape[0] // n
    los, his = [], []
    for c in range(nc):
        scratch[c] = pltpu.bitcast(ref[pl.ds(c, n, stride=nc), :], BF16).astype(F32)
        los.append(scratch[c, pl.ds(0, n, stride=2), :])
        his.append(scratch[c, pl.ds(1, n, stride=2), :])
    return jnp.concatenate(los, axis=1), jnp.concatenate(his, axis=1)


def _tile_copies(lat_hbm, ctx_hbm, buf, sem, tile, geom, to_rows, fn):
    tt = geom.tm // SUBLANES

    def run(hbm, k, chunks):
        g = lax.div(k, jnp.int32(chunks))
        c = lax.rem(k, jnp.int32(chunks))
        for b in range(SUBLANES):
            h = hbm.at[g * SUBLANES + b, pl.ds(c * tt, tt), :]
            v = buf.at[:, b, :]
            fn(pltpu.make_async_copy(h, v, sem) if to_rows else pltpu.make_async_copy(v, h, sem))

    @pl.when(tile < geom.n_lat_tiles)
    def _():
        run(lat_hbm, tile, geom.lat_chunks)

    @pl.when(tile >= geom.n_lat_tiles)
    def _():
        run(ctx_hbm, tile - geom.n_lat_tiles, geom.ctx_chunks)


def _fetch_rows(lat_hbm, ctx_hbm, xbuf, sem, step, n_steps, tile_of, geom):
    slot = lax.rem(step, 2)
    start = lambda cp: cp.start()
    wait = lambda cp: cp.wait()

    @pl.when(step == 0)
    def _():
        _tile_copies(lat_hbm, ctx_hbm, xbuf.at[0], sem.at[0], tile_of(jnp.int32(0)), geom, True, start)

    @pl.when(step + 1 < n_steps)
    def _():
        nxt = 1 - slot
        _tile_copies(lat_hbm, ctx_hbm, xbuf.at[nxt], sem.at[nxt], tile_of(step + 1), geom, True, start)

    _tile_copies(lat_hbm, ctx_hbm, xbuf.at[slot], sem.at[slot], tile_of(step), geom, True, wait)
    tt, _, d = xbuf.shape[1:]
    return xbuf[slot].reshape(tt * SUBLANES, d)


def _mod_kernel(c_ref, w_ref, b_ref, o_ref):
    c = c_ref[...]
    s = c * jax.nn.sigmoid(c)
    o_ref[...] = jnp.dot(s.astype(BF16), w_ref[...].astype(BF16),
                         preferred_element_type=F32) + b_ref[...]


def _modulation(cond, w_mod, b_mod):
    rows, d = cond.shape
    n = w_mod.shape[1]
    tn = min(MOD_COL_TILE, n)
    return pl.pallas_call(
        _mod_kernel,
        grid=(n // tn,),
        in_specs=[pl.BlockSpec((rows, d), lambda j: (0, 0)),
                  pl.BlockSpec((d, tn), lambda j: (0, j)),
                  pl.BlockSpec((1, tn), lambda j: (0, j))],
        out_specs=pl.BlockSpec((rows, tn), lambda j: (0, j)),
        out_shape=jax.ShapeDtypeStruct((rows, n), F32),
        compiler_params=_cparams(("arbitrary",)),
        name="modulation",
    )(cond, w_mod, b_mod.reshape(1, n))


def _round_weights(w_hbm, wbf_ref, stage, sem):
    rows = stage.shape[1]
    n = w_hbm.shape[0] // rows
    copy = lambda r: pltpu.make_async_copy(w_hbm.at[pl.ds(r * rows, rows), :], stage.at[r % 2], sem.at[r % 2])
    copy(0).start()
    for r in range(n):
        if r + 1 < n:
            copy(r + 1).start()
        copy(r).wait()
        wbf_ref[r * rows:(r + 1) * rows, :] = stage[r % 2].astype(wbf_ref.dtype)


def _proj_kernel(lat_hbm, ctx_hbm, sh_ref, sc_ref, g_ref, win_hbm, pw_ref, xa_ref, ga_ref, z_ref,
                 xbuf, sem, wbf_ref, wstage, wsem, *, d_lru, gw, geom):
    step = pl.program_id(0)

    @pl.when(step == 0)
    def _():
        _round_weights(win_hbm, wbf_ref, wstage, wsem)

    x = _fetch_rows(lat_hbm, ctx_hbm, xbuf, sem, step, pl.num_programs(0), lambda s: s, geom)
    h = _modulate(_rms(x, g_ref[...]), sc_ref[0], sh_ref[0])
    proj = jnp.dot(h.astype(BF16), wbf_ref[...], preferred_element_type=F32)
    xa_ref[...] = proj[:, :d_lru].astype(BF16)
    ga_ref[...] = proj[:, d_lru:2 * d_lru].astype(BF16)
    for g in range(pw_ref.shape[0]):
        lo = 2 * d_lru + g * gw
        z_ref[:, g * gw:(g + 1) * gw] = jnp.dot(
            proj[:, lo:lo + gw].astype(BF16), pw_ref[g], preferred_element_type=F32).astype(BF16)


def _input_projection(x_lat, x_ctx, modtab, norm_g, w_in, pool_w, n_rows, geom):
    d = x_lat.shape[-1]
    tm = geom.tm
    d_pool = pool_w.shape[0] * pool_w.shape[1]
    d_lru = (w_in.shape[1] - d_pool) // 2
    mset = lambda i: jnp.where(i < geom.n_lat_tiles, 0, 1)
    const = dict(pipeline_mode=pl.Buffered(1))
    return pl.pallas_call(
        functools.partial(_proj_kernel, d_lru=d_lru, gw=pool_w.shape[1], geom=geom),
        grid=(n_rows // tm,),
        in_specs=[pl.BlockSpec(memory_space=pl.ANY),
                  pl.BlockSpec(memory_space=pl.ANY),
                  pl.BlockSpec((1, SUBLANES, d), lambda i: (mset(i), 0, 0)),
                  pl.BlockSpec((1, SUBLANES, d), lambda i: (mset(i), 0, 1)),
                  pl.BlockSpec((1, d), lambda i: (0, 0)),
                  pl.BlockSpec(memory_space=pl.ANY),
                  pl.BlockSpec(pool_w.shape, lambda i: (0, 0, 0), **const)],
        out_specs=[pl.BlockSpec((tm, d_lru), lambda i: (i, 0)),
                   pl.BlockSpec((tm, d_lru), lambda i: (i, 0)),
                   pl.BlockSpec((tm, d_pool), lambda i: (i, 0))],
        out_shape=[jax.ShapeDtypeStruct((n_rows, d_lru), BF16),
                   jax.ShapeDtypeStruct((n_rows, d_lru), BF16),
                   jax.ShapeDtypeStruct((n_rows, d_pool), BF16)],
        scratch_shapes=[pltpu.VMEM((2, tm // SUBLANES, SUBLANES, d), F32),
                        pltpu.SemaphoreType.DMA((2,)),
                        pltpu.VMEM(w_in.shape, BF16),
                        pltpu.VMEM((2, min(WEIGHT_STAGE_ROWS, d), w_in.shape[1]), F32),
                        pltpu.SemaphoreType.DMA((2,))],
        compiler_params=_cparams(("arbitrary",)),
        name="input_projection",
    )(x_lat, x_ctx, modtab, modtab, norm_g.reshape(1, d), w_in, pool_w)


def _fill_ext(ext_ref, prev_ref, main_ref, next_ref, first, last):
    tm = main_ref.shape[0]
    prev = prev_ref[...].astype(F32)
    nxt = next_ref[...].astype(F32)
    ext_ref[0:BF16_ROWS, :] = jnp.where(first, 0.0, prev)
    ext_ref[BF16_ROWS:BF16_ROWS + tm, :] = main_ref[...].astype(F32)
    ext_ref[BF16_ROWS + tm:, :] = jnp.where(last, 0.0, nxt)


def _one_minus_exp(y, exp_y):
    p = 1.0 / 120.0
    for c in (1.0 / 24.0, 1.0 / 6.0, 0.5, 1.0):
        p = p * y + c
    return jnp.where(y > -EXPM1_SERIES_BELOW, -y * p, 1.0 - exp_y)


def _sqrt_nonneg(q):
    return jnp.where(q > 0.0, q * lax.rsqrt(q), 0.0)


def _decay_and_input(ext_ref, cw_ref, cb_ref, wg_ref, bg_ref, lam_ref, a_ref, u_ref, tm):
    heads, bw = wg_ref.shape[0], wg_ref.shape[1]
    for hd in range(heads):
        sl = slice(hd * bw, (hd + 1) * bw)
        xc = cb_ref[:, sl]
        for k in range(CONV_W):
            xc = xc + cw_ref[k:k + 1, sl] * ext_ref[SUBLANES * k:SUBLANES * k + tm, sl]
        g = jnp.dot(xc.astype(BF16), wg_ref[hd], preferred_element_type=F32) + bg_ref[hd]
        r = jax.nn.sigmoid(g[:, :bw])
        ig = jax.nn.sigmoid(g[:, bw:])
        log_a = (RG_C * r) * lam_ref[:, sl]
        a = jnp.exp(log_a)
        a_ref[:, sl] = a
        u_ref[:, sl] = _sqrt_nonneg(_one_minus_exp(2.0 * log_a, a * a)) * (ig * xc)


def _scan(a_ref, u_ref, h, tm, reverse):
    nblk = tm // SUBLANES

    def body(s, h):
        j = (nblk - 1 - s) if reverse else s
        rows = pl.ds(pl.multiple_of(j * SUBLANES, SUBLANES), SUBLANES)
        h = a_ref[rows, :] * h + u_ref[rows, :]
        u_ref[rows, :] = h
        return h

    return lax.fori_loop(0, nblk, body, h, unroll=8)


def _halo_specs(tm, d_lru, n_rows):
    per = tm // BF16_ROWS
    last_blk = n_rows // BF16_ROWS - 1
    return [pl.BlockSpec((BF16_ROWS, d_lru), lambda i, tr, fl: (jnp.maximum(tr[i] * per - 1, 0), 0)),
            pl.BlockSpec((tm, d_lru), lambda i, tr, fl: (tr[i], 0)),
            pl.BlockSpec((BF16_ROWS, d_lru), lambda i, tr, fl: (jnp.minimum((tr[i] + 1) * per, last_blk), 0))]


def _fwd_kernel(tr_ref, fl_ref, prev_ref, main_ref, next_ref, cw_ref, cb_ref, wg_ref, bg_ref, lam_ref,
                h0_ref, hf_ref, hlast_ref, ext_ref, a_ref, u_ref, h_ref):
    i = pl.program_id(0)
    tm = main_ref.shape[0]
    first = fl_ref[0, i] == 1
    last = fl_ref[1, i] == 1
    _fill_ext(ext_ref, prev_ref, main_ref, next_ref, first, last)
    _decay_and_input(ext_ref, cw_ref, cb_ref, wg_ref, bg_ref, lam_ref, a_ref, u_ref, tm)

    @pl.when(first)
    def _():
        h_ref[...] = h0_ref[0]

    h = _scan(a_ref, u_ref, h_ref[...], tm, reverse=False)
    h_ref[...] = h
    hlast_ref[0] = h
    hf_ref[...] = u_ref[...].astype(BF16)


def _forward_scan(xa, meta, conv_w, conv_b, wg, bg, lam, h0, tm):
    t, d_lru = xa.shape
    tile_row, flags = meta
    ngrp = h0.shape[0]
    const = dict(pipeline_mode=pl.Buffered(1))
    grid_spec = pltpu.PrefetchScalarGridSpec(
        num_scalar_prefetch=2,
        grid=(t // tm,),
        in_specs=_halo_specs(tm, d_lru, t) + [
            pl.BlockSpec(conv_w.shape, lambda i, tr, fl: (0, 0)),
            pl.BlockSpec((1, d_lru), lambda i, tr, fl: (0, 0)),
            pl.BlockSpec(wg.shape, lambda i, tr, fl: (0, 0, 0), **const),
            pl.BlockSpec(bg.shape, lambda i, tr, fl: (0, 0, 0)),
            pl.BlockSpec((1, d_lru), lambda i, tr, fl: (0, 0)),
            pl.BlockSpec((1, SUBLANES, d_lru), lambda i, tr, fl: (fl[2, i], 0, 0))],
        out_specs=[pl.BlockSpec((tm, d_lru), lambda i, tr, fl: (tr[i], 0)),
                   pl.BlockSpec((1, SUBLANES, d_lru), lambda i, tr, fl: (fl[2, i], 0, 0))],
        scratch_shapes=[pltpu.VMEM((tm + 2 * BF16_ROWS, d_lru), F32),
                        pltpu.VMEM((tm, d_lru), F32),
                        pltpu.VMEM((tm, d_lru), F32),
                        pltpu.VMEM((SUBLANES, d_lru), F32)])
    return pl.pallas_call(
        _fwd_kernel,
        grid_spec=grid_spec,
        out_shape=[jax.ShapeDtypeStruct((t, d_lru), BF16),
                   jax.ShapeDtypeStruct((ngrp, SUBLANES, d_lru), F32)],
        compiler_params=_cparams(("arbitrary",)),
        name="forward_scan",
    )(tile_row, flags, xa, xa, xa, conv_w, conv_b.reshape(1, d_lru), wg, bg, lam.reshape(1, d_lru), h0)


def _bwd_kernel(tr_ref, fl_ref, prev_ref, main_ref, next_ref, cw_ref, cb_ref, wg_ref, bg_ref, lam_ref,
                h0_ref, hf_ref, ga_ref, ybl_ref, ybc_ref, lat_hbm, ctx_hbm, g1_ref, sh2_ref, sc2_ref, wout_ref,
                n2_ref, rwt_ref, rb_ref,
                x1_ref, h2_ref, lg_ref, hlast_ref,
                ext_ref, a_ref, u_ref, h_ref, cat_ref, xbuf, xsem, pack_ref, *, geom):
    i = pl.program_id(0)
    n_steps = pl.num_programs(0)
    tm, d_lru = main_ref.shape
    x = _fetch_rows(lat_hbm, ctx_hbm, xbuf, xsem, i, n_steps,
                    lambda s: tr_ref[jnp.minimum(s, n_steps - 1)], geom)
    first = fl_ref[0, i] == 1
    last = fl_ref[1, i] == 1
    _fill_ext(ext_ref, prev_ref, main_ref, next_ref, first, last)
    _decay_and_input(ext_ref, cw_ref, cb_ref, wg_ref, bg_ref, lam_ref, a_ref, u_ref, tm)

    @pl.when(last)
    def _():
        h_ref[...] = h0_ref[0]

    h = _scan(a_ref, u_ref, h_ref[...], tm, reverse=True)
    h_ref[...] = h
    hlast_ref[0] = h

    ga = ga_ref[...].astype(F32)
    ya = (hf_ref[...].astype(F32) + u_ref[...]) * jax.nn.gelu(ga)
    cat_ref[:, :d_lru] = ya.astype(BF16)
    cat_ref[:, d_lru:] = jnp.where(tr_ref[i] < geom.n_lat_tiles, ybl_ref[...], ybc_ref[...])
    mix = jnp.dot(cat_ref[...], wout_ref[...], preferred_element_type=F32)
    x1 = x + _gate(mix, g1_ref[0])
    x1_ref[...] = x1
    h2 = _modulate(_rms(x1, n2_ref[...]), sc2_ref[0], sh2_ref[0])
    half = h2.shape[1] // 2
    _store_packed_tokens(h2_ref, h2[:, :half], h2[:, half:], pack_ref)
    lg_ref[...] = lax.dot_general(rwt_ref[...], h2.astype(BF16), (((1,), (1,)), ((), ())),
                                  preferred_element_type=F32) + rb_ref[...]


def _backward_scan_mix(xa, ga, hf, yb_lat, yb_ctx, x_lat, x_ctx, modtab, meta, conv_w, conv_b, wg, bg, lam, h0, w_out,
                       norm2_g, rwt, rb, geom):
    t, d_lru = xa.shape
    d = x_lat.shape[-1]
    d_pool = yb_lat.shape[1]
    n_lat, n_ctx = yb_lat.shape[0] // geom.tm, yb_ctx.shape[0] // geom.tm
    tm = geom.tm
    tile_row, flags = meta
    ngrp = h0.shape[0]
    nr = rwt.shape[0]
    const = dict(pipeline_mode=pl.Buffered(1))
    mset = lambda i, tr, fl: jnp.where(tr[i] < geom.n_lat_tiles, 0, 1)
    row = lambda i, tr, fl: (tr[i], 0)
    mod = lambda col: pl.BlockSpec((1, SUBLANES, d), lambda i, tr, fl: (mset(i, tr, fl), 0, col))
    grid_spec = pltpu.PrefetchScalarGridSpec(
        num_scalar_prefetch=2,
        grid=(t // tm,),
        in_specs=_halo_specs(tm, d_lru, t) + [
            pl.BlockSpec(conv_w.shape, lambda i, tr, fl: (0, 0)),
            pl.BlockSpec((1, d_lru), lambda i, tr, fl: (0, 0)),
            pl.BlockSpec(wg.shape, lambda i, tr, fl: (0, 0, 0), **const),
            pl.BlockSpec(bg.shape, lambda i, tr, fl: (0, 0, 0)),
            pl.BlockSpec((1, d_lru), lambda i, tr, fl: (0, 0)),
            pl.BlockSpec((1, SUBLANES, d_lru), lambda i, tr, fl: (fl[2, i], 0, 0)),
            pl.BlockSpec((tm, d_lru), row),
            pl.BlockSpec((tm, d_lru), row),
            pl.BlockSpec((tm, d_pool), lambda i, tr, fl: (jnp.minimum(tr[i], n_lat - 1), 0)),
            pl.BlockSpec((tm, d_pool), lambda i, tr, fl: (jnp.clip(tr[i] - n_lat, 0, n_ctx - 1), 0)),
            pl.BlockSpec(memory_space=pl.ANY),
            pl.BlockSpec(memory_space=pl.ANY),
            mod(2), mod(3), mod(4),
            pl.BlockSpec(w_out.shape, lambda i, tr, fl: (0, 0), **const),
            pl.BlockSpec((1, d), lambda i, tr, fl: (0, 0)),
            pl.BlockSpec(rwt.shape, lambda i, tr, fl: (0, 0)),
            pl.BlockSpec((nr, 1), lambda i, tr, fl: (0, 0))],
        out_specs=[pl.BlockSpec((tm, d), row),
                   pl.BlockSpec((tm * (d // 2) // LANES, LANES), row),
                   pl.BlockSpec((nr, tm), lambda i, tr, fl: (0, tr[i])),
                   pl.BlockSpec((1, SUBLANES, d_lru), lambda i, tr, fl: (fl[2, i], 0, 0))],
        scratch_shapes=[pltpu.VMEM((tm + 2 * BF16_ROWS, d_lru), F32),
                        pltpu.VMEM((tm, d_lru), F32),
                        pltpu.VMEM((tm, d_lru), F32),
                        pltpu.VMEM((SUBLANES, d_lru), F32),
                        pltpu.VMEM((tm, d_lru + d_pool), BF16),
                        pltpu.VMEM((2, tm // SUBLANES, SUBLANES, d), F32),
                        pltpu.SemaphoreType.DMA((2,)),
                        pltpu.VMEM((d // 2 // LANES, 2 * tm, LANES), F32)])
    return pl.pallas_call(
        functools.partial(_bwd_kernel, geom=geom),
        grid_spec=grid_spec,
        out_shape=[jax.ShapeDtypeStruct((t, d), F32),
                   jax.ShapeDtypeStruct((t * (d // 2) // LANES, LANES), U32),
                   jax.ShapeDtypeStruct((nr, t), F32),
                   jax.ShapeDtypeStruct((ngrp, SUBLANES, d_lru), F32)],
        compiler_params=_cparams(("arbitrary",)),
        name="backward_scan_mix",
    )(tile_row, flags, xa, xa, xa, conv_w, conv_b.reshape(1, d_lru), wg, bg, lam.reshape(1, d_lru), h0,
      hf, ga, yb_lat, yb_ctx, x_lat, x_ctx, modtab, modtab, modtab, w_out, norm2_g.reshape(1, d), rwt, rb)


def _shift_rows(v, k):
    if k == 0:
        return v
    z = jnp.zeros((abs(k) * SUBLANES, v.shape[1]), v.dtype)
    if k > 0:
        return jnp.concatenate([z, v[:-k * SUBLANES]], axis=0)
    return jnp.concatenate([v[-k * SUBLANES:], z], axis=0)


def _run_sum(v, m, direction):
    if m & (m - 1) == 0:
        k = 1
        while k < m:
            v = v + _shift_rows(v, -direction * k)
            k *= 2
        return v
    out = v
    for j in range(1, m):
        out = out + _shift_rows(v, -direction * j)
    return out


def _box_sum(v, w):
    lo = w // 2
    hi = w - 1 - lo
    s = _run_sum(v, hi + 1, +1)
    if lo:
        s = s + _shift_rows(_run_sum(v, lo, -1), 1)
    return s


def _window_count(n_rows, lanes, n_pos, w):
    lo = w // 2
    hi = w - 1 - lo
    p = lax.shift_right_logical(lax.broadcasted_iota(I32, (n_rows, lanes), 0), 3)
    return (jnp.minimum(p + hi + 1, n_pos) - jnp.maximum(p - lo, 0)).astype(F32)


def _pool_kernel(z_ref, ps_ref, o_ref, v_ref, *, tiles_per_group, grid_rows, grid_cols):
    group = pl.program_id(1) // tiles_per_group
    lanes = z_ref.shape[1]
    ps = ps_ref[...]

    def pool_1d(w):
        z = z_ref[...].astype(F32)
        n = z.shape[0] // SUBLANES
        mean = _box_sum(z, w) / _window_count(z.shape[0], lanes, n, w)
        o_ref[...] = ((mean - z) * ps).astype(o_ref.dtype)

    def pool_2d(w):
        lo = w // 2
        hi = w - 1 - lo
        blk = grid_cols * SUBLANES
        cw = _window_count(blk, lanes, grid_cols, w)

        def zrow(r):
            return z_ref[pl.ds(pl.multiple_of(r * blk, blk), blk), :].astype(F32)

        v = jnp.zeros((blk, lanes), F32)
        for r in range(hi):
            v = v + zrow(r)
        v_ref[...] = v

        def body(r, carry):
            add = r + hi
            sub = r - lo - 1
            v = v_ref[...]
            v = v + jnp.where(add < grid_rows, zrow(jnp.minimum(add, grid_rows - 1)), 0.0)
            v = v - jnp.where(sub >= 0, zrow(jnp.maximum(sub, 0)), 0.0)
            v_ref[...] = v
            ch = (jnp.minimum(r + hi + 1, grid_rows) - jnp.maximum(r - lo, 0)).astype(F32)
            mean = _box_sum(v, w) / (ch * cw)
            o_ref[pl.ds(pl.multiple_of(r * blk, blk), blk), :] = ((mean - zrow(r)) * ps).astype(o_ref.dtype)
            return carry

        lax.fori_loop(0, grid_rows, body, 0)

    for g, w in enumerate(POOL_WINDOWS):
        @pl.when(group == g)
        def _(w=w):
            if grid_rows is None:
                pool_1d(w)
            else:
                pool_2d(w)


def _pool(z, pool_scale, rows_per_group, n_groups, first_block, grid_rows):
    d_pool = z.shape[1]
    gw = d_pool // len(POOL_WINDOWS)
    lanes = LANES
    blk = GRID_W * SUBLANES
    return pl.pallas_call(
        functools.partial(_pool_kernel, tiles_per_group=gw // lanes, grid_rows=grid_rows, grid_cols=GRID_W),
        grid=(n_groups, d_pool // lanes),
        in_specs=[pl.BlockSpec((rows_per_group, lanes), lambda g, j: (first_block + g, j)),
                  pl.BlockSpec((1, lanes), lambda g, j: (0, j))],
        out_specs=pl.BlockSpec((rows_per_group, lanes), lambda g, j: (g, j)),
        out_shape=jax.ShapeDtypeStruct((n_groups * rows_per_group, d_pool), BF16),
        scratch_shapes=[pltpu.VMEM((blk, lanes), F32)],
        compiler_params=_cparams(("arbitrary", "arbitrary")),
        name="pool_grid" if grid_rows is not None else "pool_seq",
    )(z, pool_scale.reshape(1, d_pool))


def _route_kernel(lg_ref, oi_ref, of_ref, cnt_ref, bc_ref, tri_ref, carry_ref, *, n_groups, per_group):
    step = pl.program_id(0)
    tt = lg_ref.shape[1]
    n_exp = n_groups * per_group

    @pl.when(step == 0)
    def _():
        r = lax.broadcasted_iota(I32, tri_ref.shape, 0)
        c = lax.broadcasted_iota(I32, tri_ref.shape, 1)
        tri_ref[...] = (r <= c).astype(BF16)
        carry_ref[...] = jnp.zeros_like(carry_ref)

    row = lambda k: lg_ref[k:k + 1, :]
    cmax = row(0)
    gi = jnp.zeros((1, tt), I32)
    for g in range(1, n_groups):
        better = row(g) > cmax
        gi = jnp.where(better, g, gi)
        cmax = jnp.where(better, row(g), cmax)
    denom = jnp.zeros((1, tt), F32)
    for g in range(n_groups):
        denom = denom + jnp.exp(row(g) - cmax)
    pg = 1.0 / denom
    fine = []
    for j in range(per_group):
        f = row(n_groups + j)
        for g in range(1, n_groups):
            f = jnp.where(gi == g, row(n_groups + g * per_group + j), f)
        fine.append(f)
    v1 = fine[0]
    i1 = jnp.zeros((1, tt), I32)
    for j in range(1, per_group):
        better = fine[j] > v1
        i1 = jnp.where(better, j, i1)
        v1 = jnp.where(better, fine[j], v1)
    v2 = jnp.full((1, tt), -jnp.inf, F32)
    i2 = jnp.zeros((1, tt), I32)
    for j in range(per_group):
        better = jnp.logical_and(i1 != j, fine[j] > v2)
        i2 = jnp.where(better, j, i2)
        v2 = jnp.where(better, fine[j], v2)
    ex = jnp.exp(v2 - v1)
    w1 = (1.0 / (1.0 + ex)) * pg
    w2 = (ex / (1.0 + ex)) * pg
    e1 = gi * per_group + i1
    e2 = gi * per_group + i2
    eid = lax.broadcasted_iota(I32, (n_exp, tt), 0)
    hit1 = eid == e1
    hit2 = eid == e2
    member = jnp.logical_or(hit1, hit2).astype(BF16)
    cb = tri_ref.shape[0]
    carry = carry_ref[...]
    cums = []
    lane = lax.broadcasted_iota(I32, bc_ref.shape, 1)
    block_ends = jnp.zeros(bc_ref.shape, F32)
    for s in range(tt // cb):
        c = jnp.dot(member[:, s * cb:(s + 1) * cb], tri_ref[...], preferred_element_type=F32) + carry
        carry = c[:, cb - 1:cb]
        cums.append(c)
        block_ends = jnp.where(lane == s, carry, block_ends)
    bc_ref[...] = block_ends.astype(I32)
    carry_ref[...] = carry
    cum = jnp.concatenate(cums, axis=1)
    rank1 = jnp.sum(jnp.where(hit1, cum, 0.0), axis=0, keepdims=True) - 1.0
    rank2 = jnp.sum(jnp.where(hit2, cum, 0.0), axis=0, keepdims=True) - 1.0
    zi = jnp.zeros((SUBLANES - 4, tt), I32)
    oi_ref[...] = jnp.concatenate([e1, e2, rank1.astype(I32), rank2.astype(I32), zi], axis=0)
    of_ref[...] = jnp.concatenate([w1, w2, jnp.zeros((of_ref.shape[0] - 2, tt), F32)], axis=0)
    cnt_ref[...] = jnp.broadcast_to(carry, cnt_ref.shape).astype(I32)


def _route(logits_t, n_groups, per_group, cb):
    nr, t = logits_t.shape
    tt = max(k for k in range(cb, min(ROUTE_TILE, t) + 1, cb) if t % k == 0)
    n_exp = n_groups * per_group
    return pl.pallas_call(
        functools.partial(_route_kernel, n_groups=n_groups, per_group=per_group),
        grid=(t // tt,),
        in_specs=[pl.BlockSpec((nr, tt), lambda i: (0, i))],
        out_specs=[pl.BlockSpec((SUBLANES, tt), lambda i: (0, i)),
                   pl.BlockSpec((LANES, tt), lambda i: (0, i)),
                   pl.BlockSpec((n_exp, LANES), lambda i: (0, 0)),
                   pl.BlockSpec((n_exp, LANES), lambda i: (0, i))],
        out_shape=[jax.ShapeDtypeStruct((SUBLANES, t), I32),
                   jax.ShapeDtypeStruct((LANES, t), F32),
                   jax.ShapeDtypeStruct((n_exp, LANES), I32),
                   jax.ShapeDtypeStruct((n_exp, LANES * (t // tt)), I32)],
        scratch_shapes=[pltpu.VMEM((cb, cb), BF16), pltpu.VMEM((n_exp, 1), F32)],
        compiler_params=_cparams(("arbitrary",)),
        name="route",
    )(logits_t)


def _tiles_for(cnt, tmx):
    return jnp.right_shift(cnt + (SEG_CHUNK + tmx - 1), tmx.bit_length() - 1)


def _plan_kernel(ri_ref, cnt_ref, bc_ref, bcp_ref, lp_ref, dst_ref, src_ref, len_ref, te_ref, nv_ref,
                 off_ref, *, tmx, tb):
    step = pl.program_id(0)
    n_exp = cnt_ref.shape[0]
    tt = ri_ref.shape[1]
    shift = tmx.bit_length() - 1
    n_tile = _tiles_for(cnt_ref[:, 0:1], tmx)

    def exclusive_prefix(col):
        run = jnp.zeros((1, 1), I32)
        parts = []
        for e in range(n_exp):
            parts.append(run)
            run = run + col[e:e + 1, :]
        return jnp.concatenate(parts, axis=0), run

    off, total = exclusive_prefix(jnp.left_shift(n_tile, shift))
    before = jnp.where(step == 0, 0, bcp_ref[:, tt // tb - 1:tt // tb])
    lane = lax.broadcasted_iota(I32, dst_ref.shape, 1)
    eid = lax.broadcasted_iota(I32, (n_exp, tb), 0)
    dst = jnp.zeros(dst_ref.shape, I32)
    src = jnp.zeros(dst_ref.shape, I32)
    length = jnp.zeros(dst_ref.shape, I32)
    rows = []
    for k in range(tt // tb):
        end = bc_ref[:, k:k + 1]
        n = end - before
        aligned = jnp.left_shift(jnp.right_shift(n + (SEG_CHUNK - 1), SEG_SHIFT), SEG_SHIFT)
        start, _ = exclusive_prefix(aligned)
        base = start - before
        sl = slice(k * tb, (k + 1) * tb)
        place = lambda e_row, r_row: jnp.sum(jnp.where(eid == e_row, base, 0), axis=0, keepdims=True) + r_row
        rows.append(jnp.concatenate([place(ri_ref[0:1, sl], ri_ref[2:3, sl]),
                                     place(ri_ref[1:2, sl], ri_ref[3:4, sl])], axis=0))
        dst = jnp.where(lane == k, off + before, dst)
        src = jnp.where(lane == k, start, src)
        length = jnp.where(lane == k, n, length)
        before = end
    lp_ref[...] = jnp.concatenate([jnp.concatenate(rows, axis=1),
                                   jnp.zeros((SUBLANES - TOP_K_FINE, tt), I32)], axis=0)
    dst_ref[...] = dst
    src_ref[...] = src
    len_ref[...] = length
    end_tile = jnp.right_shift(off, shift) + n_tile
    k = lax.broadcasted_iota(I32, (n_exp, te_ref.shape[1]), 1)
    te = jnp.sum((k >= end_tile).astype(I32), axis=0, keepdims=True)
    te_ref[...] = jnp.minimum(te, n_exp - 1)
    nv_ref[...] = jnp.broadcast_to(jnp.right_shift(total, shift), nv_ref.shape)
    off_ref[...] = jnp.broadcast_to(off, off_ref.shape)


def _plan(ri, counts, block_counts, tmx, n_tiles, tb):
    _, t = ri.shape
    n_exp = counts.shape[0]
    steps = block_counts.shape[1] // LANES
    tt = t // steps
    ntp = -(-n_tiles // LANES) * LANES
    per_step = lambda: pl.BlockSpec((n_exp, LANES), lambda i: (0, i))
    whole = lambda w: pl.BlockSpec((n_exp, w), lambda i: (0, 0))
    tab = jax.ShapeDtypeStruct((n_exp, LANES * steps), I32)
    lp, dst, src, length, te, nv, off = pl.pallas_call(
        functools.partial(_plan_kernel, tmx=tmx, tb=tb),
        grid=(steps,),
        in_specs=[pl.BlockSpec((SUBLANES, tt), lambda i: (0, i)),
                  whole(LANES),
                  per_step(),
                  pl.BlockSpec((n_exp, LANES), lambda i: (0, jnp.maximum(i - 1, 0)))],
        out_specs=[pl.BlockSpec((SUBLANES, tt), lambda i: (0, i)),
                   per_step(), per_step(), per_step(),
                   pl.BlockSpec((1, ntp), lambda i: (0, 0)),
                   pl.BlockSpec((1, LANES), lambda i: (0, 0)),
                   whole(LANES)],
        out_shape=[jax.ShapeDtypeStruct((SUBLANES, t), I32), tab, tab, tab,
                   jax.ShapeDtypeStruct((1, ntp), I32),
                   jax.ShapeDtypeStruct((1, LANES), I32),
                   jax.ShapeDtypeStruct((n_exp, LANES), I32)],
        compiler_params=_cparams(("arbitrary",)),
        name="plan",
    )(ri, counts, block_counts, block_counts)
    flat = lambda a: a.reshape(n_exp, steps, LANES)[:, :, :tt // tb].transpose(1, 2, 0).reshape(-1)
    tables = (flat(dst), flat(src), flat(length))
    return (lp[0].reshape(t // tb, tb), lp[1].reshape(t // tb, tb), tables,
            te[0, :n_tiles], nv[0, :1], off[:, 0])


def _load_positions(pos1_hbm, pos2_hbm, p1_ref, p2_ref, isem, step, n_steps):
    tb = p1_ref.shape[0] // 2
    slot = lax.rem(step, 2)

    def copies(blk, half):
        rows = pl.ds(pl.multiple_of(half * tb, tb), tb)
        return (pltpu.make_async_copy(pos1_hbm.at[blk], p1_ref.at[rows], isem.at[0, half]),
                pltpu.make_async_copy(pos2_hbm.at[blk], p2_ref.at[rows], isem.at[1, half]))

    @pl.when(step == 0)
    def _():
        for cp in copies(step, slot):
            cp.start()

    @pl.when(step + 1 < n_steps)
    def _():
        for cp in copies(step + 1, 1 - slot):
            cp.start()

    for cp in copies(step, slot):
        cp.wait()
    return slot * tb


def _token_rows(token, count, nc):
    return pl.ds(pl.multiple_of(token * nc, nc), count * nc)


def _segment_copy(stage_ref, far_hbm, near_tok, far_tok, sem, to_far, nc):
    near = stage_ref.at[_token_rows(near_tok, SEG_CHUNK, nc), :]
    far = far_hbm.at[_token_rows(far_tok, SEG_CHUNK, nc), :]
    return pltpu.make_async_copy(near, far, sem) if to_far else pltpu.make_async_copy(far, near, sem)


def _start_segments(tables, blk, n_exp, stage_ref, far_hbm, sem, to_far, nc):
    dst_ref, src_ref, len_ref = tables

    total = jnp.int32(0)
    for e in range(n_exp):
        k = blk * n_exp + e
        n_chunks = jnp.right_shift(len_ref[k] + (SEG_CHUNK - 1), SEG_SHIFT)
        near0 = src_ref[k]
        far0 = dst_ref[k]

        def one(c, carry, near0=near0, far0=far0, queue=e % 2):
            _segment_copy(stage_ref, far_hbm, near0 + c * SEG_CHUNK, far0 + c * SEG_CHUNK, sem,
                          to_far, nc).start(priority=queue)
            return carry
        lax.fori_loop(0, n_chunks, one, 0)
        total = total + n_chunks
    return total


def _wait_segments(stage_ref, far_hbm, sem, n, to_far, nc):
    copy = _segment_copy(stage_ref, far_hbm, 0, 0, sem, to_far, nc)

    def body(_, c):
        copy.wait()
        return c
    lax.fori_loop(0, n, body, 0)


def _dispatch_kernel(dst_ref, src_ref, len_ref, off_ref, cnt_ref, lp1_hbm, lp2_hbm, h2_ref, xs_hbm,
                     p1_ref, p2_ref, stage, nd_ref, zero_ref, isem, ssem, zsem, *, tmx):
    step = pl.program_id(0)
    n_steps = pl.num_programs(0)
    slot = lax.rem(step, 2)
    n_exp = off_ref.shape[0]
    nc = zero_ref.shape[0]
    n_blk = h2_ref.shape[0] // nc // SUBLANES
    tables = (dst_ref, src_ref, len_ref)

    @pl.when(step == 0)
    def _():
        for s in range(stage.shape[0]):
            stage[s] = _packed_zeros(stage.shape[1])

    p0 = _load_positions(lp1_hbm, lp2_hbm, p1_ref, p2_ref, isem, step, n_steps)

    def place(blk, c):
        j0 = blk * SUBLANES
        for s in range(SUBLANES):
            row = h2_ref[_token_rows(j0 + s, 1, nc), :]
            stage[slot, _token_rows(p1_ref[p0 + j0 + s], 1, nc), :] = row
            stage[slot, _token_rows(p2_ref[p0 + j0 + s], 1, nc), :] = row
        return c
    lax.fori_loop(0, n_blk, place, 0, unroll=2)

    @pl.when(step >= 1)
    def _():
        _wait_segments(stage.at[1 - slot], xs_hbm, ssem.at[1 - slot], nd_ref[1 - slot], True, nc)

    nd_ref[slot] = _start_segments(tables, step, n_exp, stage.at[slot], xs_hbm, ssem.at[slot], True, nc)

    @pl.when(step == n_steps - 1)
    def _():
        _wait_segments(stage.at[slot], xs_hbm, ssem.at[slot], nd_ref[slot], True, nc)
        zero_ref[...] = _packed_zeros(nc)
        shift = tmx.bit_length() - 1

        def pad_expert(e, c):
            cnt = cnt_ref[e]
            n_pad = jnp.left_shift(_tiles_for(cnt, tmx), shift) - cnt
            first = off_ref[e] + cnt
            zero_copy = lambda r: pltpu.make_async_copy(
                zero_ref, xs_hbm.at[_token_rows(first + r, 1, nc), :], zsem)

            def fill(r, c2):
                zero_copy(r).start()
                return c2
            lax.fori_loop(0, n_pad, fill, 0)

            def done(r, c2):
                zero_copy(r).wait()
                return c2
            lax.fori_loop(0, n_pad, done, 0)
            return c
        lax.fori_loop(0, off_ref.shape[0], pad_expert, 0)


def _stage_rows(tb, n_exp):
    return TOP_K_FINE * tb + n_exp * SEG_CHUNK


def _dispatch(tables, offsets, counts, lp1, lp2, h2p, p_max, tmx):
    nb, tb = lp1.shape
    nc = h2p.shape[0] // (nb * tb)
    n_exp = offsets.shape[0]
    hbm = lambda: pl.BlockSpec(memory_space=pl.ANY)
    grid_spec = pltpu.PrefetchScalarGridSpec(
        num_scalar_prefetch=5,
        grid=(nb,),
        in_specs=[hbm(), hbm(),
                  pl.BlockSpec((tb * nc, LANES), lambda i, *_: (i, 0))],
        out_specs=hbm(),
        scratch_shapes=[pltpu.SMEM((2 * tb,), I32),
                        pltpu.SMEM((2 * tb,), I32),
                        pltpu.VMEM((2, _stage_rows(tb, n_exp) * nc, LANES), U32),
                        pltpu.SMEM((2,), I32),
                        pltpu.VMEM((nc, LANES), U32),
                        pltpu.SemaphoreType.DMA((2, 2)),
                        pltpu.SemaphoreType.DMA((2,)),
                        pltpu.SemaphoreType.DMA(())])
    return pl.pallas_call(
        functools.partial(_dispatch_kernel, tmx=tmx),
        grid_spec=grid_spec,
        out_shape=jax.ShapeDtypeStruct((p_max * nc, LANES), U32),
        compiler_params=_cparams(("arbitrary",)),
        name="dispatch",
    )(*tables, offsets, counts, lp1, lp2, h2p)


def _expert_kernel(te_ref, nv_ref, x_ref, w1_ref, w3_ref, w2_ref, y_ref, pack_ref, *, tmx):
    @pl.when(pl.program_id(0) < nv_ref[0])
    def _():
        lo, hi = _load_packed_tokens(x_ref, tmx, pack_ref)
        lo = lo.astype(BF16)
        hi = hi.astype(BF16)
        half = lo.shape[1]
        mm = lambda w_ref: (jnp.dot(lo, w_ref[0, :half, :].astype(BF16), preferred_element_type=F32)
                            + jnp.dot(hi, w_ref[0, half:, :].astype(BF16), preferred_element_type=F32))
        h1 = mm(w1_ref)
        h3 = mm(w3_ref)
        hh = (h1 * jax.nn.sigmoid(h1) * h3).astype(BF16)
        y = jnp.dot(hh, w2_ref[0].astype(BF16), preferred_element_type=F32)
        _store_packed_tokens(y_ref, y[:, :half], y[:, half:], pack_ref)


def _expert_mlp(xs, tile_expert, n_valid, w1, w3, w2, tmx):
    d, de = w1.shape[1], w1.shape[2]
    nc = d // 2 // LANES
    p = xs.shape[0] // nc
    tile = lambda i, te, nv: (jnp.minimum(i, nv[0] - 1), 0)
    grid_spec = pltpu.PrefetchScalarGridSpec(
        num_scalar_prefetch=2,
        grid=(p // tmx,),
        in_specs=[pl.BlockSpec((tmx * nc, LANES), tile),
                  pl.BlockSpec((1, d, de), lambda i, te, nv: (te[i], 0, 0)),
                  pl.BlockSpec((1, d, de), lambda i, te, nv: (te[i], 0, 0)),
                  pl.BlockSpec((1, de, d), lambda i, te, nv: (te[i], 0, 0))],
        out_specs=pl.BlockSpec((tmx * nc, LANES), tile),
        scratch_shapes=[pltpu.VMEM((nc, 2 * tmx, LANES), F32)])
    return pl.pallas_call(
        functools.partial(_expert_kernel, tmx=tmx),
        grid_spec=grid_spec,
        out_shape=jax.ShapeDtypeStruct((p * nc, LANES), U32),
        compiler_params=_cparams(("arbitrary",)),
        name="expert_mlp",
    )(tile_expert, n_valid, xs, w1, w3, w2)


def _combine_kernel(dst_ref, src_ref, len_ref, lp1_hbm, lp2_hbm, ys_hbm, x1_ref, rf_ref, g2_ref, fg_ref,
                    ylat_hbm, yctx_hbm, p1_ref, p2_ref, stage, gbuf, pack_ref, obuf, nd_ref, isem, csem, osem,
                    *, geom, n_exp):
    step = pl.program_id(0)
    n_steps = pl.num_programs(0)
    tm = geom.tm
    n_blk = tm // SUBLANES
    slot = lax.rem(step, 2)
    nc = gbuf.shape[1] // tm
    tables = (dst_ref, src_ref, len_ref)

    def fetch(blk, gs):
        nd_ref[gs] = _start_segments(tables, blk, n_exp, stage.at[gs], ys_hbm, csem.at[gs], False, nc)

    @pl.when(step == 0)
    def _():
        fetch(step, slot)

    @pl.when(step + 1 < n_steps)
    def _():
        fetch(step + 1, 1 - slot)

    _wait_segments(stage.at[slot], ys_hbm, csem.at[slot], nd_ref[slot], False, nc)
    p0 = _load_positions(lp1_hbm, lp2_hbm, p1_ref, p2_ref, isem, step, n_steps)

    def pick(blk, c):
        j0 = blk * SUBLANES
        for s in range(SUBLANES):
            rows = _token_rows(j0 + s, 1, nc)
            gbuf[0, rows, :] = stage[slot, _token_rows(p1_ref[p0 + j0 + s], 1, nc), :]
            gbuf[1, rows, :] = stage[slot, _token_rows(p2_ref[p0 + j0 + s], 1, nc), :]
        return c
    lax.fori_loop(0, n_blk, pick, 0, unroll=2)

    wt = rf_ref[...].T
    a_lo, a_hi = _load_packed_tokens(gbuf.at[0], tm, pack_ref.at[0])
    b_lo, b_hi = _load_packed_tokens(gbuf.at[1], tm, pack_ref.at[1])
    w1 = wt[:, 0:1]
    w2 = wt[:, 1:2]
    moe = jnp.concatenate([w1 * a_lo + w2 * b_lo, w1 * a_hi + w2 * b_hi], axis=1)
    x = x1_ref[...] + _gate(moe, g2_ref[0])
    out = _rms(x, fg_ref[...])

    start = lambda cp: cp.start()
    wait = lambda cp: cp.wait()
    put = functools.partial(_tile_copies, ylat_hbm, yctx_hbm, geom=geom, to_rows=False)

    @pl.when(step >= 2)
    def _():
        put(obuf.at[slot], osem.at[slot], step - 2, fn=wait)

    obuf[slot] = out.reshape(obuf.shape[1:])
    put(obuf.at[slot], osem.at[slot], step, fn=start)

    @pl.when(step == n_steps - 1)
    def _():
        @pl.when(n_steps > 1)
        def _():
            put(obuf.at[1 - slot], osem.at[1 - slot], step - 1, fn=wait)
        put(obuf.at[slot], osem.at[slot], step, fn=wait)


def _combine(tables, lp1, lp2, ys, x1, rf, modtab, final_g, lat_shape, ctx_shape, geom):
    t, d = x1.shape
    tm = geom.tm
    nc = d // 2 // LANES
    n_exp = tables[0].shape[0] // (t // tm)
    mset = lambda i, *_: jnp.where(i < geom.n_lat_tiles, 0, 1)
    hbm = lambda: pl.BlockSpec(memory_space=pl.ANY)
    grid_spec = pltpu.PrefetchScalarGridSpec(
        num_scalar_prefetch=3,
        grid=(t // tm,),
        in_specs=[hbm(), hbm(), hbm(),
                  pl.BlockSpec((tm, d), lambda i, *_: (i, 0)),
                  pl.BlockSpec((LANES, tm), lambda i, *_: (0, i)),
                  pl.BlockSpec((1, SUBLANES, d), lambda i, *_: (mset(i), 0, 5)),
                  pl.BlockSpec((1, d), lambda i, *_: (0, 0))],
        out_specs=[hbm(), hbm()],
        scratch_shapes=[pltpu.SMEM((2 * tm,), I32),
                        pltpu.SMEM((2 * tm,), I32),
                        pltpu.VMEM((2, _stage_rows(tm, n_exp) * nc, LANES), U32),
                        pltpu.VMEM((TOP_K_FINE, tm * nc, LANES), U32),
                        pltpu.VMEM((TOP_K_FINE, nc, 2 * tm, LANES), F32),
                        pltpu.VMEM((2, tm // SUBLANES, SUBLANES, d), F32),
                        pltpu.SMEM((2,), I32),
                        pltpu.SemaphoreType.DMA((2, 2)),
                        pltpu.SemaphoreType.DMA((2,)),
                        pltpu.SemaphoreType.DMA((2,))])
    return pl.pallas_call(
        functools.partial(_combine_kernel, geom=geom, n_exp=n_exp),
        grid_spec=grid_spec,
        out_shape=[jax.ShapeDtypeStruct(lat_shape, F32), jax.ShapeDtypeStruct(ctx_shape, F32)],
        compiler_params=_cparams(("arbitrary",)),
        name="combine",
    )(*tables, lp1, lp2, ys, x1, rf, modtab, final_g.reshape(1, d))


def _tile_meta(groups, tm):
    rows, first, last, grp = [], [], [], []
    blk = 0
    for g, r in enumerate(groups):
        nc = r // tm
        for c in range(nc):
            rows.append(blk + c)
            first.append(int(c == 0))
            last.append(int(c == nc - 1))
            grp.append(g)
        blk += nc
    fwd = (np.array(rows, np.int32), np.array([first, last, grp], np.int32))
    order = []
    blk = 0
    for r in groups:
        nc = r // tm
        order.extend(range(blk + nc - 1, blk - 1, -1))
        blk += nc
    order = np.array(order)
    bwd = (fwd[0][order], fwd[1][:, order])
    return fwd, bwd


def kernel(x_prompt, x_sample, state_lru, c, c_ctx, w_mod, b_mod, norm1_g, w_in, conv_w, conv_b, lru_wa, lru_ba, lru_wx, lru_bx, lru_lambda, pool_w, pool_scale, w_out, norm2_g, router_coarse_w, router_coarse_b, router_fine_w, router_fine_b, exp_w1, exp_w3, exp_w2, final_norm_g):
    bp, sp, d = x_prompt.shape
    bs, ss, _ = x_sample.shape
    d_lru = lru_lambda.shape[-1]
    heads, bw = lru_wa.shape[2], lru_wa.shape[3]
    n_groups, per_group = router_fine_w.shape[2], router_fine_w.shape[3]
    n_exp = n_groups * per_group
    assert w_mod.shape[0] == 1 and bs == SUBLANES and bp % SUBLANES == 0 and ss % GRID_W == 0
    assert EXPERT_TILE & (EXPERT_TILE - 1) == 0
    n_lat_groups, n_ctx_groups = bs // SUBLANES, bp // SUBLANES
    lat_rows, ctx_rows = ss * SUBLANES, sp * SUBLANES
    assert (n_lat_groups * lat_rows) % ctx_rows == 0
    tm = min(TOKEN_TILE, ctx_rows, lat_rows)
    geom = _Geom(tm=tm, n_lat_tiles=n_lat_groups * lat_rows // tm, lat_chunks=lat_rows // tm,
                 ctx_chunks=ctx_rows // tm)
    n_rows = n_lat_groups * lat_rows + n_ctx_groups * ctx_rows
    groups = [lat_rows] * n_lat_groups + [ctx_rows] * n_ctx_groups
    (f_rows, f_flags), (b_rows, b_flags) = _tile_meta(groups, tm)

    cond = jnp.zeros((2 * SUBLANES, d), F32).at[:bs].set(c).at[bs].set(c_ctx)
    mod = _modulation(cond, w_mod[0], b_mod[0])
    modtab = jnp.stack([mod[:SUBLANES], jnp.broadcast_to(mod[SUBLANES], (SUBLANES, mod.shape[1]))])
    h0_lat = state_lru[:, 0].reshape(n_lat_groups, SUBLANES, 2, d_lru)
    h0 = jnp.concatenate([h0_lat, jnp.zeros((n_ctx_groups, SUBLANES, 2, d_lru), F32)], axis=0)
    h0 = h0.transpose(0, 2, 1, 3)

    xa, ga, z = _input_projection(x_sample, x_prompt, modtab, norm1_g[0], w_in[0],
                                  pool_w[0].astype(BF16), n_rows, geom)

    def gate_weights(direction):
        wg = jnp.concatenate([lru_wa[0, direction], lru_wx[0, direction]], axis=-1).astype(BF16)
        bg = jnp.concatenate([lru_ba[0, direction].reshape(heads, 1, bw),
                              lru_bx[0, direction].reshape(heads, 1, bw)], axis=-1)
        return wg, bg

    log_decay = jax.nn.log_sigmoid(lru_lambda[0])
    wg_f, bg_f = gate_weights(0)
    wg_b, bg_b = gate_weights(1)
    hf, hf_last = _forward_scan(xa, (jnp.asarray(f_rows), jnp.asarray(f_flags)), conv_w[0], conv_b[0],
                                wg_f, bg_f, log_decay[0], h0[:, 0], tm)

    yb_lat = _pool(z, pool_scale[0], lat_rows, n_lat_groups, 0, lat_rows // (GRID_W * SUBLANES))
    yb_ctx = _pool(z, pool_scale[0], ctx_rows, n_ctx_groups, n_lat_groups * lat_rows // ctx_rows, None)

    n_logits = n_groups + n_exp
    rw = jnp.concatenate([router_coarse_w[0], router_fine_w[0].reshape(d, n_exp)], axis=1)
    rwt = jnp.zeros((LANES, d), BF16).at[:n_logits].set(rw.T.astype(BF16))
    rb = jnp.zeros((LANES, 1), F32).at[:n_logits, 0].set(
        jnp.concatenate([router_coarse_b[0], router_fine_b[0].reshape(n_exp)]))
    x1, h2p, logits_t, hb_last = _backward_scan_mix(
        xa, ga, hf, yb_lat, yb_ctx, x_sample, x_prompt, modtab, (jnp.asarray(b_rows), jnp.asarray(b_flags)),
        conv_w[0], conv_b[0], wg_b, bg_b, log_decay[1], h0[:, 1], w_out[0].astype(BF16), norm2_g[0],
        rwt, rb, geom)

    ri, rf, counts, block_counts = _route(logits_t, n_groups, per_group, tm)
    n_tiles = -(-(TOP_K_FINE * n_rows + n_exp * (EXPERT_TILE + SEG_CHUNK)) // EXPERT_TILE)
    lp1, lp2, tables, tile_expert, n_valid, offsets = _plan(ri, counts, block_counts, EXPERT_TILE, n_tiles, tm)
    xs = _dispatch(tables, offsets, counts[:, 0], lp1, lp2, h2p, n_tiles * EXPERT_TILE, EXPERT_TILE)
    ys = _expert_mlp(xs, tile_expert, n_valid, exp_w1[0], exp_w3[0], exp_w2[0], EXPERT_TILE)
    y_sample, y_prompt = _combine(tables, lp1, lp2, ys, x1, rf, modtab, final_norm_g, x_sample.shape,
                                  x_prompt.shape, geom)

    st = jnp.stack([hf_last[n_lat_groups:], hb_last[n_lat_groups:]], axis=2)
    state_new = st.reshape(bp, 1, 2, d_lru).astype(x_prompt.dtype)
    return (y_prompt, y_sample, state_new)
```

```python
import functools
from typing import NamedTuple

import numpy as np
import jax
import jax.numpy as jnp
from jax import lax
from jax.experimental import pallas as pl
from jax.experimental.pallas import tpu as pltpu

GRID_W = 64
CONV_W = 4
RG_C = 8.0
POOL_WINDOWS = (2, 4, 8, 16)
TOP_K_FINE = 2
EPS = 1e-6
EXPM1_SERIES_BELOW = 0.125

SUBLANES = 8
LANES = 128
BF16_ROWS = 16
TOKEN_TILE = 512
EXPERT_TILE = 512
ROUTE_TILE = 2048
SEG_SHIFT = 5
SEG_CHUNK = 1 << SEG_SHIFT
MOD_COL_TILE = 1024
WEIGHT_STAGE_ROWS = 256
VMEM_LIMIT = 60 * 1024 * 1024

F32 = jnp.float32
BF16 = jnp.bfloat16
U32 = jnp.uint32
I32 = jnp.int32


class _Geom(NamedTuple):
    tm: int
    n_lat_tiles: int
    lat_chunks: int
    ctx_chunks: int


def _cparams(sem):
    return pltpu.CompilerParams(dimension_semantics=sem, vmem_limit_bytes=VMEM_LIMIT)


def _per_sequence(y, m):
    rows, d = y.shape
    return y.reshape(rows // SUBLANES, SUBLANES, d), m[None]


def _modulate(y, scale, shift):
    y3, sc = _per_sequence(y, scale)
    _, sh = _per_sequence(y, shift)
    return (y3 * (1.0 + sc) + sh).reshape(y.shape)


def _gate(y, g):
    y3, g3 = _per_sequence(y, g)
    return (y3 * g3).reshape(y.shape)


def _rms(x, g):
    ms = jnp.mean(x * x, axis=-1, keepdims=True)
    return x * lax.rsqrt(ms + EPS) * g


def _store_packed_tokens(ref, lo, hi, scratch):
    n, words = lo.shape
    nc = words // LANES
    for c in range(nc):
        sl = slice(c * LANES, (c + 1) * LANES)
        scratch[c, pl.ds(0, n, stride=2), :] = lo[:, sl]
        scratch[c, pl.ds(1, n, stride=2), :] = hi[:, sl]
        ref[pl.ds(c, n, stride=nc), :] = pltpu.bitcast(scratch[c].astype(BF16), U32)


def _packed_zeros(rows):
    return pltpu.bitcast(jnp.zeros((2 * rows, LANES), BF16), U32)


def _load_packed_tokens(ref, n, scratch):
    nc = ref.shape[0] // n
    los, his = [], []
    for c in range(nc):
        scratch[c] = pltpu.bitcast(ref[pl.ds(c, n, stride=nc), :], BF16).astype(F32)
        los.append(scratch[c, pl.ds(0, n, stride=2), :])
        his.append(scratch[c, pl.ds(1, n, stride=2), :])
    return jnp.concatenate(los, axis=1), jnp.concatenate(his, axis=1)


def _tile_copies(lat_hbm, ctx_hbm, buf, sem, tile, geom, to_rows, fn):
    tt = geom.tm // SUBLANES

    def run(hbm, k, chunks):
        g = lax.div(k, jnp.int32(chunks))
        c = lax.rem(k, jnp.int32(chunks))
        for b in range(SUBLANES):
            h = hbm.at[g * SUBLANES + b, pl.ds(c * tt, tt), :]
            v = buf.at[:, b, :]
            fn(pltpu.make_async_copy(h, v, sem) if to_rows else pltpu.make_async_copy(v, h, sem))

    @pl.when(tile < geom.n_lat_tiles)
    def _():
        run(lat_hbm, tile, geom.lat_chunks)

    @pl.when(tile >= geom.n_lat_tiles)
    def _():
        run(ctx_hbm, tile - geom.n_lat_tiles, geom.ctx_chunks)


def _fetch_rows(lat_hbm, ctx_hbm, xbuf, sem, step, n_steps, tile_of, geom):
    slot = lax.rem(step, 2)
    start = lambda cp: cp.start()
    wait = lambda cp: cp.wait()

    @pl.when(step == 0)
    def _():
        _tile_copies(lat_hbm, ctx_hbm, xbuf.at[0], sem.at[0], tile_of(jnp.int32(0)), geom, True, start)

    @pl.when(step + 1 < n_steps)
    def _():
        nxt = 1 - slot
        _tile_copies(lat_hbm, ctx_hbm, xbuf.at[nxt], sem.at[nxt], tile_of(step + 1), geom, True, start)

    _tile_copies(lat_hbm, ctx_hbm, xbuf.at[slot], sem.at[slot], tile_of(step), geom, True, wait)
    tt, _, d = xbuf.shape[1:]
    return xbuf[slot].reshape(tt * SUBLANES, d)


def _mod_kernel(c_ref, w_ref, b_ref, o_ref):
    c = c_ref[...]
    s = c * jax.nn.sigmoid(c)
    o_ref[...] = jnp.dot(s.astype(BF16), w_ref[...].astype(BF16),
                         preferred_element_type=F32) + b_ref[...]


def _modulation(cond, w_mod, b_mod):
    rows, d = cond.shape
    n = w_mod.shape[1]
    tn = min(MOD_COL_TILE, n)
    return pl.pallas_call(
        _mod_kernel,
        grid=(n // tn,),
        in_specs=[pl.BlockSpec((rows, d), lambda j: (0, 0)),
                  pl.BlockSpec((d, tn), lambda j: (0, j)),
                  pl.BlockSpec((1, tn), lambda j: (0, j))],
        out_specs=pl.BlockSpec((rows, tn), lambda j: (0, j)),
        out_shape=jax.ShapeDtypeStruct((rows, n), F32),
        compiler_params=_cparams(("arbitrary",)),
        name="modulation",
    )(cond, w_mod, b_mod.reshape(1, n))


def _round_weights(w_hbm, wbf_ref, stage, sem):
    rows = stage.shape[1]
    n = w_hbm.shape[0] // rows
    copy = lambda r: pltpu.make_async_copy(w_hbm.at[pl.ds(r * rows, rows), :], stage.at[r % 2], sem.at[r % 2])
    copy(0).start()
    for r in range(n):
        if r + 1 < n:
            copy(r + 1).start()
        copy(r).wait()
        wbf_ref[r * rows:(r + 1) * rows, :] = stage[r % 2].astype(wbf_ref.dtype)


def _proj_kernel(lat_hbm, ctx_hbm, sh_ref, sc_ref, g_ref, win_hbm, pw_ref, xa_ref, ga_ref, z_ref,
                 xbuf, sem, wbf_ref, wstage, wsem, *, d_lru, gw, geom):
    step = pl.program_id(0)

    @pl.when(step == 0)
    def _():
        _round_weights(win_hbm, wbf_ref, wstage, wsem)

    x = _fetch_rows(lat_hbm, ctx_hbm, xbuf, sem, step, pl.num_programs(0), lambda s: s, geom)
    h = _modulate(_rms(x, g_ref[...]), sc_ref[0], sh_ref[0])
    proj = jnp.dot(h.astype(BF16), wbf_ref[...], preferred_element_type=F32)
    xa_ref[...] = proj[:, :d_lru].astype(BF16)
    ga_ref[...] = proj[:, d_lru:2 * d_lru].astype(BF16)
    for g in range(pw_ref.shape[0]):
        lo = 2 * d_lru + g * gw
        z_ref[:, g * gw:(g + 1) * gw] = jnp.dot(
            proj[:, lo:lo + gw].astype(BF16), pw_ref[g], preferred_element_type=F32).astype(BF16)


def _input_projection(x_lat, x_ctx, modtab, norm_g, w_in, pool_w, n_rows, geom):
    d = x_lat.shape[-1]
    tm = geom.tm
    d_pool = pool_w.shape[0] * pool_w.shape[1]
    d_lru = (w_in.shape[1] - d_pool) // 2
    mset = lambda i: jnp.where(i < geom.n_lat_tiles, 0, 1)
    const = dict(pipeline_mode=pl.Buffered(1))
    return pl.pallas_call(
        functools.partial(_proj_kernel, d_lru=d_lru, gw=pool_w.shape[1], geom=geom),
        grid=(n_rows // tm,),
        in_specs=[pl.BlockSpec(memory_space=pl.ANY),
                  pl.BlockSpec(memory_space=pl.ANY),
                  pl.BlockSpec((1, SUBLANES, d), lambda i: (mset(i), 0, 0)),
                  pl.BlockSpec((1, SUBLANES, d), lambda i: (mset(i), 0, 1)),
                  pl.BlockSpec((1, d), lambda i: (0, 0)),
                  pl.BlockSpec(memory_space=pl.ANY),
                  pl.BlockSpec(pool_w.shape, lambda i: (0, 0, 0), **const)],
        out_specs=[pl.BlockSpec((tm, d_lru), lambda i: (i, 0)),
                   pl.BlockSpec((tm, d_lru), lambda i: (i, 0)),
                   pl.BlockSpec((tm, d_pool), lambda i: (i, 0))],
        out_shape=[jax.ShapeDtypeStruct((n_rows, d_lru), BF16),
                   jax.ShapeDtypeStruct((n_rows, d_lru), BF16),
                   jax.ShapeDtypeStruct((n_rows, d_pool), BF16)],
        scratch_shapes=[pltpu.VMEM((2, tm // SUBLANES, SUBLANES, d), F32),
                        pltpu.SemaphoreType.DMA((2,)),
                        pltpu.VMEM(w_in.shape, BF16),
                        pltpu.VMEM((2, min(WEIGHT_STAGE_ROWS, d), w_in.shape[1]), F32),
                        pltpu.SemaphoreType.DMA((2,))],
        compiler_params=_cparams(("arbitrary",)),
        name="input_projection",
    )(x_lat, x_ctx, modtab, modtab, norm_g.reshape(1, d), w_in, pool_w)


def _fill_ext(ext_ref, prev_ref, main_ref, next_ref, first, last):
    tm = main_ref.shape[0]
    prev = prev_ref[...].astype(F32)
    nxt = next_ref[...].astype(F32)
    ext_ref[0:BF16_ROWS, :] = jnp.where(first, 0.0, prev)
    ext_ref[BF16_ROWS:BF16_ROWS + tm, :] = main_ref[...].astype(F32)
    ext_ref[BF16_ROWS + tm:, :] = jnp.where(last, 0.0, nxt)


def _one_minus_exp(y, exp_y):
    p = 1.0 / 120.0
    for c in (1.0 / 24.0, 1.0 / 6.0, 0.5, 1.0):
        p = p * y + c
    return jnp.where(y > -EXPM1_SERIES_BELOW, -y * p, 1.0 - exp_y)


def _sqrt_nonneg(q):
    return jnp.where(q > 0.0, q * lax.rsqrt(q), 0.0)


def _decay_and_input(ext_ref, cw_ref, cb_ref, wg_ref, bg_ref, lam_ref, a_ref, u_ref, tm):
    heads, bw = wg_ref.shape[0], wg_ref.shape[1]
    for hd in range(heads):
        sl = slice(hd * bw, (hd + 1) * bw)
        xc = cb_ref[:, sl]
        for k in range(CONV_W):
            xc = xc + cw_ref[k:k + 1, sl] * ext_ref[SUBLANES * k:SUBLANES * k + tm, sl]
        g = jnp.dot(xc.astype(BF16), wg_ref[hd], preferred_element_type=F32) + bg_ref[hd]
        r = jax.nn.sigmoid(g[:, :bw])
        ig = jax.nn.sigmoid(g[:, bw:])
        log_a = (RG_C * r) * lam_ref[:, sl]
        a = jnp.exp(log_a)
        a_ref[:, sl] = a
        u_ref[:, sl] = _sqrt_nonneg(_one_minus_exp(2.0 * log_a, a * a)) * (ig * xc)


def _scan(a_ref, u_ref, h, tm, reverse):
    nblk = tm // SUBLANES

    def body(s, h):
        j = (nblk - 1 - s) if reverse else s
        rows = pl.ds(pl.multiple_of(j * SUBLANES, SUBLANES), SUBLANES)
        h = a_ref[rows, :] * h + u_ref[rows, :]
        u_ref[rows, :] = h
        return h

    return lax.fori_loop(0, nblk, body, h, unroll=8)


def _halo_specs(tm, d_lru, n_rows):
    per = tm // BF16_ROWS
    last_blk = n_rows // BF16_ROWS - 1
    return [pl.BlockSpec((BF16_ROWS, d_lru), lambda i, tr, fl: (jnp.maximum(tr[i] * per - 1, 0), 0)),
            pl.BlockSpec((tm, d_lru), lambda i, tr, fl: (tr[i], 0)),
            pl.BlockSpec((BF16_ROWS, d_lru), lambda i, tr, fl: (jnp.minimum((tr[i] + 1) * per, last_blk), 0))]


def _fwd_kernel(tr_ref, fl_ref, prev_ref, main_ref, next_ref, cw_ref, cb_ref, wg_ref, bg_ref, lam_ref,
                h0_ref, hf_ref, hlast_ref, ext_ref, a_ref, u_ref, h_ref):
    i = pl.program_id(0)
    tm = main_ref.shape[0]
    first = fl_ref[0, i] == 1
    last = fl_ref[1, i] == 1
    _fill_ext(ext_ref, prev_ref, main_ref, next_ref, first, last)
    _decay_and_input(ext_ref, cw_ref, cb_ref, wg_ref, bg_ref, lam_ref, a_ref, u_ref, tm)

    @pl.when(first)
    def _():
        h_ref[...] = h0_ref[0]

    h = _scan(a_ref, u_ref, h_ref[...], tm, reverse=False)
    h_ref[...] = h
    hlast_ref[0] = h
    hf_ref[...] = u_ref[...].astype(BF16)


def _forward_scan(xa, meta, conv_w, conv_b, wg, bg, lam, h0, tm):
    t, d_lru = xa.shape
    tile_row, flags = meta
    ngrp = h0.shape[0]
    const = dict(pipeline_mode=pl.Buffered(1))
    grid_spec = pltpu.PrefetchScalarGridSpec(
        num_scalar_prefetch=2,
        grid=(t // tm,),
        in_specs=_halo_specs(tm, d_lru, t) + [
            pl.BlockSpec(conv_w.shape, lambda i, tr, fl: (0, 0)),
            pl.BlockSpec((1, d_lru), lambda i, tr, fl: (0, 0)),
            pl.BlockSpec(wg.shape, lambda i, tr, fl: (0, 0, 0), **const),
            pl.BlockSpec(bg.shape, lambda i, tr, fl: (0, 0, 0)),
            pl.BlockSpec((1, d_lru), lambda i, tr, fl: (0, 0)),
            pl.BlockSpec((1, SUBLANES, d_lru), lambda i, tr, fl: (fl[2, i], 0, 0))],
        out_specs=[pl.BlockSpec((tm, d_lru), lambda i, tr, fl: (tr[i], 0)),
                   pl.BlockSpec((1, SUBLANES, d_lru), lambda i, tr, fl: (fl[2, i], 0, 0))],
        scratch_shapes=[pltpu.VMEM((tm + 2 * BF16_ROWS, d_lru), F32),
                        pltpu.VMEM((tm, d_lru), F32),
                        pltpu.VMEM((tm, d_lru), F32),
                        pltpu.VMEM((SUBLANES, d_lru), F32)])
    return pl.pallas_call(
        _fwd_kernel,
        grid_spec=grid_spec,
        out_shape=[jax.ShapeDtypeStruct((t, d_lru), BF16),
                   jax.ShapeDtypeStruct((ngrp, SUBLANES, d_lru), F32)],
        compiler_params=_cparams(("arbitrary",)),
        name="forward_scan",
    )(tile_row, flags, xa, xa, xa, conv_w, conv_b.reshape(1, d_lru), wg, bg, lam.reshape(1, d_lru), h0)


def _bwd_kernel(tr_ref, fl_ref, prev_ref, main_ref, next_ref, cw_ref, cb_ref, wg_ref, bg_ref, lam_ref,
                h0_ref, hf_ref, ga_ref, ybl_ref, ybc_ref, lat_hbm, ctx_hbm, g1_ref, sh2_ref, sc2_ref, wout_ref,
                n2_ref, rwt_ref, rb_ref,
                x1_ref, h2_ref, lg_ref, hlast_ref,
                ext_ref, a_ref, u_ref, h_ref, cat_ref, xbuf, xsem, pack_ref, *, geom):
    i = pl.program_id(0)
    n_steps = pl.num_programs(0)
    tm, d_lru = main_ref.shape
    x = _fetch_rows(lat_hbm, ctx_hbm, xbuf, xsem, i, n_steps,
                    lambda s: tr_ref[jnp.minimum(s, n_steps - 1)], geom)
    first = fl_ref[0, i] == 1
    last = fl_ref[1, i] == 1
    _fill_ext(ext_ref, prev_ref, main_ref, next_ref, first, last)
    _decay_and_input(ext_ref, cw_ref, cb_ref, wg_ref, bg_ref, lam_ref, a_ref, u_ref, tm)

    @pl.when(last)
    def _():
        h_ref[...] = h0_ref[0]

    h = _scan(a_ref, u_ref, h_ref[...], tm, reverse=True)
    h_ref[...] = h
    hlast_ref[0] = h

    ga = ga_ref[...].astype(F32)
    ya = (hf_ref[...].astype(F32) + u_ref[...]) * jax.nn.gelu(ga)
    cat_ref[:, :d_lru] = ya.astype(BF16)
    cat_ref[:, d_lru:] = jnp.where(tr_ref[i] < geom.n_lat_tiles, ybl_ref[...], ybc_ref[...])
    mix = jnp.dot(cat_ref[...], wout_ref[...], preferred_element_type=F32)
    x1 = x + _gate(mix, g1_ref[0])
    x1_ref[...] = x1
    h2 = _modulate(_rms(x1, n2_ref[...]), sc2_ref[0], sh2_ref[0])
    half = h2.shape[1] // 2
    _store_packed_tokens(h2_ref, h2[:, :half], h2[:, half:], pack_ref)
    lg_ref[...] = lax.dot_general(rwt_ref[...], h2.astype(BF16), (((1,), (1,)), ((), ())),
                                  preferred_element_type=F32) + rb_ref[...]


def _backward_scan_mix(xa, ga, hf, yb_lat, yb_ctx, x_lat, x_ctx, modtab, meta, conv_w, conv_b, wg, bg, lam, h0, w_out,
                       norm2_g, rwt, rb, geom):
    t, d_lru = xa.shape
    d = x_lat.shape[-1]
    d_pool = yb_lat.shape[1]
    n_lat, n_ctx = yb_lat.shape[0] // geom.tm, yb_ctx.shape[0] // geom.tm
    tm = geom.tm
    tile_row, flags = meta
    ngrp = h0.shape[0]
    nr = rwt.shape[0]
    const = dict(pipeline_mode=pl.Buffered(1))
    mset = lambda i, tr, fl: jnp.where(tr[i] < geom.n_lat_tiles, 0, 1)
    row = lambda i, tr, fl: (tr[i], 0)
    mod = lambda col: pl.BlockSpec((1, SUBLANES, d), lambda i, tr, fl: (mset(i, tr, fl), 0, col))
    grid_spec = pltpu.PrefetchScalarGridSpec(
        num_scalar_prefetch=2,
        grid=(t // tm,),
        in_specs=_halo_specs(tm, d_lru, t) + [
            pl.BlockSpec(conv_w.shape, lambda i, tr, fl: (0, 0)),
            pl.BlockSpec((1, d_lru), lambda i, tr, fl: (0, 0)),
            pl.BlockSpec(wg.shape, lambda i, tr, fl: (0, 0, 0), **const),
            pl.BlockSpec(bg.shape, lambda i, tr, fl: (0, 0, 0)),
            pl.BlockSpec((1, d_lru), lambda i, tr, fl: (0, 0)),
            pl.BlockSpec((1, SUBLANES, d_lru), lambda i, tr, fl: (fl[2, i], 0, 0)),
            pl.BlockSpec((tm, d_lru), row),
            pl.BlockSpec((tm, d_lru), row),
            pl.BlockSpec((tm, d_pool), lambda i, tr, fl: (jnp.minimum(tr[i], n_lat - 1), 0)),
            pl.BlockSpec((tm, d_pool), lambda i, tr, fl: (jnp.clip(tr[i] - n_lat, 0, n_ctx - 1), 0)),
            pl.BlockSpec(memory_space=pl.ANY),
            pl.BlockSpec(memory_space=pl.ANY),
            mod(2), mod(3), mod(4),
            pl.BlockSpec(w_out.shape, lambda i, tr, fl: (0, 0), **const),
            pl.BlockSpec((1, d), lambda i, tr, fl: (0, 0)),
            pl.BlockSpec(rwt.shape, lambda i, tr, fl: (0, 0)),
            pl.BlockSpec((nr, 1), lambda i, tr, fl: (0, 0))],
        out_specs=[pl.BlockSpec((tm, d), row),
                   pl.BlockSpec((tm * (d // 2) // LANES, LANES), row),
                   pl.BlockSpec((nr, tm), lambda i, tr, fl: (0, tr[i])),
                   pl.BlockSpec((1, SUBLANES, d_lru), lambda i, tr, fl: (fl[2, i], 0, 0))],
        scratch_shapes=[pltpu.VMEM((tm + 2 * BF16_ROWS, d_lru), F32),
                        pltpu.VMEM((tm, d_lru), F32),
                        pltpu.VMEM((tm, d_lru), F32),
                        pltpu.VMEM((SUBLANES, d_lru), F32),
                        pltpu.VMEM((tm, d_lru + d_pool), BF16),
                        pltpu.VMEM((2, tm // SUBLANES, SUBLANES, d), F32),
                        pltpu.SemaphoreType.DMA((2,)),
                        pltpu.VMEM((d // 2 // LANES, 2 * tm, LANES), F32)])
    return pl.pallas_call(
        functools.partial(_bwd_kernel, geom=geom),
        grid_spec=grid_spec,
        out_shape=[jax.ShapeDtypeStruct((t, d), F32),
                   jax.ShapeDtypeStruct((t * (d // 2) // LANES, LANES), U32),
                   jax.ShapeDtypeStruct((nr, t), F32),
                   jax.ShapeDtypeStruct((ngrp, SUBLANES, d_lru), F32)],
        compiler_params=_cparams(("arbitrary",)),
        name="backward_scan_mix",
    )(tile_row, flags, xa, xa, xa, conv_w, conv_b.reshape(1, d_lru), wg, bg, lam.reshape(1, d_lru), h0,
      hf, ga, yb_lat, yb_ctx, x_lat, x_ctx, modtab, modtab, modtab, w_out, norm2_g.reshape(1, d), rwt, rb)


def _shift_rows(v, k):
    if k == 0:
        return v
    z = jnp.zeros((abs(k) * SUBLANES, v.shape[1]), v.dtype)
    if k > 0:
        return jnp.concatenate([z, v[:-k * SUBLANES]], axis=0)
    return jnp.concatenate([v[-k * SUBLANES:], z], axis=0)


def _run_sum(v, m, direction):
    if m & (m - 1) == 0:
        k = 1
        while k < m:
            v = v + _shift_rows(v, -direction * k)
            k *= 2
        return v
    out = v
    for j in range(1, m):
        out = out + _shift_rows(v, -direction * j)
    return out


def _box_sum(v, w):
    lo = w // 2
    hi = w - 1 - lo
    s = _run_sum(v, hi + 1, +1)
    if lo:
        s = s + _shift_rows(_run_sum(v, lo, -1), 1)
    return s


def _window_count(n_rows, lanes, n_pos, w):
    lo = w // 2
    hi = w - 1 - lo
    p = lax.shift_right_logical(lax.broadcasted_iota(I32, (n_rows, lanes), 0), 3)
    return (jnp.minimum(p + hi + 1, n_pos) - jnp.maximum(p - lo, 0)).astype(F32)


def _pool_kernel(z_ref, ps_ref, o_ref, v_ref, *, tiles_per_group, grid_rows, grid_cols):
    group = pl.program_id(1) // tiles_per_group
    lanes = z_ref.shape[1]
    ps = ps_ref[...]

    def pool_1d(w):
        z = z_ref[...].astype(F32)
        n = z.shape[0] // SUBLANES
        mean = _box_sum(z, w) / _window_count(z.shape[0], lanes, n, w)
        o_ref[...] = ((mean - z) * ps).astype(o_ref.dtype)

    def pool_2d(w):
        lo = w // 2
        hi = w - 1 - lo
        blk = grid_cols * SUBLANES
        cw = _window_count(blk, lanes, grid_cols, w)

        def zrow(r):
            return z_ref[pl.ds(pl.multiple_of(r * blk, blk), blk), :].astype(F32)

        v = jnp.zeros((blk, lanes), F32)
        for r in range(hi):
            v = v + zrow(r)
        v_ref[...] = v

        def body(r, carry):
            add = r + hi
            sub = r - lo - 1
            v = v_ref[...]
            v = v + jnp.where(add < grid_rows, zrow(jnp.minimum(add, grid_rows - 1)), 0.0)
            v = v - jnp.where(sub >= 0, zrow(jnp.maximum(sub, 0)), 0.0)
            v_ref[...] = v
            ch = (jnp.minimum(r + hi + 1, grid_rows) - jnp.maximum(r - lo, 0)).astype(F32)
            mean = _box_sum(v, w) / (ch * cw)
            o_ref[pl.ds(pl.multiple_of(r * blk, blk), blk), :] = ((mean - zrow(r)) * ps).astype(o_ref.dtype)
            return carry

        lax.fori_loop(0, grid_rows, body, 0)

    for g, w in enumerate(POOL_WINDOWS):
        @pl.when(group == g)
        def _(w=w):
            if grid_rows is None:
                pool_1d(w)
            else:
                pool_2d(w)


def _pool(z, pool_scale, rows_per_group, n_groups, first_block, grid_rows):
    d_pool = z.shape[1]
    gw = d_pool // len(POOL_WINDOWS)
    lanes = LANES
    blk = GRID_W * SUBLANES
    return pl.pallas_call(
        functools.partial(_pool_kernel, tiles_per_group=gw // lanes, grid_rows=grid_rows, grid_cols=GRID_W),
        grid=(n_groups, d_pool // lanes),
        in_specs=[pl.BlockSpec((rows_per_group, lanes), lambda g, j: (first_block + g, j)),
                  pl.BlockSpec((1, lanes), lambda g, j: (0, j))],
        out_specs=pl.BlockSpec((rows_per_group, lanes), lambda g, j: (g, j)),
        out_shape=jax.ShapeDtypeStruct((n_groups * rows_per_group, d_pool), BF16),
        scratch_shapes=[pltpu.VMEM((blk, lanes), F32)],
        compiler_params=_cparams(("arbitrary", "arbitrary")),
        name="pool_grid" if grid_rows is not None else "pool_seq",
    )(z, pool_scale.reshape(1, d_pool))


def _route_kernel(lg_ref, oi_ref, of_ref, cnt_ref, bc_ref, tri_ref, carry_ref, *, n_groups, per_group):
    step = pl.program_id(0)
    tt = lg_ref.shape[1]
    n_exp = n_groups * per_group

    @pl.when(step == 0)
    def _():
        r = lax.broadcasted_iota(I32, tri_ref.shape, 0)
        c = lax.broadcasted_iota(I32, tri_ref.shape, 1)
        tri_ref[...] = (r <= c).astype(BF16)
        carry_ref[...] = jnp.zeros_like(carry_ref)

    row = lambda k: lg_ref[k:k + 1, :]
    cmax = row(0)
    gi = jnp.zeros((1, tt), I32)
    for g in range(1, n_groups):
        better = row(g) > cmax
        gi = jnp.where(better, g, gi)
        cmax = jnp.where(better, row(g), cmax)
    denom = jnp.zeros((1, tt), F32)
    for g in range(n_groups):
        denom = denom + jnp.exp(row(g) - cmax)
    pg = 1.0 / denom
    fine = []
    for j in range(per_group):
        f = row(n_groups + j)
        for g in range(1, n_groups):
            f = jnp.where(gi == g, row(n_groups + g * per_group + j), f)
        fine.append(f)
    v1 = fine[0]
    i1 = jnp.zeros((1, tt), I32)
    for j in range(1, per_group):
        better = fine[j] > v1
        i1 = jnp.where(better, j, i1)
        v1 = jnp.where(better, fine[j], v1)
    v2 = jnp.full((1, tt), -jnp.inf, F32)
    i2 = jnp.zeros((1, tt), I32)
    for j in range(per_group):
        better = jnp.logical_and(i1 != j, fine[j] > v2)
        i2 = jnp.where(better, j, i2)
        v2 = jnp.where(better, fine[j], v2)
    ex = jnp.exp(v2 - v1)
    w1 = (1.0 / (1.0 + ex)) * pg
    w2 = (ex / (1.0 + ex)) * pg
    e1 = gi * per_group + i1
    e2 = gi * per_group + i2
    eid = lax.broadcasted_iota(I32, (n_exp, tt), 0)
    hit1 = eid == e1
    hit2 = eid == e2
    member = jnp.logical_or(hit1, hit2).astype(BF16)
    cb = tri_ref.shape[0]
    carry = carry_ref[...]
    cums = []
    lane = lax.broadcasted_iota(I32, bc_ref.shape, 1)
    block_ends = jnp.zeros(bc_ref.shape, F32)
    for s in range(tt // cb):
        c = jnp.dot(member[:, s * cb:(s + 1) * cb], tri_ref[...], preferred_element_type=F32) + carry
        carry = c[:, cb - 1:cb]
        cums.append(c)
        block_ends = jnp.where(lane == s, carry, block_ends)
    bc_ref[...] = block_ends.astype(I32)
    carry_ref[...] = carry
    cum = jnp.concatenate(cums, axis=1)
    rank1 = jnp.sum(jnp.where(hit1, cum, 0.0), axis=0, keepdims=True) - 1.0
    rank2 = jnp.sum(jnp.where(hit2, cum, 0.0), axis=0, keepdims=True) - 1.0
    zi = jnp.zeros((SUBLANES - 4, tt), I32)
    oi_ref[...] = jnp.concatenate([e1, e2, rank1.astype(I32), rank2.astype(I32), zi], axis=0)
    of_ref[...] = jnp.concatenate([w1, w2, jnp.zeros((of_ref.shape[0] - 2, tt), F32)], axis=0)
    cnt_ref[...] = jnp.broadcast_to(carry, cnt_ref.shape).astype(I32)


def _route(logits_t, n_groups, per_group, cb):
    nr, t = logits_t.shape
    tt = max(k for k in range(cb, min(ROUTE_TILE, t) + 1, cb) if t % k == 0)
    n_exp = n_groups * per_group
    return pl.pallas_call(
        functools.partial(_route_kernel, n_groups=n_groups, per_group=per_group),
        grid=(t // tt,),
        in_specs=[pl.BlockSpec((nr, tt), lambda i: (0, i))],
        out_specs=[pl.BlockSpec((SUBLANES, tt), lambda i: (0, i)),
                   pl.BlockSpec((LANES, tt), lambda i: (0, i)),
                   pl.BlockSpec((n_exp, LANES), lambda i: (0, 0)),
                   pl.BlockSpec((n_exp, LANES), lambda i: (0, i))],
        out_shape=[jax.ShapeDtypeStruct((SUBLANES, t), I32),
                   jax.ShapeDtypeStruct((LANES, t), F32),
                   jax.ShapeDtypeStruct((n_exp, LANES), I32),
                   jax.ShapeDtypeStruct((n_exp, LANES * (t // tt)), I32)],
        scratch_shapes=[pltpu.VMEM((cb, cb), BF16), pltpu.VMEM((n_exp, 1), F32)],
        compiler_params=_cparams(("arbitrary",)),
        name="route",
    )(logits_t)


def _tiles_for(cnt, tmx):
    return jnp.right_shift(cnt + (SEG_CHUNK + tmx - 1), tmx.bit_length() - 1)


def _plan_kernel(ri_ref, cnt_ref, bc_ref, bcp_ref, lp_ref, dst_ref, src_ref, len_ref, te_ref, nv_ref,
                 off_ref, *, tmx, tb):
    step = pl.program_id(0)
    n_exp = cnt_ref.shape[0]
    tt = ri_ref.shape[1]
    shift = tmx.bit_length() - 1
    n_tile = _tiles_for(cnt_ref[:, 0:1], tmx)

    def exclusive_prefix(col):
        run = jnp.zeros((1, 1), I32)
        parts = []
        for e in range(n_exp):
            parts.append(run)
            run = run + col[e:e + 1, :]
        return jnp.concatenate(parts, axis=0), run

    off, total = exclusive_prefix(jnp.left_shift(n_tile, shift))
    before = jnp.where(step == 0, 0, bcp_ref[:, tt // tb - 1:tt // tb])
    lane = lax.broadcasted_iota(I32, dst_ref.shape, 1)
    eid = lax.broadcasted_iota(I32, (n_exp, tb), 0)
    dst = jnp.zeros(dst_ref.shape, I32)
    src = jnp.zeros(dst_ref.shape, I32)
    length = jnp.zeros(dst_ref.shape, I32)
    rows = []
    for k in range(tt // tb):
        end = bc_ref[:, k:k + 1]
        n = end - before
        aligned = jnp.left_shift(jnp.right_shift(n + (SEG_CHUNK - 1), SEG_SHIFT), SEG_SHIFT)
        start, _ = exclusive_prefix(aligned)
        base = start - before
        sl = slice(k * tb, (k + 1) * tb)
        place = lambda e_row, r_row: jnp.sum(jnp.where(eid == e_row, base, 0), axis=0, keepdims=True) + r_row
        rows.append(jnp.concatenate([place(ri_ref[0:1, sl], ri_ref[2:3, sl]),
                                     place(ri_ref[1:2, sl], ri_ref[3:4, sl])], axis=0))
        dst = jnp.where(lane == k, off + before, dst)
        src = jnp.where(lane == k, start, src)
        length = jnp.where(lane == k, n, length)
        before = end
    lp_ref[...] = jnp.concatenate([jnp.concatenate(rows, axis=1),
                                   jnp.zeros((SUBLANES - TOP_K_FINE, tt), I32)], axis=0)
    dst_ref[...] = dst
    src_ref[...] = src
    len_ref[...] = length
    end_tile = jnp.right_shift(off, shift) + n_tile
    k = lax.broadcasted_iota(I32, (n_exp, te_ref.shape[1]), 1)
    te = jnp.sum((k >= end_tile).astype(I32), axis=0, keepdims=True)
    te_ref[...] = jnp.minimum(te, n_exp - 1)
    nv_ref[...] = jnp.broadcast_to(jnp.right_shift(total, shift), nv_ref.shape)
    off_ref[...] = jnp.broadcast_to(off, off_ref.shape)


def _plan(ri, counts, block_counts, tmx, n_tiles, tb):
    _, t = ri.shape
    n_exp = counts.shape[0]
    steps = block_counts.shape[1] // LANES
    tt = t // steps
    ntp = -(-n_tiles // LANES) * LANES
    per_step = lambda: pl.BlockSpec((n_exp, LANES), lambda i: (0, i))
    whole = lambda w: pl.BlockSpec((n_exp, w), lambda i: (0, 0))
    tab = jax.ShapeDtypeStruct((n_exp, LANES * steps), I32)
    lp, dst, src, length, te, nv, off = pl.pallas_call(
        functools.partial(_plan_kernel, tmx=tmx, tb=tb),
        grid=(steps,),
        in_specs=[pl.BlockSpec((SUBLANES, tt), lambda i: (0, i)),
                  whole(LANES),
                  per_step(),
                  pl.BlockSpec((n_exp, LANES), lambda i: (0, jnp.maximum(i - 1, 0)))],
        out_specs=[pl.BlockSpec((SUBLANES, tt), lambda i: (0, i)),
                   per_step(), per_step(), per_step(),
                   pl.BlockSpec((1, ntp), lambda i: (0, 0)),
                   pl.BlockSpec((1, LANES), lambda i: (0, 0)),
                   whole(LANES)],
        out_shape=[jax.ShapeDtypeStruct((SUBLANES, t), I32), tab, tab, tab,
                   jax.ShapeDtypeStruct((1, ntp), I32),
                   jax.ShapeDtypeStruct((1, LANES), I32),
                   jax.ShapeDtypeStruct((n_exp, LANES), I32)],
        compiler_params=_cparams(("arbitrary",)),
        name="plan",
    )(ri, counts, block_counts, block_counts)
    flat = lambda a: a.reshape(n_exp, steps, LANES)[:, :, :tt // tb].transpose(1, 2, 0).reshape(-1)
    tables = (flat(dst), flat(src), flat(length))
    return (lp[0].reshape(t // tb, tb), lp[1].reshape(t // tb, tb), tables,
            te[0, :n_tiles], nv[0, :1], off[:, 0])


def _load_positions(pos1_hbm, pos2_hbm, p1_ref, p2_ref, isem, step, n_steps):
    tb = p1_ref.shape[0] // 2
    slot = lax.rem(step, 2)

    def copies(blk, half):
        rows = pl.ds(pl.multiple_of(half * tb, tb), tb)
        return (pltpu.make_async_copy(pos1_hbm.at[blk], p1_ref.at[rows], isem.at[0, half]),
                pltpu.make_async_copy(pos2_hbm.at[blk], p2_ref.at[rows], isem.at[1, half]))

    @pl.when(step == 0)
    def _():
        for cp in copies(step, slot):
            cp.start()

    @pl.when(step + 1 < n_steps)
    def _():
        for cp in copies(step + 1, 1 - slot):
            cp.start()

    for cp in copies(step, slot):
        cp.wait()
    return slot * tb


def _token_rows(token, count, nc):
    return pl.ds(pl.multiple_of(token * nc, nc), count * nc)


def _segment_copy(stage_ref, far_hbm, near_tok, far_tok, sem, to_far, nc):
    near = stage_ref.at[_token_rows(near_tok, SEG_CHUNK, nc), :]
    far = far_hbm.at[_token_rows(far_tok, SEG_CHUNK, nc), :]
    return pltpu.make_async_copy(near, far, sem) if to_far else pltpu.make_async_copy(far, near, sem)


def _start_segments(tables, blk, n_exp, stage_ref, far_hbm, sem, to_far, nc):
    dst_ref, src_ref, len_ref = tables

    total = jnp.int32(0)
    for e in range(n_exp):
        k = blk * n_exp + e
        n_chunks = jnp.right_shift(len_ref[k] + (SEG_CHUNK - 1), SEG_SHIFT)
        near0 = src_ref[k]
        far0 = dst_ref[k]

        def one(c, carry, near0=near0, far0=far0, queue=e % 2):
            _segment_copy(stage_ref, far_hbm, near0 + c * SEG_CHUNK, far0 + c * SEG_CHUNK, sem,
                          to_far, nc).start(priority=queue)
            return carry
        lax.fori_loop(0, n_chunks, one, 0)
        total = total + n_chunks
    return total


def _wait_segments(stage_ref, far_hbm, sem, n, to_far, nc):
    copy = _segment_copy(stage_ref, far_hbm, 0, 0, sem, to_far, nc)

    def body(_, c):
        copy.wait()
        return c
    lax.fori_loop(0, n, body, 0)


def _dispatch_kernel(dst_ref, src_ref, len_ref, off_ref, cnt_ref, lp1_hbm, lp2_hbm, h2_ref, xs_hbm,
                     p1_ref, p2_ref, stage, nd_ref, zero_ref, isem, ssem, zsem, *, tmx):
    step = pl.program_id(0)
    n_steps = pl.num_programs(0)
    slot = lax.rem(step, 2)
    n_exp = off_ref.shape[0]
    nc = zero_ref.shape[0]
    n_blk = h2_ref.shape[0] // nc // SUBLANES
    tables = (dst_ref, src_ref, len_ref)

    @pl.when(step == 0)
    def _():
        for s in range(stage.shape[0]):
            stage[s] = _packed_zeros(stage.shape[1])

    p0 = _load_positions(lp1_hbm, lp2_hbm, p1_ref, p2_ref, isem, step, n_steps)

    def place(blk, c):
        j0 = blk * SUBLANES
        for s in range(SUBLANES):
            row = h2_ref[_token_rows(j0 + s, 1, nc), :]
            stage[slot, _token_rows(p1_ref[p0 + j0 + s], 1, nc), :] = row
            stage[slot, _token_rows(p2_ref[p0 + j0 + s], 1, nc), :] = row
        return c
    lax.fori_loop(0, n_blk, place, 0, unroll=2)

    @pl.when(step >= 1)
    def _():
        _wait_segments(stage.at[1 - slot], xs_hbm, ssem.at[1 - slot], nd_ref[1 - slot], True, nc)

    nd_ref[slot] = _start_segments(tables, step, n_exp, stage.at[slot], xs_hbm, ssem.at[slot], True, nc)

    @pl.when(step == n_steps - 1)
    def _():
        _wait_segments(stage.at[slot], xs_hbm, ssem.at[slot], nd_ref[slot], True, nc)
        zero_ref[...] = _packed_zeros(nc)
        shift = tmx.bit_length() - 1

        def pad_expert(e, c):
            cnt = cnt_ref[e]
            n_pad = jnp.left_shift(_tiles_for(cnt, tmx), shift) - cnt
            first = off_ref[e] + cnt
            zero_copy = lambda r: pltpu.make_async_copy(
                zero_ref, xs_hbm.at[_token_rows(first + r, 1, nc), :], zsem)

            def fill(r, c2):
                zero_copy(r).start()
                return c2
            lax.fori_loop(0, n_pad, fill, 0)

            def done(r, c2):
                zero_copy(r).wait()
                return c2
            lax.fori_loop(0, n_pad, done, 0)
            return c
        lax.fori_loop(0, off_ref.shape[0], pad_expert, 0)


def _stage_rows(tb, n_exp):
    return TOP_K_FINE * tb + n_exp * SEG_CHUNK


def _dispatch(tables, offsets, counts, lp1, lp2, h2p, p_max, tmx):
    nb, tb = lp1.shape
    nc = h2p.shape[0] // (nb * tb)
    n_exp = offsets.shape[0]
    hbm = lambda: pl.BlockSpec(memory_space=pl.ANY)
    grid_spec = pltpu.PrefetchScalarGridSpec(
        num_scalar_prefetch=5,
        grid=(nb,),
        in_specs=[hbm(), hbm(),
                  pl.BlockSpec((tb * nc, LANES), lambda i, *_: (i, 0))],
        out_specs=hbm(),
        scratch_shapes=[pltpu.SMEM((2 * tb,), I32),
                        pltpu.SMEM((2 * tb,), I32),
                        pltpu.VMEM((2, _stage_rows(tb, n_exp) * nc, LANES), U32),
                        pltpu.SMEM((2,), I32),
                        pltpu.VMEM((nc, LANES), U32),
                        pltpu.SemaphoreType.DMA((2, 2)),
                        pltpu.SemaphoreType.DMA((2,)),
                        pltpu.SemaphoreType.DMA(())])
    return pl.pallas_call(
        functools.partial(_dispatch_kernel, tmx=tmx),
        grid_spec=grid_spec,
        out_shape=jax.ShapeDtypeStruct((p_max * nc, LANES), U32),
        compiler_params=_cparams(("arbitrary",)),
        name="dispatch",
    )(*tables, offsets, counts, lp1, lp2, h2p)


def _expert_kernel(te_ref, nv_ref, x_ref, w1_ref, w3_ref, w2_ref, y_ref, pack_ref, *, tmx):
    @pl.when(pl.program_id(0) < nv_ref[0])
    def _():
        lo, hi = _load_packed_tokens(x_ref, tmx, pack_ref)
        lo = lo.astype(BF16)
        hi = hi.astype(BF16)
        half = lo.shape[1]
        mm = lambda w_ref: (jnp.dot(lo, w_ref[0, :half, :].astype(BF16), preferred_element_type=F32)
                            + jnp.dot(hi, w_ref[0, half:, :].astype(BF16), preferred_element_type=F32))
        h1 = mm(w1_ref)
        h3 = mm(w3_ref)
        hh = (h1 * jax.nn.sigmoid(h1) * h3).astype(BF16)
        y = jnp.dot(hh, w2_ref[0].astype(BF16), preferred_element_type=F32)
        _store_packed_tokens(y_ref, y[:, :half], y[:, half:], pack_ref)


def _expert_mlp(xs, tile_expert, n_valid, w1, w3, w2, tmx):
    d, de = w1.shape[1], w1.shape[2]
    nc = d // 2 // LANES
    p = xs.shape[0] // nc
    tile = lambda i, te, nv: (jnp.minimum(i, nv[0] - 1), 0)
    grid_spec = pltpu.PrefetchScalarGridSpec(
        num_scalar_prefetch=2,
        grid=(p // tmx,),
        in_specs=[pl.BlockSpec((tmx * nc, LANES), tile),
                  pl.BlockSpec((1, d, de), lambda i, te, nv: (te[i], 0, 0)),
                  pl.BlockSpec((1, d, de), lambda i, te, nv: (te[i], 0, 0)),
                  pl.BlockSpec((1, de, d), lambda i, te, nv: (te[i], 0, 0))],
        out_specs=pl.BlockSpec((tmx * nc, LANES), tile),
        scratch_shapes=[pltpu.VMEM((nc, 2 * tmx, LANES), F32)])
    return pl.pallas_call(
        functools.partial(_expert_kernel, tmx=tmx),
        grid_spec=grid_spec,
        out_shape=jax.ShapeDtypeStruct((p * nc, LANES), U32),
        compiler_params=_cparams(("arbitrary",)),
        name="expert_mlp",
    )(tile_expert, n_valid, xs, w1, w3, w2)


def _combine_kernel(dst_ref, src_ref, len_ref, lp1_hbm, lp2_hbm, ys_hbm, x1_ref, rf_ref, g2_ref, fg_ref,
                    ylat_hbm, yctx_hbm, p1_ref, p2_ref, stage, gbuf, pack_ref, obuf, nd_ref, isem, csem, osem,
                    *, geom, n_exp):
    step = pl.program_id(0)
    n_steps = pl.num_programs(0)
    tm = geom.tm
    n_blk = tm // SUBLANES
    slot = lax.rem(step, 2)
    nc = gbuf.shape[1] // tm
    tables = (dst_ref, src_ref, len_ref)

    def fetch(blk, gs):
        nd_ref[gs] = _start_segments(tables, blk, n_exp, stage.at[gs], ys_hbm, csem.at[gs], False, nc)

    @pl.when(step == 0)
    def _():
        fetch(step, slot)

    @pl.when(step + 1 < n_steps)
    def _():
        fetch(step + 1, 1 - slot)

    _wait_segments(stage.at[slot], ys_hbm, csem.at[slot], nd_ref[slot], False, nc)
    p0 = _load_positions(lp1_hbm, lp2_hbm, p1_ref, p2_ref, isem, step, n_steps)

    def pick(blk, c):
        j0 = blk * SUBLANES
        for s in range(SUBLANES):
            rows = _token_rows(j0 + s, 1, nc)
            gbuf[0, rows, :] = stage[slot, _token_rows(p1_ref[p0 + j0 + s], 1, nc), :]
            gbuf[1, rows, :] = stage[slot, _token_rows(p2_ref[p0 + j0 + s], 1, nc), :]
        return c
    lax.fori_loop(0, n_blk, pick, 0, unroll=2)

    wt = rf_ref[...].T
    a_lo, a_hi = _load_packed_tokens(gbuf.at[0], tm, pack_ref.at[0])
    b_lo, b_hi = _load_packed_tokens(gbuf.at[1], tm, pack_ref.at[1])
    w1 = wt[:, 0:1]
    w2 = wt[:, 1:2]
    moe = jnp.concatenate([w1 * a_lo + w2 * b_lo, w1 * a_hi + w2 * b_hi], axis=1)
    x = x1_ref[...] + _gate(moe, g2_ref[0])
    out = _rms(x, fg_ref[...])

    start = lambda cp: cp.start()
    wait = lambda cp: cp.wait()
    put = functools.partial(_tile_copies, ylat_hbm, yctx_hbm, geom=geom, to_rows=False)

    @pl.when(step >= 2)
    def _():
        put(obuf.at[slot], osem.at[slot], step - 2, fn=wait)

    obuf[slot] = out.reshape(obuf.shape[1:])
    put(obuf.at[slot], osem.at[slot], step, fn=start)

    @pl.when(step == n_steps - 1)
    def _():
        @pl.when(n_steps > 1)
        def _():
            put(obuf.at[1 - slot], osem.at[1 - slot], step - 1, fn=wait)
        put(obuf.at[slot], osem.at[slot], step, fn=wait)


def _combine(tables, lp1, lp2, ys, x1, rf, modtab, final_g, lat_shape, ctx_shape, geom):
    t, d = x1.shape
    tm = geom.tm
    nc = d // 2 // LANES
    n_exp = tables[0].shape[0] // (t // tm)
    mset = lambda i, *_: jnp.where(i < geom.n_lat_tiles, 0, 1)
    hbm = lambda: pl.BlockSpec(memory_space=pl.ANY)
    grid_spec = pltpu.PrefetchScalarGridSpec(
        num_scalar_prefetch=3,
        grid=(t // tm,),
        in_specs=[hbm(), hbm(), hbm(),
                  pl.BlockSpec((tm, d), lambda i, *_: (i, 0)),
                  pl.BlockSpec((LANES, tm), lambda i, *_: (0, i)),
                  pl.BlockSpec((1, SUBLANES, d), lambda i, *_: (mset(i), 0, 5)),
                  pl.BlockSpec((1, d), lambda i, *_: (0, 0))],
        out_specs=[hbm(), hbm()],
        scratch_shapes=[pltpu.SMEM((2 * tm,), I32),
                        pltpu.SMEM((2 * tm,), I32),
                        pltpu.VMEM((2, _stage_rows(tm, n_exp) * nc, LANES), U32),
                        pltpu.VMEM((TOP_K_FINE, tm * nc, LANES), U32),
                        pltpu.VMEM((TOP_K_FINE, nc, 2 * tm, LANES), F32),
                        pltpu.VMEM((2, tm // SUBLANES, SUBLANES, d), F32),
                        pltpu.SMEM((2,), I32),
                        pltpu.SemaphoreType.DMA((2, 2)),
                        pltpu.SemaphoreType.DMA((2,)),
                        pltpu.SemaphoreType.DMA((2,))])
    return pl.pallas_call(
        functools.partial(_combine_kernel, geom=geom, n_exp=n_exp),
        grid_spec=grid_spec,
        out_shape=[jax.ShapeDtypeStruct(lat_shape, F32), jax.ShapeDtypeStruct(ctx_shape, F32)],
        compiler_params=_cparams(("arbitrary",)),
        name="combine",
    )(*tables, lp1, lp2, ys, x1, rf, modtab, final_g.reshape(1, d))


def _tile_meta(groups, tm):
    rows, first, last, grp = [], [], [], []
    blk = 0
    for g, r in enumerate(groups):
        nc = r // tm
        for c in range(nc):
            rows.append(blk + c)
            first.append(int(c == 0))
            last.append(int(c == nc - 1))
            grp.append(g)
        blk += nc
    fwd = (np.array(rows, np.int32), np.array([first, last, grp], np.int32))
    order = []
    blk = 0
    for r in groups:
        nc = r // tm
        order.extend(range(blk + nc - 1, blk - 1, -1))
        blk += nc
    order = np.array(order)
    bwd = (fwd[0][order], fwd[1][:, order])
    return fwd, bwd


def kernel(x_prompt, x_sample, state_lru, c, c_ctx, w_mod, b_mod, norm1_g, w_in, conv_w, conv_b, lru_wa, lru_ba, lru_wx, lru_bx, lru_lambda, pool_w, pool_scale, w_out, norm2_g, router_coarse_w, router_coarse_b, router_fine_w, router_fine_b, exp_w1, exp_w3, exp_w2, final_norm_g):
    bp, sp, d = x_prompt.shape
    bs, ss, _ = x_sample.shape
    d_lru = lru_lambda.shape[-1]
    heads, bw = lru_wa.shape[2], lru_wa.shape[3]
    n_groups, per_group = router_fine_w.shape[2], router_fine_w.shape[3]
    n_exp = n_groups * per_group
    assert w_mod.shape[0] == 1 and bs == SUBLANES and bp % SUBLANES == 0 and ss % GRID_W == 0
    assert EXPERT_TILE & (EXPERT_TILE - 1) == 0
    n_lat_groups, n_ctx_groups = bs // SUBLANES, bp // SUBLANES
    lat_rows, ctx_rows = ss * SUBLANES, sp * SUBLANES
    assert (n_lat_groups * lat_rows) % ctx_rows == 0
    tm = min(TOKEN_TILE, ctx_rows, lat_rows)
    geom = _Geom(tm=tm, n_lat_tiles=n_lat_groups * lat_rows // tm, lat_chunks=lat_rows // tm,
                 ctx_chunks=ctx_rows // tm)
    n_rows = n_lat_groups * lat_rows + n_ctx_groups * ctx_rows
    groups = [lat_rows] * n_lat_groups + [ctx_rows] * n_ctx_groups
    (f_rows, f_flags), (b_rows, b_flags) = _tile_meta(groups, tm)

    cond = jnp.zeros((2 * SUBLANES, d), F32).at[:bs].set(c).at[bs].set(c_ctx)
    mod = _modulation(cond, w_mod[0], b_mod[0])
    modtab = jnp.stack([mod[:SUBLANES], jnp.broadcast_to(mod[SUBLANES], (SUBLANES, mod.shape[1]))])
    h0_lat = state_lru[:, 0].reshape(n_lat_groups, SUBLANES, 2, d_lru)
    h0 = jnp.concatenate([h0_lat, jnp.zeros((n_ctx_groups, SUBLANES, 2, d_lru), F32)], axis=0)
    h0 = h0.transpose(0, 2, 1, 3)

    xa, ga, z = _input_projection(x_sample, x_prompt, modtab, norm1_g[0], w_in[0],
                                  pool_w[0].astype(BF16), n_rows, geom)

    def gate_weights(direction):
        wg = jnp.concatenate([lru_wa[0, direction], lru_wx[0, direction]], axis=-1).astype(BF16)
        bg = jnp.concatenate([lru_ba[0, direction].reshape(heads, 1, bw),
                              lru_bx[0, direction].reshape(heads, 1, bw)], axis=-1)
        return wg, bg

    log_decay = jax.nn.log_sigmoid(lru_lambda[0])
    wg_f, bg_f = gate_weights(0)
    wg_b, bg_b = gate_weights(1)
    hf, hf_last = _forward_scan(xa, (jnp.asarray(f_rows), jnp.asarray(f_flags)), conv_w[0], conv_b[0],
                                wg_f, bg_f, log_decay[0], h0[:, 0], tm)

    yb_lat = _pool(z, pool_scale[0], lat_rows, n_lat_groups, 0, lat_rows // (GRID_W * SUBLANES))
    yb_ctx = _pool(z, pool_scale[0], ctx_rows, n_ctx_groups, n_lat_groups * lat_rows // ctx_rows, None)

    n_logits = n_groups + n_exp
    rw = jnp.concatenate([router_coarse_w[0], router_fine_w[0].reshape(d, n_exp)], axis=1)
    rwt = jnp.zeros((LANES, d), BF16).at[:n_logits].set(rw.T.astype(BF16))
    rb = jnp.zeros((LANES, 1), F32).at[:n_logits, 0].set(
        jnp.concatenate([router_coarse_b[0], router_fine_b[0].reshape(n_exp)]))
    x1, h2p, logits_t, hb_last = _backward_scan_mix(
        xa, ga, hf, yb_lat, yb_ctx, x_sample, x_prompt, modtab, (jnp.asarray(b_rows), jnp.asarray(b_flags)),
        conv_w[0], conv_b[0], wg_b, bg_b, log_decay[1], h0[:, 1], w_out[0].astype(BF16), norm2_g[0],
        rwt, rb, geom)

    ri, rf, counts, block_counts = _route(logits_t, n_groups, per_group, tm)
    n_tiles = -(-(TOP_K_FINE * n_rows + n_exp * (EXPERT_TILE + SEG_CHUNK)) // EXPERT_TILE)
    lp1, lp2, tables, tile_expert, n_valid, offsets = _plan(ri, counts, block_counts, EXPERT_TILE, n_tiles, tm)
    xs = _dispatch(tables, offsets, counts[:, 0], lp1, lp2, h2p, n_tiles * EXPERT_TILE, EXPERT_TILE)
    ys = _expert_mlp(xs, tile_expert, n_valid, exp_w1[0], exp_w3[0], exp_w2[0], EXPERT_TILE)
    y_sample, y_prompt = _combine(tables, lp1, lp2, ys, x1, rf, modtab, final_norm_g, x_sample.shape,
                                  x_prompt.shape, geom)

    st = jnp.stack([hf_last[n_lat_groups:], hb_last[n_lat_groups:]], axis=2)
    state_new = st.reshape(bp, 1, 2, d_lru).astype(x_prompt.dtype)
    return (y_prompt, y_sample, state_new)
```

```python
import functools
from typing import NamedTuple

import numpy as np
import jax
import jax.numpy as jnp
from jax import lax
from jax.experimental import pallas as pl
from jax.experimental.pallas import tpu as pltpu

GRID_W = 64
CONV_W = 4
RG_C = 8.0
POOL_WINDOWS = (2, 4, 8, 16)
TOP_K_FINE = 2
EPS = 1e-6
EXPM1_SERIES_BELOW = 0.125

SUBLANES = 8
LANES = 128
BF16_ROWS = 16
TOKEN_TILE = 512
EXPERT_TILE = 512
ROUTE_TILE = 2048
SEG_SHIFT = 6
SEG_CHUNK = 1 << SEG_SHIFT
MOD_COL_TILE = 1024
WEIGHT_STAGE_ROWS = 256
VMEM_LIMIT = 60 * 1024 * 1024

F32 = jnp.float32
BF16 = jnp.bfloat16
U32 = jnp.uint32
I32 = jnp.int32


class _Geom(NamedTuple):
    tm: int
    n_lat_tiles: int
    lat_chunks: int
    ctx_chunks: int


def _cparams(sem):
    return pltpu.CompilerParams(dimension_semantics=sem, vmem_limit_bytes=VMEM_LIMIT)


def _per_sequence(y, m):
    rows, d = y.shape
    return y.reshape(rows // SUBLANES, SUBLANES, d), m[None]


def _modulate(y, scale, shift):
    y3, sc = _per_sequence(y, scale)
    _, sh = _per_sequence(y, shift)
    return (y3 * (1.0 + sc) + sh).reshape(y.shape)


def _gate(y, g):
    y3, g3 = _per_sequence(y, g)
    return (y3 * g3).reshape(y.shape)


def _rms(x, g):
    ms = jnp.mean(x * x, axis=-1, keepdims=True)
    return x * lax.rsqrt(ms + EPS) * g


def _store_packed_tokens(ref, lo, hi, scratch):
    n, words = lo.shape
    nc = words // LANES
    for c in range(nc):
        sl = slice(c * LANES, (c + 1) * LANES)
        scratch[c, pl.ds(0, n, stride=2), :] = lo[:, sl]
        scratch[c, pl.ds(1, n, stride=2), :] = hi[:, sl]
        ref[pl.ds(c, n, stride=nc), :] = pltpu.bitcast(scratch[c].astype(BF16), U32)


def _packed_zeros(rows):
    return pltpu.bitcast(jnp.zeros((2 * rows, LANES), BF16), U32)


def _load_packed_tokens(ref, n, scratch):
    nc = ref.shape[0] // n
    los, his = [], []
    for c in range(nc):
        scratch[c] = pltpu.bitcast(ref[pl.ds(c, n, stride=nc), :], BF16).astype(F32)
        los.append(scratch[c, pl.ds(0, n, stride=2), :])
        his.append(scratch[c, pl.ds(1, n, stride=2), :])
    return jnp.concatenate(los, axis=1), jnp.concatenate(his, axis=1)


def _tile_copies(lat_hbm, ctx_hbm, buf, sem, tile, geom, to_rows, fn):
    tt = geom.tm // SUBLANES

    def run(hbm, k, chunks):
        g = lax.div(k, jnp.int32(chunks))
        c = lax.rem(k, jnp.int32(chunks))
        for b in range(SUBLANES):
            h = hbm.at[g * SUBLANES + b, pl.ds(c * tt, tt), :]
            v = buf.at[:, b, :]
            fn(pltpu.make_async_copy(h, v, sem) if to_rows else pltpu.make_async_copy(v, h, sem))

    @pl.when(tile < geom.n_lat_tiles)
    def _():
        run(lat_hbm, tile, geom.lat_chunks)

    @pl.when(tile >= geom.n_lat_tiles)
    def _():
        run(ctx_hbm, tile - geom.n_lat_tiles, geom.ctx_chunks)


def _fetch_rows(lat_hbm, ctx_hbm, xbuf, sem, step, n_steps, tile_of, geom):
    slot = lax.rem(step, 2)
    start = lambda cp: cp.start()
    wait = lambda cp: cp.wait()

    @pl.when(step == 0)
    def _():
        _tile_copies(lat_hbm, ctx_hbm, xbuf.at[0], sem.at[0], tile_of(jnp.int32(0)), geom, True, start)

    @pl.when(step + 1 < n_steps)
    def _():
        nxt = 1 - slot
        _tile_copies(lat_hbm, ctx_hbm, xbuf.at[nxt], sem.at[nxt], tile_of(step + 1), geom, True, start)

    _tile_copies(lat_hbm, ctx_hbm, xbuf.at[slot], sem.at[slot], tile_of(step), geom, True, wait)
    tt, _, d = xbuf.shape[1:]
    return xbuf[slot].reshape(tt * SUBLANES, d)


def _mod_kernel(c_ref, w_ref, b_ref, o_ref):
    c = c_ref[...]
    s = c * jax.nn.sigmoid(c)
    o_ref[...] = jnp.dot(s.astype(BF16), w_ref[...].astype(BF16),
                         preferred_element_type=F32) + b_ref[...]


def _modulation(cond, w_mod, b_mod):
    rows, d = cond.shape
    n = w_mod.shape[1]
    tn = min(MOD_COL_TILE, n)
    return pl.pallas_call(
        _mod_kernel,
        grid=(n // tn,),
        in_specs=[pl.BlockSpec((rows, d), lambda j: (0, 0)),
                  pl.BlockSpec((d, tn), lambda j: (0, j)),
                  pl.BlockSpec((1, tn), lambda j: (0, j))],
        out_specs=pl.BlockSpec((rows, tn), lambda j: (0, j)),
        out_shape=jax.ShapeDtypeStruct((rows, n), F32),
        compiler_params=_cparams(("arbitrary",)),
        name="modulation",
    )(cond, w_mod, b_mod.reshape(1, n))


def _round_weights(w_hbm, wbf_ref, stage, sem):
    rows = stage.shape[1]
    n = w_hbm.shape[0] // rows
    copy = lambda r: pltpu.make_async_copy(w_hbm.at[pl.ds(r * rows, rows), :], stage.at[r % 2], sem.at[r % 2])
    copy(0).start()
    for r in range(n):
        if r + 1 < n:
            copy(r + 1).start()
        copy(r).wait()
        wbf_ref[r * rows:(r + 1) * rows, :] = stage[r % 2].astype(wbf_ref.dtype)


def _proj_kernel(lat_hbm, ctx_hbm, sh_ref, sc_ref, g_ref, win_hbm, pw_ref, xa_ref, ga_ref, z_ref,
                 xbuf, sem, wbf_ref, wstage, wsem, *, d_lru, gw, geom):
    step = pl.program_id(0)

    @pl.when(step == 0)
    def _():
        _round_weights(win_hbm, wbf_ref, wstage, wsem)

    x = _fetch_rows(lat_hbm, ctx_hbm, xbuf, sem, step, pl.num_programs(0), lambda s: s, geom)
    h = _modulate(_rms(x, g_ref[...]), sc_ref[0], sh_ref[0])
    proj = jnp.dot(h.astype(BF16), wbf_ref[...], preferred_element_type=F32)
    xa_ref[...] = proj[:, :d_lru].astype(BF16)
    ga_ref[...] = proj[:, d_lru:2 * d_lru].astype(BF16)
    for g in range(pw_ref.shape[0]):
        lo = 2 * d_lru + g * gw
        z_ref[:, g * gw:(g + 1) * gw] = jnp.dot(
            proj[:, lo:lo + gw].astype(BF16), pw_ref[g], preferred_element_type=F32).astype(BF16)


def _input_projection(x_lat, x_ctx, modtab, norm_g, w_in, pool_w, n_rows, geom):
    d = x_lat.shape[-1]
    tm = geom.tm
    d_pool = pool_w.shape[0] * pool_w.shape[1]
    d_lru = (w_in.shape[1] - d_pool) // 2
    mset = lambda i: jnp.where(i < geom.n_lat_tiles, 0, 1)
    const = dict(pipeline_mode=pl.Buffered(1))
    return pl.pallas_call(
        functools.partial(_proj_kernel, d_lru=d_lru, gw=pool_w.shape[1], geom=geom),
        grid=(n_rows // tm,),
        in_specs=[pl.BlockSpec(memory_space=pl.ANY),
                  pl.BlockSpec(memory_space=pl.ANY),
                  pl.BlockSpec((1, SUBLANES, d), lambda i: (mset(i), 0, 0)),
                  pl.BlockSpec((1, SUBLANES, d), lambda i: (mset(i), 0, 1)),
                  pl.BlockSpec((1, d), lambda i: (0, 0)),
                  pl.BlockSpec(memory_space=pl.ANY),
                  pl.BlockSpec(pool_w.shape, lambda i: (0, 0, 0), **const)],
        out_specs=[pl.BlockSpec((tm, d_lru), lambda i: (i, 0)),
                   pl.BlockSpec((tm, d_lru), lambda i: (i, 0)),
                   pl.BlockSpec((tm, d_pool), lambda i: (i, 0))],
        out_shape=[jax.ShapeDtypeStruct((n_rows, d_lru), BF16),
                   jax.ShapeDtypeStruct((n_rows, d_lru), BF16),
                   jax.ShapeDtypeStruct((n_rows, d_pool), BF16)],
        scratch_shapes=[pltpu.VMEM((2, tm // SUBLANES, SUBLANES, d), F32),
                        pltpu.SemaphoreType.DMA((2,)),
                        pltpu.VMEM(w_in.shape, BF16),
                        pltpu.VMEM((2, min(WEIGHT_STAGE_ROWS, d), w_in.shape[1]), F32),
                        pltpu.SemaphoreType.DMA((2,))],
        compiler_params=_cparams(("arbitrary",)),
        name="input_projection",
    )(x_lat, x_ctx, modtab, modtab, norm_g.reshape(1, d), w_in, pool_w)


def _fill_ext(ext_ref, prev_ref, main_ref, next_ref, first, last):
    tm = main_ref.shape[0]
    prev = prev_ref[...].astype(F32)
    nxt = next_ref[...].astype(F32)
    ext_ref[0:BF16_ROWS, :] = jnp.where(first, 0.0, prev)
    ext_ref[BF16_ROWS:BF16_ROWS + tm, :] = main_ref[...].astype(F32)
    ext_ref[BF16_ROWS + tm:, :] = jnp.where(last, 0.0, nxt)


def _one_minus_exp(y, exp_y):
    p = 1.0 / 120.0
    for c in (1.0 / 24.0, 1.0 / 6.0, 0.5, 1.0):
        p = p * y + c
    return jnp.where(y > -EXPM1_SERIES_BELOW, -y * p, 1.0 - exp_y)


def _sqrt_nonneg(q):
    return jnp.where(q > 0.0, q * lax.rsqrt(q), 0.0)


def _decay_and_input(ext_ref, cw_ref, cb_ref, wg_ref, bg_ref, lam_ref, a_ref, u_ref, tm):
    heads, bw = wg_ref.shape[0], wg_ref.shape[1]
    for hd in range(heads):
        sl = slice(hd * bw, (hd + 1) * bw)
        xc = cb_ref[:, sl]
        for k in range(CONV_W):
            xc = xc + cw_ref[k:k + 1, sl] * ext_ref[SUBLANES * k:SUBLANES * k + tm, sl]
        g = jnp.dot(xc.astype(BF16), wg_ref[hd], preferred_element_type=F32) + bg_ref[hd]
        r = jax.nn.sigmoid(g[:, :bw])
        ig = jax.nn.sigmoid(g[:, bw:])
        log_a = (RG_C * r) * lam_ref[:, sl]
        a = jnp.exp(log_a)
        a_ref[:, sl] = a
        u_ref[:, sl] = _sqrt_nonneg(_one_minus_exp(2.0 * log_a, a * a)) * (ig * xc)


def _scan(a_ref, u_ref, h, tm, reverse):
    nblk = tm // SUBLANES

    def body(s, h):
        j = (nblk - 1 - s) if reverse else s
        rows = pl.ds(pl.multiple_of(j * SUBLANES, SUBLANES), SUBLANES)
        h = a_ref[rows, :] * h + u_ref[rows, :]
        u_ref[rows, :] = h
        return h

    return lax.fori_loop(0, nblk, body, h, unroll=8)


def _halo_specs(tm, d_lru, n_rows):
    per = tm // BF16_ROWS
    last_blk = n_rows // BF16_ROWS - 1
    return [pl.BlockSpec((BF16_ROWS, d_lru), lambda i, tr, fl: (jnp.maximum(tr[i] * per - 1, 0), 0)),
            pl.BlockSpec((tm, d_lru), lambda i, tr, fl: (tr[i], 0)),
            pl.BlockSpec((BF16_ROWS, d_lru), lambda i, tr, fl: (jnp.minimum((tr[i] + 1) * per, last_blk), 0))]


def _fwd_kernel(tr_ref, fl_ref, prev_ref, main_ref, next_ref, cw_ref, cb_ref, wg_ref, bg_ref, lam_ref,
                h0_ref, hf_ref, hlast_ref, ext_ref, a_ref, u_ref, h_ref):
    i = pl.program_id(0)
    tm = main_ref.shape[0]
    first = fl_ref[0, i] == 1
    last = fl_ref[1, i] == 1
    _fill_ext(ext_ref, prev_ref, main_ref, next_ref, first, last)
    _decay_and_input(ext_ref, cw_ref, cb_ref, wg_ref, bg_ref, lam_ref, a_ref, u_ref, tm)

    @pl.when(first)
    def _():
        h_ref[...] = h0_ref[0]

    h = _scan(a_ref, u_ref, h_ref[...], tm, reverse=False)
    h_ref[...] = h
    hlast_ref[0] = h
    hf_ref[...] = u_ref[...].astype(BF16)


def _forward_scan(xa, meta, conv_w, conv_b, wg, bg, lam, h0, tm):
    t, d_lru = xa.shape
    tile_row, flags = meta
    ngrp = h0.shape[0]
    const = dict(pipeline_mode=pl.Buffered(1))
    grid_spec = pltpu.PrefetchScalarGridSpec(
        num_scalar_prefetch=2,
        grid=(t // tm,),
        in_specs=_halo_specs(tm, d_lru, t) + [
            pl.BlockSpec(conv_w.shape, lambda i, tr, fl: (0, 0)),
            pl.BlockSpec((1, d_lru), lambda i, tr, fl: (0, 0)),
            pl.BlockSpec(wg.shape, lambda i, tr, fl: (0, 0, 0), **const),
            pl.BlockSpec(bg.shape, lambda i, tr, fl: (0, 0, 0)),
            pl.BlockSpec((1, d_lru), lambda i, tr, fl: (0, 0)),
            pl.BlockSpec((1, SUBLANES, d_lru), lambda i, tr, fl: (fl[2, i], 0, 0))],
        out_specs=[pl.BlockSpec((tm, d_lru), lambda i, tr, fl: (tr[i], 0)),
                   pl.BlockSpec((1, SUBLANES, d_lru), lambda i, tr, fl: (fl[2, i], 0, 0))],
        scratch_shapes=[pltpu.VMEM((tm + 2 * BF16_ROWS, d_lru), F32),
                        pltpu.VMEM((tm, d_lru), F32),
                        pltpu.VMEM((tm, d_lru), F32),
                        pltpu.VMEM((SUBLANES, d_lru), F32)])
    return pl.pallas_call(
        _fwd_kernel,
        grid_spec=grid_spec,
        out_shape=[jax.ShapeDtypeStruct((t, d_lru), BF16),
                   jax.ShapeDtypeStruct((ngrp, SUBLANES, d_lru), F32)],
        compiler_params=_cparams(("arbitrary",)),
        name="forward_scan",
    )(tile_row, flags, xa, xa, xa, conv_w, conv_b.reshape(1, d_lru), wg, bg, lam.reshape(1, d_lru), h0)


def _bwd_kernel(tr_ref, fl_ref, prev_ref, main_ref, next_ref, cw_ref, cb_ref, wg_ref, bg_ref, lam_ref,
                h0_ref, hf_ref, ga_ref, ybl_ref, ybc_ref, lat_hbm, ctx_hbm, g1_ref, sh2_ref, sc2_ref, wout_ref,
                n2_ref, rwt_ref, rb_ref,
                x1_ref, h2_ref, lg_ref, hlast_ref,
                ext_ref, a_ref, u_ref, h_ref, cat_ref, xbuf, xsem, pack_ref, *, geom):
    i = pl.program_id(0)
    n_steps = pl.num_programs(0)
    tm, d_lru = main_ref.shape
    x = _fetch_rows(lat_hbm, ctx_hbm, xbuf, xsem, i, n_steps,
                    lambda s: tr_ref[jnp.minimum(s, n_steps - 1)], geom)
    first = fl_ref[0, i] == 1
    last = fl_ref[1, i] == 1
    _fill_ext(ext_ref, prev_ref, main_ref, next_ref, first, last)
    _decay_and_input(ext_ref, cw_ref, cb_ref, wg_ref, bg_ref, lam_ref, a_ref, u_ref, tm)

    @pl.when(last)
    def _():
        h_ref[...] = h0_ref[0]

    h = _scan(a_ref, u_ref, h_ref[...], tm, reverse=True)
    h_ref[...] = h
    hlast_ref[0] = h

    ga = ga_ref[...].astype(F32)
    ya = (hf_ref[...].astype(F32) + u_ref[...]) * jax.nn.gelu(ga)
    cat_ref[:, :d_lru] = ya.astype(BF16)
    cat_ref[:, d_lru:] = jnp.where(tr_ref[i] < geom.n_lat_tiles, ybl_ref[...], ybc_ref[...])
    mix = jnp.dot(cat_ref[...], wout_ref[...], preferred_element_type=F32)
    x1 = x + _gate(mix, g1_ref[0])
    x1_ref[...] = x1
    h2 = _modulate(_rms(x1, n2_ref[...]), sc2_ref[0], sh2_ref[0])
    half = h2.shape[1] // 2
    _store_packed_tokens(h2_ref, h2[:, :half], h2[:, half:], pack_ref)
    lg_ref[...] = lax.dot_general(rwt_ref[...], h2.astype(BF16), (((1,), (1,)), ((), ())),
                                  preferred_element_type=F32) + rb_ref[...]


def _backward_scan_mix(xa, ga, hf, yb_lat, yb_ctx, x_lat, x_ctx, modtab, meta, conv_w, conv_b, wg, bg, lam, h0, w_out,
                       norm2_g, rwt, rb, geom):
    t, d_lru = xa.shape
    d = x_lat.shape[-1]
    d_pool = yb_lat.shape[1]
    n_lat, n_ctx = yb_lat.shape[0] // geom.tm, yb_ctx.shape[0] // geom.tm
    tm = geom.tm
    tile_row, flags = meta
    ngrp = h0.shape[0]
    nr = rwt.shape[0]
    const = dict(pipeline_mode=pl.Buffered(1))
    mset = lambda i, tr, fl: jnp.where(tr[i] < geom.n_lat_tiles, 0, 1)
    row = lambda i, tr, fl: (tr[i], 0)
    mod = lambda col: pl.BlockSpec((1, SUBLANES, d), lambda i, tr, fl: (mset(i, tr, fl), 0, col))
    grid_spec = pltpu.PrefetchScalarGridSpec(
        num_scalar_prefetch=2,
        grid=(t // tm,),
        in_specs=_halo_specs(tm, d_lru, t) + [
            pl.BlockSpec(conv_w.shape, lambda i, tr, fl: (0, 0)),
            pl.BlockSpec((1, d_lru), lambda i, tr, fl: (0, 0)),
            pl.BlockSpec(wg.shape, lambda i, tr, fl: (0, 0, 0), **const),
            pl.BlockSpec(bg.shape, lambda i, tr, fl: (0, 0, 0)),
            pl.BlockSpec((1, d_lru), lambda i, tr, fl: (0, 0)),
            pl.BlockSpec((1, SUBLANES, d_lru), lambda i, tr, fl: (fl[2, i], 0, 0)),
            pl.BlockSpec((tm, d_lru), row),
            pl.BlockSpec((tm, d_lru), row),
            pl.BlockSpec((tm, d_pool), lambda i, tr, fl: (jnp.minimum(tr[i], n_lat - 1), 0)),
            pl.BlockSpec((tm, d_pool), lambda i, tr, fl: (jnp.clip(tr[i] - n_lat, 0, n_ctx - 1), 0)),
            pl.BlockSpec(memory_space=pl.ANY),
            pl.BlockSpec(memory_space=pl.ANY),
            mod(2), mod(3), mod(4),
            pl.BlockSpec(w_out.shape, lambda i, tr, fl: (0, 0), **const),
            pl.BlockSpec((1, d), lambda i, tr, fl: (0, 0)),
            pl.BlockSpec(rwt.shape, lambda i, tr, fl: (0, 0)),
            pl.BlockSpec((nr, 1), lambda i, tr, fl: (0, 0))],
        out_specs=[pl.BlockSpec((tm, d), row),
                   pl.BlockSpec((tm * (d // 2) // LANES, LANES), row),
                   pl.BlockSpec((nr, tm), lambda i, tr, fl: (0, tr[i])),
                   pl.BlockSpec((1, SUBLANES, d_lru), lambda i, tr, fl: (fl[2, i], 0, 0))],
        scratch_shapes=[pltpu.VMEM((tm + 2 * BF16_ROWS, d_lru), F32),
                        pltpu.VMEM((tm, d_lru), F32),
                        pltpu.VMEM((tm, d_lru), F32),
                        pltpu.VMEM((SUBLANES, d_lru), F32),
                        pltpu.VMEM((tm, d_lru + d_pool), BF16),
                        pltpu.VMEM((2, tm // SUBLANES, SUBLANES, d), F32),
                        pltpu.SemaphoreType.DMA((2,)),
                        pltpu.VMEM((d // 2 // LANES, 2 * tm, LANES), F32)])
    return pl.pallas_call(
        functools.partial(_bwd_kernel, geom=geom),
        grid_spec=grid_spec,
        out_shape=[jax.ShapeDtypeStruct((t, d), F32),
                   jax.ShapeDtypeStruct((t * (d // 2) // LANES, LANES), U32),
                   jax.ShapeDtypeStruct((nr, t), F32),
                   jax.ShapeDtypeStruct((ngrp, SUBLANES, d_lru), F32)],
        compiler_params=_cparams(("arbitrary",)),
        name="backward_scan_mix",
    )(tile_row, flags, xa, xa, xa, conv_w, conv_b.reshape(1, d_lru), wg, bg, lam.reshape(1, d_lru), h0,
      hf, ga, yb_lat, yb_ctx, x_lat, x_ctx, modtab, modtab, modtab, w_out, norm2_g.reshape(1, d), rwt, rb)


def _shift_rows(v, k):
    if k == 0:
        return v
    z = jnp.zeros((abs(k) * SUBLANES, v.shape[1]), v.dtype)
    if k > 0:
        return jnp.concatenate([z, v[:-k * SUBLANES]], axis=0)
    return jnp.concatenate([v[-k * SUBLANES:], z], axis=0)


def _run_sum(v, m, direction):
    if m & (m - 1) == 0:
        k = 1
        while k < m:
            v = v + _shift_rows(v, -direction * k)
            k *= 2
        return v
    out = v
    for j in range(1, m):
        out = out + _shift_rows(v, -direction * j)
    return out


def _box_sum(v, w):
    lo = w // 2
    hi = w - 1 - lo
    s = _run_sum(v, hi + 1, +1)
    if lo:
        s = s + _shift_rows(_run_sum(v, lo, -1), 1)
    return s


def _window_count(n_rows, lanes, n_pos, w):
    lo = w // 2
    hi = w - 1 - lo
    p = lax.shift_right_logical(lax.broadcasted_iota(I32, (n_rows, lanes), 0), 3)
    return (jnp.minimum(p + hi + 1, n_pos) - jnp.maximum(p - lo, 0)).astype(F32)


def _pool_kernel(z_ref, ps_ref, o_ref, v_ref, *, tiles_per_group, grid_rows, grid_cols):
    group = pl.program_id(1) // tiles_per_group
    lanes = z_ref.shape[1]
    ps = ps_ref[...]

    def pool_1d(w):
        z = z_ref[...].astype(F32)
        n = z.shape[0] // SUBLANES
        mean = _box_sum(z, w) / _window_count(z.shape[0], lanes, n, w)
        o_ref[...] = ((mean - z) * ps).astype(o_ref.dtype)

    def pool_2d(w):
        lo = w // 2
        hi = w - 1 - lo
        blk = grid_cols * SUBLANES
        cw = _window_count(blk, lanes, grid_cols, w)

        def zrow(r):
            return z_ref[pl.ds(pl.multiple_of(r * blk, blk), blk), :].astype(F32)

        v = jnp.zeros((blk, lanes), F32)
        for r in range(hi):
            v = v + zrow(r)
        v_ref[...] = v

        def body(r, carry):
            add = r + hi
            sub = r - lo - 1
            v = v_ref[...]
            v = v + jnp.where(add < grid_rows, zrow(jnp.minimum(add, grid_rows - 1)), 0.0)
            v = v - jnp.where(sub >= 0, zrow(jnp.maximum(sub, 0)), 0.0)
            v_ref[...] = v
            ch = (jnp.minimum(r + hi + 1, grid_rows) - jnp.maximum(r - lo, 0)).astype(F32)
            mean = _box_sum(v, w) / (ch * cw)
            o_ref[pl.ds(pl.multiple_of(r * blk, blk), blk), :] = ((mean - zrow(r)) * ps).astype(o_ref.dtype)
            return carry

        lax.fori_loop(0, grid_rows, body, 0)

    for g, w in enumerate(POOL_WINDOWS):
        @pl.when(group == g)
        def _(w=w):
            if grid_rows is None:
                pool_1d(w)
            else:
                pool_2d(w)


def _pool(z, pool_scale, rows_per_group, n_groups, first_block, grid_rows):
    d_pool = z.shape[1]
    gw = d_pool // len(POOL_WINDOWS)
    lanes = LANES
    blk = GRID_W * SUBLANES
    return pl.pallas_call(
        functools.partial(_pool_kernel, tiles_per_group=gw // lanes, grid_rows=grid_rows, grid_cols=GRID_W),
        grid=(n_groups, d_pool // lanes),
        in_specs=[pl.BlockSpec((rows_per_group, lanes), lambda g, j: (first_block + g, j)),
                  pl.BlockSpec((1, lanes), lambda g, j: (0, j))],
        out_specs=pl.BlockSpec((rows_per_group, lanes), lambda g, j: (g, j)),
        out_shape=jax.ShapeDtypeStruct((n_groups * rows_per_group, d_pool), BF16),
        scratch_shapes=[pltpu.VMEM((blk, lanes), F32)],
        compiler_params=_cparams(("arbitrary", "arbitrary")),
        name="pool_grid" if grid_rows is not None else "pool_seq",
    )(z, pool_scale.reshape(1, d_pool))


def _route_kernel(lg_ref, oi_ref, of_ref, cnt_ref, bc_ref, tri_ref, carry_ref, *, n_groups, per_group):
    step = pl.program_id(0)
    tt = lg_ref.shape[1]
    n_exp = n_groups * per_group

    @pl.when(step == 0)
    def _():
        r = lax.broadcasted_iota(I32, tri_ref.shape, 0)
        c = lax.broadcasted_iota(I32, tri_ref.shape, 1)
        tri_ref[...] = (r <= c).astype(BF16)
        carry_ref[...] = jnp.zeros_like(carry_ref)

    row = lambda k: lg_ref[k:k + 1, :]
    cmax = row(0)
    gi = jnp.zeros((1, tt), I32)
    for g in range(1, n_groups):
        better = row(g) > cmax
        gi = jnp.where(better, g, gi)
        cmax = jnp.where(better, row(g), cmax)
    denom = jnp.zeros((1, tt), F32)
    for g in range(n_groups):
        denom = denom + jnp.exp(row(g) - cmax)
    pg = 1.0 / denom
    fine = []
    for j in range(per_group):
        f = row(n_groups + j)
        for g in range(1, n_groups):
            f = jnp.where(gi == g, row(n_groups + g * per_group + j), f)
        fine.append(f)
    v1 = fine[0]
    i1 = jnp.zeros((1, tt), I32)
    for j in range(1, per_group):
        better = fine[j] > v1
        i1 = jnp.where(better, j, i1)
        v1 = jnp.where(better, fine[j], v1)
    v2 = jnp.full((1, tt), -jnp.inf, F32)
    i2 = jnp.zeros((1, tt), I32)
    for j in range(per_group):
        better = jnp.logical_and(i1 != j, fine[j] > v2)
        i2 = jnp.where(better, j, i2)
        v2 = jnp.where(better, fine[j], v2)
    ex = jnp.exp(v2 - v1)
    w1 = (1.0 / (1.0 + ex)) * pg
    w2 = (ex / (1.0 + ex)) * pg
    e1 = gi * per_group + i1
    e2 = gi * per_group + i2
    eid = lax.broadcasted_iota(I32, (n_exp, tt), 0)
    hit1 = eid == e1
    hit2 = eid == e2
    member = jnp.logical_or(hit1, hit2).astype(BF16)
    cb = tri_ref.shape[0]
    carry = carry_ref[...]
    cums = []
    lane = lax.broadcasted_iota(I32, bc_ref.shape, 1)
    block_ends = jnp.zeros(bc_ref.shape, F32)
    for s in range(tt // cb):
        c = jnp.dot(member[:, s * cb:(s + 1) * cb], tri_ref[...], preferred_element_type=F32) + carry
        carry = c[:, cb - 1:cb]
        cums.append(c)
        block_ends = jnp.where(lane == s, carry, block_ends)
    bc_ref[...] = block_ends.astype(I32)
    carry_ref[...] = carry
    cum = jnp.concatenate(cums, axis=1)
    rank1 = jnp.sum(jnp.where(hit1, cum, 0.0), axis=0, keepdims=True) - 1.0
    rank2 = jnp.sum(jnp.where(hit2, cum, 0.0), axis=0, keepdims=True) - 1.0
    zi = jnp.zeros((SUBLANES - 4, tt), I32)
    oi_ref[...] = jnp.concatenate([e1, e2, rank1.astype(I32), rank2.astype(I32), zi], axis=0)
    of_ref[...] = jnp.concatenate([w1, w2, jnp.zeros((of_ref.shape[0] - 2, tt), F32)], axis=0)
    cnt_ref[...] = jnp.broadcast_to(carry, cnt_ref.shape).astype(I32)


def _route(logits_t, n_groups, per_group, cb):
    nr, t = logits_t.shape
    tt = max(k for k in range(cb, min(ROUTE_TILE, t) + 1, cb) if t % k == 0)
    n_exp = n_groups * per_group
    return pl.pallas_call(
        functools.partial(_route_kernel, n_groups=n_groups, per_group=per_group),
        grid=(t // tt,),
        in_specs=[pl.BlockSpec((nr, tt), lambda i: (0, i))],
        out_specs=[pl.BlockSpec((SUBLANES, tt), lambda i: (0, i)),
                   pl.BlockSpec((LANES, tt), lambda i: (0, i)),
                   pl.BlockSpec((n_exp, LANES), lambda i: (0, 0)),
                   pl.BlockSpec((n_exp, LANES), lambda i: (0, i))],
        out_shape=[jax.ShapeDtypeStruct((SUBLANES, t), I32),
                   jax.ShapeDtypeStruct((LANES, t), F32),
                   jax.ShapeDtypeStruct((n_exp, LANES), I32),
                   jax.ShapeDtypeStruct((n_exp, LANES * (t // tt)), I32)],
        scratch_shapes=[pltpu.VMEM((cb, cb), BF16), pltpu.VMEM((n_exp, 1), F32)],
        compiler_params=_cparams(("arbitrary",)),
        name="route",
    )(logits_t)


def _tiles_for(cnt, tmx):
    return jnp.right_shift(cnt + (SEG_CHUNK + tmx - 1), tmx.bit_length() - 1)


def _plan_kernel(ri_ref, cnt_ref, bc_ref, bcp_ref, lp_ref, dst_ref, src_ref, len_ref, te_ref, nv_ref,
                 off_ref, *, tmx, tb):
    step = pl.program_id(0)
    n_exp = cnt_ref.shape[0]
    tt = ri_ref.shape[1]
    shift = tmx.bit_length() - 1
    n_tile = _tiles_for(cnt_ref[:, 0:1], tmx)

    def exclusive_prefix(col):
        run = jnp.zeros((1, 1), I32)
        parts = []
        for e in range(n_exp):
            parts.append(run)
            run = run + col[e:e + 1, :]
        return jnp.concatenate(parts, axis=0), run

    off, total = exclusive_prefix(jnp.left_shift(n_tile, shift))
    before = jnp.where(step == 0, 0, bcp_ref[:, tt // tb - 1:tt // tb])
    lane = lax.broadcasted_iota(I32, dst_ref.shape, 1)
    eid = lax.broadcasted_iota(I32, (n_exp, tb), 0)
    dst = jnp.zeros(dst_ref.shape, I32)
    src = jnp.zeros(dst_ref.shape, I32)
    length = jnp.zeros(dst_ref.shape, I32)
    rows = []
    for k in range(tt // tb):
        end = bc_ref[:, k:k + 1]
        n = end - before
        aligned = jnp.left_shift(jnp.right_shift(n + (SEG_CHUNK - 1), SEG_SHIFT), SEG_SHIFT)
        start, _ = exclusive_prefix(aligned)
        base = start - before
        sl = slice(k * tb, (k + 1) * tb)
        place = lambda e_row, r_row: jnp.sum(jnp.where(eid == e_row, base, 0), axis=0, keepdims=True) + r_row
        rows.append(jnp.concatenate([place(ri_ref[0:1, sl], ri_ref[2:3, sl]),
                                     place(ri_ref[1:2, sl], ri_ref[3:4, sl])], axis=0))
        dst = jnp.where(lane == k, off + before, dst)
        src = jnp.where(lane == k, start, src)
        length = jnp.where(lane == k, n, length)
        before = end
    lp_ref[...] = jnp.concatenate([jnp.concatenate(rows, axis=1),
                                   jnp.zeros((SUBLANES - TOP_K_FINE, tt), I32)], axis=0)
    dst_ref[...] = dst
    src_ref[...] = src
    len_ref[...] = length
    end_tile = jnp.right_shift(off, shift) + n_tile
    k = lax.broadcasted_iota(I32, (n_exp, te_ref.shape[1]), 1)
    te = jnp.sum((k >= end_tile).astype(I32), axis=0, keepdims=True)
    te_ref[...] = jnp.minimum(te, n_exp - 1)
    nv_ref[...] = jnp.broadcast_to(jnp.right_shift(total, shift), nv_ref.shape)
    off_ref[...] = jnp.broadcast_to(off, off_ref.shape)


def _plan(ri, counts, block_counts, tmx, n_tiles, tb):
    _, t = ri.shape
    n_exp = counts.shape[0]
    steps = block_counts.shape[1] // LANES
    tt = t // steps
    ntp = -(-n_tiles // LANES) * LANES
    per_step = lambda: pl.BlockSpec((n_exp, LANES), lambda i: (0, i))
    whole = lambda w: pl.BlockSpec((n_exp, w), lambda i: (0, 0))
    tab = jax.ShapeDtypeStruct((n_exp, LANES * steps), I32)
    lp, dst, src, length, te, nv, off = pl.pallas_call(
        functools.partial(_plan_kernel, tmx=tmx, tb=tb),
        grid=(steps,),
        in_specs=[pl.BlockSpec((SUBLANES, tt), lambda i: (0, i)),
                  whole(LANES),
                  per_step(),
                  pl.BlockSpec((n_exp, LANES), lambda i: (0, jnp.maximum(i - 1, 0)))],
        out_specs=[pl.BlockSpec((SUBLANES, tt), lambda i: (0, i)),
                   per_step(), per_step(), per_step(),
                   pl.BlockSpec((1, ntp), lambda i: (0, 0)),
                   pl.BlockSpec((1, LANES), lambda i: (0, 0)),
                   whole(LANES)],
        out_shape=[jax.ShapeDtypeStruct((SUBLANES, t), I32), tab, tab, tab,
                   jax.ShapeDtypeStruct((1, ntp), I32),
                   jax.ShapeDtypeStruct((1, LANES), I32),
                   jax.ShapeDtypeStruct((n_exp, LANES), I32)],
        compiler_params=_cparams(("arbitrary",)),
        name="plan",
    )(ri, counts, block_counts, block_counts)
    flat = lambda a: a.reshape(n_exp, steps, LANES)[:, :, :tt // tb].transpose(1, 2, 0).reshape(-1)
    tables = (flat(dst), flat(src), flat(length))
    return (lp[0].reshape(t // tb, tb), lp[1].reshape(t // tb, tb), tables,
            te[0, :n_tiles], nv[0, :1], off[:, 0])


def _load_positions(pos1_hbm, pos2_hbm, p1_ref, p2_ref, isem, step, n_steps):
    tb = p1_ref.shape[0] // 2
    slot = lax.rem(step, 2)

    def copies(blk, half):
        rows = pl.ds(pl.multiple_of(half * tb, tb), tb)
        return (pltpu.make_async_copy(pos1_hbm.at[blk], p1_ref.at[rows], isem.at[0, half]),
                pltpu.make_async_copy(pos2_hbm.at[blk], p2_ref.at[rows], isem.at[1, half]))

    @pl.when(step == 0)
    def _():
        for cp in copies(step, slot):
            cp.start()

    @pl.when(step + 1 < n_steps)
    def _():
        for cp in copies(step + 1, 1 - slot):
            cp.start()

    for cp in copies(step, slot):
        cp.wait()
    return slot * tb


def _token_rows(token, count, nc):
    return pl.ds(pl.multiple_of(token * nc, nc), count * nc)


def _segment_copy(stage_ref, far_hbm, near_tok, far_tok, sem, to_far, nc):
    near = stage_ref.at[_token_rows(near_tok, SEG_CHUNK, nc), :]
    far = far_hbm.at[_token_rows(far_tok, SEG_CHUNK, nc), :]
    return pltpu.make_async_copy(near, far, sem) if to_far else pltpu.make_async_copy(far, near, sem)


def _start_segments(tables, blk, n_exp, stage_ref, far_hbm, sem, to_far, nc):
    dst_ref, src_ref, len_ref = tables

    total = jnp.int32(0)
    for e in range(n_exp):
        k = blk * n_exp + e
        n_chunks = jnp.right_shift(len_ref[k] + (SEG_CHUNK - 1), SEG_SHIFT)
        near0 = src_ref[k]
        far0 = dst_ref[k]

        def one(c, carry, near0=near0, far0=far0, queue=e % 2):
            _segment_copy(stage_ref, far_hbm, near0 + c * SEG_CHUNK, far0 + c * SEG_CHUNK, sem,
                          to_far, nc).start(priority=queue)
            return carry
        lax.fori_loop(0, n_chunks, one, 0)
        total = total + n_chunks
    return total


def _wait_segments(stage_ref, far_hbm, sem, n, to_far, nc):
    copy = _segment_copy(stage_ref, far_hbm, 0, 0, sem, to_far, nc)

    def body(_, c):
        copy.wait()
        return c
    lax.fori_loop(0, n, body, 0)


def _dispatch_kernel(dst_ref, src_ref, len_ref, off_ref, cnt_ref, lp1_hbm, lp2_hbm, h2_ref, xs_hbm,
                     p1_ref, p2_ref, stage, nd_ref, zero_ref, isem, ssem, zsem, *, tmx):
    step = pl.program_id(0)
    n_steps = pl.num_programs(0)
    slot = lax.rem(step, 2)
    n_exp = off_ref.shape[0]
    nc = zero_ref.shape[0]
    n_blk = h2_ref.shape[0] // nc // SUBLANES
    tables = (dst_ref, src_ref, len_ref)

    @pl.when(step == 0)
    def _():
        for s in range(stage.shape[0]):
            stage[s] = _packed_zeros(stage.shape[1])

    p0 = _load_positions(lp1_hbm, lp2_hbm, p1_ref, p2_ref, isem, step, n_steps)

    def place(blk, c):
        j0 = blk * SUBLANES
        for s in range(SUBLANES):
            row = h2_ref[_token_rows(j0 + s, 1, nc), :]
            stage[slot, _token_rows(p1_ref[p0 + j0 + s], 1, nc), :] = row
            stage[slot, _token_rows(p2_ref[p0 + j0 + s], 1, nc), :] = row
        return c
    lax.fori_loop(0, n_blk, place, 0, unroll=2)

    @pl.when(step >= 1)
    def _():
        _wait_segments(stage.at[1 - slot], xs_hbm, ssem.at[1 - slot], nd_ref[1 - slot], True, nc)

    nd_ref[slot] = _start_segments(tables, step, n_exp, stage.at[slot], xs_hbm, ssem.at[slot], True, nc)

    @pl.when(step == n_steps - 1)
    def _():
        _wait_segments(stage.at[slot], xs_hbm, ssem.at[slot], nd_ref[slot], True, nc)
        zero_ref[...] = _packed_zeros(nc)
        shift = tmx.bit_length() - 1

        def pad_expert(e, c):
            cnt = cnt_ref[e]
            n_pad = jnp.left_shift(_tiles_for(cnt, tmx), shift) - cnt
            first = off_ref[e] + cnt
            zero_copy = lambda r: pltpu.make_async_copy(
                zero_ref, xs_hbm.at[_token_rows(first + r, 1, nc), :], zsem)

            def fill(r, c2):
                zero_copy(r).start()
                return c2
            lax.fori_loop(0, n_pad, fill, 0)

            def done(r, c2):
                zero_copy(r).wait()
                return c2
            lax.fori_loop(0, n_pad, done, 0)
            return c
        lax.fori_loop(0, off_ref.shape[0], pad_expert, 0)


def _stage_rows(tb, n_exp):
    return TOP_K_FINE * tb + n_exp * SEG_CHUNK


def _dispatch(tables, offsets, counts, lp1, lp2, h2p, p_max, tmx):
    nb, tb = lp1.shape
    nc = h2p.shape[0] // (nb * tb)
    n_exp = offsets.shape[0]
    hbm = lambda: pl.BlockSpec(memory_space=pl.ANY)
    grid_spec = pltpu.PrefetchScalarGridSpec(
        num_scalar_prefetch=5,
        grid=(nb,),
        in_specs=[hbm(), hbm(),
                  pl.BlockSpec((tb * nc, LANES), lambda i, *_: (i, 0))],
        out_specs=hbm(),
        scratch_shapes=[pltpu.SMEM((2 * tb,), I32),
                        pltpu.SMEM((2 * tb,), I32),
                        pltpu.VMEM((2, _stage_rows(tb, n_exp) * nc, LANES), U32),
                        pltpu.SMEM((2,), I32),
                        pltpu.VMEM((nc, LANES), U32),
                        pltpu.SemaphoreType.DMA((2, 2)),
                        pltpu.SemaphoreType.DMA((2,)),
                        pltpu.SemaphoreType.DMA(())])
    return pl.pallas_call(
        functools.partial(_dispatch_kernel, tmx=tmx),
        grid_spec=grid_spec,
        out_shape=jax.ShapeDtypeStruct((p_max * nc, LANES), U32),
        compiler_params=_cparams(("arbitrary",)),
        name="dispatch",
    )(*tables, offsets, counts, lp1, lp2, h2p)


def _expert_kernel(te_ref, nv_ref, x_ref, w1_ref, w3_ref, w2_ref, y_ref, pack_ref, *, tmx):
    @pl.when(pl.program_id(0) < nv_ref[0])
    def _():
        lo, hi = _load_packed_tokens(x_ref, tmx, pack_ref)
        lo = lo.astype(BF16)
        hi = hi.astype(BF16)
        half = lo.shape[1]
        mm = lambda w_ref: (jnp.dot(lo, w_ref[0, :half, :].astype(BF16), preferred_element_type=F32)
                            + jnp.dot(hi, w_ref[0, half:, :].astype(BF16), preferred_element_type=F32))
        h1 = mm(w1_ref)
        h3 = mm(w3_ref)
        hh = (h1 * jax.nn.sigmoid(h1) * h3).astype(BF16)
        y = jnp.dot(hh, w2_ref[0].astype(BF16), preferred_element_type=F32)
        _store_packed_tokens(y_ref, y[:, :half], y[:, half:], pack_ref)


def _expert_mlp(xs, tile_expert, n_valid, w1, w3, w2, tmx):
    d, de = w1.shape[1], w1.shape[2]
    nc = d // 2 // LANES
    p = xs.shape[0] // nc
    tile = lambda i, te, nv: (jnp.minimum(i, nv[0] - 1), 0)
    grid_spec = pltpu.PrefetchScalarGridSpec(
        num_scalar_prefetch=2,
        grid=(p // tmx,),
        in_specs=[pl.BlockSpec((tmx * nc, LANES), tile),
                  pl.BlockSpec((1, d, de), lambda i, te, nv: (te[i], 0, 0)),
                  pl.BlockSpec((1, d, de), lambda i, te, nv: (te[i], 0, 0)),
                  pl.BlockSpec((1, de, d), lambda i, te, nv: (te[i], 0, 0))],
        out_specs=pl.BlockSpec((tmx * nc, LANES), tile),
        scratch_shapes=[pltpu.VMEM((nc, 2 * tmx, LANES), F32)])
    return pl.pallas_call(
        functools.partial(_expert_kernel, tmx=tmx),
        grid_spec=grid_spec,
        out_shape=jax.ShapeDtypeStruct((p * nc, LANES), U32),
        compiler_params=_cparams(("arbitrary",)),
        name="expert_mlp",
    )(tile_expert, n_valid, xs, w1, w3, w2)


def _combine_kernel(dst_ref, src_ref, len_ref, lp1_hbm, lp2_hbm, ys_hbm, x1_ref, rf_ref, g2_ref, fg_ref,
                    ylat_hbm, yctx_hbm, p1_ref, p2_ref, stage, gbuf, pack_ref, obuf, nd_ref, isem, csem, osem,
                    *, geom, n_exp):
    step = pl.program_id(0)
    n_steps = pl.num_programs(0)
    tm = geom.tm
    n_blk = tm // SUBLANES
    slot = lax.rem(step, 2)
    nc = gbuf.shape[1] // tm
    tables = (dst_ref, src_ref, len_ref)

    def fetch(blk, gs):
        nd_ref[gs] = _start_segments(tables, blk, n_exp, stage.at[gs], ys_hbm, csem.at[gs], False, nc)

    @pl.when(step == 0)
    def _():
        fetch(step, slot)

    @pl.when(step + 1 < n_steps)
    def _():
        fetch(step + 1, 1 - slot)

    _wait_segments(stage.at[slot], ys_hbm, csem.at[slot], nd_ref[slot], False, nc)
    p0 = _load_positions(lp1_hbm, lp2_hbm, p1_ref, p2_ref, isem, step, n_steps)

    def pick(blk, c):
        j0 = blk * SUBLANES
        for s in range(SUBLANES):
            rows = _token_rows(j0 + s, 1, nc)
            gbuf[0, rows, :] = stage[slot, _token_rows(p1_ref[p0 + j0 + s], 1, nc), :]
            gbuf[1, rows, :] = stage[slot, _token_rows(p2_ref[p0 + j0 + s], 1, nc), :]
        return c
    lax.fori_loop(0, n_blk, pick, 0, unroll=2)

    wt = rf_ref[...].T
    a_lo, a_hi = _load_packed_tokens(gbuf.at[0], tm, pack_ref.at[0])
    b_lo, b_hi = _load_packed_tokens(gbuf.at[1], tm, pack_ref.at[1])
    w1 = wt[:, 0:1]
    w2 = wt[:, 1:2]
    moe = jnp.concatenate([w1 * a_lo + w2 * b_lo, w1 * a_hi + w2 * b_hi], axis=1)
    x = x1_ref[...] + _gate(moe, g2_ref[0])
    out = _rms(x, fg_ref[...])

    start = lambda cp: cp.start()
    wait = lambda cp: cp.wait()
    put = functools.partial(_tile_copies, ylat_hbm, yctx_hbm, geom=geom, to_rows=False)

    @pl.when(step >= 2)
    def _():
        put(obuf.at[slot], osem.at[slot], step - 2, fn=wait)

    obuf[slot] = out.reshape(obuf.shape[1:])
    put(obuf.at[slot], osem.at[slot], step, fn=start)

    @pl.when(step == n_steps - 1)
    def _():
        @pl.when(n_steps > 1)
        def _():
            put(obuf.at[1 - slot], osem.at[1 - slot], step - 1, fn=wait)
        put(obuf.at[slot], osem.at[slot], step, fn=wait)


def _combine(tables, lp1, lp2, ys, x1, rf, modtab, final_g, lat_shape, ctx_shape, geom):
    t, d = x1.shape
    tm = geom.tm
    nc = d // 2 // LANES
    n_exp = tables[0].shape[0] // (t // tm)
    mset = lambda i, *_: jnp.where(i < geom.n_lat_tiles, 0, 1)
    hbm = lambda: pl.BlockSpec(memory_space=pl.ANY)
    grid_spec = pltpu.PrefetchScalarGridSpec(
        num_scalar_prefetch=3,
        grid=(t // tm,),
        in_specs=[hbm(), hbm(), hbm(),
                  pl.BlockSpec((tm, d), lambda i, *_: (i, 0)),
                  pl.BlockSpec((LANES, tm), lambda i, *_: (0, i)),
                  pl.BlockSpec((1, SUBLANES, d), lambda i, *_: (mset(i), 0, 5)),
                  pl.BlockSpec((1, d), lambda i, *_: (0, 0))],
        out_specs=[hbm(), hbm()],
        scratch_shapes=[pltpu.SMEM((2 * tm,), I32),
                        pltpu.SMEM((2 * tm,), I32),
                        pltpu.VMEM((2, _stage_rows(tm, n_exp) * nc, LANES), U32),
                        pltpu.VMEM((TOP_K_FINE, tm * nc, LANES), U32),
                        pltpu.VMEM((TOP_K_FINE, nc, 2 * tm, LANES), F32),
                        pltpu.VMEM((2, tm // SUBLANES, SUBLANES, d), F32),
                        pltpu.SMEM((2,), I32),
                        pltpu.SemaphoreType.DMA((2, 2)),
                        pltpu.SemaphoreType.DMA((2,)),
                        pltpu.SemaphoreType.DMA((2,))])
    return pl.pallas_call(
        functools.partial(_combine_kernel, geom=geom, n_exp=n_exp),
        grid_spec=grid_spec,
        out_shape=[jax.ShapeDtypeStruct(lat_shape, F32), jax.ShapeDtypeStruct(ctx_shape, F32)],
        compiler_params=_cparams(("arbitrary",)),
        name="combine",
    )(*tables, lp1, lp2, ys, x1, rf, modtab, final_g.reshape(1, d))


def _tile_meta(groups, tm):
    rows, first, last, grp = [], [], [], []
    blk = 0
    for g, r in enumerate(groups):
        nc = r // tm
        for c in range(nc):
            rows.append(blk + c)
            first.append(int(c == 0))
            last.append(int(c == nc - 1))
            grp.append(g)
        blk += nc
    fwd = (np.array(rows, np.int32), np.array([first, last, grp], np.int32))
    order = []
    blk = 0
    for r in groups:
        nc = r // tm
        order.extend(range(blk + nc - 1, blk - 1, -1))
        blk += nc
    order = np.array(order)
    bwd = (fwd[0][order], fwd[1][:, order])
    return fwd, bwd


def kernel(x_prompt, x_sample, state_lru, c, c_ctx, w_mod, b_mod, norm1_g, w_in, conv_w, conv_b, lru_wa, lru_ba, lru_wx, lru_bx, lru_lambda, pool_w, pool_scale, w_out, norm2_g, router_coarse_w, router_coarse_b, router_fine_w, router_fine_b, exp_w1, exp_w3, exp_w2, final_norm_g):
    bp, sp, d = x_prompt.shape
    bs, ss, _ = x_sample.shape
    d_lru = lru_lambda.shape[-1]
    heads, bw = lru_wa.shape[2], lru_wa.shape[3]
    n_groups, per_group = router_fine_w.shape[2], router_fine_w.shape[3]
    n_exp = n_groups * per_group
    assert w_mod.shape[0] == 1 and bs == SUBLANES and bp % SUBLANES == 0 and ss % GRID_W == 0
    assert EXPERT_TILE & (EXPERT_TILE - 1) == 0
    n_lat_groups, n_ctx_groups = bs // SUBLANES, bp // SUBLANES
    lat_rows, ctx_rows = ss * SUBLANES, sp * SUBLANES
    assert (n_lat_groups * lat_rows) % ctx_rows == 0
    tm = min(TOKEN_TILE, ctx_rows, lat_rows)
    geom = _Geom(tm=tm, n_lat_tiles=n_lat_groups * lat_rows // tm, lat_chunks=lat_rows // tm,
                 ctx_chunks=ctx_rows // tm)
    n_rows = n_lat_groups * lat_rows + n_ctx_groups * ctx_rows
    groups = [lat_rows] * n_lat_groups + [ctx_rows] * n_ctx_groups
    (f_rows, f_flags), (b_rows, b_flags) = _tile_meta(groups, tm)

    cond = jnp.zeros((2 * SUBLANES, d), F32).at[:bs].set(c).at[bs].set(c_ctx)
    mod = _modulation(cond, w_mod[0], b_mod[0])
    modtab = jnp.stack([mod[:SUBLANES], jnp.broadcast_to(mod[SUBLANES], (SUBLANES, mod.shape[1]))])
    h0_lat = state_lru[:, 0].reshape(n_lat_groups, SUBLANES, 2, d_lru)
    h0 = jnp.concatenate([h0_lat, jnp.zeros((n_ctx_groups, SUBLANES, 2, d_lru), F32)], axis=0)
    h0 = h0.transpose(0, 2, 1, 3)

    xa, ga, z = _input_projection(x_sample, x_prompt, modtab, norm1_g[0], w_in[0],
                                  pool_w[0].astype(BF16), n_rows, geom)

    def gate_weights(direction):
        wg = jnp.concatenate([lru_wa[0, direction], lru_wx[0, direction]], axis=-1).astype(BF16)
        bg = jnp.concatenate([lru_ba[0, direction].reshape(heads, 1, bw),
                              lru_bx[0, direction].reshape(heads, 1, bw)], axis=-1)
        return wg, bg

    log_decay = jax.nn.log_sigmoid(lru_lambda[0])
    wg_f, bg_f = gate_weights(0)
    wg_b, bg_b = gate_weights(1)
    hf, hf_last = _forward_scan(xa, (jnp.asarray(f_rows), jnp.asarray(f_flags)), conv_w[0], conv_b[0],
                                wg_f, bg_f, log_decay[0], h0[:, 0], tm)

    yb_lat = _pool(z, pool_scale[0], lat_rows, n_lat_groups, 0, lat_rows // (GRID_W * SUBLANES))
    yb_ctx = _pool(z, pool_scale[0], ctx_rows, n_ctx_groups, n_lat_groups * lat_rows // ctx_rows, None)

    n_logits = n_groups + n_exp
    rw = jnp.concatenate([router_coarse_w[0], router_fine_w[0].reshape(d, n_exp)], axis=1)
    rwt = jnp.zeros((LANES, d), BF16).at[:n_logits].set(rw.T.astype(BF16))
    rb = jnp.zeros((LANES, 1), F32).at[:n_logits, 0].set(
        jnp.concatenate([router_coarse_b[0], router_fine_b[0].reshape(n_exp)]))
    x1, h2p, logits_t, hb_last = _backward_scan_mix(
        xa, ga, hf, yb_lat, yb_ctx, x_sample, x_prompt, modtab, (jnp.asarray(b_rows), jnp.asarray(b_flags)),
        conv_w[0], conv_b[0], wg_b, bg_b, log_decay[1], h0[:, 1], w_out[0].astype(BF16), norm2_g[0],
        rwt, rb, geom)

    ri, rf, counts, block_counts = _route(logits_t, n_groups, per_group, tm)
    n_tiles = -(-(TOP_K_FINE * n_rows + n_exp * (EXPERT_TILE + SEG_CHUNK)) // EXPERT_TILE)
    lp1, lp2, tables, tile_expert, n_valid, offsets = _plan(ri, counts, block_counts, EXPERT_TILE, n_tiles, tm)
    xs = _dispatch(tables, offsets, counts[:, 0], lp1, lp2, h2p, n_tiles * EXPERT_TILE, EXPERT_TILE)
    ys = _expert_mlp(xs, tile_expert, n_valid, exp_w1[0], exp_w3[0], exp_w2[0], EXPERT_TILE)
    y_sample, y_prompt = _combine(tables, lp1, lp2, ys, x1, rf, modtab, final_norm_g, x_sample.shape,
                                  x_prompt.shape, geom)

    st = jnp.stack([hf_last[n_lat_groups:], hb_last[n_lat_groups:]], axis=2)
    state_new = st.reshape(bp, 1, 2, d_lru).astype(x_prompt.dtype)
    return (y_prompt, y_sample, state_new)
```
